```python
import jax, jax.numpy as jnp
from jax import lax
import numpy as np

D_MODEL = 2048
BATCH = 8
SEQ = 4096
DEPTH = 4

N_MIXERS = 3
MEM_LEN = 256
EPS = 1e-6
BLOCK = 128

SSD_EXPAND = 2
SSD_D_INNER = SSD_EXPAND * D_MODEL
SSD_HEAD_DIM = 64
SSD_HEADS = SSD_D_INNER // SSD_HEAD_DIM
SSD_GROUPS = 8
SSD_HEADS_PER_GROUP = SSD_HEADS // SSD_GROUPS
SSD_STATE = 128
SSD_CONV = 4
SSD_CHUNK = 128
SSD_CONV_DIM = SSD_D_INNER + 2 * SSD_GROUPS * SSD_STATE
SSD_IN_DIM = 2 * SSD_D_INNER + 2 * SSD_GROUPS * SSD_STATE + SSD_HEADS

SG_WIDTH = 2 * D_MODEL
SG_GROUPS = 16
SG_CHUNK = 128

SB_HEAD_DIM = 128
SB_HEADS = D_MODEL // SB_HEAD_DIM

XA_HEADS = 4
XA_HEAD_DIM = 128
XA_WIDTH = XA_HEADS * XA_HEAD_DIM

FFN_DIM = 5632
FFN_CONV = 3

N_SSD = (DEPTH + N_MIXERS - 1) // N_MIXERS
N_SG = (DEPTH + N_MIXERS - 2) // N_MIXERS
N_SB = DEPTH // N_MIXERS

kernel_name = "interleaved_ssd_gmlp_stickbreak_trunk"


def rmsnorm(x, g):
    xf = x.astype(jnp.float32)
    y = xf * lax.rsqrt(jnp.mean(xf * xf, axis=-1, keepdims=True) + EPS)
    return (y * g.astype(jnp.float32)).astype(x.dtype)


def layernorm(x, g, b):
    xf = x.astype(jnp.float32)
    mu = jnp.mean(xf, axis=-1, keepdims=True)
    xc = xf - mu
    y = xc * lax.rsqrt(jnp.mean(xc * xc, axis=-1, keepdims=True) + EPS)
    return (y * g.astype(jnp.float32) + b.astype(jnp.float32)).astype(x.dtype)


def causal_dwconv(x, w, b):
    K, C = w.shape
    y = lax.conv_general_dilated(
        x, w[:, None, :].astype(x.dtype), window_strides=(1,), padding=[(K - 1, 0)],
        dimension_numbers=("NWC", "WIO", "NWC"), feature_group_count=C)
    return y + b.astype(x.dtype)


def ssd_mixer(h, w_in, conv_w, conv_b, dt_bias, a_log, d_skip, norm_g, w_out):
    Bsz, L, _ = h.shape
    G, R, P, N, Q = SSD_GROUPS, SSD_HEADS_PER_GROUP, SSD_HEAD_DIM, SSD_STATE, SSD_CHUNK
    nc = L // Q
    f32 = jnp.float32
    proj = h @ w_in
    z, xbc, dt = jnp.split(proj, [SSD_D_INNER, SSD_D_INNER + SSD_CONV_DIM], axis=-1)
    xbc = jax.nn.silu(causal_dwconv(xbc, conv_w, conv_b))
    xs, Bm, Cm = jnp.split(xbc, [SSD_D_INNER, SSD_D_INNER + G * N], axis=-1)
    dt = jax.nn.softplus(dt.astype(f32) + dt_bias.astype(f32))
    A = -jnp.exp(a_log.astype(f32)).reshape(G, R)
    xs_f = xs.astype(f32).reshape(Bsz, nc, Q, G, R, P)
    dtc = dt.reshape(Bsz, nc, Q, G, R)
    Bc = Bm.astype(f32).reshape(Bsz, nc, Q, G, N)
    Cc = Cm.astype(f32).reshape(Bsz, nc, Q, G, N)
    a_cum = jnp.cumsum(dtc * A, axis=2)
    xdt = xs_f * dtc[..., None]
    causal = jnp.tril(jnp.ones((Q, Q), dtype=bool))[None, None, :, :, None, None]
    seg = a_cum[:, :, :, None] - a_cum[:, :, None, :]
    Lmat = jnp.exp(jnp.where(causal, seg, -jnp.inf))
    CB = jnp.einsum("bclgn,bcsgn->bclsg", Cc, Bc)
    y_diag = jnp.einsum("bclsg,bclsgr,bcsgrp->bclgrp", CB, Lmat, xdt)
    decay_states = jnp.exp(a_cum[:, :, -1:] - a_cum)
    states = jnp.einsum("bclgn,bclgr,bclgrp->bcgrpn", Bc, decay_states, xdt)
    chunk_decay = jnp.exp(a_cum[:, :, -1])

    def step(carry, inp):
        st, dec = inp
        return carry * dec[..., None, None] + st, carry

    init = jnp.zeros((Bsz, G, R, P, N), f32)
    _, prev = lax.scan(step, init, (jnp.moveaxis(states, 1, 0), jnp.moveaxis(chunk_decay, 1, 0)))
    prev = jnp.moveaxis(prev, 0, 1)
    y_off = jnp.einsum("bclgn,bcgrpn,bclgr->bclgrp", Cc, prev, jnp.exp(a_cum))
    y = y_diag + y_off + xs_f * d_skip.astype(f32).reshape(G, R)[..., None]
    y = y.reshape(Bsz, L, SSD_D_INNER) * jax.nn.silu(z.astype(f32))
    y = rmsnorm(y, norm_g).astype(h.dtype)
    return y @ w_out


def sgu_mixer(h, w_in, v_norm_g, v_norm_b, w_spatial, b_spatial, w_out):
    Bsz, L, _ = h.shape
    nc = L // SG_CHUNK
    uv = jax.nn.gelu(h @ w_in)
    u, v = jnp.split(uv, 2, axis=-1)
    v = layernorm(v, v_norm_g, v_norm_b)
    vc = v.reshape(Bsz, nc, SG_CHUNK, SG_GROUPS, SG_WIDTH // SG_GROUPS)
    mask = jnp.tril(jnp.ones((SG_CHUNK, SG_CHUNK), dtype=w_spatial.dtype))
    ws = w_spatial * mask
    mixed = jnp.einsum("gts,bcsgd->bctgd", ws, vc) + b_spatial.T[None, None, :, :, None]
    return (u * mixed.reshape(Bsz, L, SG_WIDTH)) @ w_out


def stick_breaking_mixer(h, w_qkv, w_out):
    Bsz, L, _ = h.shape
    f32 = jnp.float32
    qkv = (h @ w_qkv).reshape(Bsz, L, 3, SB_HEADS, SB_HEAD_DIM)
    q, k, v = qkv[:, :, 0], qkv[:, :, 1], qkv[:, :, 2]
    scale = SB_HEAD_DIM ** -0.5
    outs = []
    for i in range(L // BLOCK):
        q0 = i * BLOCK
        kend = q0 + BLOCK
        qb, kb, vb = q[:, q0:kend], k[:, :kend], v[:, :kend]
        z = jnp.einsum("bthd,bshd->bhts", qb, kb).astype(f32) * scale
        t_idx = q0 + jnp.arange(BLOCK)
        s_idx = jnp.arange(kend)
        valid = s_idx[None, :] < t_idx[:, None]
        log_beta = jax.nn.log_sigmoid(z)
        log_1mb = jnp.where(valid, jax.nn.log_sigmoid(-z), 0.0)
        tail = lax.cumsum(log_1mb, axis=3, reverse=True) - log_1mb
        A = jnp.where(valid, jnp.exp(log_beta + tail), 0.0)
        outs.append(jnp.einsum("bhts,bshd->bthd", A.astype(vb.dtype), vb))
    o = jnp.concatenate(outs, axis=1).reshape(Bsz, L, D_MODEL)
    return o @ w_out


def memory_cross_attention(h, mem_n, wq, wkv, wo):
    Bsz, L, _ = h.shape
    M = mem_n.shape[1]
    q = (h @ wq).reshape(Bsz, L, XA_HEADS, XA_HEAD_DIM)
    kv = (mem_n @ wkv).reshape(Bsz, M, 2, XA_HEADS, XA_HEAD_DIM)
    k, v = kv[:, :, 0], kv[:, :, 1]
    s = jnp.einsum("bthd,bmhd->bhtm", q, k).astype(jnp.float32) * (XA_HEAD_DIM ** -0.5)
    p = jax.nn.softmax(s, axis=-1).astype(v.dtype)
    o = jnp.einsum("bhtm,bmhd->bthd", p, v).reshape(Bsz, L, XA_WIDTH)
    return o @ wo


def conv_gated_ffn(h, w_in, conv_w, conv_b, w_out):
    gu = causal_dwconv(h @ w_in, conv_w, conv_b)
    g, u = jnp.split(gu, 2, axis=-1)
    return (jax.nn.gelu(g, approximate=True) * u) @ w_out


def _fwd_setup_inputs(seed: int = 0) -> dict:
    key = jax.random.key(seed)
    keys = iter(jax.random.split(key, 64))
    f32 = jnp.float32

    def nrm(shape, scale):
        return jax.random.normal(next(keys), shape, f32) * scale

    def gain(shape):
        return 1.0 + nrm(shape, 0.05)

    D = D_MODEL
    d = {}
    d["x"] = nrm((BATCH, SEQ, D), 1.0)
    d["mem"] = nrm((BATCH, MEM_LEN, D), 1.0)
    for name in ["ln_mix_pre", "ln_mix_post", "ln_mem", "ln_xa_pre", "ln_xa_post", "ln_ffn_pre", "ln_ffn_post"]:
        d[name] = gain((DEPTH, D))
    d["xa_wq"] = nrm((DEPTH, D, XA_WIDTH), D ** -0.5)
    d["xa_wkv"] = nrm((DEPTH, D, 2 * XA_WIDTH), D ** -0.5)
    d["xa_wo"] = nrm((DEPTH, XA_WIDTH, D), XA_WIDTH ** -0.5)
    d["ffn_w_in"] = nrm((DEPTH, D, 2 * FFN_DIM), D ** -0.5)
    d["ffn_conv_w"] = nrm((DEPTH, FFN_CONV, 2 * FFN_DIM), FFN_CONV ** -0.5)
    d["ffn_conv_b"] = nrm((DEPTH, 2 * FFN_DIM), 0.02)
    d["ffn_w_out"] = nrm((DEPTH, FFN_DIM, D), FFN_DIM ** -0.5)
    d["ssd_w_in"] = nrm((N_SSD, D, SSD_IN_DIM), D ** -0.5)
    d["ssd_conv_w"] = nrm((N_SSD, SSD_CONV, SSD_CONV_DIM), SSD_CONV ** -0.5)
    d["ssd_conv_b"] = nrm((N_SSD, SSD_CONV_DIM), 0.02)
    dt0 = jnp.exp(jax.random.uniform(next(keys), (N_SSD, SSD_HEADS), f32,
                                     minval=math_log(1e-3), maxval=math_log(1e-1)))
    d["ssd_dt_bias"] = dt0 + jnp.log(-jnp.expm1(-dt0))
    d["ssd_a_log"] = jnp.log(jax.random.uniform(next(keys), (N_SSD, SSD_HEADS), f32, minval=1.0, maxval=16.0))
    d["ssd_d"] = gain((N_SSD, SSD_HEADS))
    d["ssd_norm"] = gain((N_SSD, SSD_D_INNER))
    d["ssd_w_out"] = nrm((N_SSD, SSD_D_INNER, D), SSD_D_INNER ** -0.5)
    d["sg_w_in"] = nrm((N_SG, D, 2 * SG_WIDTH), D ** -0.5)
    d["sg_v_norm_g"] = gain((N_SG, SG_WIDTH))
    d["sg_v_norm_b"] = nrm((N_SG, SG_WIDTH), 0.02)
    d["sg_w_spatial"] = nrm((N_SG, SG_GROUPS, SG_CHUNK, SG_CHUNK), 0.5 * SG_CHUNK ** -0.5)
    d["sg_b_spatial"] = 1.0 + nrm((N_SG, SG_GROUPS, SG_CHUNK), 0.1)
    d["sg_w_out"] = nrm((N_SG, SG_WIDTH, D), SG_WIDTH ** -0.5)
    d["sb_w_qkv"] = nrm((N_SB, D, 3 * D), D ** -0.5)
    d["sb_w_out"] = nrm((N_SB, D, D), D ** -0.5)
    return d


def math_log(v):
    return float(np.log(v))


def _fwd_reference(x, mem, ln_mix_pre, ln_mix_post, ln_mem, ln_xa_pre, ln_xa_post, ln_ffn_pre, ln_ffn_post,
              xa_wq, xa_wkv, xa_wo, ffn_w_in, ffn_conv_w, ffn_conv_b, ffn_w_out,
              ssd_w_in, ssd_conv_w, ssd_conv_b, ssd_dt_bias, ssd_a_log, ssd_d, ssd_norm, ssd_w_out,
              sg_w_in, sg_v_norm_g, sg_v_norm_b, sg_w_spatial, sg_b_spatial, sg_w_out,
              sb_w_qkv, sb_w_out):
    for i in range(DEPTH):
        kind = i % N_MIXERS
        j = i // N_MIXERS
        hn = rmsnorm(x, ln_mix_pre[i])
        if kind == 0:
            m = ssd_mixer(hn, ssd_w_in[j], ssd_conv_w[j], ssd_conv_b[j], ssd_dt_bias[j],
                          ssd_a_log[j], ssd_d[j], ssd_norm[j], ssd_w_out[j])
        elif kind == 1:
            m = sgu_mixer(hn, sg_w_in[j], sg_v_norm_g[j], sg_v_norm_b[j],
                          sg_w_spatial[j], sg_b_spatial[j], sg_w_out[j])
        else:
            m = stick_breaking_mixer(hn, sb_w_qkv[j], sb_w_out[j])
        x = x + rmsnorm(m, ln_mix_post[i])
        mem_n = rmsnorm(mem, ln_mem[i])
        c = memory_cross_attention(rmsnorm(x, ln_xa_pre[i]), mem_n, xa_wq[i], xa_wkv[i], xa_wo[i])
        x = x + rmsnorm(c, ln_xa_post[i])
        f = conv_gated_ffn(rmsnorm(x, ln_ffn_pre[i]), ffn_w_in[i], ffn_conv_w[i], ffn_conv_b[i], ffn_w_out[i])
        x = x + rmsnorm(f, ln_ffn_post[i])
    return x


import jax as _jax
import jax.numpy as _jnp

TWIN_FORMAT = 'train_step'
FWD_PARAMS = ['x', 'mem', 'ln_mix_pre', 'ln_mix_post', 'ln_mem', 'ln_xa_pre', 'ln_xa_post', 'ln_ffn_pre', 'ln_ffn_post', 'xa_wq', 'xa_wkv', 'xa_wo', 'ffn_w_in', 'ffn_conv_w', 'ffn_conv_b', 'ffn_w_out', 'ssd_w_in', 'ssd_conv_w', 'ssd_conv_b', 'ssd_dt_bias', 'ssd_a_log', 'ssd_d', 'ssd_norm', 'ssd_w_out', 'sg_w_in', 'sg_v_norm_g', 'sg_v_norm_b', 'sg_w_spatial', 'sg_b_spatial', 'sg_w_out', 'sb_w_qkv', 'sb_w_out']
TWIN_WEIGHTS = ['ln_mix_pre', 'ln_mix_post', 'ln_mem', 'ln_xa_pre', 'ln_xa_post', 'ln_ffn_pre', 'ln_ffn_post', 'xa_wq', 'xa_wkv', 'xa_wo', 'ffn_w_in', 'ffn_conv_w', 'ffn_conv_b', 'ffn_w_out', 'ssd_w_in', 'ssd_conv_w', 'ssd_conv_b', 'ssd_dt_bias', 'ssd_a_log', 'ssd_d', 'ssd_norm', 'ssd_w_out', 'sg_w_in', 'sg_v_norm_g', 'sg_v_norm_b', 'sg_w_spatial', 'sg_b_spatial', 'sg_w_out', 'sb_w_qkv', 'sb_w_out']
TWIN_DIFF_INPUT = 'x'
TWIN_INPUTS = ['x', 'mem', 'ln_mix_pre', 'ln_mix_post', 'ln_mem', 'ln_xa_pre', 'ln_xa_post', 'ln_ffn_pre', 'ln_ffn_post', 'xa_wq', 'xa_wkv', 'xa_wo', 'ffn_w_in', 'ffn_conv_w', 'ffn_conv_b', 'ffn_w_out', 'ssd_w_in', 'ssd_conv_w', 'ssd_conv_b', 'ssd_dt_bias', 'ssd_a_log', 'ssd_d', 'ssd_norm', 'ssd_w_out', 'sg_w_in', 'sg_v_norm_g', 'sg_v_norm_b', 'sg_w_spatial', 'sg_b_spatial', 'sg_w_out', 'sb_w_qkv', 'sb_w_out', 'loss_target', 'm_ln_mix_pre', 'm_ln_mix_post', 'm_ln_mem', 'm_ln_xa_pre', 'm_ln_xa_post', 'm_ln_ffn_pre', 'm_ln_ffn_post', 'm_xa_wq', 'm_xa_wkv', 'm_xa_wo', 'm_ffn_w_in', 'm_ffn_conv_w', 'm_ffn_conv_b', 'm_ffn_w_out', 'm_ssd_w_in', 'm_ssd_conv_w', 'm_ssd_conv_b', 'm_ssd_dt_bias', 'm_ssd_a_log', 'm_ssd_d', 'm_ssd_norm', 'm_ssd_w_out', 'm_sg_w_in', 'm_sg_v_norm_g', 'm_sg_v_norm_b', 'm_sg_w_spatial', 'm_sg_b_spatial', 'm_sg_w_out', 'm_sb_w_qkv', 'm_sb_w_out', 'v_ln_mix_pre', 'v_ln_mix_post', 'v_ln_mem', 'v_ln_xa_pre', 'v_ln_xa_post', 'v_ln_ffn_pre', 'v_ln_ffn_post', 'v_xa_wq', 'v_xa_wkv', 'v_xa_wo', 'v_ffn_w_in', 'v_ffn_conv_w', 'v_ffn_conv_b', 'v_ffn_w_out', 'v_ssd_w_in', 'v_ssd_conv_w', 'v_ssd_conv_b', 'v_ssd_dt_bias', 'v_ssd_a_log', 'v_ssd_d', 'v_ssd_norm', 'v_ssd_w_out', 'v_sg_w_in', 'v_sg_v_norm_g', 'v_sg_v_norm_b', 'v_sg_w_spatial', 'v_sg_b_spatial', 'v_sg_w_out', 'v_sb_w_qkv', 'v_sb_w_out']
TWIN_OUTPUTS = ['loss', 'grad_x', 'grad_ln_mix_pre', 'grad_ln_mix_post', 'grad_ln_mem', 'grad_ln_xa_pre', 'grad_ln_xa_post', 'grad_ln_ffn_pre', 'grad_ln_ffn_post', 'grad_xa_wq', 'grad_xa_wkv', 'grad_xa_wo', 'grad_ffn_w_in', 'grad_ffn_conv_w', 'grad_ffn_conv_b', 'grad_ffn_w_out', 'grad_ssd_w_in', 'grad_ssd_conv_w', 'grad_ssd_conv_b', 'grad_ssd_dt_bias', 'grad_ssd_a_log', 'grad_ssd_d', 'grad_ssd_norm', 'grad_ssd_w_out', 'grad_sg_w_in', 'grad_sg_v_norm_g', 'grad_sg_v_norm_b', 'grad_sg_w_spatial', 'grad_sg_b_spatial', 'grad_sg_w_out', 'grad_sb_w_qkv', 'grad_sb_w_out', 'delta_ln_mix_pre', 'delta_ln_mix_post', 'delta_ln_mem', 'delta_ln_xa_pre', 'delta_ln_xa_post', 'delta_ln_ffn_pre', 'delta_ln_ffn_post', 'delta_xa_wq', 'delta_xa_wkv', 'delta_xa_wo', 'delta_ffn_w_in', 'delta_ffn_conv_w', 'delta_ffn_conv_b', 'delta_ffn_w_out', 'delta_ssd_w_in', 'delta_ssd_conv_w', 'delta_ssd_conv_b', 'delta_ssd_dt_bias', 'delta_ssd_a_log', 'delta_ssd_d', 'delta_ssd_norm', 'delta_ssd_w_out', 'delta_sg_w_in', 'delta_sg_v_norm_g', 'delta_sg_v_norm_b', 'delta_sg_w_spatial', 'delta_sg_b_spatial', 'delta_sg_w_out', 'delta_sb_w_qkv', 'delta_sb_w_out', 'new_m_ln_mix_pre', 'new_m_ln_mix_post', 'new_m_ln_mem', 'new_m_ln_xa_pre', 'new_m_ln_xa_post', 'new_m_ln_ffn_pre', 'new_m_ln_ffn_post', 'new_m_xa_wq', 'new_m_xa_wkv', 'new_m_xa_wo', 'new_m_ffn_w_in', 'new_m_ffn_conv_w', 'new_m_ffn_conv_b', 'new_m_ffn_w_out', 'new_m_ssd_w_in', 'new_m_ssd_conv_w', 'new_m_ssd_conv_b', 'new_m_ssd_dt_bias', 'new_m_ssd_a_log', 'new_m_ssd_d', 'new_m_ssd_norm', 'new_m_ssd_w_out', 'new_m_sg_w_in', 'new_m_sg_v_norm_g', 'new_m_sg_v_norm_b', 'new_m_sg_w_spatial', 'new_m_sg_b_spatial', 'new_m_sg_w_out', 'new_m_sb_w_qkv', 'new_m_sb_w_out', 'new_v_ln_mix_pre', 'new_v_ln_mix_post', 'new_v_ln_mem', 'new_v_ln_xa_pre', 'new_v_ln_xa_post', 'new_v_ln_ffn_pre', 'new_v_ln_ffn_post', 'new_v_xa_wq', 'new_v_xa_wkv', 'new_v_xa_wo', 'new_v_ffn_w_in', 'new_v_ffn_conv_w', 'new_v_ffn_conv_b', 'new_v_ffn_w_out', 'new_v_ssd_w_in', 'new_v_ssd_conv_w', 'new_v_ssd_conv_b', 'new_v_ssd_dt_bias', 'new_v_ssd_a_log', 'new_v_ssd_d', 'new_v_ssd_norm', 'new_v_ssd_w_out', 'new_v_sg_w_in', 'new_v_sg_v_norm_g', 'new_v_sg_v_norm_b', 'new_v_sg_w_spatial', 'new_v_sg_b_spatial', 'new_v_sg_w_out', 'new_v_sb_w_qkv', 'new_v_sb_w_out']
TWIN_LEAF_KINDS = {'loss': 'loss', 'grad_x': 'grad_x', 'grad_ln_mix_pre': 'grad_w', 'grad_ln_mix_post': 'grad_w', 'grad_ln_mem': 'grad_w', 'grad_ln_xa_pre': 'grad_w', 'grad_ln_xa_post': 'grad_w', 'grad_ln_ffn_pre': 'grad_w', 'grad_ln_ffn_post': 'grad_w', 'grad_xa_wq': 'grad_w', 'grad_xa_wkv': 'grad_w', 'grad_xa_wo': 'grad_w', 'grad_ffn_w_in': 'grad_w', 'grad_ffn_conv_w': 'grad_w', 'grad_ffn_conv_b': 'grad_w', 'grad_ffn_w_out': 'grad_w', 'grad_ssd_w_in': 'grad_w', 'grad_ssd_conv_w': 'grad_w', 'grad_ssd_conv_b': 'grad_w', 'grad_ssd_dt_bias': 'grad_w', 'grad_ssd_a_log': 'grad_w', 'grad_ssd_d': 'grad_w', 'grad_ssd_norm': 'grad_w', 'grad_ssd_w_out': 'grad_w', 'grad_sg_w_in': 'grad_w', 'grad_sg_v_norm_g': 'grad_w', 'grad_sg_v_norm_b': 'grad_w', 'grad_sg_w_spatial': 'grad_w', 'grad_sg_b_spatial': 'grad_w', 'grad_sg_w_out': 'grad_w', 'grad_sb_w_qkv': 'grad_w', 'grad_sb_w_out': 'grad_w', 'delta_ln_mix_pre': 'delta_w', 'delta_ln_mix_post': 'delta_w', 'delta_ln_mem': 'delta_w', 'delta_ln_xa_pre': 'delta_w', 'delta_ln_xa_post': 'delta_w', 'delta_ln_ffn_pre': 'delta_w', 'delta_ln_ffn_post': 'delta_w', 'delta_xa_wq': 'delta_w', 'delta_xa_wkv': 'delta_w', 'delta_xa_wo': 'delta_w', 'delta_ffn_w_in': 'delta_w', 'delta_ffn_conv_w': 'delta_w', 'delta_ffn_conv_b': 'delta_w', 'delta_ffn_w_out': 'delta_w', 'delta_ssd_w_in': 'delta_w', 'delta_ssd_conv_w': 'delta_w', 'delta_ssd_conv_b': 'delta_w', 'delta_ssd_dt_bias': 'delta_w', 'delta_ssd_a_log': 'delta_w', 'delta_ssd_d': 'delta_w', 'delta_ssd_norm': 'delta_w', 'delta_ssd_w_out': 'delta_w', 'delta_sg_w_in': 'delta_w', 'delta_sg_v_norm_g': 'delta_w', 'delta_sg_v_norm_b': 'delta_w', 'delta_sg_w_spatial': 'delta_w', 'delta_sg_b_spatial': 'delta_w', 'delta_sg_w_out': 'delta_w', 'delta_sb_w_qkv': 'delta_w', 'delta_sb_w_out': 'delta_w', 'new_m_ln_mix_pre': 'new_m', 'new_m_ln_mix_post': 'new_m', 'new_m_ln_mem': 'new_m', 'new_m_ln_xa_pre': 'new_m', 'new_m_ln_xa_post': 'new_m', 'new_m_ln_ffn_pre': 'new_m', 'new_m_ln_ffn_post': 'new_m', 'new_m_xa_wq': 'new_m', 'new_m_xa_wkv': 'new_m', 'new_m_xa_wo': 'new_m', 'new_m_ffn_w_in': 'new_m', 'new_m_ffn_conv_w': 'new_m', 'new_m_ffn_conv_b': 'new_m', 'new_m_ffn_w_out': 'new_m', 'new_m_ssd_w_in': 'new_m', 'new_m_ssd_conv_w': 'new_m', 'new_m_ssd_conv_b': 'new_m', 'new_m_ssd_dt_bias': 'new_m', 'new_m_ssd_a_log': 'new_m', 'new_m_ssd_d': 'new_m', 'new_m_ssd_norm': 'new_m', 'new_m_ssd_w_out': 'new_m', 'new_m_sg_w_in': 'new_m', 'new_m_sg_v_norm_g': 'new_m', 'new_m_sg_v_norm_b': 'new_m', 'new_m_sg_w_spatial': 'new_m', 'new_m_sg_b_spatial': 'new_m', 'new_m_sg_w_out': 'new_m', 'new_m_sb_w_qkv': 'new_m', 'new_m_sb_w_out': 'new_m', 'new_v_ln_mix_pre': 'new_v', 'new_v_ln_mix_post': 'new_v', 'new_v_ln_mem': 'new_v', 'new_v_ln_xa_pre': 'new_v', 'new_v_ln_xa_post': 'new_v', 'new_v_ln_ffn_pre': 'new_v', 'new_v_ln_ffn_post': 'new_v', 'new_v_xa_wq': 'new_v', 'new_v_xa_wkv': 'new_v', 'new_v_xa_wo': 'new_v', 'new_v_ffn_w_in': 'new_v', 'new_v_ffn_conv_w': 'new_v', 'new_v_ffn_conv_b': 'new_v', 'new_v_ffn_w_out': 'new_v', 'new_v_ssd_w_in': 'new_v', 'new_v_ssd_conv_w': 'new_v', 'new_v_ssd_conv_b': 'new_v', 'new_v_ssd_dt_bias': 'new_v', 'new_v_ssd_a_log': 'new_v', 'new_v_ssd_d': 'new_v', 'new_v_ssd_norm': 'new_v', 'new_v_ssd_w_out': 'new_v', 'new_v_sg_w_in': 'new_v', 'new_v_sg_v_norm_g': 'new_v', 'new_v_sg_v_norm_b': 'new_v', 'new_v_sg_w_spatial': 'new_v', 'new_v_sg_b_spatial': 'new_v', 'new_v_sg_w_out': 'new_v', 'new_v_sb_w_qkv': 'new_v', 'new_v_sb_w_out': 'new_v'}


def _forward(args):
    return _fwd_reference(*[args[k] for k in FWD_PARAMS])


def _output_shape():
    out = _jax.eval_shape(lambda: _forward(_fwd_setup_inputs(0)))
    return out.shape, out.dtype

N_MICROBATCH = 1
ADAM_LR = 0.001
ADAM_B1 = 0.9
ADAM_B2 = 0.999
ADAM_EPS = 1e-08
ADAM_WD = 0.01
ADAM_STEP = 10
PER_EXAMPLE_BATCH_AXIS = {'x': 0, 'mem': 0, 'loss_target': 0}
SHARED_INPUTS = []
_WEIGHT_DTYPES = {'ln_mix_pre': _jnp.float32, 'ln_mix_post': _jnp.float32, 'ln_mem': _jnp.float32, 'ln_xa_pre': _jnp.float32, 'ln_xa_post': _jnp.float32, 'ln_ffn_pre': _jnp.float32, 'ln_ffn_post': _jnp.float32, 'xa_wq': _jnp.float32, 'xa_wkv': _jnp.float32, 'xa_wo': _jnp.float32, 'ffn_w_in': _jnp.float32, 'ffn_conv_w': _jnp.float32, 'ffn_conv_b': _jnp.float32, 'ffn_w_out': _jnp.float32, 'ssd_w_in': _jnp.float32, 'ssd_conv_w': _jnp.float32, 'ssd_conv_b': _jnp.float32, 'ssd_dt_bias': _jnp.float32, 'ssd_a_log': _jnp.float32, 'ssd_d': _jnp.float32, 'ssd_norm': _jnp.float32, 'ssd_w_out': _jnp.float32, 'sg_w_in': _jnp.float32, 'sg_v_norm_g': _jnp.float32, 'sg_v_norm_b': _jnp.float32, 'sg_w_spatial': _jnp.float32, 'sg_b_spatial': _jnp.float32, 'sg_w_out': _jnp.float32, 'sb_w_qkv': _jnp.float32, 'sb_w_out': _jnp.float32}
MOMENT_SCALE = {'ln_mix_pre': 3.902777e+00, 'ln_mix_post': 1.756148e+01, 'ln_mem': 9.972660e+00, 'ln_xa_pre': 2.664239e+00, 'ln_xa_post': 1.914427e+01, 'ln_ffn_pre': 3.492630e+00, 'ln_ffn_post': 1.630570e+01, 'xa_wq': 5.388421e+00, 'xa_wkv': 1.404652e+01, 'xa_wo': 9.385488e+00, 'ffn_w_in': 1.446616e+00, 'ffn_conv_w': 1.688811e+00, 'ffn_conv_b': 4.451741e+00, 'ffn_w_out': 3.105345e+00, 'ssd_w_in': 1.160224e+00, 'ssd_conv_w': 1.644679e+00, 'ssd_conv_b': 4.357662e+00, 'ssd_dt_bias': 2.024130e+00, 'ssd_a_log': 8.113524e+00, 'ssd_d': 9.469636e+00, 'ssd_norm': 2.592579e+00, 'ssd_w_out': 3.678570e+00, 'sg_w_in': 1.771630e+00, 'sg_v_norm_g': 1.546686e-01, 'sg_v_norm_b': 1.823020e-01, 'sg_w_spatial': 4.377301e-01, 'sg_b_spatial': 7.192892e-01, 'sg_w_out': 8.456487e+00, 'sb_w_qkv': 4.390622e+00, 'sb_w_out': 7.628012e+00}


def _to_microbatches(a, axis):
    t = _jnp.moveaxis(a, axis, 0)
    t = t.reshape((N_MICROBATCH, t.shape[0] // N_MICROBATCH) + t.shape[1:])
    return _jnp.moveaxis(t, 1, axis + 1)


def setup_inputs(seed: int = 0) -> dict:
    inp = _fwd_setup_inputs(seed)
    key = _jax.random.fold_in(_jax.random.key(seed), 7919)
    shape, _ = _output_shape()
    out = dict(inp)
    out["loss_target"] = _jax.random.normal(_jax.random.fold_in(key, 0), shape, _jnp.float32)
    for i, name in enumerate(TWIN_WEIGHTS):
        w = inp[name].astype(_jnp.float32)
        if MOMENT_SCALE is None:
            s = _jnp.sqrt(_jnp.mean(_jnp.square(w)) + 1e-30)
        else:
            s = MOMENT_SCALE[name]
        km, kv = _jax.random.split(_jax.random.fold_in(key, i + 1))
        out[name] = w
        out["m_" + name] = s * _jax.random.normal(km, w.shape, _jnp.float32)
        out["v_" + name] = (s * s) * _jax.random.uniform(kv, w.shape, _jnp.float32, 0.5, 1.5)
    if N_MICROBATCH > 1:
        for name, axis in PER_EXAMPLE_BATCH_AXIS.items():
            out[name] = _to_microbatches(out[name], axis)
    return {'x': out['x'], 'mem': out['mem'], 'ln_mix_pre': out['ln_mix_pre'], 'ln_mix_post': out['ln_mix_post'], 'ln_mem': out['ln_mem'], 'ln_xa_pre': out['ln_xa_pre'], 'ln_xa_post': out['ln_xa_post'], 'ln_ffn_pre': out['ln_ffn_pre'], 'ln_ffn_post': out['ln_ffn_post'], 'xa_wq': out['xa_wq'], 'xa_wkv': out['xa_wkv'], 'xa_wo': out['xa_wo'], 'ffn_w_in': out['ffn_w_in'], 'ffn_conv_w': out['ffn_conv_w'], 'ffn_conv_b': out['ffn_conv_b'], 'ffn_w_out': out['ffn_w_out'], 'ssd_w_in': out['ssd_w_in'], 'ssd_conv_w': out['ssd_conv_w'], 'ssd_conv_b': out['ssd_conv_b'], 'ssd_dt_bias': out['ssd_dt_bias'], 'ssd_a_log': out['ssd_a_log'], 'ssd_d': out['ssd_d'], 'ssd_norm': out['ssd_norm'], 'ssd_w_out': out['ssd_w_out'], 'sg_w_in': out['sg_w_in'], 'sg_v_norm_g': out['sg_v_norm_g'], 'sg_v_norm_b': out['sg_v_norm_b'], 'sg_w_spatial': out['sg_w_spatial'], 'sg_b_spatial': out['sg_b_spatial'], 'sg_w_out': out['sg_w_out'], 'sb_w_qkv': out['sb_w_qkv'], 'sb_w_out': out['sb_w_out'], 'loss_target': out['loss_target'], 'm_ln_mix_pre': out['m_ln_mix_pre'], 'm_ln_mix_post': out['m_ln_mix_post'], 'm_ln_mem': out['m_ln_mem'], 'm_ln_xa_pre': out['m_ln_xa_pre'], 'm_ln_xa_post': out['m_ln_xa_post'], 'm_ln_ffn_pre': out['m_ln_ffn_pre'], 'm_ln_ffn_post': out['m_ln_ffn_post'], 'm_xa_wq': out['m_xa_wq'], 'm_xa_wkv': out['m_xa_wkv'], 'm_xa_wo': out['m_xa_wo'], 'm_ffn_w_in': out['m_ffn_w_in'], 'm_ffn_conv_w': out['m_ffn_conv_w'], 'm_ffn_conv_b': out['m_ffn_conv_b'], 'm_ffn_w_out': out['m_ffn_w_out'], 'm_ssd_w_in': out['m_ssd_w_in'], 'm_ssd_conv_w': out['m_ssd_conv_w'], 'm_ssd_conv_b': out['m_ssd_conv_b'], 'm_ssd_dt_bias': out['m_ssd_dt_bias'], 'm_ssd_a_log': out['m_ssd_a_log'], 'm_ssd_d': out['m_ssd_d'], 'm_ssd_norm': out['m_ssd_norm'], 'm_ssd_w_out': out['m_ssd_w_out'], 'm_sg_w_in': out['m_sg_w_in'], 'm_sg_v_norm_g': out['m_sg_v_norm_g'], 'm_sg_v_norm_b': out['m_sg_v_norm_b'], 'm_sg_w_spatial': out['m_sg_w_spatial'], 'm_sg_b_spatial': out['m_sg_b_spatial'], 'm_sg_w_out': out['m_sg_w_out'], 'm_sb_w_qkv': out['m_sb_w_qkv'], 'm_sb_w_out': out['m_sb_w_out'], 'v_ln_mix_pre': out['v_ln_mix_pre'], 'v_ln_mix_post': out['v_ln_mix_post'], 'v_ln_mem': out['v_ln_mem'], 'v_ln_xa_pre': out['v_ln_xa_pre'], 'v_ln_xa_post': out['v_ln_xa_post'], 'v_ln_ffn_pre': out['v_ln_ffn_pre'], 'v_ln_ffn_post': out['v_ln_ffn_post'], 'v_xa_wq': out['v_xa_wq'], 'v_xa_wkv': out['v_xa_wkv'], 'v_xa_wo': out['v_xa_wo'], 'v_ffn_w_in': out['v_ffn_w_in'], 'v_ffn_conv_w': out['v_ffn_conv_w'], 'v_ffn_conv_b': out['v_ffn_conv_b'], 'v_ffn_w_out': out['v_ffn_w_out'], 'v_ssd_w_in': out['v_ssd_w_in'], 'v_ssd_conv_w': out['v_ssd_conv_w'], 'v_ssd_conv_b': out['v_ssd_conv_b'], 'v_ssd_dt_bias': out['v_ssd_dt_bias'], 'v_ssd_a_log': out['v_ssd_a_log'], 'v_ssd_d': out['v_ssd_d'], 'v_ssd_norm': out['v_ssd_norm'], 'v_ssd_w_out': out['v_ssd_w_out'], 'v_sg_w_in': out['v_sg_w_in'], 'v_sg_v_norm_g': out['v_sg_v_norm_g'], 'v_sg_v_norm_b': out['v_sg_v_norm_b'], 'v_sg_w_spatial': out['v_sg_w_spatial'], 'v_sg_b_spatial': out['v_sg_b_spatial'], 'v_sg_w_out': out['v_sg_w_out'], 'v_sb_w_qkv': out['v_sb_w_qkv'], 'v_sb_w_out': out['v_sb_w_out']}


def _loss(weights, diff, rest, loss_target):
    with _jax.named_scope("forward"):
        args = {**rest, TWIN_DIFF_INPUT: diff, **{k: w.astype(_WEIGHT_DTYPES[k]) for k, w in weights.items()}}
        y = _forward(args)
    with _jax.named_scope("loss_head"):
        err = _jnp.square(y.astype(_jnp.float32) - loss_target)
        return 0.5 * _jnp.sum(_jnp.mean(err, axis=-1)) if err.ndim else 0.5 * err


def _adamw(w, g, m, v):
    m = ADAM_B1 * m + (1.0 - ADAM_B1) * g
    v = ADAM_B2 * v + (1.0 - ADAM_B2) * _jnp.square(g)
    m_hat = m / (1.0 - ADAM_B1 ** ADAM_STEP)
    v_hat = v / (1.0 - ADAM_B2 ** ADAM_STEP)
    delta = -ADAM_LR * (m_hat / (_jnp.sqrt(v_hat) + ADAM_EPS) + ADAM_WD * w)
    return delta, m, v


def reference(x, mem, ln_mix_pre, ln_mix_post, ln_mem, ln_xa_pre, ln_xa_post, ln_ffn_pre, ln_ffn_post, xa_wq, xa_wkv, xa_wo, ffn_w_in, ffn_conv_w, ffn_conv_b, ffn_w_out, ssd_w_in, ssd_conv_w, ssd_conv_b, ssd_dt_bias, ssd_a_log, ssd_d, ssd_norm, ssd_w_out, sg_w_in, sg_v_norm_g, sg_v_norm_b, sg_w_spatial, sg_b_spatial, sg_w_out, sb_w_qkv, sb_w_out, loss_target, m_ln_mix_pre, m_ln_mix_post, m_ln_mem, m_ln_xa_pre, m_ln_xa_post, m_ln_ffn_pre, m_ln_ffn_post, m_xa_wq, m_xa_wkv, m_xa_wo, m_ffn_w_in, m_ffn_conv_w, m_ffn_conv_b, m_ffn_w_out, m_ssd_w_in, m_ssd_conv_w, m_ssd_conv_b, m_ssd_dt_bias, m_ssd_a_log, m_ssd_d, m_ssd_norm, m_ssd_w_out, m_sg_w_in, m_sg_v_norm_g, m_sg_v_norm_b, m_sg_w_spatial, m_sg_b_spatial, m_sg_w_out, m_sb_w_qkv, m_sb_w_out, v_ln_mix_pre, v_ln_mix_post, v_ln_mem, v_ln_xa_pre, v_ln_xa_post, v_ln_ffn_pre, v_ln_ffn_post, v_xa_wq, v_xa_wkv, v_xa_wo, v_ffn_w_in, v_ffn_conv_w, v_ffn_conv_b, v_ffn_w_out, v_ssd_w_in, v_ssd_conv_w, v_ssd_conv_b, v_ssd_dt_bias, v_ssd_a_log, v_ssd_d, v_ssd_norm, v_ssd_w_out, v_sg_w_in, v_sg_v_norm_g, v_sg_v_norm_b, v_sg_w_spatial, v_sg_b_spatial, v_sg_w_out, v_sb_w_qkv, v_sb_w_out):
    given = dict(x=x, mem=mem, ln_mix_pre=ln_mix_pre, ln_mix_post=ln_mix_post, ln_mem=ln_mem, ln_xa_pre=ln_xa_pre, ln_xa_post=ln_xa_post, ln_ffn_pre=ln_ffn_pre, ln_ffn_post=ln_ffn_post, xa_wq=xa_wq, xa_wkv=xa_wkv, xa_wo=xa_wo, ffn_w_in=ffn_w_in, ffn_conv_w=ffn_conv_w, ffn_conv_b=ffn_conv_b, ffn_w_out=ffn_w_out, ssd_w_in=ssd_w_in, ssd_conv_w=ssd_conv_w, ssd_conv_b=ssd_conv_b, ssd_dt_bias=ssd_dt_bias, ssd_a_log=ssd_a_log, ssd_d=ssd_d, ssd_norm=ssd_norm, ssd_w_out=ssd_w_out, sg_w_in=sg_w_in, sg_v_norm_g=sg_v_norm_g, sg_v_norm_b=sg_v_norm_b, sg_w_spatial=sg_w_spatial, sg_b_spatial=sg_b_spatial, sg_w_out=sg_w_out, sb_w_qkv=sb_w_qkv, sb_w_out=sb_w_out, loss_target=loss_target, m_ln_mix_pre=m_ln_mix_pre, m_ln_mix_post=m_ln_mix_post, m_ln_mem=m_ln_mem, m_ln_xa_pre=m_ln_xa_pre, m_ln_xa_post=m_ln_xa_post, m_ln_ffn_pre=m_ln_ffn_pre, m_ln_ffn_post=m_ln_ffn_post, m_xa_wq=m_xa_wq, m_xa_wkv=m_xa_wkv, m_xa_wo=m_xa_wo, m_ffn_w_in=m_ffn_w_in, m_ffn_conv_w=m_ffn_conv_w, m_ffn_conv_b=m_ffn_conv_b, m_ffn_w_out=m_ffn_w_out, m_ssd_w_in=m_ssd_w_in, m_ssd_conv_w=m_ssd_conv_w, m_ssd_conv_b=m_ssd_conv_b, m_ssd_dt_bias=m_ssd_dt_bias, m_ssd_a_log=m_ssd_a_log, m_ssd_d=m_ssd_d, m_ssd_norm=m_ssd_norm, m_ssd_w_out=m_ssd_w_out, m_sg_w_in=m_sg_w_in, m_sg_v_norm_g=m_sg_v_norm_g, m_sg_v_norm_b=m_sg_v_norm_b, m_sg_w_spatial=m_sg_w_spatial, m_sg_b_spatial=m_sg_b_spatial, m_sg_w_out=m_sg_w_out, m_sb_w_qkv=m_sb_w_qkv, m_sb_w_out=m_sb_w_out, v_ln_mix_pre=v_ln_mix_pre, v_ln_mix_post=v_ln_mix_post, v_ln_mem=v_ln_mem, v_ln_xa_pre=v_ln_xa_pre, v_ln_xa_post=v_ln_xa_post, v_ln_ffn_pre=v_ln_ffn_pre, v_ln_ffn_post=v_ln_ffn_post, v_xa_wq=v_xa_wq, v_xa_wkv=v_xa_wkv, v_xa_wo=v_xa_wo, v_ffn_w_in=v_ffn_w_in, v_ffn_conv_w=v_ffn_conv_w, v_ffn_conv_b=v_ffn_conv_b, v_ffn_w_out=v_ffn_w_out, v_ssd_w_in=v_ssd_w_in, v_ssd_conv_w=v_ssd_conv_w, v_ssd_conv_b=v_ssd_conv_b, v_ssd_dt_bias=v_ssd_dt_bias, v_ssd_a_log=v_ssd_a_log, v_ssd_d=v_ssd_d, v_ssd_norm=v_ssd_norm, v_ssd_w_out=v_ssd_w_out, v_sg_w_in=v_sg_w_in, v_sg_v_norm_g=v_sg_v_norm_g, v_sg_v_norm_b=v_sg_v_norm_b, v_sg_w_spatial=v_sg_w_spatial, v_sg_b_spatial=v_sg_b_spatial, v_sg_w_out=v_sg_w_out, v_sb_w_qkv=v_sb_w_qkv, v_sb_w_out=v_sb_w_out)
    weights = {n: given[n] for n in TWIN_WEIGHTS}
    shared = {n: given[n] for n in SHARED_INPUTS}
    per_example = {n: given[n] for n in ['x', 'mem']}
    grad_fn = _jax.value_and_grad(_loss, argnums=(0, 1))

    def one_microbatch(ex, loss_target):
        ex = dict(ex)
        diff = ex.pop(TWIN_DIFF_INPUT)
        return grad_fn(weights, diff, {**shared, **ex}, loss_target)

    if N_MICROBATCH == 1:
        loss, (grad_w, grad_x) = one_microbatch(per_example, given["loss_target"])
    else:
        def body(carry, xs):
            loss_sum, grad_sum = carry
            l_k, (gw_k, gx_k) = one_microbatch(xs[0], xs[1])
            with _jax.named_scope("update"):
                return (loss_sum + l_k, _jax.tree.map(_jnp.add, grad_sum, gw_k)), gx_k

        init = (_jnp.zeros((), _jnp.float32), _jax.tree.map(_jnp.zeros_like, weights))
        (loss, grad_w), grad_x = _jax.lax.scan(body, init, (per_example, given["loss_target"]))
    with _jax.named_scope("update"):
        delta_w, new_m, new_v = {}, {}, {}
        for n in TWIN_WEIGHTS:
            delta_w[n], new_m[n], new_v[n] = _adamw(weights[n], grad_w[n], given["m_" + n], given["v_" + n])
    return (loss, grad_x, *[grad_w[n] for n in TWIN_WEIGHTS], *[delta_w[n] for n in TWIN_WEIGHTS],
            *[new_m[n] for n in TWIN_WEIGHTS], *[new_v[n] for n in TWIN_WEIGHTS])
```

```python
import functools

import jax
import jax.numpy as jnp
import numpy as np
from jax import lax
from jax.experimental import pallas as pl
from jax.experimental.pallas import tpu as pltpu

f32 = jnp.float32
bf16 = jnp.bfloat16
HIGHEST = lax.Precision.HIGHEST
MESH = pl.DeviceIdType.MESH

V7X_VMEM_BYTES = 64 * 1024 * 1024
VMEM_LIMIT = V7X_VMEM_BYTES * 3 // 4
LANES = 128
SUBLANES = 8
BF16_ROWS = 16

EPS = 1e-6
ADAM_LR = 0.001
ADAM_B1 = 0.9
ADAM_B2 = 0.999
ADAM_EPS = 1e-08
ADAM_WD = 0.01
ADAM_STEP = 10

N_DEV = 8
XA_HEADS = 4
SSD_HEADS_PER_GROUP = 8
SSD_HEAD_DIM = 64
SSD_STATE = 128
SSD_GROUPS = 8
CHUNK = 128


def _cp(*sem):
    return pltpu.CompilerParams(dimension_semantics=sem or None, vmem_limit_bytes=VMEM_LIMIT)


def _dot(a, b, dims, precision=None):
    return lax.dot_general(a, b, (dims, ((), ())), precision=precision, preferred_element_type=f32)


def _nn(a, b, precision=None):
    return _dot(a, b, ((1,), (0,)), precision)


def _nt(a, b, precision=None):
    return _dot(a, b, ((1,), (1,)), precision)


def _tn(a, b, precision=None):
    return _dot(a, b, ((0,), (0,)), precision)


def _largest_tile(n, cap, step):
    for t in range(min(n, cap) // step * step, 0, -step):
        if n % t == 0:
            return t
    return n


def all_gather(x, name):
    def body(x_ref, out_ref, send_sems, recv_sems, local_sem):
        mx, my, mc = lax.axis_index("x"), lax.axis_index("y"), lax.axis_index("c")
        me, sibling = (mx, my, mc), (mx, my, 1 - mc)
        chips = [(1 - mx, my), (mx, 1 - my), (1 - mx, 1 - my)]

        def slot(px, py, pc):
            return out_ref.at[4 * px + 2 * py + pc]

        def copy(k, block, to, src=None):
            return pltpu.make_async_remote_copy(
                src_ref=slot(*block) if src is None else src, dst_ref=slot(*block),
                send_sem=send_sems.at[k], recv_sem=recv_sems.at[k], device_id=to, device_id_type=MESH)

        mine = pltpu.make_async_copy(x_ref, slot(*me), local_sem)
        mine.start()
        first = [copy(0, me, sibling, src=x_ref)]
        first += [copy(1 + j, me, (*chip, mc), src=x_ref) for j, chip in enumerate(chips)]
        for cp in first:
            cp.start()
        passed = [copy(4 + j, (*chip, mc), sibling) for j, chip in enumerate(chips)]
        for j, chip in enumerate(chips):
            copy(1 + j, (*chip, mc), me).wait_recv()
            passed[j].start()
        copy(0, sibling, me).wait_recv()
        for j, chip in enumerate(chips):
            copy(4 + j, (*chip, 1 - mc), me).wait_recv()
        for cp in first + passed:
            cp.wait_send()
        mine.wait()

    return pl.pallas_call(
        body, name=name,
        out_shape=jax.ShapeDtypeStruct((N_DEV,) + x.shape, x.dtype),
        in_specs=[pl.BlockSpec(memory_space=pl.ANY)],
        out_specs=pl.BlockSpec(memory_space=pl.ANY),
        scratch_shapes=[pltpu.SemaphoreType.DMA((7,)), pltpu.SemaphoreType.DMA((7,)), pltpu.SemaphoreType.DMA(())],
    )(x)


def rs_sibling_exchange(g8, name):
    def body(g_ref, land_ref, send_sems, recv_sems):
        mx, my, mc = lax.axis_index("x"), lax.axis_index("y"), lax.axis_index("c")
        sibling = (mx, my, 1 - mc)
        copies = [
            pltpu.make_async_remote_copy(
                src_ref=g_ref.at[2 * k + (1 - mc)], dst_ref=land_ref.at[k],
                send_sem=send_sems.at[k], recv_sem=recv_sems.at[k], device_id=sibling, device_id_type=MESH)
            for k in range(4)
        ]
        for cp in copies:
            cp.start()
        for cp in copies:
            cp.wait()

    return pl.pallas_call(
        body, name=name,
        out_shape=jax.ShapeDtypeStruct((4,) + g8.shape[1:], g8.dtype),
        in_specs=[pl.BlockSpec(memory_space=pl.ANY)],
        out_specs=pl.BlockSpec(memory_space=pl.ANY),
        scratch_shapes=[pltpu.SemaphoreType.DMA((4,)), pltpu.SemaphoreType.DMA((4,))],
    )(g8)


def rs_chip_exchange(s4, name):
    def body(s_ref, land_ref, send_sems, recv_sems):
        mx, my, mc = lax.axis_index("x"), lax.axis_index("y"), lax.axis_index("c")
        chips = [(1 - mx, my), (mx, 1 - my), (1 - mx, 1 - my)]
        copies = [
            pltpu.make_async_remote_copy(
                src_ref=s_ref.at[2 * cx + cy], dst_ref=land_ref.at[j],
                send_sem=send_sems.at[j], recv_sem=recv_sems.at[j], device_id=(cx, cy, mc), device_id_type=MESH)
            for j, (cx, cy) in enumerate(chips)
        ]
        for cp in copies:
            cp.start()
        for cp in copies:
            cp.wait()

    return pl.pallas_call(
        body, name=name,
        out_shape=jax.ShapeDtypeStruct((3,) + s4.shape[1:], s4.dtype),
        in_specs=[pl.BlockSpec(memory_space=pl.ANY)],
        out_specs=pl.BlockSpec(memory_space=pl.ANY),
        scratch_shapes=[pltpu.SemaphoreType.DMA((3,)), pltpu.SemaphoreType.DMA((3,))],
    )(s4)


def _as_lrc(a, lead):
    rest = a.shape[lead:]
    return a.reshape(a.shape[:lead] + (int(np.prod(rest[:-2], dtype=np.int64)),) + rest[-2:])


def rs_pair_add(g8, land4, name):
    shape = land4.shape
    g = _as_lrc(g8, 1)
    ld = _as_lrc(land4, 1)
    _, L, R, C = ld.shape
    tr = _largest_tile(R, 512, BF16_ROWS)
    mc = lax.axis_index("c").astype(jnp.int32).reshape(1)

    def body(mc_ref, g_ref, l_ref, o_ref):
        o_ref[...] = (g_ref[...].astype(f32) + l_ref[...].astype(f32)).astype(o_ref.dtype)

    out = pl.pallas_call(
        body, name=name,
        out_shape=jax.ShapeDtypeStruct(ld.shape, ld.dtype),
        grid_spec=pltpu.PrefetchScalarGridSpec(
            num_scalar_prefetch=1, grid=(4, L, R // tr),
            in_specs=[pl.BlockSpec((None, None, tr, C), lambda k, l, r, mc_ref: (2 * k + mc_ref[0], l, r, 0)),
                      pl.BlockSpec((None, None, tr, C), lambda k, l, r, mc_ref: (k, l, r, 0))],
            out_specs=pl.BlockSpec((None, None, tr, C), lambda k, l, r, mc_ref: (k, l, r, 0))),
        compiler_params=_cp("parallel", "parallel", "parallel"),
    )(mc, g, ld)
    return out.reshape(shape)


def reduce_scatter_parts(g8, name):
    land4 = rs_sibling_exchange(g8, name + "_d2d")
    s4 = rs_pair_add(g8, land4, name + "_add")
    land3 = rs_chip_exchange(s4, name + "_ici")
    chip = 2 * lax.axis_index("x") + lax.axis_index("y")
    mine = lax.dynamic_index_in_dim(s4, chip, axis=0, keepdims=True)
    return jnp.concatenate([mine, land3], axis=0)


def _adam_math(w, g, m, v):
    m = ADAM_B1 * m + (1.0 - ADAM_B1) * g
    v = ADAM_B2 * v + (1.0 - ADAM_B2) * jnp.square(g)
    m_hat = m / (1.0 - ADAM_B1 ** ADAM_STEP)
    v_hat = v / (1.0 - ADAM_B2 ** ADAM_STEP)
    delta = -ADAM_LR * (m_hat / (jnp.sqrt(v_hat) + ADAM_EPS) + ADAM_WD * w)
    return delta, m, v


def adamw_sharded(w, parts, m, v, name):
    shape = w.shape
    w3, m3, v3 = (_as_lrc(a, 0) for a in (w, m, v))
    p4 = _as_lrc(parts, 1)
    L, R, C = w3.shape
    tr = _largest_tile(R, 256, BF16_ROWS)

    def body(w_ref, p_ref, m_ref, v_ref, g_out, d_out, m_out, v_out):
        g = p_ref[0].astype(f32) + p_ref[1].astype(f32) + p_ref[2].astype(f32) + p_ref[3].astype(f32)
        delta, mn, vn = _adam_math(w_ref[...], g, m_ref[...], v_ref[...])
        g_out[...] = g
        d_out[...] = delta
        m_out[...] = mn
        v_out[...] = vn

    blk = pl.BlockSpec((None, tr, C), lambda l, r: (l, r, 0))
    outs = pl.pallas_call(
        body, name=name, out_shape=[jax.ShapeDtypeStruct(w3.shape, f32)] * 4, grid=(L, R // tr),
        in_specs=[blk, pl.BlockSpec((4, None, tr, C), lambda l, r: (0, l, r, 0)), blk, blk], out_specs=[blk] * 4,
        compiler_params=_cp("parallel", "parallel"),
    )(w3, p4, m3, v3)
    return tuple(o.reshape(shape) for o in outs)


def adamw_summed8(w, g8, m, v, name):
    R, C = w.shape
    tr = _largest_tile(R, 512, SUBLANES)

    def body(w_ref, g_ref, m_ref, v_ref, g_out, d_out, m_out, v_out):
        g = g_ref[0]
        for d in range(1, N_DEV):
            g = g + g_ref[d]
        delta, mn, vn = _adam_math(w_ref[...], g, m_ref[...], v_ref[...])
        g_out[...] = g
        d_out[...] = delta
        m_out[...] = mn
        v_out[...] = vn

    blk = pl.BlockSpec((tr, C), lambda r: (r, 0))
    return pl.pallas_call(
        body, name=name, out_shape=[jax.ShapeDtypeStruct((R, C), f32)] * 4, grid=(R // tr,),
        in_specs=[blk, pl.BlockSpec((N_DEV, tr, C), lambda r: (0, r, 0)), blk, blk], out_specs=[blk] * 4,
        compiler_params=_cp("parallel"),
    )(w, g8, m, v)


def _tile_n(ns):
    if ns % 512 == 0:
        return 512
    if ns <= 1536:
        return ns
    return _largest_tile(ns, 512, LANES)


def _tile_m(m):
    return _largest_tile(m, 1024, BF16_ROWS)


def mm_nn(a, w3, out_dtype, name, j0=0, nj=None):
    M, K = a.shape
    J, _, Ns = w3.shape
    nj = J - j0 if nj is None else nj
    tm, tn, tk = _tile_m(M), _tile_n(Ns), _largest_tile(K, 512, LANES)
    nn, nk = Ns // tn, K // tk

    def body(a_ref, w_ref, o_ref, acc_ref):
        k = pl.program_id(3)

        @pl.when(k == 0)
        def _():
            acc_ref[...] = jnp.zeros_like(acc_ref)

        acc_ref[...] += _nn(a_ref[...], w_ref[...])

        @pl.when(k == nk - 1)
        def _():
            o_ref[...] = acc_ref[...].astype(o_ref.dtype)

    return pl.pallas_call(
        body, name=name, out_shape=jax.ShapeDtypeStruct((M, nj * Ns), out_dtype),
        grid=(M // tm, nj, nn, nk),
        in_specs=[pl.BlockSpec((tm, tk), lambda i, j, n, k: (i, k)),
                  pl.BlockSpec((None, tk, tn), lambda i, j, n, k: (j0 + j, k, n))],
        out_specs=pl.BlockSpec((tm, tn), lambda i, j, n, k: (i, j * nn + n)),
        scratch_shapes=[pltpu.VMEM((tm, tn), f32)],
        compiler_params=_cp("parallel", "parallel", "parallel", "arbitrary"),
    )(a, w3)


def mm_nt(g, w3, out_dtype, name, j0=0, nj=None, init=None):
    M = g.shape[0]
    J, K, Ns = w3.shape
    nj = J - j0 if nj is None else nj
    tm, tn, tk = _tile_m(M), _tile_n(Ns), _largest_tile(K, 512, LANES)
    nn = Ns // tn

    def body(*refs):
        if init is None:
            g_ref, w_ref, o_ref, acc_ref = refs
        else:
            g_ref, w_ref, i_ref, o_ref, acc_ref = refs
        j, n = pl.program_id(2), pl.program_id(3)

        @pl.when((j == 0) & (n == 0))
        def _():
            acc_ref[...] = jnp.zeros_like(acc_ref) if init is None else i_ref[...].astype(f32)

        acc_ref[...] += _nt(g_ref[...], w_ref[...])

        @pl.when((j == nj - 1) & (n == nn - 1))
        def _():
            o_ref[...] = acc_ref[...].astype(o_ref.dtype)

    in_specs = [pl.BlockSpec((tm, tn), lambda i, k, j, n: (i, j * nn + n)),
                pl.BlockSpec((None, tk, tn), lambda i, k, j, n: (j0 + j, k, n))]
    args = [g, w3]
    if init is not None:
        in_specs.append(pl.BlockSpec((tm, tk), lambda i, k, j, n: (i, k)))
        args.append(init)
    return pl.pallas_call(
        body, name=name, out_shape=jax.ShapeDtypeStruct((M, K), out_dtype),
        grid=(M // tm, K // tk, nj, nn), in_specs=in_specs,
        out_specs=pl.BlockSpec((tm, tk), lambda i, k, j, n: (i, k)),
        scratch_shapes=[pltpu.VMEM((tm, tk), f32)],
        compiler_params=_cp("parallel", "parallel", "arbitrary", "arbitrary"),
    )(*args)


def mm_tn(a, g, ns, out_dtype, name):
    M, K = a.shape
    J = g.shape[1] // ns
    tm, tn, tk = _largest_tile(M, 512, BF16_ROWS), _tile_n(ns), _largest_tile(K, 512, LANES)
    nn, nm = ns // tn, M // tm

    def body(a_ref, g_ref, o_ref, acc_ref):
        m = pl.program_id(3)

        @pl.when(m == 0)
        def _():
            acc_ref[...] = jnp.zeros_like(acc_ref)

        acc_ref[...] += _tn(a_ref[...], g_ref[...])

        @pl.when(m == nm - 1)
        def _():
            o_ref[...] = acc_ref[...].astype(o_ref.dtype)

    return pl.pallas_call(
        body, name=name, out_shape=jax.ShapeDtypeStruct((J, K, ns), out_dtype),
        grid=(J, K // tk, nn, nm),
        in_specs=[pl.BlockSpec((tm, tk), lambda j, k, n, m: (m, k)),
                  pl.BlockSpec((tm, tn), lambda j, k, n, m: (m, j * nn + n))],
        out_specs=pl.BlockSpec((None, tk, tn), lambda j, k, n, m: (j, k, n)),
        scratch_shapes=[pltpu.VMEM((tk, tn), f32)],
        compiler_params=_cp("parallel", "parallel", "parallel", "arbitrary"),
    )(a, g)


ROW_TILE = 256
ROW_STEP = 16


def row_fwd(fn, rows, consts, outs, name, n_acc=0):
    T = rows[0].shape[0]
    tr = min(T, ROW_TILE)
    n_rows, n_consts, n_row_out = len(rows), len(consts), len(outs) - n_acc

    def body(*refs):
        r_refs, c_refs, o_refs = refs[:n_rows], refs[n_rows:n_rows + n_consts], refs[n_rows + n_consts:]
        cs = [c[...] for c in c_refs]
        acc_refs = o_refs[n_row_out:]
        if n_acc:
            @pl.when(pl.program_id(0) == 0)
            def _():
                for a in acc_refs:
                    a[...] = jnp.zeros_like(a)

        def step(s, carry):
            rows_s = pl.ds(pl.multiple_of(s * ROW_STEP, ROW_STEP), ROW_STEP)
            res = fn(*[r[rows_s, :].astype(f32) for r in r_refs], *cs)
            for o, val in zip(o_refs[:n_row_out], res[:n_row_out]):
                o[rows_s, :] = val.astype(o.dtype)
            return tuple(c + val for c, val in zip(carry, res[n_row_out:]))

        accs = lax.fori_loop(0, tr // ROW_STEP, step, tuple(jnp.zeros((1, c), f32) for c, _ in outs[n_row_out:]))
        for a, val in zip(acc_refs, accs):
            a[...] += val

    in_specs = [pl.BlockSpec((tr, r.shape[1]), lambda i: (i, 0)) for r in rows]
    in_specs += [pl.BlockSpec(c.shape, lambda i: (0, 0)) for c in consts]
    out_shape = [jax.ShapeDtypeStruct((T, c), dt) for c, dt in outs[:n_row_out]]
    out_shape += [jax.ShapeDtypeStruct((1, c), f32) for c, _ in outs[n_row_out:]]
    out_specs = [pl.BlockSpec((tr, c), lambda i: (i, 0)) for c, _ in outs[:n_row_out]]
    out_specs += [pl.BlockSpec((1, c), lambda i: (0, 0)) for c, _ in outs[n_row_out:]]
    return pl.pallas_call(
        body, name=name, out_shape=out_shape, grid=(T // tr,), in_specs=in_specs, out_specs=out_specs,
        compiler_params=_cp("arbitrary" if n_acc else "parallel"),
    )(*rows, *consts)


def row_bwd(fn, rows, consts, cts, grad_dtypes, name):
    T = rows[0].shape[0]
    tr = min(T, ROW_TILE)
    n_rows, n_consts, n_cts = len(rows), len(consts), len(cts)
    wanted = [i for i, dt in enumerate(grad_dtypes) if dt is not None]

    def body(*refs):
        r_refs = refs[:n_rows]
        c_refs = refs[n_rows:n_rows + n_consts]
        t_refs = refs[n_rows + n_consts:n_rows + n_consts + n_cts]
        o_refs = refs[n_rows + n_consts + n_cts:]
        gr_refs, gc_refs = o_refs[:len(wanted)], o_refs[len(wanted):]
        cs = [c[...] for c in c_refs]

        @pl.when(pl.program_id(0) == 0)
        def _():
            for a in gc_refs:
                a[...] = jnp.zeros_like(a)

        def step(s, carry):
            rows_s = pl.ds(pl.multiple_of(s * ROW_STEP, ROW_STEP), ROW_STEP)
            ins = [r[rows_s, :].astype(f32) for r in r_refs]
            _, vjp = jax.vjp(lambda *a: tuple(fn(*a)), *ins, *cs)
            grads = vjp(tuple(t[rows_s, :].astype(f32) for t in t_refs))
            for o, i in zip(gr_refs, wanted):
                o[rows_s, :] = grads[i].astype(o.dtype)
            return tuple(c + gval for c, gval in zip(carry, grads[n_rows:]))

        accs = lax.fori_loop(0, tr // ROW_STEP, step, tuple(jnp.zeros(c.shape, f32) for c in consts))
        for a, val in zip(gc_refs, accs):
            a[...] += val

    in_specs = [pl.BlockSpec((tr, r.shape[1]), lambda i: (i, 0)) for r in list(rows) + list(cts)]
    in_specs[n_rows:n_rows] = [pl.BlockSpec(c.shape, lambda i: (0, 0)) for c in consts]
    out_shape = [jax.ShapeDtypeStruct(rows[i].shape, grad_dtypes[i]) for i in wanted]
    out_shape += [jax.ShapeDtypeStruct(c.shape, f32) for c in consts]
    out_specs = [pl.BlockSpec((tr, rows[i].shape[1]), lambda i_: (i_, 0)) for i in wanted]
    out_specs += [pl.BlockSpec(c.shape, lambda i: (0, 0)) for c in consts]
    return pl.pallas_call(
        body, name=name, out_shape=out_shape, grid=(T // tr,), in_specs=in_specs, out_specs=out_specs,
        compiler_params=_cp("arbitrary"),
    )(*rows, *consts, *cts)


def _rms(x, g):
    return x * lax.rsqrt(jnp.mean(x * x, axis=-1, keepdims=True) + EPS) * g


def fn_prenorm(x, g):
    return (_rms(x, g),)


def fn_resnorm(x, m, g_post, g_pre):
    x_new = x + _rms(m, g_post)
    return x_new, _rms(x_new, g_pre)


def fn_res(x, m, g_post):
    return (x + _rms(m, g_post),)


def fn_final(x, m, target, g_post):
    err = x + _rms(m, g_post) - target
    n = err.shape[-1]
    return err / n, (0.5 / n) * jnp.sum(jnp.sum(err * err, axis=1, keepdims=True), axis=0, keepdims=True)


def fn_gate_norm(y, z, g):
    return (_rms(y * jax.nn.silu(z), g),)


CONV_ROWS = 256


def _conv_chunk(pre_ref, w, b, r, rb, K):
    t0 = pl.multiple_of(r * rb, rb)
    halo_at = pl.multiple_of(jnp.maximum(t0 - SUBLANES, 0), SUBLANES)
    halo = jnp.where(r > 0, pre_ref[pl.ds(halo_at, SUBLANES), :], 0.0)
    main = pre_ref[pl.ds(t0, rb), :]
    ext = jnp.concatenate([halo, main], axis=0)
    shifted = [main if k == K - 1 else pltpu.roll(ext, K - 1 - k, 0)[SUBLANES:] for k in range(K)]
    conv = b
    for k in range(K):
        conv = conv + w[k:k + 1] * shifted[k]
    return conv, shifted


def conv_fwd(streams, epilogue, out_cols, out_dtype, K, name):
    T = streams[0][0].shape[0]
    rb = min(T, CONV_ROWS)
    S = len(streams)

    def body(*refs):
        pre_refs, w_refs, b_refs, o_ref = refs[:S], refs[S:2 * S], refs[2 * S:3 * S], refs[3 * S]
        ws = [w[...] for w in w_refs]
        bs = [b[...] for b in b_refs]

        def step(r, carry):
            convs = [_conv_chunk(pre_refs[s], ws[s], bs[s], r, rb, K)[0] for s in range(S)]
            o_ref[pl.ds(pl.multiple_of(r * rb, rb), rb), :] = epilogue(*convs).astype(o_ref.dtype)
            return carry

        lax.fori_loop(0, T // rb, step, 0)

    in_specs = [pl.BlockSpec((T, LANES), functools.partial(lambda off, i: (0, off + i), st[1])) for st in streams]
    in_specs += [pl.BlockSpec((K, LANES), functools.partial(lambda off, i: (0, off + i), st[4])) for st in streams]
    in_specs += [pl.BlockSpec((1, LANES), functools.partial(lambda off, i: (0, off + i), st[4])) for st in streams]
    return pl.pallas_call(
        body, name=name, out_shape=jax.ShapeDtypeStruct((T, out_cols), out_dtype), grid=(out_cols // LANES,),
        in_specs=in_specs, out_specs=pl.BlockSpec((T, LANES), lambda i: (0, i)),
        compiler_params=_cp("parallel"),
    )(*[st[0] for st in streams], *[st[2] for st in streams], *[st[3] for st in streams])


def conv_bwd(streams, epilogue, dout, K, name):
    T, cols = dout.shape
    rb = min(T, CONV_ROWS)
    S = len(streams)

    def body(*refs):
        pre_refs, w_refs, b_refs, dout_ref = refs[:S], refs[S:2 * S], refs[2 * S:3 * S], refs[3 * S]
        o = refs[3 * S + 1:]
        dpre_refs, dw_refs, db_refs, scr = o[:S], o[S:2 * S], o[2 * S:3 * S], o[3 * S:]
        ws = [w[...] for w in w_refs]
        bs = [b[...] for b in b_refs]
        for s in range(S):
            scr[s][pl.ds(T, SUBLANES), :] = jnp.zeros((SUBLANES, LANES), f32)

        def phase1(r, carry):
            rows = pl.ds(pl.multiple_of(r * rb, rb), rb)
            convs, shifted = zip(*[_conv_chunk(pre_refs[s], ws[s], bs[s], r, rb, K) for s in range(S)])
            _, vjp = jax.vjp(epilogue, *convs)
            dconvs = vjp(dout_ref[rows, :].astype(f32))
            new = []
            for s in range(S):
                scr[s][rows, :] = dconvs[s]
                sums = [jnp.sum(dconvs[s] * shifted[s][k], axis=0, keepdims=True) for k in range(K)]
                sums.append(jnp.sum(dconvs[s], axis=0, keepdims=True))
                new.append(tuple(c + v for c, v in zip(carry[s], sums)))
            return tuple(new)

        zero = tuple(tuple(jnp.zeros((1, LANES), f32) for _ in range(K + 1)) for _ in range(S))
        sums = lax.fori_loop(0, T // rb, phase1, zero)
        tap = lax.broadcasted_iota(jnp.int32, (K, LANES), 0)
        for s in range(S):
            dw = jnp.zeros((K, LANES), f32)
            for k in range(K):
                dw = jnp.where(tap == k, sums[s][k], dw)
            dw_refs[s][...] = dw
            db_refs[s][...] = sums[s][K]

        def phase2(r, carry):
            t0 = pl.multiple_of(r * rb, rb)
            for s in range(S):
                ext = scr[s][pl.ds(t0, rb + SUBLANES), :]
                dpre = ws[s][K - 1:K] * ext[:rb]
                for k in range(K - 1):
                    j = K - 1 - k
                    dpre = dpre + ws[s][k:k + 1] * pltpu.roll(ext, rb + SUBLANES - j, 0)[:rb]
                dpre_refs[s][pl.ds(t0, rb), :] = dpre.astype(dpre_refs[s].dtype)
            return carry

        lax.fori_loop(0, T // rb, phase2, 0)

    in_specs = [pl.BlockSpec((T, LANES), functools.partial(lambda off, i: (0, off + i), st[1])) for st in streams]
    in_specs += [pl.BlockSpec((K, LANES), functools.partial(lambda off, i: (0, off + i), st[4])) for st in streams]
    in_specs += [pl.BlockSpec((1, LANES), functools.partial(lambda off, i: (0, off + i), st[4])) for st in streams]
    in_specs += [pl.BlockSpec((T, LANES), lambda i: (0, i))]
    out_shape = [jax.ShapeDtypeStruct((T, cols), bf16)] * S
    out_shape += [jax.ShapeDtypeStruct((K, cols), f32)] * S + [jax.ShapeDtypeStruct((1, cols), f32)] * S
    out_specs = [pl.BlockSpec((T, LANES), lambda i: (0, i))] * S
    out_specs += [pl.BlockSpec((K, LANES), lambda i: (0, i))] * S + [pl.BlockSpec((1, LANES), lambda i: (0, i))] * S
    res = pl.pallas_call(
        body, name=name, out_shape=out_shape, grid=(cols // LANES,), in_specs=in_specs, out_specs=out_specs,
        scratch_shapes=[pltpu.VMEM((T + SUBLANES, LANES), f32)] * S,
        compiler_params=_cp("parallel"),
    )(*[st[0] for st in streams], *[st[2] for st in streams], *[st[3] for st in streams], dout)
    return res[:S], res[S:2 * S], res[2 * S:]


def epi_gelu_gate(cg, cu):
    return jax.nn.gelu(cg, approximate=True) * cu


def epi_silu(c):
    return jax.nn.silu(c)


XA_ROWS = 256


def _xa_fn(q, k, v):
    s = _nt(q.astype(bf16), k.astype(bf16)) * (q.shape[-1] ** -0.5)
    p = jax.nn.softmax(s, axis=-1)
    return _nn(p.astype(bf16), v.astype(bf16))


def xa_fwd(q, kv, name):
    T, W = q.shape
    M = kv.shape[0]
    H = W // LANES
    tr = min(T, XA_ROWS)

    def body(q_ref, k_ref, v_ref, o_ref):
        o_ref[...] = _xa_fn(q_ref[...].astype(f32), k_ref[...].astype(f32), v_ref[...].astype(f32)).astype(o_ref.dtype)

    return pl.pallas_call(
        body, name=name, out_shape=jax.ShapeDtypeStruct((T, W), bf16), grid=(T // tr, H),
        in_specs=[pl.BlockSpec((tr, LANES), lambda i, h: (i, h)),
                  pl.BlockSpec((M, LANES), lambda i, h: (0, h)),
                  pl.BlockSpec((M, LANES), lambda i, h: (0, H + h))],
        out_specs=pl.BlockSpec((tr, LANES), lambda i, h: (i, h)),
        compiler_params=_cp("parallel", "parallel"),
    )(q, kv, kv)


def xa_bwd(q, kv, do, name):
    T, W = q.shape
    M = kv.shape[0]
    H = W // LANES
    tr = min(T, XA_ROWS)

    def body(q_ref, k_ref, v_ref, do_ref, dq_ref, dk_ref, dv_ref):
        @pl.when(pl.program_id(1) == 0)
        def _():
            dk_ref[...] = jnp.zeros_like(dk_ref)
            dv_ref[...] = jnp.zeros_like(dv_ref)

        _, vjp = jax.vjp(_xa_fn, q_ref[...].astype(f32), k_ref[...].astype(f32), v_ref[...].astype(f32))
        dq, dk, dv = vjp(do_ref[...].astype(f32))
        dq_ref[...] = dq.astype(dq_ref.dtype)
        dk_ref[...] += dk
        dv_ref[...] += dv

    return pl.pallas_call(
        body, name=name,
        out_shape=[jax.ShapeDtypeStruct((T, W), bf16), jax.ShapeDtypeStruct((M, W), f32), jax.ShapeDtypeStruct((M, W), f32)],
        grid=(H, T // tr),
        in_specs=[pl.BlockSpec((tr, LANES), lambda h, i: (i, h)),
                  pl.BlockSpec((M, LANES), lambda h, i: (0, h)),
                  pl.BlockSpec((M, LANES), lambda h, i: (0, H + h)),
                  pl.BlockSpec((tr, LANES), lambda h, i: (i, h))],
        out_specs=[pl.BlockSpec((tr, LANES), lambda h, i: (i, h)),
                   pl.BlockSpec((M, LANES), lambda h, i: (0, h)),
                   pl.BlockSpec((M, LANES), lambda h, i: (0, h))],
        compiler_params=_cp("parallel", "arbitrary"),
    )(q, kv, kv, do)


def _sgu_norm_fn(v_pre, g, b):
    v = jax.nn.gelu(v_pre, approximate=True)
    mu = jnp.mean(v, axis=-1, keepdims=True)
    vc = v - mu
    return vc * lax.rsqrt(jnp.mean(vc * vc, axis=-1, keepdims=True) + EPS) * g + b


def _sgu_mix_fn(u_pre, vn, w, b):
    q = w.shape[0]
    tril = lax.broadcasted_iota(jnp.int32, (q, q), 0) >= lax.broadcasted_iota(jnp.int32, (q, q), 1)
    mixed = _nn(jnp.where(tril, w, 0.0).astype(bf16), vn.astype(bf16)) + b
    return jax.nn.gelu(u_pre, approximate=True) * mixed


def _sgu_norm_phase(v_ref, g, b, vn_ref):
    def step(s, carry):
        rows = pl.ds(pl.multiple_of(s * ROW_STEP, ROW_STEP), ROW_STEP)
        vn_ref[rows, :] = _sgu_norm_fn(v_ref[rows, :], g, b)
        return carry

    lax.fori_loop(0, CHUNK // ROW_STEP, step, 0)


def sgu_fwd(uv_pre, norm_g, norm_b, w_sp, b_sp, name):
    T, W2 = uv_pre.shape
    W = W2 // 2
    G = w_sp.shape[0]
    gw = W // G

    def body(u_ref, v_ref, g_ref, b_ref, ws_ref, bs_ref, o_ref, vn_ref):
        _sgu_norm_phase(v_ref, g_ref[...], b_ref[...], vn_ref)

        def group(gi, carry):
            cols = pl.ds(pl.multiple_of(gi * gw, LANES), gw)
            o_ref[:, cols] = _sgu_mix_fn(u_ref[:, cols], vn_ref[:, cols], ws_ref[gi], bs_ref[gi]).astype(o_ref.dtype)
            return carry

        lax.fori_loop(0, G, group, 0)

    full = lambda a: pl.BlockSpec(a.shape, lambda c: (0,) * a.ndim)
    return pl.pallas_call(
        body, name=name, out_shape=jax.ShapeDtypeStruct((T, W), bf16), grid=(T // CHUNK,),
        in_specs=[pl.BlockSpec((CHUNK, W), lambda c: (c, 0)), pl.BlockSpec((CHUNK, W), lambda c: (c, 1)),
                  full(norm_g), full(norm_b), full(w_sp), full(b_sp)],
        out_specs=pl.BlockSpec((CHUNK, W), lambda c: (c, 0)),
        scratch_shapes=[pltpu.VMEM((CHUNK, W), f32)],
        compiler_params=_cp("parallel"),
    )(uv_pre, uv_pre, norm_g, norm_b, w_sp, b_sp)


def sgu_bwd(uv_pre, norm_g, norm_b, w_sp, b_sp, dout, name):
    T, W2 = uv_pre.shape
    W = W2 // 2
    G = w_sp.shape[0]
    gw = W // G

    def body(u_ref, v_ref, g_ref, b_ref, ws_ref, bs_ref, do_ref, duv_ref, dg_ref, db_ref, dws_ref, dbs_ref,
             vn_ref, dvn_ref):
        @pl.when(pl.program_id(0) == 0)
        def _():
            for a in (dg_ref, db_ref, dws_ref, dbs_ref):
                a[...] = jnp.zeros_like(a)

        g, b = g_ref[...], b_ref[...]
        _sgu_norm_phase(v_ref, g, b, vn_ref)

        def group(gi, carry):
            cols = pl.ds(pl.multiple_of(gi * gw, LANES), gw)
            _, vjp = jax.vjp(_sgu_mix_fn, u_ref[:, cols], vn_ref[:, cols], ws_ref[gi], bs_ref[gi])
            du, dvn, dw, dbias = vjp(do_ref[:, cols])
            duv_ref[:, cols] = du.astype(duv_ref.dtype)
            dvn_ref[:, cols] = dvn
            dws_ref[gi] += dw
            dbs_ref[gi] += dbias
            return carry

        lax.fori_loop(0, G, group, 0)

        def step(s, carry):
            rows = pl.ds(pl.multiple_of(s * ROW_STEP, ROW_STEP), ROW_STEP)
            _, vjp = jax.vjp(_sgu_norm_fn, v_ref[rows, :], g, b)
            dv, dg, dbn = vjp(dvn_ref[rows, :])
            duv_ref[rows, pl.ds(W, W)] = dv.astype(duv_ref.dtype)
            return carry[0] + dg, carry[1] + dbn

        dg, dbn = lax.fori_loop(0, CHUNK // ROW_STEP, step, (jnp.zeros((1, W), f32), jnp.zeros((1, W), f32)))
        dg_ref[...] += dg
        db_ref[...] += dbn

    full = lambda a: pl.BlockSpec(a.shape, lambda c: (0,) * a.ndim)
    return pl.pallas_call(
        body, name=name,
        out_shape=[jax.ShapeDtypeStruct((T, W2), bf16), jax.ShapeDtypeStruct((1, W), f32), jax.ShapeDtypeStruct((1, W), f32),
                   jax.ShapeDtypeStruct(w_sp.shape, f32), jax.ShapeDtypeStruct(b_sp.shape, f32)],
        grid=(T // CHUNK,),
        in_specs=[pl.BlockSpec((CHUNK, W), lambda c: (c, 0)), pl.BlockSpec((CHUNK, W), lambda c: (c, 1)),
                  full(norm_g), full(norm_b), full(w_sp), full(b_sp), pl.BlockSpec((CHUNK, W), lambda c: (c, 0))],
        out_specs=[pl.BlockSpec((CHUNK, W2), lambda c: (c, 0)), full(norm_g), full(norm_b), full(w_sp), full(b_sp)],
        scratch_shapes=[pltpu.VMEM((CHUNK, W), f32), pltpu.VMEM((CHUNK, W), f32)],
        compiler_params=_cp("arbitrary"),
    )(uv_pre, uv_pre, norm_g, norm_b, w_sp, b_sp, dout)


def _sb_block(q, k, i, j):
    z = _nt(q, k) * (q.shape[-1] ** -0.5)
    t_idx = i * CHUNK + lax.broadcasted_iota(jnp.int32, (CHUNK, CHUNK), 0)
    s_idx = j * CHUNK + lax.broadcasted_iota(jnp.int32, (CHUNK, CHUNK), 1)
    valid = s_idx < t_idx
    sp = jnp.log1p(jnp.exp(-jnp.abs(z)))
    log_beta = jnp.minimum(z, 0.0) - sp
    log_1mb = jnp.where(valid, -jnp.maximum(z, 0.0) - sp, 0.0)
    return z, valid, log_beta, log_1mb


def _later_in_block():
    return (lax.broadcasted_iota(jnp.int32, (CHUNK, CHUNK), 0) > lax.broadcasted_iota(jnp.int32, (CHUNK, CHUNK), 1)).astype(f32)


def sb_fwd(qkv, name):
    T = qkv.shape[0]
    H = qkv.shape[1] // (3 * LANES)
    nq = T // CHUNK

    def body(q_ref, k_ref, v_ref, o_ref, tot_ref):
        i = pl.program_id(1)
        q = q_ref[...]
        later = _later_in_block()

        def step(jj, carry):
            acc, run = carry
            j = i - jj
            rows = pl.ds(pl.multiple_of(j * CHUNK, CHUNK), CHUNK)
            _, valid, log_beta, log_1mb = _sb_block(q, k_ref[rows, :], i, j)
            tail = _nn(log_1mb, later, HIGHEST) + run
            a = jnp.where(valid, jnp.exp(log_beta + tail), 0.0)
            acc = acc + _nn(a.astype(bf16), v_ref[rows, :])
            return acc, run + jnp.sum(log_1mb, axis=1, keepdims=True)

        acc, run = lax.fori_loop(0, i + 1, step, (jnp.zeros((CHUNK, LANES), f32), jnp.zeros((CHUNK, 1), f32)))
        o_ref[...] = acc.astype(o_ref.dtype)
        tot_ref[...] = run

    return pl.pallas_call(
        body, name=name,
        out_shape=[jax.ShapeDtypeStruct((T, H * LANES), bf16), jax.ShapeDtypeStruct((H, T, 1), f32)],
        grid=(H, nq),
        in_specs=[pl.BlockSpec((CHUNK, LANES), lambda h, i: (i, h)),
                  pl.BlockSpec((T, LANES), lambda h, i: (0, H + h)),
                  pl.BlockSpec((T, LANES), lambda h, i: (0, 2 * H + h))],
        out_specs=[pl.BlockSpec((CHUNK, LANES), lambda h, i: (i, h)),
                   pl.BlockSpec((None, CHUNK, 1), lambda h, i: (h, i, 0))],
        compiler_params=_cp("parallel", "parallel"),
    )(qkv, qkv, qkv)


def sb_bwd(qkv, tot, do, name):
    T = qkv.shape[0]
    H = qkv.shape[1] // (3 * LANES)
    nq = T // CHUNK

    def body(q_ref, k_ref, v_ref, tot_ref, do_ref, dq_ref, dk_ref, dv_ref):
        i = pl.program_id(1)

        @pl.when(i == 0)
        def _():
            dk_ref[...] = jnp.zeros_like(dk_ref)
            dv_ref[...] = jnp.zeros_like(dv_ref)

        q = q_ref[...]
        do = do_ref[...].astype(bf16)
        tot = tot_ref[...]
        later = _later_in_block()
        scale = q.shape[-1] ** -0.5

        def step(j, carry):
            dq, before, d_run = carry
            rows = pl.ds(pl.multiple_of(j * CHUNK, CHUNK), CHUNK)
            k, v = k_ref[rows, :], v_ref[rows, :]
            z, valid, log_beta, log_1mb = _sb_block(q, k, i, j)
            before = before + jnp.sum(log_1mb, axis=1, keepdims=True)
            tail = _nn(log_1mb, later, HIGHEST) + (tot - before)
            a = jnp.where(valid, jnp.exp(log_beta + tail), 0.0)
            d_e = _nt(do, v) * a
            d_l1 = jnp.where(valid, _nt(d_e, later, HIGHEST) + d_run, 0.0)
            sig = jax.nn.sigmoid(z)
            dz = ((d_e * (1.0 - sig) - d_l1 * sig) * scale).astype(bf16)
            dk_ref[rows, :] += _tn(dz, q)
            dv_ref[rows, :] += _tn(a.astype(bf16), do)
            return dq + _nn(dz, k), before, d_run + jnp.sum(d_e, axis=1, keepdims=True)

        zero_col = jnp.zeros((CHUNK, 1), f32)
        dq, _, _ = lax.fori_loop(0, i + 1, step, (jnp.zeros((CHUNK, LANES), f32), zero_col, zero_col))
        dq_ref[...] = dq.astype(dq_ref.dtype)

    W = H * LANES
    return pl.pallas_call(
        body, name=name,
        out_shape=[jax.ShapeDtypeStruct((T, W), bf16), jax.ShapeDtypeStruct((T, W), f32), jax.ShapeDtypeStruct((T, W), f32)],
        grid=(H, nq),
        in_specs=[pl.BlockSpec((CHUNK, LANES), lambda h, i: (i, h)),
                  pl.BlockSpec((T, LANES), lambda h, i: (0, H + h)),
                  pl.BlockSpec((T, LANES), lambda h, i: (0, 2 * H + h)),
                  pl.BlockSpec((None, CHUNK, 1), lambda h, i: (h, i, 0)),
                  pl.BlockSpec((CHUNK, LANES), lambda h, i: (i, h))],
        out_specs=[pl.BlockSpec((CHUNK, LANES), lambda h, i: (i, h)),
                   pl.BlockSpec((T, LANES), lambda h, i: (0, h)),
                   pl.BlockSpec((T, LANES), lambda h, i: (0, h))],
        compiler_params=_cp("parallel", "arbitrary"),
    )(qkv, qkv, qkv, tot, do)


def _softplus(x):
    return jnp.maximum(x, 0.0) + jnp.log1p(jnp.exp(-jnp.abs(x)))


def _ssd_chunk_fn(head0, xs, b_mat, c_mat, dt_raw, dt_bias, a_log, d_skip, prev):
    q = dt_raw.shape[0]
    lane = lax.broadcasted_iota(jnp.int32, (q, LANES), 1)
    sub = lax.broadcasted_iota(jnp.int32, (q, LANES), 0)
    causal = sub >= lane
    dt = _softplus(dt_raw + dt_bias)
    a_cum = _nn(causal.astype(f32), dt * (-jnp.exp(a_log)), HIGHEST)
    a_cum_t = a_cum.T
    cb = _nt(c_mat.astype(bf16), b_mat.astype(bf16))
    bm, cm = b_mat.astype(bf16), c_mat.astype(bf16)
    ys, new = [], []
    for r in range(len(xs)):
        in_lane, in_sub = lane == head0 + r, sub == head0 + r
        col_a = jnp.sum(jnp.where(in_lane, a_cum, 0.0), axis=1, keepdims=True)
        row_a = jnp.sum(jnp.where(in_sub, a_cum_t, 0.0), axis=0, keepdims=True)
        col_dt = jnp.sum(jnp.where(in_lane, dt, 0.0), axis=1, keepdims=True)
        skip = jnp.sum(jnp.where(in_lane[:1], d_skip, 0.0), axis=1, keepdims=True)
        a_last = jnp.sum(jnp.where(sub[:, :1] == q - 1, col_a, 0.0), axis=0, keepdims=True)
        decay_in = jnp.exp(jnp.where(causal, col_a - row_a, -jnp.inf))
        xdt = xs[r] * col_dt
        y_diag = _nn((cb * decay_in).astype(bf16), xdt.astype(bf16))
        y_off = _nt(cm, prev[r].astype(bf16)) * jnp.exp(col_a)
        ys.append(y_diag + y_off + xs[r] * skip)
        state = _tn((xdt * jnp.exp(a_last - col_a)).astype(bf16), bm)
        new.append(prev[r] * jnp.exp(a_last) + state)
    return ys, new


def _ssd_specs(T, G, reverse):
    nc = T // CHUNK
    R, P, N = SSD_HEADS_PER_GROUP, SSD_HEAD_DIM, SSD_STATE
    ch = (lambda c: nc - 1 - c) if reverse else (lambda c: c)
    xs = pl.BlockSpec((CHUNK, R * P), lambda g, c: (ch(c), g))
    bm = pl.BlockSpec((CHUNK, N), lambda g, c: (ch(c), G * R * P // N + g))
    cm = pl.BlockSpec((CHUNK, N), lambda g, c: (ch(c), G * R * P // N + G + g))
    dt = pl.BlockSpec((CHUNK, LANES), lambda g, c: (ch(c), 0))
    row = pl.BlockSpec((1, LANES), lambda g, c: (0, 0))
    st = pl.BlockSpec((None, None, R * P, N), lambda g, c: (g, ch(c), 0, 0))
    return nc, xs, bm, cm, dt, row, st


def ssd_fwd(xbc, dt_raw, dt_bias, a_log, d_skip, name):
    T = xbc.shape[0]
    R, P, N = SSD_HEADS_PER_GROUP, SSD_HEAD_DIM, SSD_STATE
    G = xbc.shape[1] // (R * P + 2 * N)
    nc, xs_s, bm_s, cm_s, dt_s, row_s, st_s = _ssd_specs(T, G, False)

    def body(xs_ref, b_ref, c_ref, dt_ref, bias_ref, alog_ref, skip_ref, y_ref, st_ref, state):
        @pl.when(pl.program_id(1) == 0)
        def _():
            state[...] = jnp.zeros_like(state)

        x = xs_ref[...]
        xs = [x[:, r * P:(r + 1) * P] for r in range(R)]
        prev = [state[r] for r in range(R)]
        ys, new = _ssd_chunk_fn(pl.program_id(0) * R, xs, b_ref[...], c_ref[...], dt_ref[...],
                                bias_ref[...], alog_ref[...], skip_ref[...], prev)
        y_ref[...] = jnp.concatenate(ys, axis=1)
        for r in range(R):
            st_ref[pl.ds(r * P, P), :] = prev[r]
            state[r] = new[r]

    return pl.pallas_call(
        body, name=name,
        out_shape=[jax.ShapeDtypeStruct((T, G * R * P), f32), jax.ShapeDtypeStruct((G, nc, R * P, N), f32)],
        grid=(G, nc), in_specs=[xs_s, bm_s, cm_s, dt_s, row_s, row_s, row_s], out_specs=[xs_s, st_s],
        scratch_shapes=[pltpu.VMEM((R, P, N), f32)],
        compiler_params=_cp("parallel", "arbitrary"),
    )(xbc, xbc, xbc, dt_raw, dt_bias, a_log, d_skip)


def ssd_bwd(xbc, dt_raw, dt_bias, a_log, d_skip, states, dy, name):
    T = xbc.shape[0]
    R, P, N = SSD_HEADS_PER_GROUP, SSD_HEAD_DIM, SSD_STATE
    G = xbc.shape[1] // (R * P + 2 * N)
    nc, xs_s, bm_s, cm_s, dt_s, row_s, st_s = _ssd_specs(T, G, True)

    def body(xs_ref, b_ref, c_ref, dt_ref, bias_ref, alog_ref, skip_ref, st_ref, dy_ref,
             dx_ref, db_ref, dc_ref, ddt_ref, dbias_ref, dalog_ref, dskip_ref, dstate):
        g, c = pl.program_id(0), pl.program_id(1)

        @pl.when(c == 0)
        def _():
            dstate[...] = jnp.zeros_like(dstate)

        @pl.when((c == 0) & (g == 0))
        def _():
            for a in (dbias_ref, dalog_ref, dskip_ref):
                a[...] = jnp.zeros_like(a)

        x, dyv = xs_ref[...], dy_ref[...]
        xs = [x[:, r * P:(r + 1) * P] for r in range(R)]
        prev = [st_ref[pl.ds(r * P, P), :] for r in range(R)]
        _, vjp = jax.vjp(functools.partial(_ssd_chunk_fn, g * R), xs, b_ref[...], c_ref[...], dt_ref[...],
                         bias_ref[...], alog_ref[...], skip_ref[...], prev)
        dxs, dbm, dcm, ddt, dbias, dalog, dskip, dprev = vjp(
            ([dyv[:, r * P:(r + 1) * P] for r in range(R)], [dstate[r] for r in range(R)]))
        dx_ref[...] = jnp.concatenate(dxs, axis=1)
        db_ref[...] = dbm
        dc_ref[...] = dcm
        ddt_ref[...] = ddt
        dbias_ref[...] += dbias
        dalog_ref[...] += dalog
        dskip_ref[...] += dskip
        for r in range(R):
            dstate[r] = dprev[r]

    small = pl.BlockSpec((CHUNK, N), lambda g, c: (nc - 1 - c, g))
    return pl.pallas_call(
        body, name=name,
        out_shape=[jax.ShapeDtypeStruct((T, G * R * P), f32), jax.ShapeDtypeStruct((T, G * N), f32),
                   jax.ShapeDtypeStruct((T, G * N), f32), jax.ShapeDtypeStruct((G, T, LANES), f32)]
        + [jax.ShapeDtypeStruct((1, LANES), f32)] * 3,
        grid=(G, nc), in_specs=[xs_s, bm_s, cm_s, dt_s, row_s, row_s, row_s, st_s, xs_s],
        out_specs=[xs_s, small, small, pl.BlockSpec((None, CHUNK, LANES), lambda g, c: (g, nc - 1 - c, 0)),
                   row_s, row_s, row_s],
        scratch_shapes=[pltpu.VMEM((R, P, N), f32)],
        compiler_params=_cp("arbitrary", "arbitrary"),
    )(xbc, xbc, xbc, dt_raw, dt_bias, a_log, d_skip, states, dy)


WEIGHTS = ["ln_mix_pre", "ln_mix_post", "ln_mem", "ln_xa_pre", "ln_xa_post", "ln_ffn_pre", "ln_ffn_post",
           "xa_wq", "xa_wkv", "xa_wo", "ffn_w_in", "ffn_conv_w", "ffn_conv_b", "ffn_w_out",
           "ssd_w_in", "ssd_conv_w", "ssd_conv_b", "ssd_dt_bias", "ssd_a_log", "ssd_d", "ssd_norm", "ssd_w_out",
           "sg_w_in", "sg_v_norm_g", "sg_v_norm_b", "sg_w_spatial", "sg_b_spatial", "sg_w_out", "sb_w_qkv", "sb_w_out"]
BIG = ["xa_wq", "xa_wkv", "xa_wo", "ffn_w_in", "ffn_w_out", "ssd_w_in", "ssd_w_out", "sg_w_in", "sg_w_out",
       "sb_w_qkv", "sb_w_out"]
SMALL_SHARDED = ["ffn_conv_w", "ssd_conv_w", "ssd_conv_b", "ssd_norm"]
REPLICATED = [n for n in WEIGHTS if n not in BIG and n not in SMALL_SHARDED]


def fn_xpre(x, g):
    return x, _rms(x, g)


def _pack_rows(arrs, lead=0):
    head = arrs[0].shape[:lead]
    flat = jnp.concatenate([a.reshape(head + (-1,)) for a in arrs], axis=-1)
    n = flat.shape[-1]
    rows = -(-n // (SUBLANES * LANES)) * SUBLANES
    flat = jnp.pad(flat, [(0, 0)] * lead + [(0, rows * LANES - n)])
    return flat.reshape(head + (rows, LANES))


def _unpack_rows(packed, shapes):
    head = packed.shape[:-2]
    flat = packed.reshape(head + (-1,))
    out, off = [], 0
    for shp in shapes:
        n = int(np.prod(shp, dtype=np.int64))
        out.append(flat[..., off:off + n].reshape(head + tuple(shp)))
        off += n
    return out


def _merge_last(a8):
    return jnp.moveaxis(a8, 0, -2).reshape(a8.shape[1:-1] + (N_DEV * a8.shape[-1],))


def _split_last(a):
    return jnp.moveaxis(a.reshape(a.shape[:-1] + (N_DEV, a.shape[-1] // N_DEV)), -2, 0)


def kernel(x, mem, ln_mix_pre, ln_mix_post, ln_mem, ln_xa_pre, ln_xa_post, ln_ffn_pre, ln_ffn_post, xa_wq, xa_wkv, xa_wo, ffn_w_in, ffn_conv_w, ffn_conv_b, ffn_w_out, ssd_w_in, ssd_conv_w, ssd_conv_b, ssd_dt_bias, ssd_a_log, ssd_d, ssd_norm, ssd_w_out, sg_w_in, sg_v_norm_g, sg_v_norm_b, sg_w_spatial, sg_b_spatial, sg_w_out, sb_w_qkv, sb_w_out, loss_target, m_ln_mix_pre, m_ln_mix_post, m_ln_mem, m_ln_xa_pre, m_ln_xa_post, m_ln_ffn_pre, m_ln_ffn_post, m_xa_wq, m_xa_wkv, m_xa_wo, m_ffn_w_in, m_ffn_conv_w, m_ffn_conv_b, m_ffn_w_out, m_ssd_w_in, m_ssd_conv_w, m_ssd_conv_b, m_ssd_dt_bias, m_ssd_a_log, m_ssd_d, m_ssd_norm, m_ssd_w_out, m_sg_w_in, m_sg_v_norm_g, m_sg_v_norm_b, m_sg_w_spatial, m_sg_b_spatial, m_sg_w_out, m_sb_w_qkv, m_sb_w_out, v_ln_mix_pre, v_ln_mix_post, v_ln_mem, v_ln_xa_pre, v_ln_xa_post, v_ln_ffn_pre, v_ln_ffn_post, v_xa_wq, v_xa_wkv, v_xa_wo, v_ffn_w_in, v_ffn_conv_w, v_ffn_conv_b, v_ffn_w_out, v_ssd_w_in, v_ssd_conv_w, v_ssd_conv_b, v_ssd_dt_bias, v_ssd_a_log, v_ssd_d, v_ssd_norm, v_ssd_w_out, v_sg_w_in, v_sg_v_norm_g, v_sg_v_norm_b, v_sg_w_spatial, v_sg_b_spatial, v_sg_w_out, v_sb_w_qkv, v_sb_w_out):
    p = dict(locals())
    x, mem, target = p["x"][0], p["mem"][0], p["loss_target"][0]
    T, D = x.shape
    depth = ln_mix_pre.shape[0]
    me = 4 * lax.axis_index("x") + 2 * lax.axis_index("y") + lax.axis_index("c")

    def gather(w, name):
        return all_gather(w.astype(bf16), name)

    small8 = all_gather(_pack_rows([p[n] for n in SMALL_SHARDED]), "ag_small")
    full = {n: _merge_last(a) for n, a in zip(SMALL_SHARDED, _unpack_rows(small8, [p[n].shape for n in SMALL_SHARDED]))}

    grads = {n: {} for n in WEIGHTS}

    def ssd(h, j):
        g_in = gather(p["ssd_w_in"][j], "ag_ssd_in")
        w_full = jnp.moveaxis(g_in, 0, 1).reshape(D, -1)
        w_out = gather(p["ssd_w_out"][j], "ag_ssd_out").reshape(1, -1, D)
        d_inner, conv_dim, heads = w_out.shape[1], full["ssd_conv_w"].shape[-1], ssd_dt_bias.shape[1]
        w_z, w_x = w_full[None, :, :d_inner], w_full[None, :, d_inner:d_inner + conv_dim]
        w_dt = jnp.pad(w_full[:, d_inner + conv_dim:], ((0, 0), (0, LANES - heads)))[None]
        lane_row = lambda a: jnp.pad(a[j:j + 1], ((0, 0), (0, LANES - heads)))
        bias, a_log, d_skip = lane_row(ssd_dt_bias), lane_row(ssd_a_log), lane_row(ssd_d)
        norm_g = full["ssd_norm"][j:j + 1]
        z = mm_nn(h, w_z, f32, "ssd_in_z")
        xbc_pre = mm_nn(h, w_x, f32, "ssd_in_x")
        dt_raw = mm_nn(h, w_dt, f32, "ssd_in_dt")
        streams = [(xbc_pre, 0, full["ssd_conv_w"][j], full["ssd_conv_b"][j:j + 1], 0)]
        xbc = conv_fwd(streams, epi_silu, conv_dim, f32, 4, "ssd_conv")
        y, states = ssd_fwd(xbc, dt_raw, bias, a_log, d_skip, "ssd_core")
        gated = row_fwd(fn_gate_norm, [y, z], [norm_g], [(d_inner, bf16)], "ssd_gate")[0]
        out = mm_nn(gated, w_out, f32, "ssd_out")

        def bwd(d_out):
            d_gated = mm_nt(d_out, w_out, f32, "ssd_out_dx")
            grads["ssd_w_out"][j] = mm_tn(gated, d_out, D, bf16, "ssd_out_dw").reshape(N_DEV, -1, D)
            dy, dz, d_norm = row_bwd(fn_gate_norm, [y, z], [norm_g], [d_gated], [f32, bf16], "ssd_gate_bwd")
            dxs, dbm, dcm, ddt_g, d_bias, d_alog, d_skipg = ssd_bwd(xbc, dt_raw, bias, a_log, d_skip, states, dy, "ssd_core_bwd")
            (dx_pre,), (d_cw,), (d_cb,) = conv_bwd(streams, epi_silu, jnp.concatenate([dxs, dbm, dcm], axis=1), 4, "ssd_conv_bwd")
            ddt = jnp.sum(ddt_g, axis=0).astype(bf16)
            dh = mm_nt(dz, w_z, f32, "ssd_in_z_dx")
            dh = mm_nt(dx_pre, w_x, f32, "ssd_in_x_dx", init=dh)
            dh = mm_nt(ddt, w_dt, f32, "ssd_in_dt_dx", init=dh)
            dw = jnp.concatenate([mm_tn(h, dz, d_inner, bf16, "ssd_in_z_dw")[0], mm_tn(h, dx_pre, conv_dim, bf16, "ssd_in_x_dw")[0],
                                  mm_tn(h, ddt, LANES, bf16, "ssd_in_dt_dw")[0][:, :heads]], axis=1)
            grads["ssd_w_in"][j] = _split_last(dw)
            grads["ssd_conv_w"][j], grads["ssd_conv_b"][j], grads["ssd_norm"][j] = d_cw, d_cb[0], d_norm[0]
            grads["ssd_dt_bias"][j], grads["ssd_a_log"][j], grads["ssd_d"][j] = d_bias[0, :heads], d_alog[0, :heads], d_skipg[0, :heads]
            return dh

        return out, bwd

    def sgu(h, j):
        w_in = gather(p["sg_w_in"][j], "ag_sg_in")
        w_out = gather(p["sg_w_out"][j], "ag_sg_out").reshape(1, -1, D)
        norm_g, norm_b = sg_v_norm_g[j:j + 1], sg_v_norm_b[j:j + 1]
        w_sp, b_sp = sg_w_spatial[j], sg_b_spatial[j][..., None]
        uv = mm_nn(h, w_in, f32, "sg_in")
        gated = sgu_fwd(uv, norm_g, norm_b, w_sp, b_sp, "sg_core")
        out = mm_nn(gated, w_out, f32, "sg_out")

        def bwd(d_out):
            d_gated = mm_nt(d_out, w_out, f32, "sg_out_dx")
            grads["sg_w_out"][j] = mm_tn(gated, d_out, D, bf16, "sg_out_dw").reshape(N_DEV, -1, D)
            duv, d_ng, d_nb, d_ws, d_bs = sgu_bwd(uv, norm_g, norm_b, w_sp, b_sp, d_gated, "sg_core_bwd")
            grads["sg_v_norm_g"][j], grads["sg_v_norm_b"][j] = d_ng[0], d_nb[0]
            grads["sg_w_spatial"][j], grads["sg_b_spatial"][j] = d_ws, d_bs[..., 0]
            grads["sg_w_in"][j] = mm_tn(h, duv, w_in.shape[2], bf16, "sg_in_dw")
            return mm_nt(duv, w_in, f32, "sg_in_dx")

        return out, bwd

    def stick(h, j):
        w_qkv = gather(p["sb_w_qkv"][j], "ag_sb_qkv")
        w_out = gather(p["sb_w_out"][j], "ag_sb_out").reshape(1, -1, D)
        qkv = mm_nn(h, w_qkv, bf16, "sb_qkv")
        o, tot = sb_fwd(qkv, "sb_core")
        out = mm_nn(o, w_out, f32, "sb_out")

        def bwd(d_out):
            d_o = mm_nt(d_out, w_out, f32, "sb_out_dx")
            grads["sb_w_out"][j] = mm_tn(o, d_out, D, bf16, "sb_out_dw").reshape(N_DEV, -1, D)
            dq, dk, dv = sb_bwd(qkv, tot, d_o, "sb_core_bwd")
            dqkv = jnp.concatenate([dq, dk.astype(bf16), dv.astype(bf16)], axis=1)
            grads["sb_w_qkv"][j] = mm_tn(h, dqkv, w_qkv.shape[2], bf16, "sb_qkv_dw")
            return mm_nt(dqkv, w_qkv, f32, "sb_qkv_dx")

        return out, bwd

    def cross(h, i):
        gain = ln_mem[i:i + 1]
        mem_n = row_fwd(fn_prenorm, [mem], [gain], [(D, bf16)], "mem_norm")[0]
        w_q = gather(p["xa_wq"][i], "ag_xa_q").reshape(1, D, -1)
        w_kv = gather(p["xa_wkv"][i], "ag_xa_kv").reshape(1, D, -1)
        w_o = gather(p["xa_wo"][i], "ag_xa_o")
        q = mm_nn(h, w_q, bf16, "xa_q")
        kv = mm_nn(mem_n, w_kv, bf16, "xa_kv")
        o = xa_fwd(q, kv, "xa_core")
        out = mm_nn(o, w_o, f32, "xa_out")

        def bwd(d_out):
            d_o = mm_nt(d_out, w_o, f32, "xa_out_dx")
            grads["xa_wo"][i] = mm_tn(o, d_out, w_o.shape[2], bf16, "xa_out_dw")
            dq, dk, dv = xa_bwd(q, kv, d_o, "xa_core_bwd")
            dkv = jnp.concatenate([dk, dv], axis=1).astype(bf16)
            grads["xa_wq"][i] = mm_tn(h, dq, w_q.shape[2], bf16, "xa_q_dw").reshape(N_DEV, -1, w_q.shape[2])
            grads["xa_wkv"][i] = mm_tn(mem_n, dkv, w_kv.shape[2], bf16, "xa_kv_dw").reshape(N_DEV, -1, w_kv.shape[2])
            d_mem_n = mm_nt(dkv, w_kv, f32, "xa_kv_dx")
            grads["ln_mem"][i] = row_bwd(fn_prenorm, [mem], [gain], [d_mem_n], [None], "mem_norm_bwd")[0][0]
            return mm_nt(dq, w_q, f32, "xa_q_dx")

        return out, bwd

    def ffn(h, i):
        w_in = gather(p["ffn_w_in"][i], "ag_ffn_in")
        w_out = gather(p["ffn_w_out"][i], "ag_ffn_out").reshape(1, -1, D)
        width = w_out.shape[1]
        conv_w, conv_b = full["ffn_conv_w"][i], ffn_conv_b[i:i + 1]
        gu = mm_nn(h, w_in, f32, "ffn_in")
        streams = [(gu, 0, conv_w, conv_b, 0), (gu, width // LANES, conv_w, conv_b, width // LANES)]
        act = conv_fwd(streams, epi_gelu_gate, width, bf16, 3, "ffn_gate")
        out = mm_nn(act, w_out, f32, "ffn_out")

        def bwd(d_out):
            d_act = mm_nt(d_out, w_out, f32, "ffn_out_dx")
            grads["ffn_w_out"][i] = mm_tn(act, d_out, D, bf16, "ffn_out_dw").reshape(N_DEV, -1, D)
            d_pre, d_cw, d_cb = conv_bwd(streams, epi_gelu_gate, d_act, 3, "ffn_gate_bwd")
            dgu = jnp.concatenate(d_pre, axis=1)
            grads["ffn_conv_w"][i], grads["ffn_conv_b"][i] = jnp.concatenate(d_cw, axis=1), jnp.concatenate(d_cb, axis=1)[0]
            grads["ffn_w_in"][i] = mm_tn(h, dgu, w_in.shape[2], bf16, "ffn_in_dw")
            return mm_nt(dgu, w_in, f32, "ffn_in_dx")

        return out, bwd

    n_sub = 3 * depth
    pre = [w[i:i + 1] for i in range(depth) for w in (ln_mix_pre, ln_xa_pre, ln_ffn_pre)]
    post = [w[i:i + 1] for i in range(depth) for w in (ln_mix_post, ln_xa_post, ln_ffn_post)]
    stream, outs, bwds = [x], [], []
    h = row_fwd(fn_prenorm, [x], [pre[0]], [(D, bf16)], "pre_norm")[0]
    for s in range(n_sub):
        i, t = divmod(s, 3)
        out, bwd = ((ssd, sgu, stick)[i % 3](h, i // 3) if t == 0 else cross(h, i) if t == 1 else ffn(h, i))
        outs.append(out)
        bwds.append(bwd)
        if s < n_sub - 1:
            x_new, h = row_fwd(fn_resnorm, [stream[s], out], [post[s], pre[s + 1]], [(D, f32), (D, bf16)], "res_norm")
            stream.append(x_new)
    dy, loss = row_fwd(fn_final, [stream[-1], outs[-1], target], [post[-1]], [(D, f32), (1, f32)], "loss_head", n_acc=1)
    loss = lax.psum(loss[0, 0], ("x", "y", "c"))

    d_pre, d_post = [None] * n_sub, [None] * n_sub
    dx, d_out, d_post[-1] = row_bwd(fn_res, [stream[-1], outs[-1]], [post[-1]], [dy], [f32, bf16], "res_bwd")
    for s in reversed(range(n_sub)):
        dh = bwds[s](d_out)
        if s > 0:
            dx, d_out, d_post[s - 1], d_pre[s] = row_bwd(
                fn_resnorm, [stream[s - 1], outs[s - 1]], [post[s - 1], pre[s]], [dx, dh], [f32, bf16], "res_norm_bwd")
        else:
            grad_x, d_pre[0] = row_bwd(fn_xpre, [x], [pre[0]], [dx, dh], [f32], "pre_norm_bwd")
    for t, kind in enumerate(("mix", "xa", "ffn")):
        for i in range(depth):
            grads["ln_%s_pre" % kind][i] = d_pre[3 * i + t][0]
            grads["ln_%s_post" % kind][i] = d_post[3 * i + t][0]

    def stacked(name):
        return jnp.stack([grads[name][l] for l in range(len(grads[name]))], axis=0)

    new = {}
    for name in BIG:
        parts = jnp.stack([reduce_scatter_parts(grads[name][l], "rs_" + name) for l in range(len(grads[name]))], axis=1)
        new[name] = adamw_sharded(p[name], parts, p["m_" + name], p["v_" + name], "adamw_" + name)

    rep_shapes = [p[n].shape for n in REPLICATED]
    g8 = all_gather(_pack_rows([stacked(n) for n in REPLICATED]), "ag_grad_rep")
    rep = adamw_summed8(_pack_rows([p[n] for n in REPLICATED]), g8, _pack_rows([p["m_" + n] for n in REPLICATED]),
                        _pack_rows([p["v_" + n] for n in REPLICATED]), "adamw_rep")
    for k, packed in enumerate(rep):
        for n, a in zip(REPLICATED, _unpack_rows(packed, rep_shapes)):
            new.setdefault(n, [None] * 4)[k] = a

    sh_shapes = [p[n].shape for n in SMALL_SHARDED]
    by_owner = _pack_rows([_split_last(stacked(n)) for n in SMALL_SHARDED], lead=1)
    mine8 = lax.dynamic_index_in_dim(all_gather(by_owner, "ag_grad_small"), me, axis=1, keepdims=False)
    sh = adamw_summed8(_pack_rows([p[n] for n in SMALL_SHARDED]), mine8, _pack_rows([p["m_" + n] for n in SMALL_SHARDED]),
                       _pack_rows([p["v_" + n] for n in SMALL_SHARDED]), "adamw_small")
    for k, packed in enumerate(sh):
        for n, a in zip(SMALL_SHARDED, _unpack_rows(packed, sh_shapes)):
            new.setdefault(n, [None] * 4)[k] = a

    return (loss, grad_x[None], *[new[n][0] for n in WEIGHTS], *[new[n][1] for n in WEIGHTS],
            *[new[n][2] for n in WEIGHTS], *[new[n][3] for n in WEIGHTS])
```

```python
import functools

import jax
import jax.numpy as jnp
import numpy as np
from jax import lax
from jax.experimental import pallas as pl
from jax.experimental.pallas import tpu as pltpu

f32 = jnp.float32
bf16 = jnp.bfloat16
HIGHEST = lax.Precision.HIGHEST
MESH = pl.DeviceIdType.MESH

V7X_VMEM_BYTES = 64 * 1024 * 1024
VMEM_LIMIT = V7X_VMEM_BYTES * 3 // 4
LANES = 128
SUBLANES = 8
BF16_ROWS = 16

EPS = 1e-6
ADAM_LR = 0.001
ADAM_B1 = 0.9
ADAM_B2 = 0.999
ADAM_EPS = 1e-08
ADAM_WD = 0.01
ADAM_STEP = 10

N_DEV = 8
XA_HEADS = 4
SSD_HEADS_PER_GROUP = 8
SSD_HEAD_DIM = 64
SSD_STATE = 128
SSD_GROUPS = 8
CHUNK = 128


def _cp(*sem):
    return pltpu.CompilerParams(dimension_semantics=sem or None, vmem_limit_bytes=VMEM_LIMIT)


def _dot(a, b, dims, precision=None):
    return lax.dot_general(a, b, (dims, ((), ())), precision=precision, preferred_element_type=f32)


def _nn(a, b, precision=None):
    return _dot(a, b, ((1,), (0,)), precision)


def _nt(a, b, precision=None):
    return _dot(a, b, ((1,), (1,)), precision)


def _tn(a, b, precision=None):
    return _dot(a, b, ((0,), (0,)), precision)


def _largest_tile(n, cap, step):
    for t in range(min(n, cap) // step * step, 0, -step):
        if n % t == 0:
            return t
    return n


def all_gather(x, name):
    def body(x_ref, out_ref, send_sems, recv_sems, local_sem):
        mx, my, mc = lax.axis_index("x"), lax.axis_index("y"), lax.axis_index("c")
        me, sibling = (mx, my, mc), (mx, my, 1 - mc)
        chips = [(1 - mx, my), (mx, 1 - my), (1 - mx, 1 - my)]

        def slot(px, py, pc):
            return out_ref.at[4 * px + 2 * py + pc]

        def copy(k, block, to, src=None):
            return pltpu.make_async_remote_copy(
                src_ref=slot(*block) if src is None else src, dst_ref=slot(*block),
                send_sem=send_sems.at[k], recv_sem=recv_sems.at[k], device_id=to, device_id_type=MESH)

        mine = pltpu.make_async_copy(x_ref, slot(*me), local_sem)
        mine.start()
        first = [copy(0, me, sibling, src=x_ref)]
        first += [copy(1 + j, me, (*chip, mc), src=x_ref) for j, chip in enumerate(chips)]
        for cp in first:
            cp.start()
        passed = [copy(4 + j, (*chip, mc), sibling) for j, chip in enumerate(chips)]
        for j, chip in enumerate(chips):
            copy(1 + j, (*chip, mc), me).wait_recv()
            passed[j].start()
        copy(0, sibling, me).wait_recv()
        for j, chip in enumerate(chips):
            copy(4 + j, (*chip, 1 - mc), me).wait_recv()
        for cp in first + passed:
            cp.wait_send()
        mine.wait()

    return pl.pallas_call(
        body, name=name,
        out_shape=jax.ShapeDtypeStruct((N_DEV,) + x.shape, x.dtype),
        in_specs=[pl.BlockSpec(memory_space=pl.ANY)],
        out_specs=pl.BlockSpec(memory_space=pl.ANY),
        scratch_shapes=[pltpu.SemaphoreType.DMA((7,)), pltpu.SemaphoreType.DMA((7,)), pltpu.SemaphoreType.DMA(())],
    )(x)


def rs_sibling_exchange(g8, name):
    def body(g_ref, land_ref, send_sems, recv_sems):
        mx, my, mc = lax.axis_index("x"), lax.axis_index("y"), lax.axis_index("c")
        sibling = (mx, my, 1 - mc)
        copies = [
            pltpu.make_async_remote_copy(
                src_ref=g_ref.at[2 * k + (1 - mc)], dst_ref=land_ref.at[k],
                send_sem=send_sems.at[k], recv_sem=recv_sems.at[k], device_id=sibling, device_id_type=MESH)
            for k in range(4)
        ]
        for cp in copies:
            cp.start()
        for cp in copies:
            cp.wait()

    return pl.pallas_call(
        body, name=name,
        out_shape=jax.ShapeDtypeStruct((4,) + g8.shape[1:], g8.dtype),
        in_specs=[pl.BlockSpec(memory_space=pl.ANY)],
        out_specs=pl.BlockSpec(memory_space=pl.ANY),
        scratch_shapes=[pltpu.SemaphoreType.DMA((4,)), pltpu.SemaphoreType.DMA((4,))],
    )(g8)


def rs_chip_exchange(s4, name):
    def body(s_ref, land_ref, send_sems, recv_sems):
        mx, my, mc = lax.axis_index("x"), lax.axis_index("y"), lax.axis_index("c")
        chips = [(1 - mx, my), (mx, 1 - my), (1 - mx, 1 - my)]
        copies = [
            pltpu.make_async_remote_copy(
                src_ref=s_ref.at[2 * cx + cy], dst_ref=land_ref.at[j],
                send_sem=send_sems.at[j], recv_sem=recv_sems.at[j], device_id=(cx, cy, mc), device_id_type=MESH)
            for j, (cx, cy) in enumerate(chips)
        ]
        for cp in copies:
            cp.start()
        for cp in copies:
            cp.wait()

    return pl.pallas_call(
        body, name=name,
        out_shape=jax.ShapeDtypeStruct((3,) + s4.shape[1:], s4.dtype),
        in_specs=[pl.BlockSpec(memory_space=pl.ANY)],
        out_specs=pl.BlockSpec(memory_space=pl.ANY),
        scratch_shapes=[pltpu.SemaphoreType.DMA((3,)), pltpu.SemaphoreType.DMA((3,))],
    )(s4)


def _as_lrc(a, lead):
    rest = a.shape[lead:]
    return a.reshape(a.shape[:lead] + (int(np.prod(rest[:-2], dtype=np.int64)),) + rest[-2:])


def rs_pair_add(g8, land4, name):
    shape = land4.shape
    g = _as_lrc(g8, 1)
    ld = _as_lrc(land4, 1)
    _, L, R, C = ld.shape
    tr = _largest_tile(R, 512, BF16_ROWS)
    mc = lax.axis_index("c").astype(jnp.int32).reshape(1)

    def body(mc_ref, g_ref, l_ref, o_ref):
        o_ref[...] = (g_ref[...].astype(f32) + l_ref[...].astype(f32)).astype(o_ref.dtype)

    out = pl.pallas_call(
        body, name=name,
        out_shape=jax.ShapeDtypeStruct(ld.shape, ld.dtype),
        grid_spec=pltpu.PrefetchScalarGridSpec(
            num_scalar_prefetch=1, grid=(4, L, R // tr),
            in_specs=[pl.BlockSpec((None, None, tr, C), lambda k, l, r, mc_ref: (2 * k + mc_ref[0], l, r, 0)),
                      pl.BlockSpec((None, None, tr, C), lambda k, l, r, mc_ref: (k, l, r, 0))],
            out_specs=pl.BlockSpec((None, None, tr, C), lambda k, l, r, mc_ref: (k, l, r, 0))),
        compiler_params=_cp("parallel", "parallel", "parallel"),
    )(mc, g, ld)
    return out.reshape(shape)


def reduce_scatter_parts(g8, name):
    land4 = rs_sibling_exchange(g8, name + "_d2d")
    s4 = rs_pair_add(g8, land4, name + "_add")
    land3 = rs_chip_exchange(s4, name + "_ici")
    chip = 2 * lax.axis_index("x") + lax.axis_index("y")
    mine = lax.dynamic_index_in_dim(s4, chip, axis=0, keepdims=True)
    return jnp.concatenate([mine, land3], axis=0)


def _adam_math(w, g, m, v):
    m = ADAM_B1 * m + (1.0 - ADAM_B1) * g
    v = ADAM_B2 * v + (1.0 - ADAM_B2) * jnp.square(g)
    m_hat = m / (1.0 - ADAM_B1 ** ADAM_STEP)
    v_hat = v / (1.0 - ADAM_B2 ** ADAM_STEP)
    delta = -ADAM_LR * (m_hat / (jnp.sqrt(v_hat) + ADAM_EPS) + ADAM_WD * w)
    return delta, m, v


def adamw_sharded(w, parts, m, v, name):
    shape = w.shape
    w3, m3, v3 = (_as_lrc(a, 0) for a in (w, m, v))
    p4 = _as_lrc(parts, 1)
    L, R, C = w3.shape
    tr = _largest_tile(R, 256, BF16_ROWS)

    def body(w_ref, p_ref, m_ref, v_ref, g_out, d_out, m_out, v_out):
        g = p_ref[0].astype(f32) + p_ref[1].astype(f32) + p_ref[2].astype(f32) + p_ref[3].astype(f32)
        delta, mn, vn = _adam_math(w_ref[...], g, m_ref[...], v_ref[...])
        g_out[...] = g
        d_out[...] = delta
        m_out[...] = mn
        v_out[...] = vn

    blk = pl.BlockSpec((None, tr, C), lambda l, r: (l, r, 0))
    outs = pl.pallas_call(
        body, name=name, out_shape=[jax.ShapeDtypeStruct(w3.shape, f32)] * 4, grid=(L, R // tr),
        in_specs=[blk, pl.BlockSpec((4, None, tr, C), lambda l, r: (0, l, r, 0)), blk, blk], out_specs=[blk] * 4,
        compiler_params=_cp("parallel", "parallel"),
    )(w3, p4, m3, v3)
    return tuple(o.reshape(shape) for o in outs)


def adamw_summed8(w, g8, m, v, name):
    R, C = w.shape
    tr = _largest_tile(R, 512, SUBLANES)

    def body(w_ref, g_ref, m_ref, v_ref, g_out, d_out, m_out, v_out):
        g = g_ref[0]
        for d in range(1, N_DEV):
            g = g + g_ref[d]
        delta, mn, vn = _adam_math(w_ref[...], g, m_ref[...], v_ref[...])
        g_out[...] = g
        d_out[...] = delta
        m_out[...] = mn
        v_out[...] = vn

    blk = pl.BlockSpec((tr, C), lambda r: (r, 0))
    return pl.pallas_call(
        body, name=name, out_shape=[jax.ShapeDtypeStruct((R, C), f32)] * 4, grid=(R // tr,),
        in_specs=[blk, pl.BlockSpec((N_DEV, tr, C), lambda r: (0, r, 0)), blk, blk], out_specs=[blk] * 4,
        compiler_params=_cp("parallel"),
    )(w, g8, m, v)


def _tile_n(ns):
    if ns % 512 == 0:
        return 512
    if ns <= 1536:
        return ns
    return _largest_tile(ns, 512, LANES)


MM_VMEM_BUDGET = VMEM_LIMIT - 8 * 1024 * 1024


def _tiles_desc(n, cap, step):
    return [t for t in range(min(n, cap) // step * step, 0, -step) if n % t == 0] or [n]


def _mm_fits(in_tiles, out_tile, out_dtype, n_red, extra=0):
    rows, cols = out_tile
    total = sum(2 * 2 * r * c for r, c in in_tiles) + 2 * rows * cols * jnp.dtype(out_dtype).itemsize + extra
    if n_red > 1:
        total += 4 * rows * cols
    return total <= MM_VMEM_BUDGET


def _reduce_into(o_ref, acc, part, first, last):
    if acc is None:
        o_ref[...] = part().astype(o_ref.dtype)
        return

    @pl.when(first)
    def _():
        acc[...] = jnp.zeros_like(acc)

    acc[...] += part()

    @pl.when(last)
    def _():
        o_ref[...] = acc[...].astype(o_ref.dtype)


def mm_nn(a, w3, out_dtype, name):
    M, K = a.shape
    J, _, Ns = w3.shape
    tn = _tile_n(Ns)
    tm, tk = next(((tm, tk) for tk in _tiles_desc(K, 4096, LANES) for tm in _tiles_desc(M, 1024, BF16_ROWS)
                   if tm >= min(M, 256) and _mm_fits([(tm, tk), (tk, tn)], (tm, tn), out_dtype, K // tk)),
                  (min(M, 256), _largest_tile(K, 512, LANES)))
    nn, nk = Ns // tn, K // tk

    def body(a_ref, w_ref, o_ref, *acc):
        k = pl.program_id(3)
        _reduce_into(o_ref, acc[0] if acc else None, lambda: _nn(a_ref[...], w_ref[...]), k == 0, k == nk - 1)

    return pl.pallas_call(
        body, name=name, out_shape=jax.ShapeDtypeStruct((M, J * Ns), out_dtype),
        grid=(M // tm, J, nn, nk),
        in_specs=[pl.BlockSpec((tm, tk), lambda i, j, n, k: (i, k)),
                  pl.BlockSpec((None, tk, tn), lambda i, j, n, k: (j, k, n))],
        out_specs=pl.BlockSpec((tm, tn), lambda i, j, n, k: (i, j * nn + n)),
        scratch_shapes=[pltpu.VMEM((tm, tn), f32)] if nk > 1 else [],
        compiler_params=_cp("parallel", "parallel", "parallel", "arbitrary"),
    )(a, w3)


MM_NT_REDUCE_CAP = 5632


def mm_nt(g, w3, out_dtype, name, init=None):
    M = g.shape[0]
    J, K, Ns = w3.shape
    if Ns <= MM_NT_REDUCE_CAP:
        tn, jb = Ns, max(b for b in range(1, J + 1) if J % b == 0 and (b == 1 or b * Ns <= MM_NT_REDUCE_CAP))
    else:
        tn, jb = _largest_tile(Ns, MM_NT_REDUCE_CAP, LANES), 1
    nj, nn = J // jb, Ns // tn
    n_red = nj * nn + (init is not None)
    tk = _largest_tile(K, 512, LANES)
    tm = next((tm for tm in _tiles_desc(M, 1024, BF16_ROWS)
               if _mm_fits([(tm, jb * tn), (jb * tk, tn)], (tm, tk), out_dtype, n_red,
                           extra=0 if init is None else 2 * 4 * tm * tk)), min(M, 256))

    def body(*refs):
        g_ref, w_ref = refs[:2]
        i_ref = None if init is None else refs[2]
        o_ref = refs[2 + (init is not None)]
        acc = refs[3 + (init is not None):]
        j, n = pl.program_id(2), pl.program_id(3)
        first = (j == 0) & (n == 0)

        def part():
            prod = _nt(g_ref[:, :tn], w_ref[0])
            for b in range(1, jb):
                prod = prod + _nt(g_ref[:, b * tn:(b + 1) * tn], w_ref[b])
            return prod if init is None else prod + jnp.where(first, i_ref[...].astype(f32), 0.0)

        _reduce_into(o_ref, acc[0] if acc else None, part, first, (j == nj - 1) & (n == nn - 1))

    in_specs = [pl.BlockSpec((tm, jb * tn), lambda i, k, j, n: (i, j * nn + n)),
                pl.BlockSpec((jb, tk, tn), lambda i, k, j, n: (j, k, n))]
    args = [g, w3]
    if init is not None:
        in_specs.append(pl.BlockSpec((tm, tk), lambda i, k, j, n: (i, k)))
        args.append(init)
    return pl.pallas_call(
        body, name=name, out_shape=jax.ShapeDtypeStruct((M, K), out_dtype),
        grid=(M // tm, K // tk, nj, nn), in_specs=in_specs,
        out_specs=pl.BlockSpec((tm, tk), lambda i, k, j, n: (i, k)),
        scratch_shapes=[pltpu.VMEM((tm, tk), f32)] if n_red > 1 else [],
        compiler_params=_cp("parallel", "parallel", "arbitrary", "arbitrary"),
    )(*args)


def mm_tn(a, g, ns, out_dtype, name):
    M, K = a.shape
    J = g.shape[1] // ns
    tn, tk = _tile_n(ns), _largest_tile(K, 512, LANES)
    tm = next((tm for tm in _tiles_desc(M, 4096, BF16_ROWS)
               if _mm_fits([(tm, tk), (tm, tn)], (tk, tn), out_dtype, M // tm)), _largest_tile(M, 512, BF16_ROWS))
    nn, nm = ns // tn, M // tm

    def body(a_ref, g_ref, o_ref, *acc):
        m = pl.program_id(3)
        _reduce_into(o_ref, acc[0] if acc else None, lambda: _tn(a_ref[...], g_ref[...]), m == 0, m == nm - 1)

    return pl.pallas_call(
        body, name=name, out_shape=jax.ShapeDtypeStruct((J, K, ns), out_dtype),
        grid=(J, K // tk, nn, nm),
        in_specs=[pl.BlockSpec((tm, tk), lambda j, k, n, m: (m, k)),
                  pl.BlockSpec((tm, tn), lambda j, k, n, m: (m, j * nn + n))],
        out_specs=pl.BlockSpec((None, tk, tn), lambda j, k, n, m: (j, k, n)),
        scratch_shapes=[pltpu.VMEM((tk, tn), f32)] if nm > 1 else [],
        compiler_params=_cp("parallel", "parallel", "parallel", "arbitrary"),
    )(a, g)


ROW_TILE = 256
ROW_STEP = 16


def row_fwd(fn, rows, consts, outs, name, n_acc=0):
    T = rows[0].shape[0]
    tr = min(T, ROW_TILE)
    n_rows, n_consts, n_row_out = len(rows), len(consts), len(outs) - n_acc

    def body(*refs):
        r_refs, c_refs, o_refs = refs[:n_rows], refs[n_rows:n_rows + n_consts], refs[n_rows + n_consts:]
        cs = [c[...] for c in c_refs]
        acc_refs = o_refs[n_row_out:]
        if n_acc:
            @pl.when(pl.program_id(0) == 0)
            def _():
                for a in acc_refs:
                    a[...] = jnp.zeros_like(a)

        def step(s, carry):
            rows_s = pl.ds(pl.multiple_of(s * ROW_STEP, ROW_STEP), ROW_STEP)
            res = fn(*[r[rows_s, :].astype(f32) for r in r_refs], *cs)
            for o, val in zip(o_refs[:n_row_out], res[:n_row_out]):
                o[rows_s, :] = val.astype(o.dtype)
            return tuple(c + val for c, val in zip(carry, res[n_row_out:]))

        accs = lax.fori_loop(0, tr // ROW_STEP, step, tuple(jnp.zeros((1, c), f32) for c, _ in outs[n_row_out:]))
        for a, val in zip(acc_refs, accs):
            a[...] += val

    in_specs = [pl.BlockSpec((tr, r.shape[1]), lambda i: (i, 0)) for r in rows]
    in_specs += [pl.BlockSpec(c.shape, lambda i: (0, 0)) for c in consts]
    out_shape = [jax.ShapeDtypeStruct((T, c), dt) for c, dt in outs[:n_row_out]]
    out_shape += [jax.ShapeDtypeStruct((1, c), f32) for c, _ in outs[n_row_out:]]
    out_specs = [pl.BlockSpec((tr, c), lambda i: (i, 0)) for c, _ in outs[:n_row_out]]
    out_specs += [pl.BlockSpec((1, c), lambda i: (0, 0)) for c, _ in outs[n_row_out:]]
    return pl.pallas_call(
        body, name=name, out_shape=out_shape, grid=(T // tr,), in_specs=in_specs, out_specs=out_specs,
        compiler_params=_cp("arbitrary" if n_acc else "parallel"),
    )(*rows, *consts)


def row_bwd(fn, rows, consts, cts, grad_dtypes, name):
    T = rows[0].shape[0]
    tr = min(T, ROW_TILE)
    n_rows, n_consts, n_cts = len(rows), len(consts), len(cts)
    wanted = [i for i, dt in enumerate(grad_dtypes) if dt is not None]

    def body(*refs):
        r_refs = refs[:n_rows]
        c_refs = refs[n_rows:n_rows + n_consts]
        t_refs = refs[n_rows + n_consts:n_rows + n_consts + n_cts]
        o_refs = refs[n_rows + n_consts + n_cts:]
        gr_refs, gc_refs = o_refs[:len(wanted)], o_refs[len(wanted):]
        cs = [c[...] for c in c_refs]

        @pl.when(pl.program_id(0) == 0)
        def _():
            for a in gc_refs:
                a[...] = jnp.zeros_like(a)

        def step(s, carry):
            rows_s = pl.ds(pl.multiple_of(s * ROW_STEP, ROW_STEP), ROW_STEP)
            ins = [r[rows_s, :].astype(f32) for r in r_refs]
            _, vjp = jax.vjp(lambda *a: tuple(fn(*a)), *ins, *cs)
            grads = vjp(tuple(t[rows_s, :].astype(f32) for t in t_refs))
            for o, i in zip(gr_refs, wanted):
                o[rows_s, :] = grads[i].astype(o.dtype)
            return tuple(c + gval for c, gval in zip(carry, grads[n_rows:]))

        accs = lax.fori_loop(0, tr // ROW_STEP, step, tuple(jnp.zeros(c.shape, f32) for c in consts))
        for a, val in zip(gc_refs, accs):
            a[...] += val

    in_specs = [pl.BlockSpec((tr, r.shape[1]), lambda i: (i, 0)) for r in list(rows) + list(cts)]
    in_specs[n_rows:n_rows] = [pl.BlockSpec(c.shape, lambda i: (0, 0)) for c in consts]
    out_shape = [jax.ShapeDtypeStruct(rows[i].shape, grad_dtypes[i]) for i in wanted]
    out_shape += [jax.ShapeDtypeStruct(c.shape, f32) for c in consts]
    out_specs = [pl.BlockSpec((tr, rows[i].shape[1]), lambda i_: (i_, 0)) for i in wanted]
    out_specs += [pl.BlockSpec(c.shape, lambda i: (0, 0)) for c in consts]
    return pl.pallas_call(
        body, name=name, out_shape=out_shape, grid=(T // tr,), in_specs=in_specs, out_specs=out_specs,
        compiler_params=_cp("arbitrary"),
    )(*rows, *consts, *cts)


def _rms(x, g):
    return x * lax.rsqrt(jnp.mean(x * x, axis=-1, keepdims=True) + EPS) * g


def fn_prenorm(x, g):
    return (_rms(x, g),)


def fn_resnorm(x, m, g_post, g_pre):
    x_new = x + _rms(m, g_post)
    return x_new, _rms(x_new, g_pre)


def fn_res(x, m, g_post):
    return (x + _rms(m, g_post),)


def fn_final(x, m, target, g_post):
    err = x + _rms(m, g_post) - target
    n = err.shape[-1]
    return err / n, (0.5 / n) * jnp.sum(jnp.sum(err * err, axis=1, keepdims=True), axis=0, keepdims=True)


def fn_gate_norm(y, z, g):
    return (_rms(y * jax.nn.silu(z), g),)


CONV_ROWS = 256


def _conv_chunk(pre_ref, w, b, r, rb, K):
    t0 = pl.multiple_of(r * rb, rb)
    halo_at = pl.multiple_of(jnp.maximum(t0 - SUBLANES, 0), SUBLANES)
    halo = jnp.where(r > 0, pre_ref[pl.ds(halo_at, SUBLANES), :], 0.0)
    main = pre_ref[pl.ds(t0, rb), :]
    ext = jnp.concatenate([halo, main], axis=0)
    shifted = [main if k == K - 1 else pltpu.roll(ext, K - 1 - k, 0)[SUBLANES:] for k in range(K)]
    conv = b
    for k in range(K):
        conv = conv + w[k:k + 1] * shifted[k]
    return conv, shifted


def conv_fwd(streams, epilogue, out_cols, out_dtype, K, name):
    T = streams[0][0].shape[0]
    rb = min(T, CONV_ROWS)
    S = len(streams)

    def body(*refs):
        pre_refs, w_refs, b_refs, o_ref = refs[:S], refs[S:2 * S], refs[2 * S:3 * S], refs[3 * S]
        ws = [w[...] for w in w_refs]
        bs = [b[...] for b in b_refs]

        def step(r, carry):
            convs = [_conv_chunk(pre_refs[s], ws[s], bs[s], r, rb, K)[0] for s in range(S)]
            o_ref[pl.ds(pl.multiple_of(r * rb, rb), rb), :] = epilogue(*convs).astype(o_ref.dtype)
            return carry

        lax.fori_loop(0, T // rb, step, 0)

    in_specs = [pl.BlockSpec((T, LANES), functools.partial(lambda off, i: (0, off + i), st[1])) for st in streams]
    in_specs += [pl.BlockSpec((K, LANES), functools.partial(lambda off, i: (0, off + i), st[4])) for st in streams]
    in_specs += [pl.BlockSpec((1, LANES), functools.partial(lambda off, i: (0, off + i), st[4])) for st in streams]
    return pl.pallas_call(
        body, name=name, out_shape=jax.ShapeDtypeStruct((T, out_cols), out_dtype), grid=(out_cols // LANES,),
        in_specs=in_specs, out_specs=pl.BlockSpec((T, LANES), lambda i: (0, i)),
        compiler_params=_cp("parallel"),
    )(*[st[0] for st in streams], *[st[2] for st in streams], *[st[3] for st in streams])


def conv_bwd(streams, epilogue, dout, K, name):
    T, cols = dout.shape
    rb = min(T, CONV_ROWS)
    S = len(streams)

    def body(*refs):
        pre_refs, w_refs, b_refs, dout_ref = refs[:S], refs[S:2 * S], refs[2 * S:3 * S], refs[3 * S]
        o = refs[3 * S + 1:]
        dpre_refs, dw_refs, db_refs, scr = o[:S], o[S:2 * S], o[2 * S:3 * S], o[3 * S:]
        ws = [w[...] for w in w_refs]
        bs = [b[...] for b in b_refs]
        for s in range(S):
            scr[s][pl.ds(T, SUBLANES), :] = jnp.zeros((SUBLANES, LANES), f32)

        def phase1(r, carry):
            rows = pl.ds(pl.multiple_of(r * rb, rb), rb)
            convs, shifted = zip(*[_conv_chunk(pre_refs[s], ws[s], bs[s], r, rb, K) for s in range(S)])
            _, vjp = jax.vjp(epilogue, *convs)
            dconvs = vjp(dout_ref[rows, :].astype(f32))
            new = []
            for s in range(S):
                scr[s][rows, :] = dconvs[s]
                sums = [jnp.sum(dconvs[s] * shifted[s][k], axis=0, keepdims=True) for k in range(K)]
                sums.append(jnp.sum(dconvs[s], axis=0, keepdims=True))
                new.append(tuple(c + v for c, v in zip(carry[s], sums)))
            return tuple(new)

        zero = tuple(tuple(jnp.zeros((1, LANES), f32) for _ in range(K + 1)) for _ in range(S))
        sums = lax.fori_loop(0, T // rb, phase1, zero)
        tap = lax.broadcasted_iota(jnp.int32, (K, LANES), 0)
        for s in range(S):
            dw = jnp.zeros((K, LANES), f32)
            for k in range(K):
                dw = jnp.where(tap == k, sums[s][k], dw)
            dw_refs[s][...] = dw
            db_refs[s][...] = sums[s][K]

        def phase2(r, carry):
            t0 = pl.multiple_of(r * rb, rb)
            for s in range(S):
                ext = scr[s][pl.ds(t0, rb + SUBLANES), :]
                dpre = ws[s][K - 1:K] * ext[:rb]
                for k in range(K - 1):
                    j = K - 1 - k
                    dpre = dpre + ws[s][k:k + 1] * pltpu.roll(ext, rb + SUBLANES - j, 0)[:rb]
                dpre_refs[s][pl.ds(t0, rb), :] = dpre.astype(dpre_refs[s].dtype)
            return carry

        lax.fori_loop(0, T // rb, phase2, 0)

    in_specs = [pl.BlockSpec((T, LANES), functools.partial(lambda off, i: (0, off + i), st[1])) for st in streams]
    in_specs += [pl.BlockSpec((K, LANES), functools.partial(lambda off, i: (0, off + i), st[4])) for st in streams]
    in_specs += [pl.BlockSpec((1, LANES), functools.partial(lambda off, i: (0, off + i), st[4])) for st in streams]
    in_specs += [pl.BlockSpec((T, LANES), lambda i: (0, i))]
    out_shape = [jax.ShapeDtypeStruct((T, cols), bf16)] * S
    out_shape += [jax.ShapeDtypeStruct((K, cols), f32)] * S + [jax.ShapeDtypeStruct((1, cols), f32)] * S
    out_specs = [pl.BlockSpec((T, LANES), lambda i: (0, i))] * S
    out_specs += [pl.BlockSpec((K, LANES), lambda i: (0, i))] * S + [pl.BlockSpec((1, LANES), lambda i: (0, i))] * S
    res = pl.pallas_call(
        body, name=name, out_shape=out_shape, grid=(cols // LANES,), in_specs=in_specs, out_specs=out_specs,
        scratch_shapes=[pltpu.VMEM((T + SUBLANES, LANES), f32)] * S,
        compiler_params=_cp("parallel"),
    )(*[st[0] for st in streams], *[st[2] for st in streams], *[st[3] for st in streams], dout)
    return res[:S], res[S:2 * S], res[2 * S:]


def epi_gelu_gate(cg, cu):
    return jax.nn.gelu(cg, approximate=True) * cu


def epi_silu(c):
    return jax.nn.silu(c)


XA_ROWS = 256


def _xa_fn(q, k, v):
    s = _nt(q.astype(bf16), k.astype(bf16)) * (q.shape[-1] ** -0.5)
    p = jax.nn.softmax(s, axis=-1)
    return _nn(p.astype(bf16), v.astype(bf16))


def xa_fwd(q, kv, name):
    T, W = q.shape
    M = kv.shape[0]
    H = W // LANES
    tr = min(T, XA_ROWS)

    def body(q_ref, k_ref, v_ref, o_ref):
        o_ref[...] = _xa_fn(q_ref[...].astype(f32), k_ref[...].astype(f32), v_ref[...].astype(f32)).astype(o_ref.dtype)

    return pl.pallas_call(
        body, name=name, out_shape=jax.ShapeDtypeStruct((T, W), bf16), grid=(T // tr, H),
        in_specs=[pl.BlockSpec((tr, LANES), lambda i, h: (i, h)),
                  pl.BlockSpec((M, LANES), lambda i, h: (0, h)),
                  pl.BlockSpec((M, LANES), lambda i, h: (0, H + h))],
        out_specs=pl.BlockSpec((tr, LANES), lambda i, h: (i, h)),
        compiler_params=_cp("parallel", "parallel"),
    )(q, kv, kv)


def xa_bwd(q, kv, do, name):
    T, W = q.shape
    M = kv.shape[0]
    H = W // LANES
    tr = min(T, XA_ROWS)

    def body(q_ref, k_ref, v_ref, do_ref, dq_ref, dk_ref, dv_ref):
        @pl.when(pl.program_id(1) == 0)
        def _():
            dk_ref[...] = jnp.zeros_like(dk_ref)
            dv_ref[...] = jnp.zeros_like(dv_ref)

        _, vjp = jax.vjp(_xa_fn, q_ref[...].astype(f32), k_ref[...].astype(f32), v_ref[...].astype(f32))
        dq, dk, dv = vjp(do_ref[...].astype(f32))
        dq_ref[...] = dq.astype(dq_ref.dtype)
        dk_ref[...] += dk
        dv_ref[...] += dv

    return pl.pallas_call(
        body, name=name,
        out_shape=[jax.ShapeDtypeStruct((T, W), bf16), jax.ShapeDtypeStruct((M, W), f32), jax.ShapeDtypeStruct((M, W), f32)],
        grid=(H, T // tr),
        in_specs=[pl.BlockSpec((tr, LANES), lambda h, i: (i, h)),
                  pl.BlockSpec((M, LANES), lambda h, i: (0, h)),
                  pl.BlockSpec((M, LANES), lambda h, i: (0, H + h)),
                  pl.BlockSpec((tr, LANES), lambda h, i: (i, h))],
        out_specs=[pl.BlockSpec((tr, LANES), lambda h, i: (i, h)),
                   pl.BlockSpec((M, LANES), lambda h, i: (0, h)),
                   pl.BlockSpec((M, LANES), lambda h, i: (0, h))],
        compiler_params=_cp("parallel", "arbitrary"),
    )(q, kv, kv, do)


def _sgu_norm_fn(v_pre, g, b):
    v = jax.nn.gelu(v_pre, approximate=True)
    mu = jnp.mean(v, axis=-1, keepdims=True)
    vc = v - mu
    return vc * lax.rsqrt(jnp.mean(vc * vc, axis=-1, keepdims=True) + EPS) * g + b


def _sgu_mix_fn(u_pre, vn, w, b):
    q = w.shape[0]
    tril = lax.broadcasted_iota(jnp.int32, (q, q), 0) >= lax.broadcasted_iota(jnp.int32, (q, q), 1)
    mixed = _nn(jnp.where(tril, w, 0.0).astype(bf16), vn.astype(bf16)) + b
    return jax.nn.gelu(u_pre, approximate=True) * mixed


def _sgu_norm_phase(v_ref, g, b, vn_ref):
    def step(s, carry):
        rows = pl.ds(pl.multiple_of(s * ROW_STEP, ROW_STEP), ROW_STEP)
        vn_ref[rows, :] = _sgu_norm_fn(v_ref[rows, :], g, b)
        return carry

    lax.fori_loop(0, CHUNK // ROW_STEP, step, 0)


def sgu_fwd(uv_pre, norm_g, norm_b, w_sp, b_sp, name):
    T, W2 = uv_pre.shape
    W = W2 // 2
    G = w_sp.shape[0]
    gw = W // G

    def body(u_ref, v_ref, g_ref, b_ref, ws_ref, bs_ref, o_ref, vn_ref):
        _sgu_norm_phase(v_ref, g_ref[...], b_ref[...], vn_ref)

        def group(gi, carry):
            cols = pl.ds(pl.multiple_of(gi * gw, LANES), gw)
            o_ref[:, cols] = _sgu_mix_fn(u_ref[:, cols], vn_ref[:, cols], ws_ref[gi], bs_ref[gi]).astype(o_ref.dtype)
            return carry

        lax.fori_loop(0, G, group, 0)

    full = lambda a: pl.BlockSpec(a.shape, lambda c: (0,) * a.ndim)
    return pl.pallas_call(
        body, name=name, out_shape=jax.ShapeDtypeStruct((T, W), bf16), grid=(T // CHUNK,),
        in_specs=[pl.BlockSpec((CHUNK, W), lambda c: (c, 0)), pl.BlockSpec((CHUNK, W), lambda c: (c, 1)),
                  full(norm_g), full(norm_b), full(w_sp), full(b_sp)],
        out_specs=pl.BlockSpec((CHUNK, W), lambda c: (c, 0)),
        scratch_shapes=[pltpu.VMEM((CHUNK, W), f32)],
        compiler_params=_cp("parallel"),
    )(uv_pre, uv_pre, norm_g, norm_b, w_sp, b_sp)


def sgu_bwd(uv_pre, norm_g, norm_b, w_sp, b_sp, dout, name):
    T, W2 = uv_pre.shape
    W = W2 // 2
    G = w_sp.shape[0]
    gw = W // G

    def body(u_ref, v_ref, g_ref, b_ref, ws_ref, bs_ref, do_ref, duv_ref, dg_ref, db_ref, dws_ref, dbs_ref,
             vn_ref, dvn_ref):
        @pl.when(pl.program_id(0) == 0)
        def _():
            for a in (dg_ref, db_ref, dws_ref, dbs_ref):
                a[...] = jnp.zeros_like(a)

        g, b = g_ref[...], b_ref[...]
        _sgu_norm_phase(v_ref, g, b, vn_ref)

        def group(gi, carry):
            cols = pl.ds(pl.multiple_of(gi * gw, LANES), gw)
            _, vjp = jax.vjp(_sgu_mix_fn, u_ref[:, cols], vn_ref[:, cols], ws_ref[gi], bs_ref[gi])
            du, dvn, dw, dbias = vjp(do_ref[:, cols])
            duv_ref[:, cols] = du.astype(duv_ref.dtype)
            dvn_ref[:, cols] = dvn
            dws_ref[gi] += dw
            dbs_ref[gi] += dbias
            return carry

        lax.fori_loop(0, G, group, 0)

        def step(s, carry):
            rows = pl.ds(pl.multiple_of(s * ROW_STEP, ROW_STEP), ROW_STEP)
            _, vjp = jax.vjp(_sgu_norm_fn, v_ref[rows, :], g, b)
            dv, dg, dbn = vjp(dvn_ref[rows, :])
            duv_ref[rows, pl.ds(W, W)] = dv.astype(duv_ref.dtype)
            return carry[0] + dg, carry[1] + dbn

        dg, dbn = lax.fori_loop(0, CHUNK // ROW_STEP, step, (jnp.zeros((1, W), f32), jnp.zeros((1, W), f32)))
        dg_ref[...] += dg
        db_ref[...] += dbn

    full = lambda a: pl.BlockSpec(a.shape, lambda c: (0,) * a.ndim)
    return pl.pallas_call(
        body, name=name,
        out_shape=[jax.ShapeDtypeStruct((T, W2), bf16), jax.ShapeDtypeStruct((1, W), f32), jax.ShapeDtypeStruct((1, W), f32),
                   jax.ShapeDtypeStruct(w_sp.shape, f32), jax.ShapeDtypeStruct(b_sp.shape, f32)],
        grid=(T // CHUNK,),
        in_specs=[pl.BlockSpec((CHUNK, W), lambda c: (c, 0)), pl.BlockSpec((CHUNK, W), lambda c: (c, 1)),
                  full(norm_g), full(norm_b), full(w_sp), full(b_sp), pl.BlockSpec((CHUNK, W), lambda c: (c, 0))],
        out_specs=[pl.BlockSpec((CHUNK, W2), lambda c: (c, 0)), full(norm_g), full(norm_b), full(w_sp), full(b_sp)],
        scratch_shapes=[pltpu.VMEM((CHUNK, W), f32), pltpu.VMEM((CHUNK, W), f32)],
        compiler_params=_cp("arbitrary"),
    )(uv_pre, uv_pre, norm_g, norm_b, w_sp, b_sp, dout)


SB_SUM_COLS = 256


def _sb_key_group(T):
    return 512 if T % 512 == 0 else T


def _sb_query_rows(T):
    return 256 if T % 256 == 0 else CHUNK


def _sb_block(q, k, i, g):
    qb, kg = q.shape[0], k.shape[0]
    z = _nt(q, k) * (q.shape[-1] ** -0.5)
    t_idx = i * qb + lax.broadcasted_iota(jnp.int32, (qb, kg), 0)
    s_idx = g * kg + lax.broadcasted_iota(jnp.int32, (qb, kg), 1)
    valid = s_idx < t_idx
    sp = jnp.log1p(jnp.exp(-jnp.abs(z)))
    log_beta = jnp.minimum(z, 0.0) - sp
    log_1mb = jnp.where(valid, -jnp.maximum(z, 0.0) - sp, 0.0)
    return z, valid, log_beta, log_1mb


def _order_matrix(later):
    r = lax.broadcasted_iota(jnp.int32, (SB_SUM_COLS, SB_SUM_COLS), 0)
    c = lax.broadcasted_iota(jnp.int32, (SB_SUM_COLS, SB_SUM_COLS), 1)
    return (r > c if later else r < c).astype(bf16)


def _masked_sums(parts, order):
    terms = []
    for x in parts:
        hi = x.astype(bf16)
        rest = x - hi.astype(f32)
        mid = rest.astype(bf16)
        terms += [hi, mid, (rest - mid.astype(f32)).astype(bf16)]
    rows = parts[0].shape[0]
    prod = _nn(jnp.concatenate(terms, axis=0), order)
    piece = lambda n: prod[n * rows:(n + 1) * rows]
    return [piece(3 * p) + piece(3 * p + 1) + piece(3 * p + 2) for p in range(len(parts))]


def sb_fwd(qkv, name):
    T = qkv.shape[0]
    H = qkv.shape[1] // (3 * LANES)
    qb, kg = _sb_query_rows(T), _sb_key_group(T)
    halves = kg // SB_SUM_COLS

    def body(q_ref, k_ref, v_ref, o_ref, tot_ref):
        i = pl.program_id(1)
        q = q_ref[...]
        later = _order_matrix(True)
        n_groups = (i * qb + qb + kg - 1) // kg

        def step(gg, carry):
            acc, run = carry
            g = n_groups - 1 - gg
            rows = pl.ds(pl.multiple_of(g * kg, kg), kg)
            _, valid, log_beta, log_1mb = _sb_block(q, k_ref[rows, :], i, g)
            parts = [log_1mb[:, hh * SB_SUM_COLS:(hh + 1) * SB_SUM_COLS] for hh in range(halves)]
            tails = _masked_sums(parts, later)
            for hh in reversed(range(halves)):
                tails[hh] = tails[hh] + run
                run = run + jnp.sum(parts[hh], axis=1, keepdims=True)
            a = jnp.where(valid, jnp.exp(log_beta + jnp.concatenate(tails, axis=1)), 0.0)
            return acc + _nn(a.astype(bf16), v_ref[rows, :]), run

        acc, run = lax.fori_loop(0, n_groups, step, (jnp.zeros((qb, LANES), f32), jnp.zeros((qb, 1), f32)))
        o_ref[...] = acc.astype(o_ref.dtype)
        tot_ref[...] = run

    return pl.pallas_call(
        body, name=name,
        out_shape=[jax.ShapeDtypeStruct((T, H * LANES), bf16), jax.ShapeDtypeStruct((H, T, 1), f32)],
        grid=(H, T // qb),
        in_specs=[pl.BlockSpec((qb, LANES), lambda h, i: (i, h)),
                  pl.BlockSpec((T, LANES), lambda h, i: (0, H + h)),
                  pl.BlockSpec((T, LANES), lambda h, i: (0, 2 * H + h))],
        out_specs=[pl.BlockSpec((qb, LANES), lambda h, i: (i, h)),
                   pl.BlockSpec((None, qb, 1), lambda h, i: (h, i, 0))],
        compiler_params=_cp("parallel", "parallel"),
    )(qkv, qkv, qkv)


def sb_bwd(qkv, tot, do, name):
    T = qkv.shape[0]
    H = qkv.shape[1] // (3 * LANES)
    qb, kg = _sb_query_rows(T), _sb_key_group(T)
    halves = kg // SB_SUM_COLS

    def body(q_ref, k_ref, v_ref, tot_ref, do_ref, dq_ref, dk_ref, dv_ref):
        i = pl.program_id(1)

        @pl.when(i == 0)
        def _():
            dk_ref[...] = jnp.zeros_like(dk_ref)
            dv_ref[...] = jnp.zeros_like(dv_ref)

        q = q_ref[...]
        do = do_ref[...].astype(bf16)
        tot = tot_ref[...]
        later, earlier = _order_matrix(True), _order_matrix(False)
        scale = q.shape[-1] ** -0.5
        n_groups = (i * qb + qb + kg - 1) // kg

        def step(g, carry):
            dq, before, d_run = carry
            rows = pl.ds(pl.multiple_of(g * kg, kg), kg)
            k, v = k_ref[rows, :], v_ref[rows, :]
            z, valid, log_beta, log_1mb = _sb_block(q, k, i, g)
            parts = [log_1mb[:, hh * SB_SUM_COLS:(hh + 1) * SB_SUM_COLS] for hh in range(halves)]
            tails = _masked_sums(parts, later)
            for hh in range(halves):
                before = before + jnp.sum(parts[hh], axis=1, keepdims=True)
                tails[hh] = tails[hh] + (tot - before)
            a = jnp.where(valid, jnp.exp(log_beta + jnp.concatenate(tails, axis=1)), 0.0)
            d_e = _nt(do, v) * a
            parts = [d_e[:, hh * SB_SUM_COLS:(hh + 1) * SB_SUM_COLS] for hh in range(halves)]
            d_l1 = _masked_sums(parts, earlier)
            for hh in range(halves):
                d_l1[hh] = d_l1[hh] + d_run
                d_run = d_run + jnp.sum(parts[hh], axis=1, keepdims=True)
            d_l1 = jnp.where(valid, jnp.concatenate(d_l1, axis=1), 0.0)
            sig = jax.nn.sigmoid(z)
            dz = ((d_e * (1.0 - sig) - d_l1 * sig) * scale).astype(bf16)
            dk_ref[rows, :] += _tn(dz, q)
            dv_ref[rows, :] += _tn(a.astype(bf16), do)
            return dq + _nn(dz, k), before, d_run

        zero_col = jnp.zeros((qb, 1), f32)
        dq, _, _ = lax.fori_loop(0, n_groups, step, (jnp.zeros((qb, LANES), f32), zero_col, zero_col))
        dq_ref[...] = dq.astype(dq_ref.dtype)

    W = H * LANES
    return pl.pallas_call(
        body, name=name,
        out_shape=[jax.ShapeDtypeStruct((T, W), bf16), jax.ShapeDtypeStruct((T, W), f32), jax.ShapeDtypeStruct((T, W), f32)],
        grid=(H, T // qb),
        in_specs=[pl.BlockSpec((qb, LANES), lambda h, i: (i, h)),
                  pl.BlockSpec((T, LANES), lambda h, i: (0, H + h)),
                  pl.BlockSpec((T, LANES), lambda h, i: (0, 2 * H + h)),
                  pl.BlockSpec((None, qb, 1), lambda h, i: (h, i, 0)),
                  pl.BlockSpec((qb, LANES), lambda h, i: (i, h))],
        out_specs=[pl.BlockSpec((qb, LANES), lambda h, i: (i, h)),
                   pl.BlockSpec((T, LANES), lambda h, i: (0, h)),
                   pl.BlockSpec((T, LANES), lambda h, i: (0, h))],
        compiler_params=_cp("parallel", "arbitrary"),
    )(qkv, qkv, qkv, tot, do)


def _softplus(x):
    return jnp.maximum(x, 0.0) + jnp.log1p(jnp.exp(-jnp.abs(x)))


def _ssd_chunk_fn(head0, xs, b_mat, c_mat, dt_raw, dt_bias, a_log, d_skip, prev):
    q = dt_raw.shape[0]
    lane = lax.broadcasted_iota(jnp.int32, (q, LANES), 1)
    sub = lax.broadcasted_iota(jnp.int32, (q, LANES), 0)
    causal = sub >= lane
    dt = _softplus(dt_raw + dt_bias)
    a_cum = _nn(causal.astype(f32), dt * (-jnp.exp(a_log)), HIGHEST)
    a_cum_t = a_cum.T
    cb = _nt(c_mat.astype(bf16), b_mat.astype(bf16))
    bm, cm = b_mat.astype(bf16), c_mat.astype(bf16)
    ys, new = [], []
    for r in range(len(xs)):
        in_lane, in_sub = lane == head0 + r, sub == head0 + r
        col_a = jnp.sum(jnp.where(in_lane, a_cum, 0.0), axis=1, keepdims=True)
        row_a = jnp.sum(jnp.where(in_sub, a_cum_t, 0.0), axis=0, keepdims=True)
        col_dt = jnp.sum(jnp.where(in_lane, dt, 0.0), axis=1, keepdims=True)
        skip = jnp.sum(jnp.where(in_lane[:1], d_skip, 0.0), axis=1, keepdims=True)
        a_last = jnp.sum(jnp.where(sub[:, :1] == q - 1, col_a, 0.0), axis=0, keepdims=True)
        decay_in = jnp.exp(jnp.where(causal, col_a - row_a, -jnp.inf))
        xdt = xs[r] * col_dt
        y_diag = _nn((cb * decay_in).astype(bf16), xdt.astype(bf16))
        y_off = _nt(cm, prev[r].astype(bf16)) * jnp.exp(col_a)
        ys.append(y_diag + y_off + xs[r] * skip)
        state = _tn((xdt * jnp.exp(a_last - col_a)).astype(bf16), bm)
        new.append(prev[r] * jnp.exp(a_last) + state)
    return ys, new


def _ssd_specs(T, G, reverse):
    nc = T // CHUNK
    R, P, N = SSD_HEADS_PER_GROUP, SSD_HEAD_DIM, SSD_STATE
    ch = (lambda c: nc - 1 - c) if reverse else (lambda c: c)
    xs = pl.BlockSpec((CHUNK, R * P), lambda g, c: (ch(c), g))
    bm = pl.BlockSpec((CHUNK, N), lambda g, c: (ch(c), G * R * P // N + g))
    cm = pl.BlockSpec((CHUNK, N), lambda g, c: (ch(c), G * R * P // N + G + g))
    dt = pl.BlockSpec((CHUNK, LANES), lambda g, c: (ch(c), 0))
    row = pl.BlockSpec((1, LANES), lambda g, c: (0, 0))
    st = pl.BlockSpec((None, None, R * P, N), lambda g, c: (g, ch(c), 0, 0))
    return nc, xs, bm, cm, dt, row, st


def ssd_fwd(xbc, dt_raw, dt_bias, a_log, d_skip, name):
    T = xbc.shape[0]
    R, P, N = SSD_HEADS_PER_GROUP, SSD_HEAD_DIM, SSD_STATE
    G = xbc.shape[1] // (R * P + 2 * N)
    nc, xs_s, bm_s, cm_s, dt_s, row_s, st_s = _ssd_specs(T, G, False)

    def body(xs_ref, b_ref, c_ref, dt_ref, bias_ref, alog_ref, skip_ref, y_ref, st_ref, state):
        @pl.when(pl.program_id(1) == 0)
        def _():
            state[...] = jnp.zeros_like(state)

        x = xs_ref[...]
        xs = [x[:, r * P:(r + 1) * P] for r in range(R)]
        prev = [state[r] for r in range(R)]
        ys, new = _ssd_chunk_fn(pl.program_id(0) * R, xs, b_ref[...], c_ref[...], dt_ref[...],
                                bias_ref[...], alog_ref[...], skip_ref[...], prev)
        y_ref[...] = jnp.concatenate(ys, axis=1)
        for r in range(R):
            st_ref[pl.ds(r * P, P), :] = prev[r]
            state[r] = new[r]

    return pl.pallas_call(
        body, name=name,
        out_shape=[jax.ShapeDtypeStruct((T, G * R * P), f32), jax.ShapeDtypeStruct((G, nc, R * P, N), f32)],
        grid=(G, nc), in_specs=[xs_s, bm_s, cm_s, dt_s, row_s, row_s, row_s], out_specs=[xs_s, st_s],
        scratch_shapes=[pltpu.VMEM((R, P, N), f32)],
        compiler_params=_cp("parallel", "arbitrary"),
    )(xbc, xbc, xbc, dt_raw, dt_bias, a_log, d_skip)


def ssd_bwd(xbc, dt_raw, dt_bias, a_log, d_skip, states, dy, name):
    T = xbc.shape[0]
    R, P, N = SSD_HEADS_PER_GROUP, SSD_HEAD_DIM, SSD_STATE
    G = xbc.shape[1] // (R * P + 2 * N)
    nc, xs_s, bm_s, cm_s, dt_s, row_s, st_s = _ssd_specs(T, G, True)

    def body(xs_ref, b_ref, c_ref, dt_ref, bias_ref, alog_ref, skip_ref, st_ref, dy_ref,
             dx_ref, db_ref, dc_ref, ddt_ref, dbias_ref, dalog_ref, dskip_ref, dstate):
        g, c = pl.program_id(0), pl.program_id(1)

        @pl.when(c == 0)
        def _():
            dstate[...] = jnp.zeros_like(dstate)

        @pl.when((c == 0) & (g == 0))
        def _():
            for a in (dbias_ref, dalog_ref, dskip_ref):
                a[...] = jnp.zeros_like(a)

        x, dyv = xs_ref[...], dy_ref[...]
        xs = [x[:, r * P:(r + 1) * P] for r in range(R)]
        prev = [st_ref[pl.ds(r * P, P), :] for r in range(R)]
        _, vjp = jax.vjp(functools.partial(_ssd_chunk_fn, g * R), xs, b_ref[...], c_ref[...], dt_ref[...],
                         bias_ref[...], alog_ref[...], skip_ref[...], prev)
        dxs, dbm, dcm, ddt, dbias, dalog, dskip, dprev = vjp(
            ([dyv[:, r * P:(r + 1) * P] for r in range(R)], [dstate[r] for r in range(R)]))
        dx_ref[...] = jnp.concatenate(dxs, axis=1)
        db_ref[...] = dbm
        dc_ref[...] = dcm
        ddt_ref[...] = ddt
        dbias_ref[...] += dbias
        dalog_ref[...] += dalog
        dskip_ref[...] += dskip
        for r in range(R):
            dstate[r] = dprev[r]

    small = pl.BlockSpec((CHUNK, N), lambda g, c: (nc - 1 - c, g))
    return pl.pallas_call(
        body, name=name,
        out_shape=[jax.ShapeDtypeStruct((T, G * R * P), f32), jax.ShapeDtypeStruct((T, G * N), f32),
                   jax.ShapeDtypeStruct((T, G * N), f32), jax.ShapeDtypeStruct((G, T, LANES), f32)]
        + [jax.ShapeDtypeStruct((1, LANES), f32)] * 3,
        grid=(G, nc), in_specs=[xs_s, bm_s, cm_s, dt_s, row_s, row_s, row_s, st_s, xs_s],
        out_specs=[xs_s, small, small, pl.BlockSpec((None, CHUNK, LANES), lambda g, c: (g, nc - 1 - c, 0)),
                   row_s, row_s, row_s],
        scratch_shapes=[pltpu.VMEM((R, P, N), f32)],
        compiler_params=_cp("arbitrary", "arbitrary"),
    )(xbc, xbc, xbc, dt_raw, dt_bias, a_log, d_skip, states, dy)


WEIGHTS = ["ln_mix_pre", "ln_mix_post", "ln_mem", "ln_xa_pre", "ln_xa_post", "ln_ffn_pre", "ln_ffn_post",
           "xa_wq", "xa_wkv", "xa_wo", "ffn_w_in", "ffn_conv_w", "ffn_conv_b", "ffn_w_out",
           "ssd_w_in", "ssd_conv_w", "ssd_conv_b", "ssd_dt_bias", "ssd_a_log", "ssd_d", "ssd_norm", "ssd_w_out",
           "sg_w_in", "sg_v_norm_g", "sg_v_norm_b", "sg_w_spatial", "sg_b_spatial", "sg_w_out", "sb_w_qkv", "sb_w_out"]
BIG = ["xa_wq", "xa_wkv", "xa_wo", "ffn_w_in", "ffn_w_out", "ssd_w_in", "ssd_w_out", "sg_w_in", "sg_w_out",
       "sb_w_qkv", "sb_w_out"]
SMALL_SHARDED = ["ffn_conv_w", "ssd_conv_w", "ssd_conv_b", "ssd_norm"]
REPLICATED = [n for n in WEIGHTS if n not in BIG and n not in SMALL_SHARDED]


def fn_xpre(x, g):
    return x, _rms(x, g)


def _pack_rows(arrs, lead=0):
    head = arrs[0].shape[:lead]
    flat = jnp.concatenate([a.reshape(head + (-1,)) for a in arrs], axis=-1)
    n = flat.shape[-1]
    rows = -(-n // (SUBLANES * LANES)) * SUBLANES
    flat = jnp.pad(flat, [(0, 0)] * lead + [(0, rows * LANES - n)])
    return flat.reshape(head + (rows, LANES))


def _unpack_rows(packed, shapes):
    head = packed.shape[:-2]
    flat = packed.reshape(head + (-1,))
    out, off = [], 0
    for shp in shapes:
        n = int(np.prod(shp, dtype=np.int64))
        out.append(flat[..., off:off + n].reshape(head + tuple(shp)))
        off += n
    return out


def _merge_last(a8):
    return jnp.moveaxis(a8, 0, -2).reshape(a8.shape[1:-1] + (N_DEV * a8.shape[-1],))


def _split_last(a):
    return jnp.moveaxis(a.reshape(a.shape[:-1] + (N_DEV, a.shape[-1] // N_DEV)), -2, 0)


def kernel(x, mem, ln_mix_pre, ln_mix_post, ln_mem, ln_xa_pre, ln_xa_post, ln_ffn_pre, ln_ffn_post, xa_wq, xa_wkv, xa_wo, ffn_w_in, ffn_conv_w, ffn_conv_b, ffn_w_out, ssd_w_in, ssd_conv_w, ssd_conv_b, ssd_dt_bias, ssd_a_log, ssd_d, ssd_norm, ssd_w_out, sg_w_in, sg_v_norm_g, sg_v_norm_b, sg_w_spatial, sg_b_spatial, sg_w_out, sb_w_qkv, sb_w_out, loss_target, m_ln_mix_pre, m_ln_mix_post, m_ln_mem, m_ln_xa_pre, m_ln_xa_post, m_ln_ffn_pre, m_ln_ffn_post, m_xa_wq, m_xa_wkv, m_xa_wo, m_ffn_w_in, m_ffn_conv_w, m_ffn_conv_b, m_ffn_w_out, m_ssd_w_in, m_ssd_conv_w, m_ssd_conv_b, m_ssd_dt_bias, m_ssd_a_log, m_ssd_d, m_ssd_norm, m_ssd_w_out, m_sg_w_in, m_sg_v_norm_g, m_sg_v_norm_b, m_sg_w_spatial, m_sg_b_spatial, m_sg_w_out, m_sb_w_qkv, m_sb_w_out, v_ln_mix_pre, v_ln_mix_post, v_ln_mem, v_ln_xa_pre, v_ln_xa_post, v_ln_ffn_pre, v_ln_ffn_post, v_xa_wq, v_xa_wkv, v_xa_wo, v_ffn_w_in, v_ffn_conv_w, v_ffn_conv_b, v_ffn_w_out, v_ssd_w_in, v_ssd_conv_w, v_ssd_conv_b, v_ssd_dt_bias, v_ssd_a_log, v_ssd_d, v_ssd_norm, v_ssd_w_out, v_sg_w_in, v_sg_v_norm_g, v_sg_v_norm_b, v_sg_w_spatial, v_sg_b_spatial, v_sg_w_out, v_sb_w_qkv, v_sb_w_out):
    p = dict(locals())
    x, mem, target = p["x"][0], p["mem"][0], p["loss_target"][0]
    T, D = x.shape
    depth = ln_mix_pre.shape[0]
    me = 4 * lax.axis_index("x") + 2 * lax.axis_index("y") + lax.axis_index("c")

    def gather(w, name):
        return all_gather(w.astype(bf16), name)

    small8 = all_gather(_pack_rows([p[n] for n in SMALL_SHARDED]), "ag_small")
    full = {n: _merge_last(a) for n, a in zip(SMALL_SHARDED, _unpack_rows(small8, [p[n].shape for n in SMALL_SHARDED]))}

    grads = {n: {} for n in WEIGHTS}

    def ssd(h, j):
        g_in = gather(p["ssd_w_in"][j], "ag_ssd_in")
        w_full = jnp.moveaxis(g_in, 0, 1).reshape(D, -1)
        w_out = gather(p["ssd_w_out"][j], "ag_ssd_out").reshape(1, -1, D)
        d_inner, conv_dim, heads = w_out.shape[1], full["ssd_conv_w"].shape[-1], ssd_dt_bias.shape[1]
        w_z, w_x = w_full[None, :, :d_inner], w_full[None, :, d_inner:d_inner + conv_dim]
        w_dt = jnp.pad(w_full[:, d_inner + conv_dim:], ((0, 0), (0, LANES - heads)))[None]
        lane_row = lambda a: jnp.pad(a[j:j + 1], ((0, 0), (0, LANES - heads)))
        bias, a_log, d_skip = lane_row(ssd_dt_bias), lane_row(ssd_a_log), lane_row(ssd_d)
        norm_g = full["ssd_norm"][j:j + 1]
        z = mm_nn(h, w_z, f32, "ssd_in_z")
        xbc_pre = mm_nn(h, w_x, f32, "ssd_in_x")
        dt_raw = mm_nn(h, w_dt, f32, "ssd_in_dt")
        streams = [(xbc_pre, 0, full["ssd_conv_w"][j], full["ssd_conv_b"][j:j + 1], 0)]
        xbc = conv_fwd(streams, epi_silu, conv_dim, f32, 4, "ssd_conv")
        y, states = ssd_fwd(xbc, dt_raw, bias, a_log, d_skip, "ssd_core")
        gated = row_fwd(fn_gate_norm, [y, z], [norm_g], [(d_inner, bf16)], "ssd_gate")[0]
        out = mm_nn(gated, w_out, f32, "ssd_out")

        def bwd(d_out):
            d_gated = mm_nt(d_out, w_out, f32, "ssd_out_dx")
            grads["ssd_w_out"][j] = mm_tn(gated, d_out, D, bf16, "ssd_out_dw").reshape(N_DEV, -1, D)
            dy, dz, d_norm = row_bwd(fn_gate_norm, [y, z], [norm_g], [d_gated], [f32, bf16], "ssd_gate_bwd")
            dxs, dbm, dcm, ddt_g, d_bias, d_alog, d_skipg = ssd_bwd(xbc, dt_raw, bias, a_log, d_skip, states, dy, "ssd_core_bwd")
            (dx_pre,), (d_cw,), (d_cb,) = conv_bwd(streams, epi_silu, jnp.concatenate([dxs, dbm, dcm], axis=1), 4, "ssd_conv_bwd")
            ddt = jnp.sum(ddt_g, axis=0).astype(bf16)
            dh = mm_nt(dz, w_z, f32, "ssd_in_z_dx")
            dh = mm_nt(dx_pre, w_x, f32, "ssd_in_x_dx", init=dh)
            dh = mm_nt(ddt, w_dt, f32, "ssd_in_dt_dx", init=dh)
            dw = jnp.concatenate([mm_tn(h, dz, d_inner, bf16, "ssd_in_z_dw")[0], mm_tn(h, dx_pre, conv_dim, bf16, "ssd_in_x_dw")[0],
                                  mm_tn(h, ddt, LANES, bf16, "ssd_in_dt_dw")[0][:, :heads]], axis=1)
            grads["ssd_w_in"][j] = _split_last(dw)
            grads["ssd_conv_w"][j], grads["ssd_conv_b"][j], grads["ssd_norm"][j] = d_cw, d_cb[0], d_norm[0]
            grads["ssd_dt_bias"][j], grads["ssd_a_log"][j], grads["ssd_d"][j] = d_bias[0, :heads], d_alog[0, :heads], d_skipg[0, :heads]
            return dh

        return out, bwd

    def sgu(h, j):
        w_in = gather(p["sg_w_in"][j], "ag_sg_in")
        w_out = gather(p["sg_w_out"][j], "ag_sg_out").reshape(1, -1, D)
        norm_g, norm_b = sg_v_norm_g[j:j + 1], sg_v_norm_b[j:j + 1]
        w_sp, b_sp = sg_w_spatial[j], sg_b_spatial[j][..., None]
        uv = mm_nn(h, w_in, f32, "sg_in")
        gated = sgu_fwd(uv, norm_g, norm_b, w_sp, b_sp, "sg_core")
        out = mm_nn(gated, w_out, f32, "sg_out")

        def bwd(d_out):
            d_gated = mm_nt(d_out, w_out, f32, "sg_out_dx")
            grads["sg_w_out"][j] = mm_tn(gated, d_out, D, bf16, "sg_out_dw").reshape(N_DEV, -1, D)
            duv, d_ng, d_nb, d_ws, d_bs = sgu_bwd(uv, norm_g, norm_b, w_sp, b_sp, d_gated, "sg_core_bwd")
            grads["sg_v_norm_g"][j], grads["sg_v_norm_b"][j] = d_ng[0], d_nb[0]
            grads["sg_w_spatial"][j], grads["sg_b_spatial"][j] = d_ws, d_bs[..., 0]
            grads["sg_w_in"][j] = mm_tn(h, duv, w_in.shape[2], bf16, "sg_in_dw")
            return mm_nt(duv, w_in, f32, "sg_in_dx")

        return out, bwd

    def stick(h, j):
        w_qkv = gather(p["sb_w_qkv"][j], "ag_sb_qkv")
        w_out = gather(p["sb_w_out"][j], "ag_sb_out").reshape(1, -1, D)
        qkv = mm_nn(h, w_qkv, bf16, "sb_qkv")
        o, tot = sb_fwd(qkv, "sb_core")
        out = mm_nn(o, w_out, f32, "sb_out")

        def bwd(d_out):
            d_o = mm_nt(d_out, w_out, f32, "sb_out_dx")
            grads["sb_w_out"][j] = mm_tn(o, d_out, D, bf16, "sb_out_dw").reshape(N_DEV, -1, D)
            dq, dk, dv = sb_bwd(qkv, tot, d_o, "sb_core_bwd")
            dqkv = jnp.concatenate([dq, dk.astype(bf16), dv.astype(bf16)], axis=1)
            grads["sb_w_qkv"][j] = mm_tn(h, dqkv, w_qkv.shape[2], bf16, "sb_qkv_dw")
            return mm_nt(dqkv, w_qkv, f32, "sb_qkv_dx")

        return out, bwd

    def cross(h, i):
        gain = ln_mem[i:i + 1]
        mem_n = row_fwd(fn_prenorm, [mem], [gain], [(D, bf16)], "mem_norm")[0]
        w_q = gather(p["xa_wq"][i], "ag_xa_q").reshape(1, D, -1)
        w_kv = gather(p["xa_wkv"][i], "ag_xa_kv").reshape(1, D, -1)
        w_o = gather(p["xa_wo"][i], "ag_xa_o")
        q = mm_nn(h, w_q, bf16, "xa_q")
        kv = mm_nn(mem_n, w_kv, bf16, "xa_kv")
        o = xa_fwd(q, kv, "xa_core")
        out = mm_nn(o, w_o, f32, "xa_out")

        def bwd(d_out):
            d_o = mm_nt(d_out, w_o, f32, "xa_out_dx")
            grads["xa_wo"][i] = mm_tn(o, d_out, w_o.shape[2], bf16, "xa_out_dw")
            dq, dk, dv = xa_bwd(q, kv, d_o, "xa_core_bwd")
            dkv = jnp.concatenate([dk, dv], axis=1).astype(bf16)
            grads["xa_wq"][i] = mm_tn(h, dq, w_q.shape[2], bf16, "xa_q_dw").reshape(N_DEV, -1, w_q.shape[2])
            grads["xa_wkv"][i] = mm_tn(mem_n, dkv, w_kv.shape[2], bf16, "xa_kv_dw").reshape(N_DEV, -1, w_kv.shape[2])
            d_mem_n = mm_nt(dkv, w_kv, f32, "xa_kv_dx")
            grads["ln_mem"][i] = row_bwd(fn_prenorm, [mem], [gain], [d_mem_n], [None], "mem_norm_bwd")[0][0]
            return mm_nt(dq, w_q, f32, "xa_q_dx")

        return out, bwd

    def ffn(h, i):
        w_in = gather(p["ffn_w_in"][i], "ag_ffn_in")
        w_out = gather(p["ffn_w_out"][i], "ag_ffn_out").reshape(1, -1, D)
        width = w_out.shape[1]
        conv_w, conv_b = full["ffn_conv_w"][i], ffn_conv_b[i:i + 1]
        gu = mm_nn(h, w_in, f32, "ffn_in")
        streams = [(gu, 0, conv_w, conv_b, 0), (gu, width // LANES, conv_w, conv_b, width // LANES)]
        act = conv_fwd(streams, epi_gelu_gate, width, bf16, 3, "ffn_gate")
        out = mm_nn(act, w_out, f32, "ffn_out")

        def bwd(d_out):
            d_act = mm_nt(d_out, w_out, f32, "ffn_out_dx")
            grads["ffn_w_out"][i] = mm_tn(act, d_out, D, bf16, "ffn_out_dw").reshape(N_DEV, -1, D)
            d_pre, d_cw, d_cb = conv_bwd(streams, epi_gelu_gate, d_act, 3, "ffn_gate_bwd")
            dgu = jnp.concatenate(d_pre, axis=1)
            grads["ffn_conv_w"][i], grads["ffn_conv_b"][i] = jnp.concatenate(d_cw, axis=1), jnp.concatenate(d_cb, axis=1)[0]
            grads["ffn_w_in"][i] = mm_tn(h, dgu, w_in.shape[2], bf16, "ffn_in_dw")
            return mm_nt(dgu, w_in, f32, "ffn_in_dx")

        return out, bwd

    n_sub = 3 * depth
    pre = [w[i:i + 1] for i in range(depth) for w in (ln_mix_pre, ln_xa_pre, ln_ffn_pre)]
    post = [w[i:i + 1] for i in range(depth) for w in (ln_mix_post, ln_xa_post, ln_ffn_post)]
    stream, outs, bwds = [x], [], []
    h = row_fwd(fn_prenorm, [x], [pre[0]], [(D, bf16)], "pre_norm")[0]
    for s in range(n_sub):
        i, t = divmod(s, 3)
        out, bwd = ((ssd, sgu, stick)[i % 3](h, i // 3) if t == 0 else cross(h, i) if t == 1 else ffn(h, i))
        outs.append(out)
        bwds.append(bwd)
        if s < n_sub - 1:
            x_new, h = row_fwd(fn_resnorm, [stream[s], out], [post[s], pre[s + 1]], [(D, f32), (D, bf16)], "res_norm")
            stream.append(x_new)
    dy, loss = row_fwd(fn_final, [stream[-1], outs[-1], target], [post[-1]], [(D, f32), (1, f32)], "loss_head", n_acc=1)
    loss = lax.psum(loss[0, 0], ("x", "y", "c"))

    d_pre, d_post = [None] * n_sub, [None] * n_sub
    dx, d_out, d_post[-1] = row_bwd(fn_res, [stream[-1], outs[-1]], [post[-1]], [dy], [f32, bf16], "res_bwd")
    for s in reversed(range(n_sub)):
        dh = bwds[s](d_out)
        if s > 0:
            dx, d_out, d_post[s - 1], d_pre[s] = row_bwd(
                fn_resnorm, [stream[s - 1], outs[s - 1]], [post[s - 1], pre[s]], [dx, dh], [f32, bf16], "res_norm_bwd")
        else:
            grad_x, d_pre[0] = row_bwd(fn_xpre, [x], [pre[0]], [dx, dh], [f32], "pre_norm_bwd")
    for t, kind in enumerate(("mix", "xa", "ffn")):
        for i in range(depth):
            grads["ln_%s_pre" % kind][i] = d_pre[3 * i + t][0]
            grads["ln_%s_post" % kind][i] = d_post[3 * i + t][0]

    def stacked(name):
        return jnp.stack([grads[name][l] for l in range(len(grads[name]))], axis=0)

    new = {}
    for name in BIG:
        parts = jnp.stack([reduce_scatter_parts(grads[name][l], "rs_" + name) for l in range(len(grads[name]))], axis=1)
        new[name] = adamw_sharded(p[name], parts, p["m_" + name], p["v_" + name], "adamw_" + name)

    rep_shapes = [p[n].shape for n in REPLICATED]
    g8 = all_gather(_pack_rows([stacked(n) for n in REPLICATED]), "ag_grad_rep")
    rep = adamw_summed8(_pack_rows([p[n] for n in REPLICATED]), g8, _pack_rows([p["m_" + n] for n in REPLICATED]),
                        _pack_rows([p["v_" + n] for n in REPLICATED]), "adamw_rep")
    for k, packed in enumerate(rep):
        for n, a in zip(REPLICATED, _unpack_rows(packed, rep_shapes)):
            new.setdefault(n, [None] * 4)[k] = a

    sh_shapes = [p[n].shape for n in SMALL_SHARDED]
    by_owner = _pack_rows([_split_last(stacked(n)) for n in SMALL_SHARDED], lead=1)
    mine8 = lax.dynamic_index_in_dim(all_gather(by_owner, "ag_grad_small"), me, axis=1, keepdims=False)
    sh = adamw_summed8(_pack_rows([p[n] for n in SMALL_SHARDED]), mine8, _pack_rows([p["m_" + n] for n in SMALL_SHARDED]),
                       _pack_rows([p["v_" + n] for n in SMALL_SHARDED]), "adamw_small")
    for k, packed in enumerate(sh):
        for n, a in zip(SMALL_SHARDED, _unpack_rows(packed, sh_shapes)):
            new.setdefault(n, [None] * 4)[k] = a

    return (loss, grad_x[None], *[new[n][0] for n in WEIGHTS], *[new[n][1] for n in WEIGHTS],
            *[new[n][2] for n in WEIGHTS], *[new[n][3] for n in WEIGHTS])
```

```python
import functools

import jax
import jax.numpy as jnp
import numpy as np
from jax import lax
from jax.experimental import pallas as pl
from jax.experimental.pallas import tpu as pltpu
from jax.experimental.pallas import tpu_sc as plsc

f32 = jnp.float32
bf16 = jnp.bfloat16
HIGHEST = lax.Precision.HIGHEST
MESH = pl.DeviceIdType.MESH

V7X_VMEM_BYTES = 64 * 1024 * 1024
VMEM_LIMIT = V7X_VMEM_BYTES * 3 // 4
LANES = 128
SUBLANES = 8
BF16_ROWS = 16

EPS = 1e-6
ADAM_LR = 0.001
ADAM_B1 = 0.9
ADAM_B2 = 0.999
ADAM_EPS = 1e-08
ADAM_WD = 0.01
ADAM_STEP = 10

N_DEV = 8
XA_HEADS = 4
SSD_HEADS_PER_GROUP = 8
SSD_HEAD_DIM = 64
SSD_STATE = 128
SSD_GROUPS = 8
CHUNK = 128


def _cp(*sem):
    return pltpu.CompilerParams(dimension_semantics=sem or None, vmem_limit_bytes=VMEM_LIMIT)


def _dot(a, b, dims, precision=None):
    return lax.dot_general(a, b, (dims, ((), ())), precision=precision, preferred_element_type=f32)


def _nn(a, b, precision=None):
    return _dot(a, b, ((1,), (0,)), precision)


def _nt(a, b, precision=None):
    return _dot(a, b, ((1,), (1,)), precision)


def _tn(a, b, precision=None):
    return _dot(a, b, ((0,), (0,)), precision)


def _largest_tile(n, cap, step):
    for t in range(min(n, cap) // step * step, 0, -step):
        if n % t == 0:
            return t
    return n


def all_gather(x, name):
    def body(x_ref, out_ref, send_sems, recv_sems, local_sem):
        mx, my, mc = lax.axis_index("x"), lax.axis_index("y"), lax.axis_index("c")
        me, sibling = (mx, my, mc), (mx, my, 1 - mc)
        chips = [(1 - mx, my), (mx, 1 - my), (1 - mx, 1 - my)]

        def slot(px, py, pc):
            return out_ref.at[4 * px + 2 * py + pc]

        def copy(k, block, to, src=None):
            return pltpu.make_async_remote_copy(
                src_ref=slot(*block) if src is None else src, dst_ref=slot(*block),
                send_sem=send_sems.at[k], recv_sem=recv_sems.at[k], device_id=to, device_id_type=MESH)

        mine = pltpu.make_async_copy(x_ref, slot(*me), local_sem)
        mine.start()
        first = [copy(0, me, sibling, src=x_ref)]
        first += [copy(1 + j, me, (*chip, mc), src=x_ref) for j, chip in enumerate(chips)]
        for cp in first:
            cp.start()
        passed = [copy(4 + j, (*chip, mc), sibling) for j, chip in enumerate(chips)]
        for j, chip in enumerate(chips):
            copy(1 + j, (*chip, mc), me).wait_recv()
            passed[j].start()
        copy(0, sibling, me).wait_recv()
        for j, chip in enumerate(chips):
            copy(4 + j, (*chip, 1 - mc), me).wait_recv()
        for cp in first + passed:
            cp.wait_send()
        mine.wait()

    return pl.pallas_call(
        body, name=name,
        out_shape=jax.ShapeDtypeStruct((N_DEV,) + x.shape, x.dtype),
        in_specs=[pl.BlockSpec(memory_space=pl.ANY)],
        out_specs=pl.BlockSpec(memory_space=pl.ANY),
        scratch_shapes=[pltpu.SemaphoreType.DMA((7,)), pltpu.SemaphoreType.DMA((7,)), pltpu.SemaphoreType.DMA(())],
    )(x)


AG_COLLECTIVE_ID = 1
RS_COLLECTIVE_ID = 2
SCATTER_LAG = 3
_SEQUENCER = dict(axis_name="sequencer", num_cores=1)
_HBM = pltpu.MemorySpace.HBM


def _peers():
    mx, my, mc = lax.axis_index("x"), lax.axis_index("y"), lax.axis_index("c")
    return [((mx + (k >> 2)) % 2, (my + ((k >> 1) & 1)) % 2, (mc + (k & 1)) % 2) for k in range(1, N_DEV)]


def _handshake_all(peers):
    barrier = pltpu.get_barrier_semaphore()
    for peer in peers:
        pl.semaphore_signal(barrier, inc=1, device_id=peer, device_id_type=MESH)
    pl.semaphore_wait(barrier, len(peers))


def _block_of(peer):
    return 4 * peer[0] + 2 * peer[1] + peer[2]


def sc_all_gather(x, name):
    x_ref = jax.new_ref(x, memory_space=_HBM)
    out_ref = jax.empty_ref(jax.ShapeDtypeStruct((N_DEV,) + x.shape, x.dtype), memory_space=_HBM)

    @pl.kernel(mesh=plsc.ScalarSubcoreMesh(**_SEQUENCER), name=name,
               scratch_types=(pltpu.SemaphoreType.DMA((N_DEV - 1,)), pltpu.SemaphoreType.DMA((N_DEV - 1,)),
                              pltpu.SemaphoreType.DMA(())),
               compiler_params=pltpu.CompilerParams(collective_id=AG_COLLECTIVE_ID))
    def launch(send_sems, recv_sems, local_sem):
        peers = _peers()
        _handshake_all(peers)
        me = 4 * lax.axis_index("x") + 2 * lax.axis_index("y") + lax.axis_index("c")
        mine = pltpu.make_async_copy(x_ref, out_ref.at[me], local_sem)
        mine.start()
        for k, peer in enumerate(peers):
            pltpu.make_async_remote_copy(src_ref=x_ref, dst_ref=out_ref.at[me], send_sem=send_sems.at[k],
                                         recv_sem=recv_sems.at[k], device_id=peer, device_id_type=MESH).start()
        for k, peer in enumerate(peers):
            pltpu.make_async_remote_copy(src_ref=x_ref, dst_ref=out_ref.at[_block_of(peer)], send_sem=send_sems.at[k],
                                         recv_sem=recv_sems.at[k], device_id=peer, device_id_type=MESH).wait()
        mine.wait()

    launch()
    return out_ref[...]


def sc_scatter_parts(g8, name):
    g_ref = jax.new_ref(g8, memory_space=_HBM)
    parts_ref = jax.empty_ref(jax.ShapeDtypeStruct(g8.shape, g8.dtype), memory_space=_HBM)

    @pl.kernel(mesh=plsc.ScalarSubcoreMesh(**_SEQUENCER), name=name,
               scratch_types=(pltpu.SemaphoreType.DMA((N_DEV - 1,)), pltpu.SemaphoreType.DMA((N_DEV - 1,)),
                              pltpu.SemaphoreType.DMA(())),
               compiler_params=pltpu.CompilerParams(collective_id=RS_COLLECTIVE_ID))
    def launch(send_sems, recv_sems, local_sem):
        peers = _peers()
        _handshake_all(peers)
        me = 4 * lax.axis_index("x") + 2 * lax.axis_index("y") + lax.axis_index("c")
        mine = pltpu.make_async_copy(g_ref.at[me], parts_ref.at[0], local_sem)
        mine.start()
        copies = [pltpu.make_async_remote_copy(src_ref=g_ref.at[_block_of(peer)], dst_ref=parts_ref.at[k + 1],
                                               send_sem=send_sems.at[k], recv_sem=recv_sems.at[k],
                                               device_id=peer, device_id_type=MESH) for k, peer in enumerate(peers)]
        for cp in copies:
            cp.start()
        for cp in copies:
            cp.wait()
        mine.wait()

    launch()
    return parts_ref[...]


def _as_lrc(a, lead):
    rest = a.shape[lead:]
    return a.reshape(a.shape[:lead] + (int(np.prod(rest[:-2], dtype=np.int64)),) + rest[-2:])


def _adam_math(w, g, m, v):
    m = ADAM_B1 * m + (1.0 - ADAM_B1) * g
    v = ADAM_B2 * v + (1.0 - ADAM_B2) * jnp.square(g)
    m_hat = m / (1.0 - ADAM_B1 ** ADAM_STEP)
    v_hat = v / (1.0 - ADAM_B2 ** ADAM_STEP)
    delta = -ADAM_LR * (m_hat / (jnp.sqrt(v_hat) + ADAM_EPS) + ADAM_WD * w)
    return delta, m, v


def adamw_sharded(w, parts, m, v, name):
    L, R, C = w.shape
    tr = _largest_tile(R, 64, BF16_ROWS)

    def body(w_ref, *refs):
        p_refs, (m_ref, v_ref, g_out, d_out, m_out, v_out) = refs[:L], refs[L:]
        for layer in range(L):
            @pl.when(pl.program_id(0) == layer)
            def _(p_ref=p_refs[layer]):
                g = p_ref[0].astype(f32)
                for k in range(1, N_DEV):
                    g = g + p_ref[k].astype(f32)
                delta, mn, vn = _adam_math(w_ref[...], g, m_ref[...], v_ref[...])
                g_out[...] = g
                d_out[...] = delta
                m_out[...] = mn
                v_out[...] = vn

    blk = pl.BlockSpec((None, tr, C), lambda l, r: (l, r, 0))
    part_specs = [pl.BlockSpec((N_DEV, tr, C), functools.partial(lambda layer, l, r: (0, jnp.where(l == layer, r, 0), 0), layer))
                  for layer in range(L)]
    return pl.pallas_call(
        body, name=name, out_shape=[jax.ShapeDtypeStruct(w.shape, f32)] * 4, grid=(L, R // tr),
        in_specs=[blk] + part_specs + [blk, blk], out_specs=[blk] * 4,
        compiler_params=_cp("arbitrary", "arbitrary"),
    )(w, *parts, m, v)


def adamw_summed8(w, g8, m, v, name):
    R, C = w.shape
    tr = _largest_tile(R, 512, SUBLANES)

    def body(w_ref, g_ref, m_ref, v_ref, g_out, d_out, m_out, v_out):
        g = g_ref[0]
        for d in range(1, N_DEV):
            g = g + g_ref[d]
        delta, mn, vn = _adam_math(w_ref[...], g, m_ref[...], v_ref[...])
        g_out[...] = g
        d_out[...] = delta
        m_out[...] = mn
        v_out[...] = vn

    blk = pl.BlockSpec((tr, C), lambda r: (r, 0))
    return pl.pallas_call(
        body, name=name, out_shape=[jax.ShapeDtypeStruct((R, C), f32)] * 4, grid=(R // tr,),
        in_specs=[blk, pl.BlockSpec((N_DEV, tr, C), lambda r: (0, r, 0)), blk, blk], out_specs=[blk] * 4,
        compiler_params=_cp("parallel"),
    )(w, g8, m, v)


def _tile_n(ns):
    if ns % 512 == 0:
        return 512
    if ns <= 1536:
        return ns
    return _largest_tile(ns, 512, LANES)


MM_VMEM_BUDGET = VMEM_LIMIT - 8 * 1024 * 1024


def _tiles_desc(n, cap, step):
    return [t for t in range(min(n, cap) // step * step, 0, -step) if n % t == 0] or [n]


def _mm_fits(in_tiles, out_tile, out_dtype, n_red, extra=0):
    rows, cols = out_tile
    total = sum(2 * 2 * r * c for r, c in in_tiles) + 2 * rows * cols * jnp.dtype(out_dtype).itemsize + extra
    if n_red > 1:
        total += 4 * rows * cols
    return total <= MM_VMEM_BUDGET


def _reduce_into(o_ref, acc, part, first, last):
    if acc is None:
        o_ref[...] = part().astype(o_ref.dtype)
        return

    @pl.when(first)
    def _():
        acc[...] = jnp.zeros_like(acc)

    acc[...] += part()

    @pl.when(last)
    def _():
        o_ref[...] = acc[...].astype(o_ref.dtype)


def mm_nn(a, w3, out_dtype, name):
    M, K = a.shape
    J, _, Ns = w3.shape
    tn = _tile_n(Ns)
    tm, tk = next(((tm, tk) for tk in _tiles_desc(K, 4096, LANES) for tm in _tiles_desc(M, 1024, BF16_ROWS)
                   if tm >= min(M, 256) and _mm_fits([(tm, tk), (tk, tn)], (tm, tn), out_dtype, K // tk)),
                  (min(M, 256), _largest_tile(K, 512, LANES)))
    nn, nk = Ns // tn, K // tk

    def body(a_ref, w_ref, o_ref, *acc):
        k = pl.program_id(3)
        _reduce_into(o_ref, acc[0] if acc else None, lambda: _nn(a_ref[...], w_ref[...]), k == 0, k == nk - 1)

    return pl.pallas_call(
        body, name=name, out_shape=jax.ShapeDtypeStruct((M, J * Ns), out_dtype),
        grid=(M // tm, J, nn, nk),
        in_specs=[pl.BlockSpec((tm, tk), lambda i, j, n, k: (i, k)),
                  pl.BlockSpec((None, tk, tn), lambda i, j, n, k: (j, k, n))],
        out_specs=pl.BlockSpec((tm, tn), lambda i, j, n, k: (i, j * nn + n)),
        scratch_shapes=[pltpu.VMEM((tm, tn), f32)] if nk > 1 else [],
        compiler_params=_cp("parallel", "parallel", "parallel", "arbitrary"),
    )(a, w3)


MM_NT_REDUCE_CAP = 5632


def mm_nt(g, w3, out_dtype, name, init=None):
    M = g.shape[0]
    J, K, Ns = w3.shape
    if Ns <= MM_NT_REDUCE_CAP:
        tn, jb = Ns, max(b for b in range(1, J + 1) if J % b == 0 and (b == 1 or b * Ns <= MM_NT_REDUCE_CAP))
    else:
        tn, jb = _largest_tile(Ns, MM_NT_REDUCE_CAP, LANES), 1
    nj, nn = J // jb, Ns // tn
    n_red = nj * nn + (init is not None)
    tk = _largest_tile(K, 512, LANES)
    tm = next((tm for tm in _tiles_desc(M, 1024, BF16_ROWS)
               if _mm_fits([(tm, jb * tn), (jb * tk, tn)], (tm, tk), out_dtype, n_red,
                           extra=0 if init is None else 2 * 4 * tm * tk)), min(M, 256))

    def body(*refs):
        g_ref, w_ref = refs[:2]
        i_ref = None if init is None else refs[2]
        o_ref = refs[2 + (init is not None)]
        acc = refs[3 + (init is not None):]
        j, n = pl.program_id(2), pl.program_id(3)
        first = (j == 0) & (n == 0)

        def part():
            prod = _nt(g_ref[:, :tn], w_ref[0])
            for b in range(1, jb):
                prod = prod + _nt(g_ref[:, b * tn:(b + 1) * tn], w_ref[b])
            return prod if init is None else prod + jnp.where(first, i_ref[...].astype(f32), 0.0)

        _reduce_into(o_ref, acc[0] if acc else None, part, first, (j == nj - 1) & (n == nn - 1))

    in_specs = [pl.BlockSpec((tm, jb * tn), lambda i, k, j, n: (i, j * nn + n)),
                pl.BlockSpec((jb, tk, tn), lambda i, k, j, n: (j, k, n))]
    args = [g, w3]
    if init is not None:
        in_specs.append(pl.BlockSpec((tm, tk), lambda i, k, j, n: (i, k)))
        args.append(init)
    return pl.pallas_call(
        body, name=name, out_shape=jax.ShapeDtypeStruct((M, K), out_dtype),
        grid=(M // tm, K // tk, nj, nn), in_specs=in_specs,
        out_specs=pl.BlockSpec((tm, tk), lambda i, k, j, n: (i, k)),
        scratch_shapes=[pltpu.VMEM((tm, tk), f32)] if n_red > 1 else [],
        compiler_params=_cp("parallel", "parallel", "arbitrary", "arbitrary"),
    )(*args)


def mm_tn(a, g, ns, out_dtype, name):
    M, K = a.shape
    J = g.shape[1] // ns
    tn, tk = _tile_n(ns), _largest_tile(K, 512, LANES)
    tm = next((tm for tm in _tiles_desc(M, 4096, BF16_ROWS)
               if _mm_fits([(tm, tk), (tm, tn)], (tk, tn), out_dtype, M // tm)), _largest_tile(M, 512, BF16_ROWS))
    nn, nm = ns // tn, M // tm

    def body(a_ref, g_ref, o_ref, *acc):
        m = pl.program_id(3)
        _reduce_into(o_ref, acc[0] if acc else None, lambda: _tn(a_ref[...], g_ref[...]), m == 0, m == nm - 1)

    return pl.pallas_call(
        body, name=name, out_shape=jax.ShapeDtypeStruct((J, K, ns), out_dtype),
        grid=(J, K // tk, nn, nm),
        in_specs=[pl.BlockSpec((tm, tk), lambda j, k, n, m: (m, k)),
                  pl.BlockSpec((tm, tn), lambda j, k, n, m: (m, j * nn + n))],
        out_specs=pl.BlockSpec((None, tk, tn), lambda j, k, n, m: (j, k, n)),
        scratch_shapes=[pltpu.VMEM((tk, tn), f32)] if nm > 1 else [],
        compiler_params=_cp("parallel", "parallel", "parallel", "arbitrary"),
    )(a, g)


ROW_TILE = 256
ROW_STEP = 16


def row_fwd(fn, rows, consts, outs, name, n_acc=0):
    T = rows[0].shape[0]
    tr = min(T, ROW_TILE)
    n_rows, n_consts, n_row_out = len(rows), len(consts), len(outs) - n_acc

    def body(*refs):
        r_refs, c_refs, o_refs = refs[:n_rows], refs[n_rows:n_rows + n_consts], refs[n_rows + n_consts:]
        cs = [c[...] for c in c_refs]
        acc_refs = o_refs[n_row_out:]
        if n_acc:
            @pl.when(pl.program_id(0) == 0)
            def _():
                for a in acc_refs:
                    a[...] = jnp.zeros_like(a)

        def step(s, carry):
            rows_s = pl.ds(pl.multiple_of(s * ROW_STEP, ROW_STEP), ROW_STEP)
            res = fn(*[r[rows_s, :].astype(f32) for r in r_refs], *cs)
            for o, val in zip(o_refs[:n_row_out], res[:n_row_out]):
                o[rows_s, :] = val.astype(o.dtype)
            return tuple(c + val for c, val in zip(carry, res[n_row_out:]))

        accs = lax.fori_loop(0, tr // ROW_STEP, step, tuple(jnp.zeros((1, c), f32) for c, _ in outs[n_row_out:]))
        for a, val in zip(acc_refs, accs):
            a[...] += val

    in_specs = [pl.BlockSpec((tr, r.shape[1]), lambda i: (i, 0)) for r in rows]
    in_specs += [pl.BlockSpec(c.shape, lambda i: (0, 0)) for c in consts]
    out_shape = [jax.ShapeDtypeStruct((T, c), dt) for c, dt in outs[:n_row_out]]
    out_shape += [jax.ShapeDtypeStruct((1, c), f32) for c, _ in outs[n_row_out:]]
    out_specs = [pl.BlockSpec((tr, c), lambda i: (i, 0)) for c, _ in outs[:n_row_out]]
    out_specs += [pl.BlockSpec((1, c), lambda i: (0, 0)) for c, _ in outs[n_row_out:]]
    return pl.pallas_call(
        body, name=name, out_shape=out_shape, grid=(T // tr,), in_specs=in_specs, out_specs=out_specs,
        compiler_params=_cp("arbitrary" if n_acc else "parallel"),
    )(*rows, *consts)


def row_bwd(fn, rows, consts, cts, grad_dtypes, name):
    T = rows[0].shape[0]
    tr = min(T, ROW_TILE)
    n_rows, n_consts, n_cts = len(rows), len(consts), len(cts)
    wanted = [i for i, dt in enumerate(grad_dtypes) if dt is not None]

    def body(*refs):
        r_refs = refs[:n_rows]
        c_refs = refs[n_rows:n_rows + n_consts]
        t_refs = refs[n_rows + n_consts:n_rows + n_consts + n_cts]
        o_refs = refs[n_rows + n_consts + n_cts:]
        gr_refs, gc_refs = o_refs[:len(wanted)], o_refs[len(wanted):]
        cs = [c[...] for c in c_refs]

        @pl.when(pl.program_id(0) == 0)
        def _():
            for a in gc_refs:
                a[...] = jnp.zeros_like(a)

        def step(s, carry):
            rows_s = pl.ds(pl.multiple_of(s * ROW_STEP, ROW_STEP), ROW_STEP)
            ins = [r[rows_s, :].astype(f32) for r in r_refs]
            _, vjp = jax.vjp(lambda *a: tuple(fn(*a)), *ins, *cs)
            grads = vjp(tuple(t[rows_s, :].astype(f32) for t in t_refs))
            for o, i in zip(gr_refs, wanted):
                o[rows_s, :] = grads[i].astype(o.dtype)
            return tuple(c + gval for c, gval in zip(carry, grads[n_rows:]))

        accs = lax.fori_loop(0, tr // ROW_STEP, step, tuple(jnp.zeros(c.shape, f32) for c in consts))
        for a, val in zip(gc_refs, accs):
            a[...] += val

    in_specs = [pl.BlockSpec((tr, r.shape[1]), lambda i: (i, 0)) for r in list(rows) + list(cts)]
    in_specs[n_rows:n_rows] = [pl.BlockSpec(c.shape, lambda i: (0, 0)) for c in consts]
    out_shape = [jax.ShapeDtypeStruct(rows[i].shape, grad_dtypes[i]) for i in wanted]
    out_shape += [jax.ShapeDtypeStruct(c.shape, f32) for c in consts]
    out_specs = [pl.BlockSpec((tr, rows[i].shape[1]), lambda i_: (i_, 0)) for i in wanted]
    out_specs += [pl.BlockSpec(c.shape, lambda i: (0, 0)) for c in consts]
    return pl.pallas_call(
        body, name=name, out_shape=out_shape, grid=(T // tr,), in_specs=in_specs, out_specs=out_specs,
        compiler_params=_cp("arbitrary"),
    )(*rows, *consts, *cts)


def _rms(x, g):
    return x * lax.rsqrt(jnp.mean(x * x, axis=-1, keepdims=True) + EPS) * g


def fn_prenorm(x, g):
    return (_rms(x, g),)


def fn_resnorm(x, m, g_post, g_pre):
    x_new = x + _rms(m, g_post)
    return x_new, _rms(x_new, g_pre)


def fn_res(x, m, g_post):
    return (x + _rms(m, g_post),)


def fn_final(x, m, target, g_post):
    err = x + _rms(m, g_post) - target
    n = err.shape[-1]
    return err / n, (0.5 / n) * jnp.sum(jnp.sum(err * err, axis=1, keepdims=True), axis=0, keepdims=True)


def fn_gate_norm(y, z, g):
    return (_rms(y * jax.nn.silu(z), g),)


CONV_ROWS = 256


def _conv_chunk(pre_ref, w, b, r, rb, K):
    t0 = pl.multiple_of(r * rb, rb)
    halo_at = pl.multiple_of(jnp.maximum(t0 - SUBLANES, 0), SUBLANES)
    halo = jnp.where(r > 0, pre_ref[pl.ds(halo_at, SUBLANES), :], 0.0)
    main = pre_ref[pl.ds(t0, rb), :]
    ext = jnp.concatenate([halo, main], axis=0)
    shifted = [main if k == K - 1 else pltpu.roll(ext, K - 1 - k, 0)[SUBLANES:] for k in range(K)]
    conv = b
    for k in range(K):
        conv = conv + w[k:k + 1] * shifted[k]
    return conv, shifted


def conv_fwd(streams, epilogue, out_cols, out_dtype, K, name):
    T = streams[0][0].shape[0]
    rb = min(T, CONV_ROWS)
    S = len(streams)

    def body(*refs):
        pre_refs, w_refs, b_refs, o_ref = refs[:S], refs[S:2 * S], refs[2 * S:3 * S], refs[3 * S]
        ws = [w[...] for w in w_refs]
        bs = [b[...] for b in b_refs]

        def step(r, carry):
            convs = [_conv_chunk(pre_refs[s], ws[s], bs[s], r, rb, K)[0] for s in range(S)]
            o_ref[pl.ds(pl.multiple_of(r * rb, rb), rb), :] = epilogue(*convs).astype(o_ref.dtype)
            return carry

        lax.fori_loop(0, T // rb, step, 0)

    in_specs = [pl.BlockSpec((T, LANES), functools.partial(lambda off, i: (0, off + i), st[1])) for st in streams]
    in_specs += [pl.BlockSpec((K, LANES), functools.partial(lambda off, i: (0, off + i), st[4])) for st in streams]
    in_specs += [pl.BlockSpec((1, LANES), functools.partial(lambda off, i: (0, off + i), st[4])) for st in streams]
    return pl.pallas_call(
        body, name=name, out_shape=jax.ShapeDtypeStruct((T, out_cols), out_dtype), grid=(out_cols // LANES,),
        in_specs=in_specs, out_specs=pl.BlockSpec((T, LANES), lambda i: (0, i)),
        compiler_params=_cp("parallel"),
    )(*[st[0] for st in streams], *[st[2] for st in streams], *[st[3] for st in streams])


def conv_bwd(streams, epilogue, dout, K, name):
    T, cols = dout.shape
    rb = min(T, CONV_ROWS)
    S = len(streams)

    def body(*refs):
        pre_refs, w_refs, b_refs, dout_ref = refs[:S], refs[S:2 * S], refs[2 * S:3 * S], refs[3 * S]
        o = refs[3 * S + 1:]
        dpre_refs, dw_refs, db_refs, scr = o[:S], o[S:2 * S], o[2 * S:3 * S], o[3 * S:]
        ws = [w[...] for w in w_refs]
        bs = [b[...] for b in b_refs]
        for s in range(S):
            scr[s][pl.ds(T, SUBLANES), :] = jnp.zeros((SUBLANES, LANES), f32)

        def phase1(r, carry):
            rows = pl.ds(pl.multiple_of(r * rb, rb), rb)
            convs, shifted = zip(*[_conv_chunk(pre_refs[s], ws[s], bs[s], r, rb, K) for s in range(S)])
            _, vjp = jax.vjp(epilogue, *convs)
            dconvs = vjp(dout_ref[rows, :].astype(f32))
            new = []
            for s in range(S):
                scr[s][rows, :] = dconvs[s]
                sums = [jnp.sum(dconvs[s] * shifted[s][k], axis=0, keepdims=True) for k in range(K)]
                sums.append(jnp.sum(dconvs[s], axis=0, keepdims=True))
                new.append(tuple(c + v for c, v in zip(carry[s], sums)))
            return tuple(new)

        zero = tuple(tuple(jnp.zeros((1, LANES), f32) for _ in range(K + 1)) for _ in range(S))
        sums = lax.fori_loop(0, T // rb, phase1, zero)
        tap = lax.broadcasted_iota(jnp.int32, (K, LANES), 0)
        for s in range(S):
            dw = jnp.zeros((K, LANES), f32)
            for k in range(K):
                dw = jnp.where(tap == k, sums[s][k], dw)
            dw_refs[s][...] = dw
            db_refs[s][...] = sums[s][K]

        def phase2(r, carry):
            t0 = pl.multiple_of(r * rb, rb)
            for s in range(S):
                ext = scr[s][pl.ds(t0, rb + SUBLANES), :]
                dpre = ws[s][K - 1:K] * ext[:rb]
                for k in range(K - 1):
                    j = K - 1 - k
                    dpre = dpre + ws[s][k:k + 1] * pltpu.roll(ext, rb + SUBLANES - j, 0)[:rb]
                dpre_refs[s][pl.ds(t0, rb), :] = dpre.astype(dpre_refs[s].dtype)
            return carry

        lax.fori_loop(0, T // rb, phase2, 0)

    in_specs = [pl.BlockSpec((T, LANES), functools.partial(lambda off, i: (0, off + i), st[1])) for st in streams]
    in_specs += [pl.BlockSpec((K, LANES), functools.partial(lambda off, i: (0, off + i), st[4])) for st in streams]
    in_specs += [pl.BlockSpec((1, LANES), functools.partial(lambda off, i: (0, off + i), st[4])) for st in streams]
    in_specs += [pl.BlockSpec((T, LANES), lambda i: (0, i))]
    out_shape = [jax.ShapeDtypeStruct((T, cols), bf16)] * S
    out_shape += [jax.ShapeDtypeStruct((K, cols), f32)] * S + [jax.ShapeDtypeStruct((1, cols), f32)] * S
    out_specs = [pl.BlockSpec((T, LANES), lambda i: (0, i))] * S
    out_specs += [pl.BlockSpec((K, LANES), lambda i: (0, i))] * S + [pl.BlockSpec((1, LANES), lambda i: (0, i))] * S
    res = pl.pallas_call(
        body, name=name, out_shape=out_shape, grid=(cols // LANES,), in_specs=in_specs, out_specs=out_specs,
        scratch_shapes=[pltpu.VMEM((T + SUBLANES, LANES), f32)] * S,
        compiler_params=_cp("parallel"),
    )(*[st[0] for st in streams], *[st[2] for st in streams], *[st[3] for st in streams], dout)
    return res[:S], res[S:2 * S], res[2 * S:]


def epi_gelu_gate(cg, cu):
    return jax.nn.gelu(cg, approximate=True) * cu


def epi_silu(c):
    return jax.nn.silu(c)


XA_ROWS = 256


def _xa_fn(q, k, v):
    s = _nt(q.astype(bf16), k.astype(bf16)) * (q.shape[-1] ** -0.5)
    p = jax.nn.softmax(s, axis=-1)
    return _nn(p.astype(bf16), v.astype(bf16))


def xa_fwd(q, kv, name):
    T, W = q.shape
    M = kv.shape[0]
    H = W // LANES
    tr = min(T, XA_ROWS)

    def body(q_ref, k_ref, v_ref, o_ref):
        o_ref[...] = _xa_fn(q_ref[...].astype(f32), k_ref[...].astype(f32), v_ref[...].astype(f32)).astype(o_ref.dtype)

    return pl.pallas_call(
        body, name=name, out_shape=jax.ShapeDtypeStruct((T, W), bf16), grid=(T // tr, H),
        in_specs=[pl.BlockSpec((tr, LANES), lambda i, h: (i, h)),
                  pl.BlockSpec((M, LANES), lambda i, h: (0, h)),
                  pl.BlockSpec((M, LANES), lambda i, h: (0, H + h))],
        out_specs=pl.BlockSpec((tr, LANES), lambda i, h: (i, h)),
        compiler_params=_cp("parallel", "parallel"),
    )(q, kv, kv)


def xa_bwd(q, kv, do, name):
    T, W = q.shape
    M = kv.shape[0]
    H = W // LANES
    tr = min(T, XA_ROWS)

    def body(q_ref, k_ref, v_ref, do_ref, dq_ref, dk_ref, dv_ref):
        @pl.when(pl.program_id(1) == 0)
        def _():
            dk_ref[...] = jnp.zeros_like(dk_ref)
            dv_ref[...] = jnp.zeros_like(dv_ref)

        _, vjp = jax.vjp(_xa_fn, q_ref[...].astype(f32), k_ref[...].astype(f32), v_ref[...].astype(f32))
        dq, dk, dv = vjp(do_ref[...].astype(f32))
        dq_ref[...] = dq.astype(dq_ref.dtype)
        dk_ref[...] += dk
        dv_ref[...] += dv

    return pl.pallas_call(
        body, name=name,
        out_shape=[jax.ShapeDtypeStruct((T, W), bf16), jax.ShapeDtypeStruct((M, W), f32), jax.ShapeDtypeStruct((M, W), f32)],
        grid=(H, T // tr),
        in_specs=[pl.BlockSpec((tr, LANES), lambda h, i: (i, h)),
                  pl.BlockSpec((M, LANES), lambda h, i: (0, h)),
                  pl.BlockSpec((M, LANES), lambda h, i: (0, H + h)),
                  pl.BlockSpec((tr, LANES), lambda h, i: (i, h))],
        out_specs=[pl.BlockSpec((tr, LANES), lambda h, i: (i, h)),
                   pl.BlockSpec((M, LANES), lambda h, i: (0, h)),
                   pl.BlockSpec((M, LANES), lambda h, i: (0, h))],
        compiler_params=_cp("parallel", "arbitrary"),
    )(q, kv, kv, do)


def _sgu_norm_fn(v_pre, g, b):
    v = jax.nn.gelu(v_pre, approximate=True)
    mu = jnp.mean(v, axis=-1, keepdims=True)
    vc = v - mu
    return vc * lax.rsqrt(jnp.mean(vc * vc, axis=-1, keepdims=True) + EPS) * g + b


def _sgu_mix_fn(u_pre, vn, w, b):
    q = w.shape[0]
    tril = lax.broadcasted_iota(jnp.int32, (q, q), 0) >= lax.broadcasted_iota(jnp.int32, (q, q), 1)
    mixed = _nn(jnp.where(tril, w, 0.0).astype(bf16), vn.astype(bf16)) + b
    return jax.nn.gelu(u_pre, approximate=True) * mixed


def _sgu_norm_phase(v_ref, g, b, vn_ref):
    def step(s, carry):
        rows = pl.ds(pl.multiple_of(s * ROW_STEP, ROW_STEP), ROW_STEP)
        vn_ref[rows, :] = _sgu_norm_fn(v_ref[rows, :], g, b)
        return carry

    lax.fori_loop(0, CHUNK // ROW_STEP, step, 0)


def sgu_fwd(uv_pre, norm_g, norm_b, w_sp, b_sp, name):
    T, W2 = uv_pre.shape
    W = W2 // 2
    G = w_sp.shape[0]
    gw = W // G

    def body(u_ref, v_ref, g_ref, b_ref, ws_ref, bs_ref, o_ref, vn_ref):
        _sgu_norm_phase(v_ref, g_ref[...], b_ref[...], vn_ref)

        def group(gi, carry):
            cols = pl.ds(pl.multiple_of(gi * gw, LANES), gw)
            o_ref[:, cols] = _sgu_mix_fn(u_ref[:, cols], vn_ref[:, cols], ws_ref[gi], bs_ref[gi]).astype(o_ref.dtype)
            return carry

        lax.fori_loop(0, G, group, 0)

    full = lambda a: pl.BlockSpec(a.shape, lambda c: (0,) * a.ndim)
    return pl.pallas_call(
        body, name=name, out_shape=jax.ShapeDtypeStruct((T, W), bf16), grid=(T // CHUNK,),
        in_specs=[pl.BlockSpec((CHUNK, W), lambda c: (c, 0)), pl.BlockSpec((CHUNK, W), lambda c: (c, 1)),
                  full(norm_g), full(norm_b), full(w_sp), full(b_sp)],
        out_specs=pl.BlockSpec((CHUNK, W), lambda c: (c, 0)),
        scratch_shapes=[pltpu.VMEM((CHUNK, W), f32)],
        compiler_params=_cp("parallel"),
    )(uv_pre, uv_pre, norm_g, norm_b, w_sp, b_sp)


def sgu_bwd(uv_pre, norm_g, norm_b, w_sp, b_sp, dout, name):
    T, W2 = uv_pre.shape
    W = W2 // 2
    G = w_sp.shape[0]
    gw = W // G

    def body(u_ref, v_ref, g_ref, b_ref, ws_ref, bs_ref, do_ref, duv_ref, dg_ref, db_ref, dws_ref, dbs_ref,
             vn_ref, dvn_ref):
        @pl.when(pl.program_id(0) == 0)
        def _():
            for a in (dg_ref, db_ref, dws_ref, dbs_ref):
                a[...] = jnp.zeros_like(a)

        g, b = g_ref[...], b_ref[...]
        _sgu_norm_phase(v_ref, g, b, vn_ref)

        def group(gi, carry):
            cols = pl.ds(pl.multiple_of(gi * gw, LANES), gw)
            _, vjp = jax.vjp(_sgu_mix_fn, u_ref[:, cols], vn_ref[:, cols], ws_ref[gi], bs_ref[gi])
            du, dvn, dw, dbias = vjp(do_ref[:, cols])
            duv_ref[:, cols] = du.astype(duv_ref.dtype)
            dvn_ref[:, cols] = dvn
            dws_ref[gi] += dw
            dbs_ref[gi] += dbias
            return carry

        lax.fori_loop(0, G, group, 0)

        def step(s, carry):
            rows = pl.ds(pl.multiple_of(s * ROW_STEP, ROW_STEP), ROW_STEP)
            _, vjp = jax.vjp(_sgu_norm_fn, v_ref[rows, :], g, b)
            dv, dg, dbn = vjp(dvn_ref[rows, :])
            duv_ref[rows, pl.ds(W, W)] = dv.astype(duv_ref.dtype)
            return carry[0] + dg, carry[1] + dbn

        dg, dbn = lax.fori_loop(0, CHUNK // ROW_STEP, step, (jnp.zeros((1, W), f32), jnp.zeros((1, W), f32)))
        dg_ref[...] += dg
        db_ref[...] += dbn

    full = lambda a: pl.BlockSpec(a.shape, lambda c: (0,) * a.ndim)
    return pl.pallas_call(
        body, name=name,
        out_shape=[jax.ShapeDtypeStruct((T, W2), bf16), jax.ShapeDtypeStruct((1, W), f32), jax.ShapeDtypeStruct((1, W), f32),
                   jax.ShapeDtypeStruct(w_sp.shape, f32), jax.ShapeDtypeStruct(b_sp.shape, f32)],
        grid=(T // CHUNK,),
        in_specs=[pl.BlockSpec((CHUNK, W), lambda c: (c, 0)), pl.BlockSpec((CHUNK, W), lambda c: (c, 1)),
                  full(norm_g), full(norm_b), full(w_sp), full(b_sp), pl.BlockSpec((CHUNK, W), lambda c: (c, 0))],
        out_specs=[pl.BlockSpec((CHUNK, W2), lambda c: (c, 0)), full(norm_g), full(norm_b), full(w_sp), full(b_sp)],
        scratch_shapes=[pltpu.VMEM((CHUNK, W), f32), pltpu.VMEM((CHUNK, W), f32)],
        compiler_params=_cp("arbitrary"),
    )(uv_pre, uv_pre, norm_g, norm_b, w_sp, b_sp, dout)


SB_SUM_COLS = 256


def _sb_key_group(T):
    return 512 if T % 512 == 0 else T


def _sb_query_rows(T):
    return 256 if T % 256 == 0 else CHUNK


def _sb_block(q, k, i, g):
    qb, kg = q.shape[0], k.shape[0]
    z = _nt(q, k) * (q.shape[-1] ** -0.5)
    t_idx = i * qb + lax.broadcasted_iota(jnp.int32, (qb, kg), 0)
    s_idx = g * kg + lax.broadcasted_iota(jnp.int32, (qb, kg), 1)
    valid = s_idx < t_idx
    sp = jnp.log1p(jnp.exp(-jnp.abs(z)))
    log_beta = jnp.minimum(z, 0.0) - sp
    log_1mb = jnp.where(valid, -jnp.maximum(z, 0.0) - sp, 0.0)
    return z, valid, log_beta, log_1mb


def _order_matrix(later):
    r = lax.broadcasted_iota(jnp.int32, (SB_SUM_COLS, SB_SUM_COLS), 0)
    c = lax.broadcasted_iota(jnp.int32, (SB_SUM_COLS, SB_SUM_COLS), 1)
    return (r > c if later else r < c).astype(bf16)


def _masked_sums(parts, order):
    terms = []
    for x in parts:
        hi = x.astype(bf16)
        rest = x - hi.astype(f32)
        mid = rest.astype(bf16)
        terms += [hi, mid, (rest - mid.astype(f32)).astype(bf16)]
    rows = parts[0].shape[0]
    prod = _nn(jnp.concatenate(terms, axis=0), order)
    piece = lambda n: prod[n * rows:(n + 1) * rows]
    return [piece(3 * p) + piece(3 * p + 1) + piece(3 * p + 2) for p in range(len(parts))]


def sb_fwd(qkv, name):
    T = qkv.shape[0]
    H = qkv.shape[1] // (3 * LANES)
    qb, kg = _sb_query_rows(T), _sb_key_group(T)
    halves = kg // SB_SUM_COLS

    def body(q_ref, k_ref, v_ref, o_ref, tot_ref):
        i = pl.program_id(1)
        q = q_ref[...]
        later = _order_matrix(True)
        n_groups = (i * qb + qb + kg - 1) // kg

        def step(gg, carry):
            acc, run = carry
            g = n_groups - 1 - gg
            rows = pl.ds(pl.multiple_of(g * kg, kg), kg)
            _, valid, log_beta, log_1mb = _sb_block(q, k_ref[rows, :], i, g)
            parts = [log_1mb[:, hh * SB_SUM_COLS:(hh + 1) * SB_SUM_COLS] for hh in range(halves)]
            tails = _masked_sums(parts, later)
            for hh in reversed(range(halves)):
                tails[hh] = tails[hh] + run
                run = run + jnp.sum(parts[hh], axis=1, keepdims=True)
            a = jnp.where(valid, jnp.exp(log_beta + jnp.concatenate(tails, axis=1)), 0.0)
            return acc + _nn(a.astype(bf16), v_ref[rows, :]), run

        acc, run = lax.fori_loop(0, n_groups, step, (jnp.zeros((qb, LANES), f32), jnp.zeros((qb, 1), f32)))
        o_ref[...] = acc.astype(o_ref.dtype)
        tot_ref[...] = run

    return pl.pallas_call(
        body, name=name,
        out_shape=[jax.ShapeDtypeStruct((T, H * LANES), bf16), jax.ShapeDtypeStruct((H, T, 1), f32)],
        grid=(H, T // qb),
        in_specs=[pl.BlockSpec((qb, LANES), lambda h, i: (i, h)),
                  pl.BlockSpec((T, LANES), lambda h, i: (0, H + h)),
                  pl.BlockSpec((T, LANES), lambda h, i: (0, 2 * H + h))],
        out_specs=[pl.BlockSpec((qb, LANES), lambda h, i: (i, h)),
                   pl.BlockSpec((None, qb, 1), lambda h, i: (h, i, 0))],
        compiler_params=_cp("parallel", "parallel"),
    )(qkv, qkv, qkv)


def sb_bwd(qkv, tot, do, name):
    T = qkv.shape[0]
    H = qkv.shape[1] // (3 * LANES)
    qb, kg = _sb_query_rows(T), _sb_key_group(T)
    halves = kg // SB_SUM_COLS

    def body(q_ref, k_ref, v_ref, tot_ref, do_ref, dq_ref, dk_ref, dv_ref):
        i = pl.program_id(1)

        @pl.when(i == 0)
        def _():
            dk_ref[...] = jnp.zeros_like(dk_ref)
            dv_ref[...] = jnp.zeros_like(dv_ref)

        q = q_ref[...]
        do = do_ref[...].astype(bf16)
        tot = tot_ref[...]
        later, earlier = _order_matrix(True), _order_matrix(False)
        scale = q.shape[-1] ** -0.5
        n_groups = (i * qb + qb + kg - 1) // kg

        def step(g, carry):
            dq, before, d_run = carry
            rows = pl.ds(pl.multiple_of(g * kg, kg), kg)
            k, v = k_ref[rows, :], v_ref[rows, :]
            z, valid, log_beta, log_1mb = _sb_block(q, k, i, g)
            parts = [log_1mb[:, hh * SB_SUM_COLS:(hh + 1) * SB_SUM_COLS] for hh in range(halves)]
            tails = _masked_sums(parts, later)
            for hh in range(halves):
                before = before + jnp.sum(parts[hh], axis=1, keepdims=True)
                tails[hh] = tails[hh] + (tot - before)
            a = jnp.where(valid, jnp.exp(log_beta + jnp.concatenate(tails, axis=1)), 0.0)
            d_e = _nt(do, v) * a
            parts = [d_e[:, hh * SB_SUM_COLS:(hh + 1) * SB_SUM_COLS] for hh in range(halves)]
            d_l1 = _masked_sums(parts, earlier)
            for hh in range(halves):
                d_l1[hh] = d_l1[hh] + d_run
                d_run = d_run + jnp.sum(parts[hh], axis=1, keepdims=True)
            d_l1 = jnp.where(valid, jnp.concatenate(d_l1, axis=1), 0.0)
            sig = jax.nn.sigmoid(z)
            dz = ((d_e * (1.0 - sig) - d_l1 * sig) * scale).astype(bf16)
            dk_ref[rows, :] += _tn(dz, q)
            dv_ref[rows, :] += _tn(a.astype(bf16), do)
            return dq + _nn(dz, k), before, d_run

        zero_col = jnp.zeros((qb, 1), f32)
        dq, _, _ = lax.fori_loop(0, n_groups, step, (jnp.zeros((qb, LANES), f32), zero_col, zero_col))
        dq_ref[...] = dq.astype(dq_ref.dtype)

    W = H * LANES
    return pl.pallas_call(
        body, name=name,
        out_shape=[jax.ShapeDtypeStruct((T, W), bf16), jax.ShapeDtypeStruct((T, W), f32), jax.ShapeDtypeStruct((T, W), f32)],
        grid=(H, T // qb),
        in_specs=[pl.BlockSpec((qb, LANES), lambda h, i: (i, h)),
                  pl.BlockSpec((T, LANES), lambda h, i: (0, H + h)),
                  pl.BlockSpec((T, LANES), lambda h, i: (0, 2 * H + h)),
                  pl.BlockSpec((None, qb, 1), lambda h, i: (h, i, 0)),
                  pl.BlockSpec((qb, LANES), lambda h, i: (i, h))],
        out_specs=[pl.BlockSpec((qb, LANES), lambda h, i: (i, h)),
                   pl.BlockSpec((T, LANES), lambda h, i: (0, h)),
                   pl.BlockSpec((T, LANES), lambda h, i: (0, h))],
        compiler_params=_cp("parallel", "arbitrary"),
    )(qkv, qkv, qkv, tot, do)


def _softplus(x):
    return jnp.maximum(x, 0.0) + jnp.log1p(jnp.exp(-jnp.abs(x)))


def _ssd_chunk_fn(head0, xs, b_mat, c_mat, dt_raw, dt_bias, a_log, d_skip, prev):
    q = dt_raw.shape[0]
    lane = lax.broadcasted_iota(jnp.int32, (q, LANES), 1)
    sub = lax.broadcasted_iota(jnp.int32, (q, LANES), 0)
    causal = sub >= lane
    dt = _softplus(dt_raw + dt_bias)
    a_cum = _nn(causal.astype(f32), dt * (-jnp.exp(a_log)), HIGHEST)
    a_cum_t = a_cum.T
    cb = _nt(c_mat.astype(bf16), b_mat.astype(bf16))
    bm, cm = b_mat.astype(bf16), c_mat.astype(bf16)
    ys, new = [], []
    for r in range(len(xs)):
        in_lane, in_sub = lane == head0 + r, sub == head0 + r
        col_a = jnp.sum(jnp.where(in_lane, a_cum, 0.0), axis=1, keepdims=True)
        row_a = jnp.sum(jnp.where(in_sub, a_cum_t, 0.0), axis=0, keepdims=True)
        col_dt = jnp.sum(jnp.where(in_lane, dt, 0.0), axis=1, keepdims=True)
        skip = jnp.sum(jnp.where(in_lane[:1], d_skip, 0.0), axis=1, keepdims=True)
        a_last = jnp.sum(jnp.where(sub[:, :1] == q - 1, col_a, 0.0), axis=0, keepdims=True)
        decay_in = jnp.exp(jnp.where(causal, col_a - row_a, -jnp.inf))
        xdt = xs[r] * col_dt
        y_diag = _nn((cb * decay_in).astype(bf16), xdt.astype(bf16))
        y_off = _nt(cm, prev[r].astype(bf16)) * jnp.exp(col_a)
        ys.append(y_diag + y_off + xs[r] * skip)
        state = _tn((xdt * jnp.exp(a_last - col_a)).astype(bf16), bm)
        new.append(prev[r] * jnp.exp(a_last) + state)
    return ys, new


def _ssd_specs(T, G, reverse):
    nc = T // CHUNK
    R, P, N = SSD_HEADS_PER_GROUP, SSD_HEAD_DIM, SSD_STATE
    ch = (lambda c: nc - 1 - c) if reverse else (lambda c: c)
    xs = pl.BlockSpec((CHUNK, R * P), lambda g, c: (ch(c), g))
    bm = pl.BlockSpec((CHUNK, N), lambda g, c: (ch(c), G * R * P // N + g))
    cm = pl.BlockSpec((CHUNK, N), lambda g, c: (ch(c), G * R * P // N + G + g))
    dt = pl.BlockSpec((CHUNK, LANES), lambda g, c: (ch(c), 0))
    row = pl.BlockSpec((1, LANES), lambda g, c: (0, 0))
    st = pl.BlockSpec((None, None, R * P, N), lambda g, c: (g, ch(c), 0, 0))
    return nc, xs, bm, cm, dt, row, st


def ssd_fwd(xbc, dt_raw, dt_bias, a_log, d_skip, name):
    T = xbc.shape[0]
    R, P, N = SSD_HEADS_PER_GROUP, SSD_HEAD_DIM, SSD_STATE
    G = xbc.shape[1] // (R * P + 2 * N)
    nc, xs_s, bm_s, cm_s, dt_s, row_s, st_s = _ssd_specs(T, G, False)

    def body(xs_ref, b_ref, c_ref, dt_ref, bias_ref, alog_ref, skip_ref, y_ref, st_ref, state):
        @pl.when(pl.program_id(1) == 0)
        def _():
            state[...] = jnp.zeros_like(state)

        x = xs_ref[...]
        xs = [x[:, r * P:(r + 1) * P] for r in range(R)]
        prev = [state[r] for r in range(R)]
        ys, new = _ssd_chunk_fn(pl.program_id(0) * R, xs, b_ref[...], c_ref[...], dt_ref[...],
                                bias_ref[...], alog_ref[...], skip_ref[...], prev)
        y_ref[...] = jnp.concatenate(ys, axis=1)
        for r in range(R):
            st_ref[pl.ds(r * P, P), :] = prev[r]
            state[r] = new[r]

    return pl.pallas_call(
        body, name=name,
        out_shape=[jax.ShapeDtypeStruct((T, G * R * P), f32), jax.ShapeDtypeStruct((G, nc, R * P, N), f32)],
        grid=(G, nc), in_specs=[xs_s, bm_s, cm_s, dt_s, row_s, row_s, row_s], out_specs=[xs_s, st_s],
        scratch_shapes=[pltpu.VMEM((R, P, N), f32)],
        compiler_params=_cp("parallel", "arbitrary"),
    )(xbc, xbc, xbc, dt_raw, dt_bias, a_log, d_skip)


def ssd_bwd(xbc, dt_raw, dt_bias, a_log, d_skip, states, dy, name):
    T = xbc.shape[0]
    R, P, N = SSD_HEADS_PER_GROUP, SSD_HEAD_DIM, SSD_STATE
    G = xbc.shape[1] // (R * P + 2 * N)
    nc, xs_s, bm_s, cm_s, dt_s, row_s, st_s = _ssd_specs(T, G, True)

    def body(xs_ref, b_ref, c_ref, dt_ref, bias_ref, alog_ref, skip_ref, st_ref, dy_ref,
             dx_ref, db_ref, dc_ref, ddt_ref, dbias_ref, dalog_ref, dskip_ref, dstate):
        g, c = pl.program_id(0), pl.program_id(1)

        @pl.when(c == 0)
        def _():
            dstate[...] = jnp.zeros_like(dstate)

        @pl.when((c == 0) & (g == 0))
        def _():
            for a in (dbias_ref, dalog_ref, dskip_ref):
                a[...] = jnp.zeros_like(a)

        x, dyv = xs_ref[...], dy_ref[...]
        xs = [x[:, r * P:(r + 1) * P] for r in range(R)]
        prev = [st_ref[pl.ds(r * P, P), :] for r in range(R)]
        _, vjp = jax.vjp(functools.partial(_ssd_chunk_fn, g * R), xs, b_ref[...], c_ref[...], dt_ref[...],
                         bias_ref[...], alog_ref[...], skip_ref[...], prev)
        dxs, dbm, dcm, ddt, dbias, dalog, dskip, dprev = vjp(
            ([dyv[:, r * P:(r + 1) * P] for r in range(R)], [dstate[r] for r in range(R)]))
        dx_ref[...] = jnp.concatenate(dxs, axis=1)
        db_ref[...] = dbm
        dc_ref[...] = dcm
        ddt_ref[...] = ddt
        dbias_ref[...] += dbias
        dalog_ref[...] += dalog
        dskip_ref[...] += dskip
        for r in range(R):
            dstate[r] = dprev[r]

    small = pl.BlockSpec((CHUNK, N), lambda g, c: (nc - 1 - c, g))
    return pl.pallas_call(
        body, name=name,
        out_shape=[jax.ShapeDtypeStruct((T, G * R * P), f32), jax.ShapeDtypeStruct((T, G * N), f32),
                   jax.ShapeDtypeStruct((T, G * N), f32), jax.ShapeDtypeStruct((G, T, LANES), f32)]
        + [jax.ShapeDtypeStruct((1, LANES), f32)] * 3,
        grid=(G, nc), in_specs=[xs_s, bm_s, cm_s, dt_s, row_s, row_s, row_s, st_s, xs_s],
        out_specs=[xs_s, small, small, pl.BlockSpec((None, CHUNK, LANES), lambda g, c: (g, nc - 1 - c, 0)),
                   row_s, row_s, row_s],
        scratch_shapes=[pltpu.VMEM((R, P, N), f32)],
        compiler_params=_cp("arbitrary", "arbitrary"),
    )(xbc, xbc, xbc, dt_raw, dt_bias, a_log, d_skip, states, dy)


WEIGHTS = ["ln_mix_pre", "ln_mix_post", "ln_mem", "ln_xa_pre", "ln_xa_post", "ln_ffn_pre", "ln_ffn_post",
           "xa_wq", "xa_wkv", "xa_wo", "ffn_w_in", "ffn_conv_w", "ffn_conv_b", "ffn_w_out",
           "ssd_w_in", "ssd_conv_w", "ssd_conv_b", "ssd_dt_bias", "ssd_a_log", "ssd_d", "ssd_norm", "ssd_w_out",
           "sg_w_in", "sg_v_norm_g", "sg_v_norm_b", "sg_w_spatial", "sg_b_spatial", "sg_w_out", "sb_w_qkv", "sb_w_out"]
BIG = ["xa_wq", "xa_wkv", "xa_wo", "ffn_w_in", "ffn_w_out", "ssd_w_in", "ssd_w_out", "sg_w_in", "sg_w_out",
       "sb_w_qkv", "sb_w_out"]
SMALL_SHARDED = ["ffn_conv_w", "ssd_conv_w", "ssd_conv_b", "ssd_norm"]
REPLICATED = [n for n in WEIGHTS if n not in BIG and n not in SMALL_SHARDED]


def fn_xpre(x, g):
    return x, _rms(x, g)


def _pack_rows(arrs, lead=0):
    head = arrs[0].shape[:lead]
    flat = jnp.concatenate([a.reshape(head + (-1,)) for a in arrs], axis=-1)
    n = flat.shape[-1]
    rows = -(-n // (SUBLANES * LANES)) * SUBLANES
    flat = jnp.pad(flat, [(0, 0)] * lead + [(0, rows * LANES - n)])
    return flat.reshape(head + (rows, LANES))


def _unpack_rows(packed, shapes):
    head = packed.shape[:-2]
    flat = packed.reshape(head + (-1,))
    out, off = [], 0
    for shp in shapes:
        n = int(np.prod(shp, dtype=np.int64))
        out.append(flat[..., off:off + n].reshape(head + tuple(shp)))
        off += n
    return out


def _merge_last(a8):
    return jnp.moveaxis(a8, 0, -2).reshape(a8.shape[1:-1] + (N_DEV * a8.shape[-1],))


def _split_last(a):
    return jnp.moveaxis(a.reshape(a.shape[:-1] + (N_DEV, a.shape[-1] // N_DEV)), -2, 0)


def kernel(x, mem, ln_mix_pre, ln_mix_post, ln_mem, ln_xa_pre, ln_xa_post, ln_ffn_pre, ln_ffn_post, xa_wq, xa_wkv, xa_wo, ffn_w_in, ffn_conv_w, ffn_conv_b, ffn_w_out, ssd_w_in, ssd_conv_w, ssd_conv_b, ssd_dt_bias, ssd_a_log, ssd_d, ssd_norm, ssd_w_out, sg_w_in, sg_v_norm_g, sg_v_norm_b, sg_w_spatial, sg_b_spatial, sg_w_out, sb_w_qkv, sb_w_out, loss_target, m_ln_mix_pre, m_ln_mix_post, m_ln_mem, m_ln_xa_pre, m_ln_xa_post, m_ln_ffn_pre, m_ln_ffn_post, m_xa_wq, m_xa_wkv, m_xa_wo, m_ffn_w_in, m_ffn_conv_w, m_ffn_conv_b, m_ffn_w_out, m_ssd_w_in, m_ssd_conv_w, m_ssd_conv_b, m_ssd_dt_bias, m_ssd_a_log, m_ssd_d, m_ssd_norm, m_ssd_w_out, m_sg_w_in, m_sg_v_norm_g, m_sg_v_norm_b, m_sg_w_spatial, m_sg_b_spatial, m_sg_w_out, m_sb_w_qkv, m_sb_w_out, v_ln_mix_pre, v_ln_mix_post, v_ln_mem, v_ln_xa_pre, v_ln_xa_post, v_ln_ffn_pre, v_ln_ffn_post, v_xa_wq, v_xa_wkv, v_xa_wo, v_ffn_w_in, v_ffn_conv_w, v_ffn_conv_b, v_ffn_w_out, v_ssd_w_in, v_ssd_conv_w, v_ssd_conv_b, v_ssd_dt_bias, v_ssd_a_log, v_ssd_d, v_ssd_norm, v_ssd_w_out, v_sg_w_in, v_sg_v_norm_g, v_sg_v_norm_b, v_sg_w_spatial, v_sg_b_spatial, v_sg_w_out, v_sb_w_qkv, v_sb_w_out):
    p = dict(locals())
    x, mem, target = p["x"][0], p["mem"][0], p["loss_target"][0]
    T, D = x.shape
    depth = ln_mix_pre.shape[0]
    me = 4 * lax.axis_index("x") + 2 * lax.axis_index("y") + lax.axis_index("c")

    def gather(w, name):
        return sc_all_gather(w.astype(bf16), name)

    parts = {n: {} for n in BIG}
    pending, due = [], {}

    def scatter(name, layer, g8):
        pending.append((name, layer, g8))

    def ship_pending(carry, s):
        carry, *held = lax.optimization_barrier((carry, *[g8 for _, _, g8 in pending]))
        for (name, layer, _), g8 in zip(pending, held):
            parts[name][layer] = sc_scatter_parts(g8, "rs_" + name)
            due.setdefault(max(s - SCATTER_LAG, 0), []).append((name, layer))
        pending.clear()
        return carry

    def collect_due(carry, s):
        names = due.pop(s, [])
        if names:
            carry, *landed = lax.optimization_barrier((carry, *[parts[n][l] for n, l in names]))
            for (n, l), a in zip(names, landed):
                parts[n][l] = a
        return carry

    small8 = all_gather(_pack_rows([p[n] for n in SMALL_SHARDED]), "ag_small")
    full = {n: _merge_last(a) for n, a in zip(SMALL_SHARDED, _unpack_rows(small8, [p[n].shape for n in SMALL_SHARDED]))}

    grads = {n: {} for n in WEIGHTS}

    def ssd(h, j):
        g_in = gather(p["ssd_w_in"][j], "ag_ssd_in")
        g_in, h = lax.optimization_barrier((g_in, h))
        w_full = jnp.moveaxis(g_in, 0, 1).reshape(D, -1)
        w_out = gather(p["ssd_w_out"][j], "ag_ssd_out").reshape(1, -1, D)
        d_inner, conv_dim, heads = w_out.shape[1], full["ssd_conv_w"].shape[-1], ssd_dt_bias.shape[1]
        w_z, w_x = w_full[None, :, :d_inner], w_full[None, :, d_inner:d_inner + conv_dim]
        w_dt = jnp.pad(w_full[:, d_inner + conv_dim:], ((0, 0), (0, LANES - heads)))[None]
        lane_row = lambda a: jnp.pad(a[j:j + 1], ((0, 0), (0, LANES - heads)))
        bias, a_log, d_skip = lane_row(ssd_dt_bias), lane_row(ssd_a_log), lane_row(ssd_d)
        norm_g = full["ssd_norm"][j:j + 1]
        z = mm_nn(h, w_z, f32, "ssd_in_z")
        xbc_pre = mm_nn(h, w_x, f32, "ssd_in_x")
        dt_raw = mm_nn(h, w_dt, f32, "ssd_in_dt")
        streams = [(xbc_pre, 0, full["ssd_conv_w"][j], full["ssd_conv_b"][j:j + 1], 0)]
        xbc = conv_fwd(streams, epi_silu, conv_dim, f32, 4, "ssd_conv")
        y, states = ssd_fwd(xbc, dt_raw, bias, a_log, d_skip, "ssd_core")
        gated = row_fwd(fn_gate_norm, [y, z], [norm_g], [(d_inner, bf16)], "ssd_gate")[0]
        out = mm_nn(gated, w_out, f32, "ssd_out")

        def bwd(d_out):
            d_gated = mm_nt(d_out, w_out, f32, "ssd_out_dx")
            scatter("ssd_w_out", j, mm_tn(gated, d_out, D, bf16, "ssd_out_dw").reshape(N_DEV, -1, D))
            dy, dz, d_norm = row_bwd(fn_gate_norm, [y, z], [norm_g], [d_gated], [f32, bf16], "ssd_gate_bwd")
            dxs, dbm, dcm, ddt_g, d_bias, d_alog, d_skipg = ssd_bwd(xbc, dt_raw, bias, a_log, d_skip, states, dy, "ssd_core_bwd")
            (dx_pre,), (d_cw,), (d_cb,) = conv_bwd(streams, epi_silu, jnp.concatenate([dxs, dbm, dcm], axis=1), 4, "ssd_conv_bwd")
            ddt = jnp.sum(ddt_g, axis=0).astype(bf16)
            dh = mm_nt(dz, w_z, f32, "ssd_in_z_dx")
            dh = mm_nt(dx_pre, w_x, f32, "ssd_in_x_dx", init=dh)
            dh = mm_nt(ddt, w_dt, f32, "ssd_in_dt_dx", init=dh)
            dw = jnp.concatenate([mm_tn(h, dz, d_inner, bf16, "ssd_in_z_dw")[0], mm_tn(h, dx_pre, conv_dim, bf16, "ssd_in_x_dw")[0],
                                  mm_tn(h, ddt, LANES, bf16, "ssd_in_dt_dw")[0][:, :heads]], axis=1)
            scatter("ssd_w_in", j, _split_last(dw))
            grads["ssd_conv_w"][j], grads["ssd_conv_b"][j], grads["ssd_norm"][j] = d_cw, d_cb[0], d_norm[0]
            grads["ssd_dt_bias"][j], grads["ssd_a_log"][j], grads["ssd_d"][j] = d_bias[0, :heads], d_alog[0, :heads], d_skipg[0, :heads]
            return dh

        return out, bwd

    def sgu(h, j):
        w_in = gather(p["sg_w_in"][j], "ag_sg_in")
        w_out = gather(p["sg_w_out"][j], "ag_sg_out").reshape(1, -1, D)
        norm_g, norm_b = sg_v_norm_g[j:j + 1], sg_v_norm_b[j:j + 1]
        w_sp, b_sp = sg_w_spatial[j], sg_b_spatial[j][..., None]
        uv = mm_nn(h, w_in, f32, "sg_in")
        gated = sgu_fwd(uv, norm_g, norm_b, w_sp, b_sp, "sg_core")
        out = mm_nn(gated, w_out, f32, "sg_out")

        def bwd(d_out):
            d_gated = mm_nt(d_out, w_out, f32, "sg_out_dx")
            scatter("sg_w_out", j, mm_tn(gated, d_out, D, bf16, "sg_out_dw").reshape(N_DEV, -1, D))
            duv, d_ng, d_nb, d_ws, d_bs = sgu_bwd(uv, norm_g, norm_b, w_sp, b_sp, d_gated, "sg_core_bwd")
            grads["sg_v_norm_g"][j], grads["sg_v_norm_b"][j] = d_ng[0], d_nb[0]
            grads["sg_w_spatial"][j], grads["sg_b_spatial"][j] = d_ws, d_bs[..., 0]
            scatter("sg_w_in", j, mm_tn(h, duv, w_in.shape[2], bf16, "sg_in_dw"))
            return mm_nt(duv, w_in, f32, "sg_in_dx")

        return out, bwd

    def stick(h, j):
        w_qkv = gather(p["sb_w_qkv"][j], "ag_sb_qkv")
        w_out = gather(p["sb_w_out"][j], "ag_sb_out").reshape(1, -1, D)
        qkv = mm_nn(h, w_qkv, bf16, "sb_qkv")
        o, tot = sb_fwd(qkv, "sb_core")
        out = mm_nn(o, w_out, f32, "sb_out")

        def bwd(d_out):
            d_o = mm_nt(d_out, w_out, f32, "sb_out_dx")
            scatter("sb_w_out", j, mm_tn(o, d_out, D, bf16, "sb_out_dw").reshape(N_DEV, -1, D))
            dq, dk, dv = sb_bwd(qkv, tot, d_o, "sb_core_bwd")
            dqkv = jnp.concatenate([dq, dk.astype(bf16), dv.astype(bf16)], axis=1)
            scatter("sb_w_qkv", j, mm_tn(h, dqkv, w_qkv.shape[2], bf16, "sb_qkv_dw"))
            return mm_nt(dqkv, w_qkv, f32, "sb_qkv_dx")

        return out, bwd

    def cross(h, i):
        gain = ln_mem[i:i + 1]
        mem_n = row_fwd(fn_prenorm, [mem], [gain], [(D, bf16)], "mem_norm")[0]
        w_q = gather(p["xa_wq"][i], "ag_xa_q").reshape(1, D, -1)
        w_kv = gather(p["xa_wkv"][i], "ag_xa_kv").reshape(1, D, -1)
        w_o = gather(p["xa_wo"][i], "ag_xa_o")
        q = mm_nn(h, w_q, bf16, "xa_q")
        kv = mm_nn(mem_n, w_kv, bf16, "xa_kv")
        o = xa_fwd(q, kv, "xa_core")
        out = mm_nn(o, w_o, f32, "xa_out")

        def bwd(d_out):
            d_o = mm_nt(d_out, w_o, f32, "xa_out_dx")
            scatter("xa_wo", i, mm_tn(o, d_out, w_o.shape[2], bf16, "xa_out_dw"))
            dq, dk, dv = xa_bwd(q, kv, d_o, "xa_core_bwd")
            dkv = jnp.concatenate([dk, dv], axis=1).astype(bf16)
            scatter("xa_wq", i, mm_tn(h, dq, w_q.shape[2], bf16, "xa_q_dw").reshape(N_DEV, -1, w_q.shape[2]))
            scatter("xa_wkv", i, mm_tn(mem_n, dkv, w_kv.shape[2], bf16, "xa_kv_dw").reshape(N_DEV, -1, w_kv.shape[2]))
            d_mem_n = mm_nt(dkv, w_kv, f32, "xa_kv_dx")
            grads["ln_mem"][i] = row_bwd(fn_prenorm, [mem], [gain], [d_mem_n], [None], "mem_norm_bwd")[0][0]
            return mm_nt(dq, w_q, f32, "xa_q_dx")

        return out, bwd

    def ffn(h, i):
        w_in = gather(p["ffn_w_in"][i], "ag_ffn_in")
        w_out = gather(p["ffn_w_out"][i], "ag_ffn_out").reshape(1, -1, D)
        width = w_out.shape[1]
        conv_w, conv_b = full["ffn_conv_w"][i], ffn_conv_b[i:i + 1]
        gu = mm_nn(h, w_in, f32, "ffn_in")
        streams = [(gu, 0, conv_w, conv_b, 0), (gu, width // LANES, conv_w, conv_b, width // LANES)]
        act = conv_fwd(streams, epi_gelu_gate, width, bf16, 3, "ffn_gate")
        out = mm_nn(act, w_out, f32, "ffn_out")

        def bwd(d_out):
            d_act = mm_nt(d_out, w_out, f32, "ffn_out_dx")
            scatter("ffn_w_out", i, mm_tn(act, d_out, D, bf16, "ffn_out_dw").reshape(N_DEV, -1, D))
            d_pre, d_cw, d_cb = conv_bwd(streams, epi_gelu_gate, d_act, 3, "ffn_gate_bwd")
            dgu = jnp.concatenate(d_pre, axis=1)
            grads["ffn_conv_w"][i], grads["ffn_conv_b"][i] = jnp.concatenate(d_cw, axis=1), jnp.concatenate(d_cb, axis=1)[0]
            scatter("ffn_w_in", i, mm_tn(h, dgu, w_in.shape[2], bf16, "ffn_in_dw"))
            return mm_nt(dgu, w_in, f32, "ffn_in_dx")

        return out, bwd

    n_sub = 3 * depth
    pre = [w[i:i + 1] for i in range(depth) for w in (ln_mix_pre, ln_xa_pre, ln_ffn_pre)]
    post = [w[i:i + 1] for i in range(depth) for w in (ln_mix_post, ln_xa_post, ln_ffn_post)]
    stream, outs, bwds = [x], [], []
    h = row_fwd(fn_prenorm, [x], [pre[0]], [(D, bf16)], "pre_norm")[0]
    for s in range(n_sub):
        i, t = divmod(s, 3)
        out, bwd = ((ssd, sgu, stick)[i % 3](h, i // 3) if t == 0 else cross(h, i) if t == 1 else ffn(h, i))
        outs.append(out)
        bwds.append(bwd)
        if s < n_sub - 1:
            x_new, h = row_fwd(fn_resnorm, [stream[s], out], [post[s], pre[s + 1]], [(D, f32), (D, bf16)], "res_norm")
            stream.append(x_new)
    dy, loss = row_fwd(fn_final, [stream[-1], outs[-1], target], [post[-1]], [(D, f32), (1, f32)], "loss_head", n_acc=1)
    loss = lax.psum(loss[0, 0], ("x", "y", "c"))

    d_pre, d_post = [None] * n_sub, [None] * n_sub
    dx, d_out, d_post[-1] = row_bwd(fn_res, [stream[-1], outs[-1]], [post[-1]], [dy], [f32, bf16], "res_bwd")
    for s in reversed(range(n_sub)):
        dh = ship_pending(bwds[s](collect_due(d_out, s)), s)
        if s > 0:
            dx, d_out, d_post[s - 1], d_pre[s] = row_bwd(
                fn_resnorm, [stream[s - 1], outs[s - 1]], [post[s - 1], pre[s]], [dx, dh], [f32, bf16], "res_norm_bwd")
        else:
            grad_x, d_pre[0] = row_bwd(fn_xpre, [x], [pre[0]], [dx, dh], [f32], "pre_norm_bwd")
    for t, kind in enumerate(("mix", "xa", "ffn")):
        for i in range(depth):
            grads["ln_%s_pre" % kind][i] = d_pre[3 * i + t][0]
            grads["ln_%s_post" % kind][i] = d_post[3 * i + t][0]

    def stacked(name):
        return jnp.stack([grads[name][l] for l in range(len(grads[name]))], axis=0)

    new = {}
    for name in BIG:
        new[name] = adamw_sharded(p[name], [parts[name][l] for l in range(len(parts[name]))],
                                  p["m_" + name], p["v_" + name], "adamw_" + name)

    rep_shapes = [p[n].shape for n in REPLICATED]
    g8 = all_gather(_pack_rows([stacked(n) for n in REPLICATED]), "ag_grad_rep")
    rep = adamw_summed8(_pack_rows([p[n] for n in REPLICATED]), g8, _pack_rows([p["m_" + n] for n in REPLICATED]),
                        _pack_rows([p["v_" + n] for n in REPLICATED]), "adamw_rep")
    for k, packed in enumerate(rep):
        for n, a in zip(REPLICATED, _unpack_rows(packed, rep_shapes)):
            new.setdefault(n, [None] * 4)[k] = a

    sh_shapes = [p[n].shape for n in SMALL_SHARDED]
    by_owner = _pack_rows([_split_last(stacked(n)) for n in SMALL_SHARDED], lead=1)
    mine8 = lax.dynamic_index_in_dim(all_gather(by_owner, "ag_grad_small"), me, axis=1, keepdims=False)
    sh = adamw_summed8(_pack_rows([p[n] for n in SMALL_SHARDED]), mine8, _pack_rows([p["m_" + n] for n in SMALL_SHARDED]),
                       _pack_rows([p["v_" + n] for n in SMALL_SHARDED]), "adamw_small")
    for k, packed in enumerate(sh):
        for n, a in zip(SMALL_SHARDED, _unpack_rows(packed, sh_shapes)):
            new.setdefault(n, [None] * 4)[k] = a

    return (loss, grad_x[None], *[new[n][0] for n in WEIGHTS], *[new[n][1] for n in WEIGHTS],
            *[new[n][2] for n in WEIGHTS], *[new[n][3] for n in WEIGHTS])
```

```python
import functools

import jax
import jax.numpy as jnp
import numpy as np
from jax import lax
from jax.experimental import pallas as pl
from jax.experimental.pallas import tpu as pltpu
from jax.experimental.pallas import tpu_sc as plsc

f32 = jnp.float32
bf16 = jnp.bfloat16
HIGHEST = lax.Precision.HIGHEST
MESH = pl.DeviceIdType.MESH

V7X_VMEM_BYTES = 64 * 1024 * 1024
VMEM_LIMIT = V7X_VMEM_BYTES * 3 // 4
LANES = 128
SUBLANES = 8
BF16_ROWS = 16

EPS = 1e-6
ADAM_LR = 0.001
ADAM_B1 = 0.9
ADAM_B2 = 0.999
ADAM_EPS = 1e-08
ADAM_WD = 0.01
ADAM_STEP = 10

N_DEV = 8
XA_HEADS = 4
SSD_HEADS_PER_GROUP = 8
SSD_HEAD_DIM = 64
SSD_STATE = 128
SSD_GROUPS = 8
CHUNK = 128


def _cp(*sem):
    return pltpu.CompilerParams(dimension_semantics=sem or None, vmem_limit_bytes=VMEM_LIMIT)


def _dot(a, b, dims, precision=None):
    return lax.dot_general(a, b, (dims, ((), ())), precision=precision, preferred_element_type=f32)


def _nn(a, b, precision=None):
    return _dot(a, b, ((1,), (0,)), precision)


def _nt(a, b, precision=None):
    return _dot(a, b, ((1,), (1,)), precision)


def _tn(a, b, precision=None):
    return _dot(a, b, ((0,), (0,)), precision)


def _largest_tile(n, cap, step):
    for t in range(min(n, cap) // step * step, 0, -step):
        if n % t == 0:
            return t
    return n


def all_gather(x, name):
    def body(x_ref, out_ref, send_sems, recv_sems, local_sem):
        mx, my, mc = lax.axis_index("x"), lax.axis_index("y"), lax.axis_index("c")
        me, sibling = (mx, my, mc), (mx, my, 1 - mc)
        chips = [(1 - mx, my), (mx, 1 - my), (1 - mx, 1 - my)]

        def slot(px, py, pc):
            return out_ref.at[4 * px + 2 * py + pc]

        def copy(k, block, to, src=None):
            return pltpu.make_async_remote_copy(
                src_ref=slot(*block) if src is None else src, dst_ref=slot(*block),
                send_sem=send_sems.at[k], recv_sem=recv_sems.at[k], device_id=to, device_id_type=MESH)

        mine = pltpu.make_async_copy(x_ref, slot(*me), local_sem)
        mine.start()
        first = [copy(0, me, sibling, src=x_ref)]
        first += [copy(1 + j, me, (*chip, mc), src=x_ref) for j, chip in enumerate(chips)]
        for cp in first:
            cp.start()
        passed = [copy(4 + j, (*chip, mc), sibling) for j, chip in enumerate(chips)]
        for j, chip in enumerate(chips):
            copy(1 + j, (*chip, mc), me).wait_recv()
            passed[j].start()
        copy(0, sibling, me).wait_recv()
        for j, chip in enumerate(chips):
            copy(4 + j, (*chip, 1 - mc), me).wait_recv()
        for cp in first + passed:
            cp.wait_send()
        mine.wait()

    return pl.pallas_call(
        body, name=name,
        out_shape=jax.ShapeDtypeStruct((N_DEV,) + x.shape, x.dtype),
        in_specs=[pl.BlockSpec(memory_space=pl.ANY)],
        out_specs=pl.BlockSpec(memory_space=pl.ANY),
        scratch_shapes=[pltpu.SemaphoreType.DMA((7,)), pltpu.SemaphoreType.DMA((7,)), pltpu.SemaphoreType.DMA(())],
    )(x)


AG_COLLECTIVE_ID = 1
RS_COLLECTIVE_ID = 2
_SEQUENCER = dict(axis_name="sequencer", num_cores=1)
_HBM = pltpu.MemorySpace.HBM


def _peers():
    mx, my, mc = lax.axis_index("x"), lax.axis_index("y"), lax.axis_index("c")
    return [((mx + (k >> 2)) % 2, (my + ((k >> 1) & 1)) % 2, (mc + (k & 1)) % 2) for k in range(1, N_DEV)]


def _handshake_all(peers):
    barrier = pltpu.get_barrier_semaphore()
    for peer in peers:
        pl.semaphore_signal(barrier, inc=1, device_id=peer, device_id_type=MESH)
    pl.semaphore_wait(barrier, len(peers))


def _block_of(peer):
    return 4 * peer[0] + 2 * peer[1] + peer[2]


def sc_all_gather(x, name):
    x_ref = jax.new_ref(x, memory_space=_HBM)
    out_ref = jax.empty_ref(jax.ShapeDtypeStruct((N_DEV,) + x.shape, x.dtype), memory_space=_HBM)

    @pl.kernel(mesh=plsc.ScalarSubcoreMesh(**_SEQUENCER), name=name,
               scratch_types=(pltpu.SemaphoreType.DMA((N_DEV - 1,)), pltpu.SemaphoreType.DMA((N_DEV - 1,)),
                              pltpu.SemaphoreType.DMA(())),
               compiler_params=pltpu.CompilerParams(collective_id=AG_COLLECTIVE_ID))
    def launch(send_sems, recv_sems, local_sem):
        peers = _peers()
        _handshake_all(peers)
        me = 4 * lax.axis_index("x") + 2 * lax.axis_index("y") + lax.axis_index("c")
        mine = pltpu.make_async_copy(x_ref, out_ref.at[me], local_sem)
        mine.start()
        for k, peer in enumerate(peers):
            pltpu.make_async_remote_copy(src_ref=x_ref, dst_ref=out_ref.at[me], send_sem=send_sems.at[k],
                                         recv_sem=recv_sems.at[k], device_id=peer, device_id_type=MESH).start()
        for k, peer in enumerate(peers):
            pltpu.make_async_remote_copy(src_ref=x_ref, dst_ref=out_ref.at[_block_of(peer)], send_sem=send_sems.at[k],
                                         recv_sem=recv_sems.at[k], device_id=peer, device_id_type=MESH).wait()
        mine.wait()

    launch()
    return out_ref[...]


def sc_scatter_parts(g8, name):
    g_ref = jax.new_ref(g8, memory_space=_HBM)
    parts_ref = jax.empty_ref(jax.ShapeDtypeStruct(g8.shape, g8.dtype), memory_space=_HBM)

    @pl.kernel(mesh=plsc.ScalarSubcoreMesh(**_SEQUENCER), name=name,
               scratch_types=(pltpu.SemaphoreType.DMA((N_DEV - 1,)), pltpu.SemaphoreType.DMA((N_DEV - 1,)),
                              pltpu.SemaphoreType.DMA(())),
               compiler_params=pltpu.CompilerParams(collective_id=RS_COLLECTIVE_ID))
    def launch(send_sems, recv_sems, local_sem):
        peers = _peers()
        _handshake_all(peers)
        me = 4 * lax.axis_index("x") + 2 * lax.axis_index("y") + lax.axis_index("c")
        mine = pltpu.make_async_copy(g_ref.at[me], parts_ref.at[0], local_sem)
        mine.start()
        copies = [pltpu.make_async_remote_copy(src_ref=g_ref.at[_block_of(peer)], dst_ref=parts_ref.at[k + 1],
                                               send_sem=send_sems.at[k], recv_sem=recv_sems.at[k],
                                               device_id=peer, device_id_type=MESH) for k, peer in enumerate(peers)]
        for cp in copies:
            cp.start()
        for cp in copies:
            cp.wait()
        mine.wait()

    launch()
    return parts_ref[...]


def _as_lrc(a, lead):
    rest = a.shape[lead:]
    return a.reshape(a.shape[:lead] + (int(np.prod(rest[:-2], dtype=np.int64)),) + rest[-2:])


def _adam_math(w, g, m, v):
    m = ADAM_B1 * m + (1.0 - ADAM_B1) * g
    v = ADAM_B2 * v + (1.0 - ADAM_B2) * jnp.square(g)
    m_hat = m / (1.0 - ADAM_B1 ** ADAM_STEP)
    v_hat = v / (1.0 - ADAM_B2 ** ADAM_STEP)
    delta = -ADAM_LR * (m_hat / (jnp.sqrt(v_hat) + ADAM_EPS) + ADAM_WD * w)
    return delta, m, v


def adamw_sharded(w, parts, m, v, name):
    L, R, C = w.shape
    tr = _largest_tile(R, 64, BF16_ROWS)

    def body(w_ref, *refs):
        p_refs, (m_ref, v_ref, g_out, d_out, m_out, v_out) = refs[:L], refs[L:]
        for layer in range(L):
            @pl.when(pl.program_id(0) == layer)
            def _(p_ref=p_refs[layer]):
                g = p_ref[0].astype(f32)
                for k in range(1, N_DEV):
                    g = g + p_ref[k].astype(f32)
                delta, mn, vn = _adam_math(w_ref[...], g, m_ref[...], v_ref[...])
                g_out[...] = g
                d_out[...] = delta
                m_out[...] = mn
                v_out[...] = vn

    blk = pl.BlockSpec((None, tr, C), lambda l, r: (l, r, 0))
    part_specs = [pl.BlockSpec((N_DEV, tr, C), functools.partial(lambda layer, l, r: (0, jnp.where(l == layer, r, 0), 0), layer))
                  for layer in range(L)]
    return pl.pallas_call(
        body, name=name, out_shape=[jax.ShapeDtypeStruct(w.shape, f32)] * 4, grid=(L, R // tr),
        in_specs=[blk] + part_specs + [blk, blk], out_specs=[blk] * 4,
        compiler_params=_cp("arbitrary", "arbitrary"),
    )(w, *parts, m, v)


def adamw_summed8(w, g8, m, v, name):
    R, C = w.shape
    tr = _largest_tile(R, 512, SUBLANES)

    def body(w_ref, g_ref, m_ref, v_ref, g_out, d_out, m_out, v_out):
        g = g_ref[0]
        for d in range(1, N_DEV):
            g = g + g_ref[d]
        delta, mn, vn = _adam_math(w_ref[...], g, m_ref[...], v_ref[...])
        g_out[...] = g
        d_out[...] = delta
        m_out[...] = mn
        v_out[...] = vn

    blk = pl.BlockSpec((tr, C), lambda r: (r, 0))
    return pl.pallas_call(
        body, name=name, out_shape=[jax.ShapeDtypeStruct((R, C), f32)] * 4, grid=(R // tr,),
        in_specs=[blk, pl.BlockSpec((N_DEV, tr, C), lambda r: (0, r, 0)), blk, blk], out_specs=[blk] * 4,
        compiler_params=_cp("parallel"),
    )(w, g8, m, v)


def _tile_n(ns):
    if ns % 512 == 0:
        return 512
    if ns <= 1536:
        return ns
    return _largest_tile(ns, 512, LANES)


MM_VMEM_BUDGET = VMEM_LIMIT - 8 * 1024 * 1024


def _tiles_desc(n, cap, step):
    return [t for t in range(min(n, cap) // step * step, 0, -step) if n % t == 0] or [n]


def _mm_fits(in_tiles, out_tile, out_dtype, n_red, extra=0):
    rows, cols = out_tile
    total = sum(2 * 2 * r * c for r, c in in_tiles) + 2 * rows * cols * jnp.dtype(out_dtype).itemsize + extra
    if n_red > 1:
        total += 4 * rows * cols
    return total <= MM_VMEM_BUDGET


def _reduce_into(o_ref, acc, part, first, last):
    if acc is None:
        o_ref[...] = part().astype(o_ref.dtype)
        return

    @pl.when(first)
    def _():
        acc[...] = jnp.zeros_like(acc)

    acc[...] += part()

    @pl.when(last)
    def _():
        o_ref[...] = acc[...].astype(o_ref.dtype)


def mm_nn(a, w3, out_dtype, name):
    M, K = a.shape
    J, _, Ns = w3.shape
    tn = _tile_n(Ns)
    tm, tk = next(((tm, tk) for tk in _tiles_desc(K, 4096, LANES) for tm in _tiles_desc(M, 1024, BF16_ROWS)
                   if tm >= min(M, 256) and _mm_fits([(tm, tk), (tk, tn)], (tm, tn), out_dtype, K // tk)),
                  (min(M, 256), _largest_tile(K, 512, LANES)))
    nn, nk = Ns // tn, K // tk

    def body(a_ref, w_ref, o_ref, *acc):
        k = pl.program_id(3)
        _reduce_into(o_ref, acc[0] if acc else None, lambda: _nn(a_ref[...], w_ref[...]), k == 0, k == nk - 1)

    return pl.pallas_call(
        body, name=name, out_shape=jax.ShapeDtypeStruct((M, J * Ns), out_dtype),
        grid=(M // tm, J, nn, nk),
        in_specs=[pl.BlockSpec((tm, tk), lambda i, j, n, k: (i, k)),
                  pl.BlockSpec((None, tk, tn), lambda i, j, n, k: (j, k, n))],
        out_specs=pl.BlockSpec((tm, tn), lambda i, j, n, k: (i, j * nn + n)),
        scratch_shapes=[pltpu.VMEM((tm, tn), f32)] if nk > 1 else [],
        compiler_params=_cp("parallel", "parallel", "parallel", "arbitrary"),
    )(a, w3)


MM_NT_REDUCE_CAP = 5632


def mm_nt(g, w3, out_dtype, name, init=None):
    M = g.shape[0]
    J, K, Ns = w3.shape
    if Ns <= MM_NT_REDUCE_CAP:
        tn, jb = Ns, max(b for b in range(1, J + 1) if J % b == 0 and (b == 1 or b * Ns <= MM_NT_REDUCE_CAP))
    else:
        tn, jb = _largest_tile(Ns, MM_NT_REDUCE_CAP, LANES), 1
    nj, nn = J // jb, Ns // tn
    n_red = nj * nn + (init is not None)
    tk = _largest_tile(K, 512, LANES)
    tm = next((tm for tm in _tiles_desc(M, 1024, BF16_ROWS)
               if _mm_fits([(tm, jb * tn), (jb * tk, tn)], (tm, tk), out_dtype, n_red,
                           extra=0 if init is None else 2 * 4 * tm * tk)), min(M, 256))

    def body(*refs):
        g_ref, w_ref = refs[:2]
        i_ref = None if init is None else refs[2]
        o_ref = refs[2 + (init is not None)]
        acc = refs[3 + (init is not None):]
        j, n = pl.program_id(2), pl.program_id(3)
        first = (j == 0) & (n == 0)

        def part():
            prod = _nt(g_ref[:, :tn], w_ref[0])
            for b in range(1, jb):
                prod = prod + _nt(g_ref[:, b * tn:(b + 1) * tn], w_ref[b])
            return prod if init is None else prod + jnp.where(first, i_ref[...].astype(f32), 0.0)

        _reduce_into(o_ref, acc[0] if acc else None, part, first, (j == nj - 1) & (n == nn - 1))

    in_specs = [pl.BlockSpec((tm, jb * tn), lambda i, k, j, n: (i, j * nn + n)),
                pl.BlockSpec((jb, tk, tn), lambda i, k, j, n: (j, k, n))]
    args = [g, w3]
    if init is not None:
        in_specs.append(pl.BlockSpec((tm, tk), lambda i, k, j, n: (i, k)))
        args.append(init)
    return pl.pallas_call(
        body, name=name, out_shape=jax.ShapeDtypeStruct((M, K), out_dtype),
        grid=(M // tm, K // tk, nj, nn), in_specs=in_specs,
        out_specs=pl.BlockSpec((tm, tk), lambda i, k, j, n: (i, k)),
        scratch_shapes=[pltpu.VMEM((tm, tk), f32)] if n_red > 1 else [],
        compiler_params=_cp("parallel", "parallel", "arbitrary", "arbitrary"),
    )(*args)


def mm_tn(a, g, ns, out_dtype, name):
    M, K = a.shape
    J = g.shape[1] // ns
    tn, tk = _tile_n(ns), _largest_tile(K, 512, LANES)
    tm = next((tm for tm in _tiles_desc(M, 4096, BF16_ROWS)
               if _mm_fits([(tm, tk), (tm, tn)], (tk, tn), out_dtype, M // tm)), _largest_tile(M, 512, BF16_ROWS))
    nn, nm = ns // tn, M // tm

    def body(a_ref, g_ref, o_ref, *acc):
        m = pl.program_id(3)
        _reduce_into(o_ref, acc[0] if acc else None, lambda: _tn(a_ref[...], g_ref[...]), m == 0, m == nm - 1)

    return pl.pallas_call(
        body, name=name, out_shape=jax.ShapeDtypeStruct((J, K, ns), out_dtype),
        grid=(J, K // tk, nn, nm),
        in_specs=[pl.BlockSpec((tm, tk), lambda j, k, n, m: (m, k)),
                  pl.BlockSpec((tm, tn), lambda j, k, n, m: (m, j * nn + n))],
        out_specs=pl.BlockSpec((None, tk, tn), lambda j, k, n, m: (j, k, n)),
        scratch_shapes=[pltpu.VMEM((tk, tn), f32)] if nm > 1 else [],
        compiler_params=_cp("parallel", "parallel", "parallel", "arbitrary"),
    )(a, g)


ROW_TILE = 256
ROW_STEP = 64


def row_fwd(fn, rows, consts, outs, name, n_acc=0):
    T = rows[0].shape[0]
    tr = min(T, ROW_TILE)
    n_rows, n_consts, n_row_out = len(rows), len(consts), len(outs) - n_acc

    def body(*refs):
        r_refs, c_refs, o_refs = refs[:n_rows], refs[n_rows:n_rows + n_consts], refs[n_rows + n_consts:]
        cs = [c[...] for c in c_refs]
        acc_refs = o_refs[n_row_out:]
        if n_acc:
            @pl.when(pl.program_id(0) == 0)
            def _():
                for a in acc_refs:
                    a[...] = jnp.zeros_like(a)

        def step(s, carry):
            rows_s = pl.ds(pl.multiple_of(s * ROW_STEP, ROW_STEP), ROW_STEP)
            res = fn(*[r[rows_s, :].astype(f32) for r in r_refs], *cs)
            for o, val in zip(o_refs[:n_row_out], res[:n_row_out]):
                o[rows_s, :] = val.astype(o.dtype)
            return tuple(c + val for c, val in zip(carry, res[n_row_out:]))

        accs = lax.fori_loop(0, tr // ROW_STEP, step, tuple(jnp.zeros((1, c), f32) for c, _ in outs[n_row_out:]))
        for a, val in zip(acc_refs, accs):
            a[...] += val

    in_specs = [pl.BlockSpec((tr, r.shape[1]), lambda i: (i, 0)) for r in rows]
    in_specs += [pl.BlockSpec(c.shape, lambda i: (0, 0)) for c in consts]
    out_shape = [jax.ShapeDtypeStruct((T, c), dt) for c, dt in outs[:n_row_out]]
    out_shape += [jax.ShapeDtypeStruct((1, c), f32) for c, _ in outs[n_row_out:]]
    out_specs = [pl.BlockSpec((tr, c), lambda i: (i, 0)) for c, _ in outs[:n_row_out]]
    out_specs += [pl.BlockSpec((1, c), lambda i: (0, 0)) for c, _ in outs[n_row_out:]]
    return pl.pallas_call(
        body, name=name, out_shape=out_shape, grid=(T // tr,), in_specs=in_specs, out_specs=out_specs,
        compiler_params=_cp("arbitrary" if n_acc else "parallel"),
    )(*rows, *consts)


def row_bwd(fn, rows, consts, cts, grad_dtypes, name):
    T = rows[0].shape[0]
    tr = min(T, ROW_TILE)
    n_rows, n_consts, n_cts = len(rows), len(consts), len(cts)
    wanted = [i for i, dt in enumerate(grad_dtypes) if dt is not None]

    def body(*refs):
        r_refs = refs[:n_rows]
        c_refs = refs[n_rows:n_rows + n_consts]
        t_refs = refs[n_rows + n_consts:n_rows + n_consts + n_cts]
        o_refs = refs[n_rows + n_consts + n_cts:]
        gr_refs, gc_refs = o_refs[:len(wanted)], o_refs[len(wanted):]
        cs = [c[...] for c in c_refs]

        @pl.when(pl.program_id(0) == 0)
        def _():
            for a in gc_refs:
                a[...] = jnp.zeros_like(a)

        def step(s, carry):
            rows_s = pl.ds(pl.multiple_of(s * ROW_STEP, ROW_STEP), ROW_STEP)
            ins = [r[rows_s, :].astype(f32) for r in r_refs]
            _, vjp = jax.vjp(lambda *a: tuple(fn(*a)), *ins, *cs)
            grads = vjp(tuple(t[rows_s, :].astype(f32) for t in t_refs))
            for o, i in zip(gr_refs, wanted):
                o[rows_s, :] = grads[i].astype(o.dtype)
            return tuple(c + gval for c, gval in zip(carry, grads[n_rows:]))

        accs = lax.fori_loop(0, tr // ROW_STEP, step, tuple(jnp.zeros(c.shape, f32) for c in consts))
        for a, val in zip(gc_refs, accs):
            a[...] += val

    in_specs = [pl.BlockSpec((tr, r.shape[1]), lambda i: (i, 0)) for r in list(rows) + list(cts)]
    in_specs[n_rows:n_rows] = [pl.BlockSpec(c.shape, lambda i: (0, 0)) for c in consts]
    out_shape = [jax.ShapeDtypeStruct(rows[i].shape, grad_dtypes[i]) for i in wanted]
    out_shape += [jax.ShapeDtypeStruct(c.shape, f32) for c in consts]
    out_specs = [pl.BlockSpec((tr, rows[i].shape[1]), lambda i_: (i_, 0)) for i in wanted]
    out_specs += [pl.BlockSpec(c.shape, lambda i: (0, 0)) for c in consts]
    return pl.pallas_call(
        body, name=name, out_shape=out_shape, grid=(T // tr,), in_specs=in_specs, out_specs=out_specs,
        compiler_params=_cp("arbitrary"),
    )(*rows, *consts, *cts)


def _rms(x, g):
    return x * lax.rsqrt(jnp.mean(x * x, axis=-1, keepdims=True) + EPS) * g


def fn_prenorm(x, g):
    return (_rms(x, g),)


def fn_resnorm(x, m, g_post, g_pre):
    x_new = x + _rms(m, g_post)
    return x_new, _rms(x_new, g_pre)


def fn_res(x, m, g_post):
    return (x + _rms(m, g_post),)


def fn_final(x, m, target, g_post):
    err = x + _rms(m, g_post) - target
    n = err.shape[-1]
    return err / n, (0.5 / n) * jnp.sum(jnp.sum(err * err, axis=1, keepdims=True), axis=0, keepdims=True)


def fn_gate_norm(y, z, g):
    return (_rms(y * jax.nn.silu(z), g),)


CONV_ROWS = 256


def _conv_chunk(pre_ref, w, b, r, rb, K):
    t0 = pl.multiple_of(r * rb, rb)
    halo_at = pl.multiple_of(jnp.maximum(t0 - SUBLANES, 0), SUBLANES)
    halo = jnp.where(r > 0, pre_ref[pl.ds(halo_at, SUBLANES), :], 0.0)
    main = pre_ref[pl.ds(t0, rb), :]
    ext = jnp.concatenate([halo, main], axis=0)
    shifted = [main if k == K - 1 else pltpu.roll(ext, K - 1 - k, 0)[SUBLANES:] for k in range(K)]
    conv = b
    for k in range(K):
        conv = conv + w[k:k + 1] * shifted[k]
    return conv, shifted


def conv_fwd(streams, epilogue, out_cols, out_dtype, K, name):
    T = streams[0][0].shape[0]
    rb = min(T, CONV_ROWS)
    S = len(streams)

    def body(*refs):
        pre_refs, w_refs, b_refs, o_ref = refs[:S], refs[S:2 * S], refs[2 * S:3 * S], refs[3 * S]
        ws = [w[...] for w in w_refs]
        bs = [b[...] for b in b_refs]

        def step(r, carry):
            convs = [_conv_chunk(pre_refs[s], ws[s], bs[s], r, rb, K)[0] for s in range(S)]
            o_ref[pl.ds(pl.multiple_of(r * rb, rb), rb), :] = epilogue(*convs).astype(o_ref.dtype)
            return carry

        lax.fori_loop(0, T // rb, step, 0)

    in_specs = [pl.BlockSpec((T, LANES), functools.partial(lambda off, i: (0, off + i), st[1])) for st in streams]
    in_specs += [pl.BlockSpec((K, LANES), functools.partial(lambda off, i: (0, off + i), st[4])) for st in streams]
    in_specs += [pl.BlockSpec((1, LANES), functools.partial(lambda off, i: (0, off + i), st[4])) for st in streams]
    return pl.pallas_call(
        body, name=name, out_shape=jax.ShapeDtypeStruct((T, out_cols), out_dtype), grid=(out_cols // LANES,),
        in_specs=in_specs, out_specs=pl.BlockSpec((T, LANES), lambda i: (0, i)),
        compiler_params=_cp("parallel"),
    )(*[st[0] for st in streams], *[st[2] for st in streams], *[st[3] for st in streams])


def conv_bwd(streams, epilogue, dout, K, name):
    T, cols = dout.shape
    rb = min(T, CONV_ROWS)
    S = len(streams)

    def body(*refs):
        pre_refs, w_refs, b_refs, dout_ref = refs[:S], refs[S:2 * S], refs[2 * S:3 * S], refs[3 * S]
        o = refs[3 * S + 1:]
        dpre_refs, dw_refs, db_refs, scr = o[:S], o[S:2 * S], o[2 * S:3 * S], o[3 * S:]
        ws = [w[...] for w in w_refs]
        bs = [b[...] for b in b_refs]
        for s in range(S):
            scr[s][pl.ds(T, SUBLANES), :] = jnp.zeros((SUBLANES, LANES), f32)

        def phase1(r, carry):
            rows = pl.ds(pl.multiple_of(r * rb, rb), rb)
            convs, shifted = zip(*[_conv_chunk(pre_refs[s], ws[s], bs[s], r, rb, K) for s in range(S)])
            _, vjp = jax.vjp(epilogue, *convs)
            dconvs = vjp(dout_ref[rows, :].astype(f32))
            new = []
            for s in range(S):
                scr[s][rows, :] = dconvs[s]
                sums = [jnp.sum(dconvs[s] * shifted[s][k], axis=0, keepdims=True) for k in range(K)]
                sums.append(jnp.sum(dconvs[s], axis=0, keepdims=True))
                new.append(tuple(c + v for c, v in zip(carry[s], sums)))
            return tuple(new)

        zero = tuple(tuple(jnp.zeros((1, LANES), f32) for _ in range(K + 1)) for _ in range(S))
        sums = lax.fori_loop(0, T // rb, phase1, zero)
        tap = lax.broadcasted_iota(jnp.int32, (K, LANES), 0)
        for s in range(S):
            dw = jnp.zeros((K, LANES), f32)
            for k in range(K):
                dw = jnp.where(tap == k, sums[s][k], dw)
            dw_refs[s][...] = dw
            db_refs[s][...] = sums[s][K]

        def phase2(r, carry):
            t0 = pl.multiple_of(r * rb, rb)
            for s in range(S):
                ext = scr[s][pl.ds(t0, rb + SUBLANES), :]
                dpre = ws[s][K - 1:K] * ext[:rb]
                for k in range(K - 1):
                    j = K - 1 - k
                    dpre = dpre + ws[s][k:k + 1] * pltpu.roll(ext, rb + SUBLANES - j, 0)[:rb]
                dpre_refs[s][pl.ds(t0, rb), :] = dpre.astype(dpre_refs[s].dtype)
            return carry

        lax.fori_loop(0, T // rb, phase2, 0)

    in_specs = [pl.BlockSpec((T, LANES), functools.partial(lambda off, i: (0, off + i), st[1])) for st in streams]
    in_specs += [pl.BlockSpec((K, LANES), functools.partial(lambda off, i: (0, off + i), st[4])) for st in streams]
    in_specs += [pl.BlockSpec((1, LANES), functools.partial(lambda off, i: (0, off + i), st[4])) for st in streams]
    in_specs += [pl.BlockSpec((T, LANES), lambda i: (0, i))]
    out_shape = [jax.ShapeDtypeStruct((T, cols), bf16)] * S
    out_shape += [jax.ShapeDtypeStruct((K, cols), f32)] * S + [jax.ShapeDtypeStruct((1, cols), f32)] * S
    out_specs = [pl.BlockSpec((T, LANES), lambda i: (0, i))] * S
    out_specs += [pl.BlockSpec((K, LANES), lambda i: (0, i))] * S + [pl.BlockSpec((1, LANES), lambda i: (0, i))] * S
    res = pl.pallas_call(
        body, name=name, out_shape=out_shape, grid=(cols // LANES,), in_specs=in_specs, out_specs=out_specs,
        scratch_shapes=[pltpu.VMEM((T + SUBLANES, LANES), f32)] * S,
        compiler_params=_cp("parallel"),
    )(*[st[0] for st in streams], *[st[2] for st in streams], *[st[3] for st in streams], dout)
    return res[:S], res[S:2 * S], res[2 * S:]


def epi_gelu_gate(cg, cu):
    return jax.nn.gelu(cg, approximate=True) * cu


def epi_silu(c):
    return jax.nn.silu(c)


XA_ROWS = 256


def _xa_fn(q, k, v):
    s = _nt(q.astype(bf16), k.astype(bf16)) * (q.shape[-1] ** -0.5)
    p = jax.nn.softmax(s, axis=-1)
    return _nn(p.astype(bf16), v.astype(bf16))


def xa_fwd(q, kv, name):
    T, W = q.shape
    M = kv.shape[0]
    H = W // LANES
    tr = min(T, XA_ROWS)

    def body(q_ref, k_ref, v_ref, o_ref):
        o_ref[...] = _xa_fn(q_ref[...].astype(f32), k_ref[...].astype(f32), v_ref[...].astype(f32)).astype(o_ref.dtype)

    return pl.pallas_call(
        body, name=name, out_shape=jax.ShapeDtypeStruct((T, W), bf16), grid=(T // tr, H),
        in_specs=[pl.BlockSpec((tr, LANES), lambda i, h: (i, h)),
                  pl.BlockSpec((M, LANES), lambda i, h: (0, h)),
                  pl.BlockSpec((M, LANES), lambda i, h: (0, H + h))],
        out_specs=pl.BlockSpec((tr, LANES), lambda i, h: (i, h)),
        compiler_params=_cp("parallel", "parallel"),
    )(q, kv, kv)


def xa_bwd(q, kv, do, name):
    T, W = q.shape
    M = kv.shape[0]
    H = W // LANES
    tr = min(T, XA_ROWS)

    def body(q_ref, k_ref, v_ref, do_ref, dq_ref, dk_ref, dv_ref):
        @pl.when(pl.program_id(1) == 0)
        def _():
            dk_ref[...] = jnp.zeros_like(dk_ref)
            dv_ref[...] = jnp.zeros_like(dv_ref)

        _, vjp = jax.vjp(_xa_fn, q_ref[...].astype(f32), k_ref[...].astype(f32), v_ref[...].astype(f32))
        dq, dk, dv = vjp(do_ref[...].astype(f32))
        dq_ref[...] = dq.astype(dq_ref.dtype)
        dk_ref[...] += dk
        dv_ref[...] += dv

    return pl.pallas_call(
        body, name=name,
        out_shape=[jax.ShapeDtypeStruct((T, W), bf16), jax.ShapeDtypeStruct((M, W), f32), jax.ShapeDtypeStruct((M, W), f32)],
        grid=(H, T // tr),
        in_specs=[pl.BlockSpec((tr, LANES), lambda h, i: (i, h)),
                  pl.BlockSpec((M, LANES), lambda h, i: (0, h)),
                  pl.BlockSpec((M, LANES), lambda h, i: (0, H + h)),
                  pl.BlockSpec((tr, LANES), lambda h, i: (i, h))],
        out_specs=[pl.BlockSpec((tr, LANES), lambda h, i: (i, h)),
                   pl.BlockSpec((M, LANES), lambda h, i: (0, h)),
                   pl.BlockSpec((M, LANES), lambda h, i: (0, h))],
        compiler_params=_cp("parallel", "arbitrary"),
    )(q, kv, kv, do)


def _sgu_norm_fn(v_pre, g, b):
    v = jax.nn.gelu(v_pre, approximate=True)
    mu = jnp.mean(v, axis=-1, keepdims=True)
    vc = v - mu
    return vc * lax.rsqrt(jnp.mean(vc * vc, axis=-1, keepdims=True) + EPS) * g + b


def _sgu_mix_fn(u_pre, vn, w, b):
    q = w.shape[0]
    tril = lax.broadcasted_iota(jnp.int32, (q, q), 0) >= lax.broadcasted_iota(jnp.int32, (q, q), 1)
    mixed = _nn(jnp.where(tril, w, 0.0).astype(bf16), vn.astype(bf16)) + b
    return jax.nn.gelu(u_pre, approximate=True) * mixed


def _sgu_norm_phase(v_ref, g, b, vn_ref):
    def step(s, carry):
        rows = pl.ds(pl.multiple_of(s * ROW_STEP, ROW_STEP), ROW_STEP)
        vn_ref[rows, :] = _sgu_norm_fn(v_ref[rows, :], g, b)
        return carry

    lax.fori_loop(0, CHUNK // ROW_STEP, step, 0)


def sgu_fwd(uv_pre, norm_g, norm_b, w_sp, b_sp, name):
    T, W2 = uv_pre.shape
    W = W2 // 2
    G = w_sp.shape[0]
    gw = W // G

    def body(u_ref, v_ref, g_ref, b_ref, ws_ref, bs_ref, o_ref, vn_ref):
        _sgu_norm_phase(v_ref, g_ref[...], b_ref[...], vn_ref)

        def group(gi, carry):
            cols = pl.ds(pl.multiple_of(gi * gw, LANES), gw)
            o_ref[:, cols] = _sgu_mix_fn(u_ref[:, cols], vn_ref[:, cols], ws_ref[gi], bs_ref[gi]).astype(o_ref.dtype)
            return carry

        lax.fori_loop(0, G, group, 0)

    full = lambda a: pl.BlockSpec(a.shape, lambda c: (0,) * a.ndim)
    return pl.pallas_call(
        body, name=name, out_shape=jax.ShapeDtypeStruct((T, W), bf16), grid=(T // CHUNK,),
        in_specs=[pl.BlockSpec((CHUNK, W), lambda c: (c, 0)), pl.BlockSpec((CHUNK, W), lambda c: (c, 1)),
                  full(norm_g), full(norm_b), full(w_sp), full(b_sp)],
        out_specs=pl.BlockSpec((CHUNK, W), lambda c: (c, 0)),
        scratch_shapes=[pltpu.VMEM((CHUNK, W), f32)],
        compiler_params=_cp("parallel"),
    )(uv_pre, uv_pre, norm_g, norm_b, w_sp, b_sp)


def sgu_bwd(uv_pre, norm_g, norm_b, w_sp, b_sp, dout, name):
    T, W2 = uv_pre.shape
    W = W2 // 2
    G = w_sp.shape[0]
    gw = W // G

    def body(u_ref, v_ref, g_ref, b_ref, ws_ref, bs_ref, do_ref, duv_ref, dg_ref, db_ref, dws_ref, dbs_ref,
             vn_ref, dvn_ref):
        @pl.when(pl.program_id(0) == 0)
        def _():
            for a in (dg_ref, db_ref, dws_ref, dbs_ref):
                a[...] = jnp.zeros_like(a)

        g, b = g_ref[...], b_ref[...]
        _sgu_norm_phase(v_ref, g, b, vn_ref)

        def group(gi, carry):
            cols = pl.ds(pl.multiple_of(gi * gw, LANES), gw)
            _, vjp = jax.vjp(_sgu_mix_fn, u_ref[:, cols], vn_ref[:, cols], ws_ref[gi], bs_ref[gi])
            du, dvn, dw, dbias = vjp(do_ref[:, cols])
            duv_ref[:, cols] = du.astype(duv_ref.dtype)
            dvn_ref[:, cols] = dvn
            dws_ref[gi] += dw
            dbs_ref[gi] += dbias
            return carry

        lax.fori_loop(0, G, group, 0)

        def step(s, carry):
            rows = pl.ds(pl.multiple_of(s * ROW_STEP, ROW_STEP), ROW_STEP)
            _, vjp = jax.vjp(_sgu_norm_fn, v_ref[rows, :], g, b)
            dv, dg, dbn = vjp(dvn_ref[rows, :])
            duv_ref[rows, pl.ds(W, W)] = dv.astype(duv_ref.dtype)
            return carry[0] + dg, carry[1] + dbn

        dg, dbn = lax.fori_loop(0, CHUNK // ROW_STEP, step, (jnp.zeros((1, W), f32), jnp.zeros((1, W), f32)))
        dg_ref[...] += dg
        db_ref[...] += dbn

    full = lambda a: pl.BlockSpec(a.shape, lambda c: (0,) * a.ndim)
    return pl.pallas_call(
        body, name=name,
        out_shape=[jax.ShapeDtypeStruct((T, W2), bf16), jax.ShapeDtypeStruct((1, W), f32), jax.ShapeDtypeStruct((1, W), f32),
                   jax.ShapeDtypeStruct(w_sp.shape, f32), jax.ShapeDtypeStruct(b_sp.shape, f32)],
        grid=(T // CHUNK,),
        in_specs=[pl.BlockSpec((CHUNK, W), lambda c: (c, 0)), pl.BlockSpec((CHUNK, W), lambda c: (c, 1)),
                  full(norm_g), full(norm_b), full(w_sp), full(b_sp), pl.BlockSpec((CHUNK, W), lambda c: (c, 0))],
        out_specs=[pl.BlockSpec((CHUNK, W2), lambda c: (c, 0)), full(norm_g), full(norm_b), full(w_sp), full(b_sp)],
        scratch_shapes=[pltpu.VMEM((CHUNK, W), f32), pltpu.VMEM((CHUNK, W), f32)],
        compiler_params=_cp("arbitrary"),
    )(uv_pre, uv_pre, norm_g, norm_b, w_sp, b_sp, dout)


SB_SUM_COLS = 256


def _sb_key_group(T):
    return 512 if T % 512 == 0 else T


def _sb_query_rows(T):
    return 256 if T % 256 == 0 else CHUNK


def _sb_block(q, k, i, g):
    qb, kg = q.shape[0], k.shape[0]
    z = _nt(q, k) * (q.shape[-1] ** -0.5)
    t_idx = i * qb + lax.broadcasted_iota(jnp.int32, (qb, kg), 0)
    s_idx = g * kg + lax.broadcasted_iota(jnp.int32, (qb, kg), 1)
    valid = s_idx < t_idx
    sp = jnp.log1p(jnp.exp(-jnp.abs(z)))
    log_beta = jnp.minimum(z, 0.0) - sp
    log_1mb = jnp.where(valid, -jnp.maximum(z, 0.0) - sp, 0.0)
    return z, valid, log_beta, log_1mb


def _order_matrix(later):
    r = lax.broadcasted_iota(jnp.int32, (SB_SUM_COLS, SB_SUM_COLS), 0)
    c = lax.broadcasted_iota(jnp.int32, (SB_SUM_COLS, SB_SUM_COLS), 1)
    return (r > c if later else r < c).astype(bf16)


def _masked_sums(parts, order):
    terms = []
    for x in parts:
        hi = x.astype(bf16)
        rest = x - hi.astype(f32)
        mid = rest.astype(bf16)
        terms += [hi, mid, (rest - mid.astype(f32)).astype(bf16)]
    rows = parts[0].shape[0]
    prod = _nn(jnp.concatenate(terms, axis=0), order)
    piece = lambda n: prod[n * rows:(n + 1) * rows]
    return [piece(3 * p) + piece(3 * p + 1) + piece(3 * p + 2) for p in range(len(parts))]


def sb_fwd(qkv, name):
    T = qkv.shape[0]
    H = qkv.shape[1] // (3 * LANES)
    qb, kg = _sb_query_rows(T), _sb_key_group(T)
    halves = kg // SB_SUM_COLS

    def body(q_ref, k_ref, v_ref, o_ref, tot_ref):
        i = pl.program_id(1)
        q = q_ref[...]
        later = _order_matrix(True)
        n_groups = (i * qb + qb + kg - 1) // kg

        def step(gg, carry):
            acc, run = carry
            g = n_groups - 1 - gg
            rows = pl.ds(pl.multiple_of(g * kg, kg), kg)
            _, valid, log_beta, log_1mb = _sb_block(q, k_ref[rows, :], i, g)
            parts = [log_1mb[:, hh * SB_SUM_COLS:(hh + 1) * SB_SUM_COLS] for hh in range(halves)]
            tails = _masked_sums(parts, later)
            for hh in reversed(range(halves)):
                tails[hh] = tails[hh] + run
                run = run + jnp.sum(parts[hh], axis=1, keepdims=True)
            a = jnp.where(valid, jnp.exp(log_beta + jnp.concatenate(tails, axis=1)), 0.0)
            return acc + _nn(a.astype(bf16), v_ref[rows, :]), run

        acc, run = lax.fori_loop(0, n_groups, step, (jnp.zeros((qb, LANES), f32), jnp.zeros((qb, 1), f32)))
        o_ref[...] = acc.astype(o_ref.dtype)
        tot_ref[...] = run

    return pl.pallas_call(
        body, name=name,
        out_shape=[jax.ShapeDtypeStruct((T, H * LANES), bf16), jax.ShapeDtypeStruct((H, T, 1), f32)],
        grid=(H, T // qb),
        in_specs=[pl.BlockSpec((qb, LANES), lambda h, i: (i, h)),
                  pl.BlockSpec((T, LANES), lambda h, i: (0, H + h)),
                  pl.BlockSpec((T, LANES), lambda h, i: (0, 2 * H + h))],
        out_specs=[pl.BlockSpec((qb, LANES), lambda h, i: (i, h)),
                   pl.BlockSpec((None, qb, 1), lambda h, i: (h, i, 0))],
        compiler_params=_cp("parallel", "parallel"),
    )(qkv, qkv, qkv)


def sb_bwd(qkv, tot, do, name):
    T = qkv.shape[0]
    H = qkv.shape[1] // (3 * LANES)
    qb, kg = _sb_query_rows(T), _sb_key_group(T)
    halves = kg // SB_SUM_COLS

    def body(q_ref, k_ref, v_ref, tot_ref, do_ref, dq_ref, dk_ref, dv_ref):
        i = pl.program_id(1)

        @pl.when(i == 0)
        def _():
            dk_ref[...] = jnp.zeros_like(dk_ref)
            dv_ref[...] = jnp.zeros_like(dv_ref)

        q = q_ref[...]
        do = do_ref[...].astype(bf16)
        tot = tot_ref[...]
        later, earlier = _order_matrix(True), _order_matrix(False)
        scale = q.shape[-1] ** -0.5
        n_groups = (i * qb + qb + kg - 1) // kg

        def step(g, carry):
            dq, before, d_run = carry
            rows = pl.ds(pl.multiple_of(g * kg, kg), kg)
            k, v = k_ref[rows, :], v_ref[rows, :]
            z, valid, log_beta, log_1mb = _sb_block(q, k, i, g)
            parts = [log_1mb[:, hh * SB_SUM_COLS:(hh + 1) * SB_SUM_COLS] for hh in range(halves)]
            tails = _masked_sums(parts, later)
            for hh in range(halves):
                before = before + jnp.sum(parts[hh], axis=1, keepdims=True)
                tails[hh] = tails[hh] + (tot - before)
            a = jnp.where(valid, jnp.exp(log_beta + jnp.concatenate(tails, axis=1)), 0.0)
            d_e = _nt(do, v) * a
            parts = [d_e[:, hh * SB_SUM_COLS:(hh + 1) * SB_SUM_COLS] for hh in range(halves)]
            d_l1 = _masked_sums(parts, earlier)
            for hh in range(halves):
                d_l1[hh] = d_l1[hh] + d_run
                d_run = d_run + jnp.sum(parts[hh], axis=1, keepdims=True)
            d_l1 = jnp.where(valid, jnp.concatenate(d_l1, axis=1), 0.0)
            sig = jax.nn.sigmoid(z)
            dz = ((d_e * (1.0 - sig) - d_l1 * sig) * scale).astype(bf16)
            dk_ref[rows, :] += _tn(dz, q)
            dv_ref[rows, :] += _tn(a.astype(bf16), do)
            return dq + _nn(dz, k), before, d_run

        zero_col = jnp.zeros((qb, 1), f32)
        dq, _, _ = lax.fori_loop(0, n_groups, step, (jnp.zeros((qb, LANES), f32), zero_col, zero_col))
        dq_ref[...] = dq.astype(dq_ref.dtype)

    W = H * LANES
    return pl.pallas_call(
        body, name=name,
        out_shape=[jax.ShapeDtypeStruct((T, W), bf16), jax.ShapeDtypeStruct((T, W), f32), jax.ShapeDtypeStruct((T, W), f32)],
        grid=(H, T // qb),
        in_specs=[pl.BlockSpec((qb, LANES), lambda h, i: (i, h)),
                  pl.BlockSpec((T, LANES), lambda h, i: (0, H + h)),
                  pl.BlockSpec((T, LANES), lambda h, i: (0, 2 * H + h)),
                  pl.BlockSpec((None, qb, 1), lambda h, i: (h, i, 0)),
                  pl.BlockSpec((qb, LANES), lambda h, i: (i, h))],
        out_specs=[pl.BlockSpec((qb, LANES), lambda h, i: (i, h)),
                   pl.BlockSpec((T, LANES), lambda h, i: (0, h)),
                   pl.BlockSpec((T, LANES), lambda h, i: (0, h))],
        compiler_params=_cp("parallel", "arbitrary"),
    )(qkv, qkv, qkv, tot, do)


def _softplus(x):
    return jnp.maximum(x, 0.0) + jnp.log1p(jnp.exp(-jnp.abs(x)))


def _ssd_chunk_fn(head0, xs, b_mat, c_mat, dt_raw, dt_bias, a_log, d_skip, prev):
    q = dt_raw.shape[0]
    lane = lax.broadcasted_iota(jnp.int32, (q, LANES), 1)
    sub = lax.broadcasted_iota(jnp.int32, (q, LANES), 0)
    causal = sub >= lane
    dt = _softplus(dt_raw + dt_bias)
    a_cum = _nn(causal.astype(f32), dt * (-jnp.exp(a_log)), HIGHEST)
    a_cum_t = a_cum.T
    cb = _nt(c_mat.astype(bf16), b_mat.astype(bf16))
    bm, cm = b_mat.astype(bf16), c_mat.astype(bf16)
    ys, new = [], []
    for r in range(len(xs)):
        in_lane, in_sub = lane == head0 + r, sub == head0 + r
        col_a = jnp.sum(jnp.where(in_lane, a_cum, 0.0), axis=1, keepdims=True)
        row_a = jnp.sum(jnp.where(in_sub, a_cum_t, 0.0), axis=0, keepdims=True)
        col_dt = jnp.sum(jnp.where(in_lane, dt, 0.0), axis=1, keepdims=True)
        skip = jnp.sum(jnp.where(in_lane[:1], d_skip, 0.0), axis=1, keepdims=True)
        a_last = jnp.sum(jnp.where(sub[:, :1] == q - 1, col_a, 0.0), axis=0, keepdims=True)
        decay_in = jnp.exp(jnp.where(causal, col_a - row_a, -jnp.inf))
        xdt = xs[r] * col_dt
        y_diag = _nn((cb * decay_in).astype(bf16), xdt.astype(bf16))
        y_off = _nt(cm, prev[r].astype(bf16)) * jnp.exp(col_a)
        ys.append(y_diag + y_off + xs[r] * skip)
        state = _tn((xdt * jnp.exp(a_last - col_a)).astype(bf16), bm)
        new.append(prev[r] * jnp.exp(a_last) + state)
    return ys, new


SSD_GROUPS_PER_STEP = 2


def _ssd_specs(T, G, reverse):
    nc = T // CHUNK
    R, P, N = SSD_HEADS_PER_GROUP, SSD_HEAD_DIM, SSD_STATE
    gp = SSD_GROUPS_PER_STEP if G % SSD_GROUPS_PER_STEP == 0 else 1
    ch = (lambda c: nc - 1 - c) if reverse else (lambda c: c)
    xs = pl.BlockSpec((CHUNK, gp * R * P), lambda g, c: (ch(c), g))
    bm = pl.BlockSpec((CHUNK, gp * N), lambda g, c: (ch(c), G * R * P // (gp * N) + g))
    cm = pl.BlockSpec((CHUNK, gp * N), lambda g, c: (ch(c), (G * R * P + G * N) // (gp * N) + g))
    dt = pl.BlockSpec((CHUNK, LANES), lambda g, c: (ch(c), 0))
    row = pl.BlockSpec((1, LANES), lambda g, c: (0, 0))
    st = pl.BlockSpec((gp, None, R * P, N), lambda g, c: (g, ch(c), 0, 0))
    return nc, gp, xs, bm, cm, dt, row, st


def ssd_fwd(xbc, dt_raw, dt_bias, a_log, d_skip, name):
    T = xbc.shape[0]
    R, P, N = SSD_HEADS_PER_GROUP, SSD_HEAD_DIM, SSD_STATE
    G = xbc.shape[1] // (R * P + 2 * N)
    nc, gp, xs_s, bm_s, cm_s, dt_s, row_s, st_s = _ssd_specs(T, G, False)

    def body(xs_ref, b_ref, c_ref, dt_ref, bias_ref, alog_ref, skip_ref, y_ref, st_ref, state):
        @pl.when(pl.program_id(1) == 0)
        def _():
            state[...] = jnp.zeros_like(state)

        done = []
        for gi in range(gp):
            x = xs_ref[:, gi * R * P:(gi + 1) * R * P]
            xs = [x[:, r * P:(r + 1) * P] for r in range(R)]
            prev = [state[gi, r] for r in range(R)]
            ys, new = _ssd_chunk_fn((pl.program_id(0) * gp + gi) * R, xs, b_ref[:, gi * N:(gi + 1) * N],
                                    c_ref[:, gi * N:(gi + 1) * N], dt_ref[...], bias_ref[...], alog_ref[...],
                                    skip_ref[...], prev)
            done.append((prev, ys, new))
        for gi, (prev, ys, new) in enumerate(done):
            y_ref[:, gi * R * P:(gi + 1) * R * P] = jnp.concatenate(ys, axis=1)
            for r in range(R):
                st_ref[gi, pl.ds(r * P, P), :] = prev[r]
                state[gi, r] = new[r]

    return pl.pallas_call(
        body, name=name,
        out_shape=[jax.ShapeDtypeStruct((T, G * R * P), f32), jax.ShapeDtypeStruct((G, nc, R * P, N), f32)],
        grid=(G // gp, nc), in_specs=[xs_s, bm_s, cm_s, dt_s, row_s, row_s, row_s], out_specs=[xs_s, st_s],
        scratch_shapes=[pltpu.VMEM((gp, R, P, N), f32)],
        compiler_params=_cp("parallel", "arbitrary"),
    )(xbc, xbc, xbc, dt_raw, dt_bias, a_log, d_skip)


def ssd_bwd(xbc, dt_raw, dt_bias, a_log, d_skip, states, dy, name):
    T = xbc.shape[0]
    R, P, N = SSD_HEADS_PER_GROUP, SSD_HEAD_DIM, SSD_STATE
    G = xbc.shape[1] // (R * P + 2 * N)
    nc, gp, xs_s, bm_s, cm_s, dt_s, row_s, st_s = _ssd_specs(T, G, True)

    def body(xs_ref, b_ref, c_ref, dt_ref, bias_ref, alog_ref, skip_ref, st_ref, dy_ref,
             dx_ref, db_ref, dc_ref, ddt_ref, dbias_ref, dalog_ref, dskip_ref, dstate):
        g, c = pl.program_id(0), pl.program_id(1)

        @pl.when(c == 0)
        def _():
            dstate[...] = jnp.zeros_like(dstate)

        @pl.when((c == 0) & (g == 0))
        def _():
            for a in (dbias_ref, dalog_ref, dskip_ref):
                a[...] = jnp.zeros_like(a)

        done = []
        for gi in range(gp):
            cols, ncols = slice(gi * R * P, (gi + 1) * R * P), slice(gi * N, (gi + 1) * N)
            x, dyv = xs_ref[:, cols], dy_ref[:, cols]
            xs = [x[:, r * P:(r + 1) * P] for r in range(R)]
            prev = [st_ref[gi, pl.ds(r * P, P), :] for r in range(R)]
            _, vjp = jax.vjp(functools.partial(_ssd_chunk_fn, (g * gp + gi) * R), xs, b_ref[:, ncols], c_ref[:, ncols],
                             dt_ref[...], bias_ref[...], alog_ref[...], skip_ref[...], prev)
            done.append(vjp(([dyv[:, r * P:(r + 1) * P] for r in range(R)], [dstate[gi, r] for r in range(R)])))
        for gi, (dxs, dbm, dcm, ddt, dbias, dalog, dskip, dprev) in enumerate(done):
            dx_ref[:, gi * R * P:(gi + 1) * R * P] = jnp.concatenate(dxs, axis=1)
            db_ref[:, gi * N:(gi + 1) * N] = dbm
            dc_ref[:, gi * N:(gi + 1) * N] = dcm
            ddt_ref[gi] = ddt
            for r in range(R):
                dstate[gi, r] = dprev[r]
        dbias_ref[...] += sum(d[4] for d in done)
        dalog_ref[...] += sum(d[5] for d in done)
        dskip_ref[...] += sum(d[6] for d in done)

    small = pl.BlockSpec((CHUNK, gp * N), lambda g, c: (nc - 1 - c, g))
    return pl.pallas_call(
        body, name=name,
        out_shape=[jax.ShapeDtypeStruct((T, G * R * P), f32), jax.ShapeDtypeStruct((T, G * N), f32),
                   jax.ShapeDtypeStruct((T, G * N), f32), jax.ShapeDtypeStruct((G, T, LANES), f32)]
        + [jax.ShapeDtypeStruct((1, LANES), f32)] * 3,
        grid=(G // gp, nc), in_specs=[xs_s, bm_s, cm_s, dt_s, row_s, row_s, row_s, st_s, xs_s],
        out_specs=[xs_s, small, small, pl.BlockSpec((gp, CHUNK, LANES), lambda g, c: (g, nc - 1 - c, 0)),
                   row_s, row_s, row_s],
        scratch_shapes=[pltpu.VMEM((gp, R, P, N), f32)],
        compiler_params=_cp("arbitrary", "arbitrary"),
    )(xbc, xbc, xbc, dt_raw, dt_bias, a_log, d_skip, states, dy)


WEIGHTS = ["ln_mix_pre", "ln_mix_post", "ln_mem", "ln_xa_pre", "ln_xa_post", "ln_ffn_pre", "ln_ffn_post",
           "xa_wq", "xa_wkv", "xa_wo", "ffn_w_in", "ffn_conv_w", "ffn_conv_b", "ffn_w_out",
           "ssd_w_in", "ssd_conv_w", "ssd_conv_b", "ssd_dt_bias", "ssd_a_log", "ssd_d", "ssd_norm", "ssd_w_out",
           "sg_w_in", "sg_v_norm_g", "sg_v_norm_b", "sg_w_spatial", "sg_b_spatial", "sg_w_out", "sb_w_qkv", "sb_w_out"]
BIG = ["xa_wq", "xa_wkv", "xa_wo", "ffn_w_in", "ffn_w_out", "ssd_w_in", "ssd_w_out", "sg_w_in", "sg_w_out",
       "sb_w_qkv", "sb_w_out"]
SMALL_SHARDED = ["ffn_conv_w", "ssd_conv_w", "ssd_conv_b", "ssd_norm"]
REPLICATED = [n for n in WEIGHTS if n not in BIG and n not in SMALL_SHARDED]


def fn_xpre(x, g):
    return x, _rms(x, g)


def _pack_rows(arrs, lead=0):
    head = arrs[0].shape[:lead]
    flat = jnp.concatenate([a.reshape(head + (-1,)) for a in arrs], axis=-1)
    n = flat.shape[-1]
    rows = -(-n // (SUBLANES * LANES)) * SUBLANES
    flat = jnp.pad(flat, [(0, 0)] * lead + [(0, rows * LANES - n)])
    return flat.reshape(head + (rows, LANES))


def _unpack_rows(packed, shapes):
    head = packed.shape[:-2]
    flat = packed.reshape(head + (-1,))
    out, off = [], 0
    for shp in shapes:
        n = int(np.prod(shp, dtype=np.int64))
        out.append(flat[..., off:off + n].reshape(head + tuple(shp)))
        off += n
    return out


def _merge_last(a8):
    return jnp.moveaxis(a8, 0, -2).reshape(a8.shape[1:-1] + (N_DEV * a8.shape[-1],))


def _split_last(a):
    return jnp.moveaxis(a.reshape(a.shape[:-1] + (N_DEV, a.shape[-1] // N_DEV)), -2, 0)


def kernel(x, mem, ln_mix_pre, ln_mix_post, ln_mem, ln_xa_pre, ln_xa_post, ln_ffn_pre, ln_ffn_post, xa_wq, xa_wkv, xa_wo, ffn_w_in, ffn_conv_w, ffn_conv_b, ffn_w_out, ssd_w_in, ssd_conv_w, ssd_conv_b, ssd_dt_bias, ssd_a_log, ssd_d, ssd_norm, ssd_w_out, sg_w_in, sg_v_norm_g, sg_v_norm_b, sg_w_spatial, sg_b_spatial, sg_w_out, sb_w_qkv, sb_w_out, loss_target, m_ln_mix_pre, m_ln_mix_post, m_ln_mem, m_ln_xa_pre, m_ln_xa_post, m_ln_ffn_pre, m_ln_ffn_post, m_xa_wq, m_xa_wkv, m_xa_wo, m_ffn_w_in, m_ffn_conv_w, m_ffn_conv_b, m_ffn_w_out, m_ssd_w_in, m_ssd_conv_w, m_ssd_conv_b, m_ssd_dt_bias, m_ssd_a_log, m_ssd_d, m_ssd_norm, m_ssd_w_out, m_sg_w_in, m_sg_v_norm_g, m_sg_v_norm_b, m_sg_w_spatial, m_sg_b_spatial, m_sg_w_out, m_sb_w_qkv, m_sb_w_out, v_ln_mix_pre, v_ln_mix_post, v_ln_mem, v_ln_xa_pre, v_ln_xa_post, v_ln_ffn_pre, v_ln_ffn_post, v_xa_wq, v_xa_wkv, v_xa_wo, v_ffn_w_in, v_ffn_conv_w, v_ffn_conv_b, v_ffn_w_out, v_ssd_w_in, v_ssd_conv_w, v_ssd_conv_b, v_ssd_dt_bias, v_ssd_a_log, v_ssd_d, v_ssd_norm, v_ssd_w_out, v_sg_w_in, v_sg_v_norm_g, v_sg_v_norm_b, v_sg_w_spatial, v_sg_b_spatial, v_sg_w_out, v_sb_w_qkv, v_sb_w_out):
    p = dict(locals())
    x, mem, target = p["x"][0], p["mem"][0], p["loss_target"][0]
    T, D = x.shape
    depth = ln_mix_pre.shape[0]
    me = 4 * lax.axis_index("x") + 2 * lax.axis_index("y") + lax.axis_index("c")

    def gather(w, name):
        return sc_all_gather(w.astype(bf16), name)

    parts = {n: {} for n in BIG}
    pending, due = [], {}

    def scatter(name, layer, g8):
        pending.append((name, layer, g8))

    def deadline(name, s):
        if name == "ffn_w_out":
            return s - 2, 0
        if name == "ffn_w_in":
            return s - 2, 1
        if name.startswith("xa_"):
            return (s - 2, 0) if s >= 2 else (0, 1)
        return (s - 1, 0) if name.endswith("_out") else (s - 2, 0)

    def ship_pending(carry, s):
        carry, *held = lax.optimization_barrier((carry, *[g8 for _, _, g8 in pending]))
        for (name, layer, _), g8 in zip(pending, held):
            parts[name][layer] = sc_scatter_parts(g8, "rs_" + name)
            due.setdefault(deadline(name, s), []).append((name, layer))
        pending.clear()
        return carry

    def collect_due(carry, s):
        names = due.pop(s, [])
        if names:
            carry, *landed = lax.optimization_barrier((carry, *[parts[n][l] for n, l in names]))
            for (n, l), a in zip(names, landed):
                parts[n][l] = a
        return carry

    small8 = all_gather(_pack_rows([p[n] for n in SMALL_SHARDED]), "ag_small")
    full = {n: _merge_last(a) for n, a in zip(SMALL_SHARDED, _unpack_rows(small8, [p[n].shape for n in SMALL_SHARDED]))}

    grads = {n: {} for n in WEIGHTS}

    def ssd(h, j):
        g_in = gather(p["ssd_w_in"][j], "ag_ssd_in")
        g_in, h = lax.optimization_barrier((g_in, h))
        w_full = jnp.moveaxis(g_in, 0, 1).reshape(D, -1)
        w_out = gather(p["ssd_w_out"][j], "ag_ssd_out").reshape(1, -1, D)
        d_inner, conv_dim, heads = w_out.shape[1], full["ssd_conv_w"].shape[-1], ssd_dt_bias.shape[1]
        w_z, w_x = w_full[None, :, :d_inner], w_full[None, :, d_inner:d_inner + conv_dim]
        w_dt = jnp.pad(w_full[:, d_inner + conv_dim:], ((0, 0), (0, LANES - heads)))[None]
        lane_row = lambda a: jnp.pad(a[j:j + 1], ((0, 0), (0, LANES - heads)))
        bias, a_log, d_skip = lane_row(ssd_dt_bias), lane_row(ssd_a_log), lane_row(ssd_d)
        norm_g = full["ssd_norm"][j:j + 1]
        z = mm_nn(h, w_z, f32, "ssd_in_z")
        xbc_pre = mm_nn(h, w_x, f32, "ssd_in_x")
        dt_raw = mm_nn(h, w_dt, f32, "ssd_in_dt")
        streams = [(xbc_pre, 0, full["ssd_conv_w"][j], full["ssd_conv_b"][j:j + 1], 0)]
        xbc = conv_fwd(streams, epi_silu, conv_dim, f32, 4, "ssd_conv")
        y, states = ssd_fwd(xbc, dt_raw, bias, a_log, d_skip, "ssd_core")
        gated = row_fwd(fn_gate_norm, [y, z], [norm_g], [(d_inner, bf16)], "ssd_gate")[0]
        out = mm_nn(gated, w_out, f32, "ssd_out")

        def bwd(d_out, mid):
            d_gated = mm_nt(d_out, w_out, f32, "ssd_out_dx")
            scatter("ssd_w_out", j, mm_tn(gated, d_out, D, bf16, "ssd_out_dw").reshape(N_DEV, -1, D))
            dy, dz, d_norm = row_bwd(fn_gate_norm, [y, z], [norm_g], [d_gated], [f32, bf16], "ssd_gate_bwd")
            dxs, dbm, dcm, ddt_g, d_bias, d_alog, d_skipg = ssd_bwd(xbc, dt_raw, bias, a_log, d_skip, states, dy, "ssd_core_bwd")
            dxs = mid(dxs)
            (dx_pre,), (d_cw,), (d_cb,) = conv_bwd(streams, epi_silu, jnp.concatenate([dxs, dbm, dcm], axis=1), 4, "ssd_conv_bwd")
            ddt = jnp.sum(ddt_g, axis=0).astype(bf16)
            dh = mm_nt(dz, w_z, f32, "ssd_in_z_dx")
            dh = mm_nt(dx_pre, w_x, f32, "ssd_in_x_dx", init=dh)
            dh = mm_nt(ddt, w_dt, f32, "ssd_in_dt_dx", init=dh)
            dw = jnp.concatenate([mm_tn(h, dz, d_inner, bf16, "ssd_in_z_dw")[0], mm_tn(h, dx_pre, conv_dim, bf16, "ssd_in_x_dw")[0],
                                  mm_tn(h, ddt, LANES, bf16, "ssd_in_dt_dw")[0][:, :heads]], axis=1)
            scatter("ssd_w_in", j, _split_last(dw))
            grads["ssd_conv_w"][j], grads["ssd_conv_b"][j], grads["ssd_norm"][j] = d_cw, d_cb[0], d_norm[0]
            grads["ssd_dt_bias"][j], grads["ssd_a_log"][j], grads["ssd_d"][j] = d_bias[0, :heads], d_alog[0, :heads], d_skipg[0, :heads]
            return dh

        return out, bwd

    def sgu(h, j):
        w_in = gather(p["sg_w_in"][j], "ag_sg_in")
        w_out = gather(p["sg_w_out"][j], "ag_sg_out").reshape(1, -1, D)
        norm_g, norm_b = sg_v_norm_g[j:j + 1], sg_v_norm_b[j:j + 1]
        w_sp, b_sp = sg_w_spatial[j], sg_b_spatial[j][..., None]
        uv = mm_nn(h, w_in, f32, "sg_in")
        gated = sgu_fwd(uv, norm_g, norm_b, w_sp, b_sp, "sg_core")
        out = mm_nn(gated, w_out, f32, "sg_out")

        def bwd(d_out, mid):
            d_gated = mm_nt(d_out, w_out, f32, "sg_out_dx")
            scatter("sg_w_out", j, mm_tn(gated, d_out, D, bf16, "sg_out_dw").reshape(N_DEV, -1, D))
            duv, d_ng, d_nb, d_ws, d_bs = sgu_bwd(uv, norm_g, norm_b, w_sp, b_sp, d_gated, "sg_core_bwd")
            duv = mid(duv)
            grads["sg_v_norm_g"][j], grads["sg_v_norm_b"][j] = d_ng[0], d_nb[0]
            grads["sg_w_spatial"][j], grads["sg_b_spatial"][j] = d_ws, d_bs[..., 0]
            scatter("sg_w_in", j, mm_tn(h, duv, w_in.shape[2], bf16, "sg_in_dw"))
            return mm_nt(duv, w_in, f32, "sg_in_dx")

        return out, bwd

    def stick(h, j):
        w_qkv = gather(p["sb_w_qkv"][j], "ag_sb_qkv")
        w_out = gather(p["sb_w_out"][j], "ag_sb_out").reshape(1, -1, D)
        qkv = mm_nn(h, w_qkv, bf16, "sb_qkv")
        o, tot = sb_fwd(qkv, "sb_core")
        out = mm_nn(o, w_out, f32, "sb_out")

        def bwd(d_out, mid):
            d_o = mm_nt(d_out, w_out, f32, "sb_out_dx")
            scatter("sb_w_out", j, mm_tn(o, d_out, D, bf16, "sb_out_dw").reshape(N_DEV, -1, D))
            dq, dk, dv = sb_bwd(qkv, tot, d_o, "sb_core_bwd")
            dq = mid(dq)
            dqkv = jnp.concatenate([dq, dk.astype(bf16), dv.astype(bf16)], axis=1)
            scatter("sb_w_qkv", j, mm_tn(h, dqkv, w_qkv.shape[2], bf16, "sb_qkv_dw"))
            return mm_nt(dqkv, w_qkv, f32, "sb_qkv_dx")

        return out, bwd

    def cross(h, i):
        gain = ln_mem[i:i + 1]
        mem_n = row_fwd(fn_prenorm, [mem], [gain], [(D, bf16)], "mem_norm")[0]
        w_q = gather(p["xa_wq"][i], "ag_xa_q").reshape(1, D, -1)
        w_kv = gather(p["xa_wkv"][i], "ag_xa_kv").reshape(1, D, -1)
        w_o = gather(p["xa_wo"][i], "ag_xa_o")
        q = mm_nn(h, w_q, bf16, "xa_q")
        kv = mm_nn(mem_n, w_kv, bf16, "xa_kv")
        o = xa_fwd(q, kv, "xa_core")
        out = mm_nn(o, w_o, f32, "xa_out")

        def bwd(d_out, mid):
            d_o = mm_nt(d_out, w_o, f32, "xa_out_dx")
            scatter("xa_wo", i, mm_tn(o, d_out, w_o.shape[2], bf16, "xa_out_dw"))
            dq, dk, dv = xa_bwd(q, kv, d_o, "xa_core_bwd")
            dkv = jnp.concatenate([dk, dv], axis=1).astype(bf16)
            scatter("xa_wq", i, mm_tn(h, dq, w_q.shape[2], bf16, "xa_q_dw").reshape(N_DEV, -1, w_q.shape[2]))
            scatter("xa_wkv", i, mm_tn(mem_n, dkv, w_kv.shape[2], bf16, "xa_kv_dw").reshape(N_DEV, -1, w_kv.shape[2]))
            d_mem_n = mm_nt(dkv, w_kv, f32, "xa_kv_dx")
            grads["ln_mem"][i] = row_bwd(fn_prenorm, [mem], [gain], [d_mem_n], [None], "mem_norm_bwd")[0][0]
            return mm_nt(dq, w_q, f32, "xa_q_dx")

        return out, bwd

    def ffn(h, i):
        w_in = gather(p["ffn_w_in"][i], "ag_ffn_in")
        w_out = gather(p["ffn_w_out"][i], "ag_ffn_out").reshape(1, -1, D)
        width = w_out.shape[1]
        conv_w, conv_b = full["ffn_conv_w"][i], ffn_conv_b[i:i + 1]
        gu = mm_nn(h, w_in, f32, "ffn_in")
        streams = [(gu, 0, conv_w, conv_b, 0), (gu, width // LANES, conv_w, conv_b, width // LANES)]
        act = conv_fwd(streams, epi_gelu_gate, width, bf16, 3, "ffn_gate")
        out = mm_nn(act, w_out, f32, "ffn_out")

        def bwd(d_out, mid):
            d_act = mm_nt(d_out, w_out, f32, "ffn_out_dx")
            scatter("ffn_w_out", i, mm_tn(act, d_out, D, bf16, "ffn_out_dw").reshape(N_DEV, -1, D))
            d_pre, d_cw, d_cb = conv_bwd(streams, epi_gelu_gate, d_act, 3, "ffn_gate_bwd")
            dgu = jnp.concatenate(d_pre, axis=1)
            grads["ffn_conv_w"][i], grads["ffn_conv_b"][i] = jnp.concatenate(d_cw, axis=1), jnp.concatenate(d_cb, axis=1)[0]
            scatter("ffn_w_in", i, mm_tn(h, dgu, w_in.shape[2], bf16, "ffn_in_dw"))
            return mm_nt(dgu, w_in, f32, "ffn_in_dx")

        return out, bwd

    n_sub = 3 * depth
    pre = [w[i:i + 1] for i in range(depth) for w in (ln_mix_pre, ln_xa_pre, ln_ffn_pre)]
    post = [w[i:i + 1] for i in range(depth) for w in (ln_mix_post, ln_xa_post, ln_ffn_post)]
    stream, outs, bwds = [x], [], []
    h = row_fwd(fn_prenorm, [x], [pre[0]], [(D, bf16)], "pre_norm")[0]
    for s in range(n_sub):
        i, t = divmod(s, 3)
        out, bwd = ((ssd, sgu, stick)[i % 3](h, i // 3) if t == 0 else cross(h, i) if t == 1 else ffn(h, i))
        outs.append(out)
        bwds.append(bwd)
        if s < n_sub - 1:
            x_new, h = row_fwd(fn_resnorm, [stream[s], out], [post[s], pre[s + 1]], [(D, f32), (D, bf16)], "res_norm")
            stream.append(x_new)
    dy, loss = row_fwd(fn_final, [stream[-1], outs[-1], target], [post[-1]], [(D, f32), (1, f32)], "loss_head", n_acc=1)
    loss = lax.psum(loss[0, 0], ("x", "y", "c"))

    d_pre, d_post = [None] * n_sub, [None] * n_sub
    dx, d_out, d_post[-1] = row_bwd(fn_res, [stream[-1], outs[-1]], [post[-1]], [dy], [f32, bf16], "res_bwd")
    for s in reversed(range(n_sub)):
        dh = ship_pending(bwds[s](collect_due(d_out, (s, 0)), functools.partial(collect_due, s=(s, 1))), s)
        if s > 0:
            dx, d_out, d_post[s - 1], d_pre[s] = row_bwd(
                fn_resnorm, [stream[s - 1], outs[s - 1]], [post[s - 1], pre[s]], [dx, dh], [f32, bf16], "res_norm_bwd")
        else:
            grad_x, d_pre[0] = row_bwd(fn_xpre, [x], [pre[0]], [dx, dh], [f32], "pre_norm_bwd")
    for t, kind in enumerate(("mix", "xa", "ffn")):
        for i in range(depth):
            grads["ln_%s_pre" % kind][i] = d_pre[3 * i + t][0]
            grads["ln_%s_post" % kind][i] = d_post[3 * i + t][0]

    def stacked(name):
        return jnp.stack([grads[name][l] for l in range(len(grads[name]))], axis=0)

    new = {}
    for name in BIG:
        new[name] = adamw_sharded(p[name], [parts[name][l] for l in range(len(parts[name]))],
                                  p["m_" + name], p["v_" + name], "adamw_" + name)

    rep_shapes = [p[n].shape for n in REPLICATED]
    g8 = all_gather(_pack_rows([stacked(n) for n in REPLICATED]), "ag_grad_rep")
    rep = adamw_summed8(_pack_rows([p[n] for n in REPLICATED]), g8, _pack_rows([p["m_" + n] for n in REPLICATED]),
                        _pack_rows([p["v_" + n] for n in REPLICATED]), "adamw_rep")
    for k, packed in enumerate(rep):
        for n, a in zip(REPLICATED, _unpack_rows(packed, rep_shapes)):
            new.setdefault(n, [None] * 4)[k] = a

    sh_shapes = [p[n].shape for n in SMALL_SHARDED]
    by_owner = _pack_rows([_split_last(stacked(n)) for n in SMALL_SHARDED], lead=1)
    mine8 = lax.dynamic_index_in_dim(all_gather(by_owner, "ag_grad_small"), me, axis=1, keepdims=False)
    sh = adamw_summed8(_pack_rows([p[n] for n in SMALL_SHARDED]), mine8, _pack_rows([p["m_" + n] for n in SMALL_SHARDED]),
                       _pack_rows([p["v_" + n] for n in SMALL_SHARDED]), "adamw_small")
    for k, packed in enumerate(sh):
        for n, a in zip(SMALL_SHARDED, _unpack_rows(packed, sh_shapes)):
            new.setdefault(n, [None] * 4)[k] = a

    return (loss, grad_x[None], *[new[n][0] for n in WEIGHTS], *[new[n][1] for n in WEIGHTS],
            *[new[n][2] for n in WEIGHTS], *[new[n][3] for n in WEIGHTS])
```

```python
import functools

import jax
import jax.numpy as jnp
import numpy as np
from jax import lax
from jax.experimental import pallas as pl
from jax.experimental.pallas import tpu as pltpu
from jax.experimental.pallas import tpu_sc as plsc

f32 = jnp.float32
bf16 = jnp.bfloat16
HIGHEST = lax.Precision.HIGHEST
MESH = pl.DeviceIdType.MESH

V7X_VMEM_BYTES = 64 * 1024 * 1024
VMEM_LIMIT = V7X_VMEM_BYTES * 3 // 4
LANES = 128
SUBLANES = 8
BF16_ROWS = 16

EPS = 1e-6
ADAM_LR = 0.001
ADAM_B1 = 0.9
ADAM_B2 = 0.999
ADAM_EPS = 1e-08
ADAM_WD = 0.01
ADAM_STEP = 10

N_DEV = 8
XA_HEADS = 4
SSD_HEADS_PER_GROUP = 8
SSD_HEAD_DIM = 64
SSD_STATE = 128
SSD_GROUPS = 8
CHUNK = 128


def _cp(*sem):
    return pltpu.CompilerParams(dimension_semantics=sem or None, vmem_limit_bytes=VMEM_LIMIT)


def _dot(a, b, dims, precision=None):
    return lax.dot_general(a, b, (dims, ((), ())), precision=precision, preferred_element_type=f32)


def _nn(a, b, precision=None):
    return _dot(a, b, ((1,), (0,)), precision)


def _nt(a, b, precision=None):
    return _dot(a, b, ((1,), (1,)), precision)


def _tn(a, b, precision=None):
    return _dot(a, b, ((0,), (0,)), precision)


def _largest_tile(n, cap, step):
    for t in range(min(n, cap) // step * step, 0, -step):
        if n % t == 0:
            return t
    return n


def _gather_partners():
    mx, my, mc = lax.axis_index("x"), lax.axis_index("y"), lax.axis_index("c")
    chips = [(1 - mx, my), (mx, 1 - my), (1 - mx, 1 - my)]
    return (mx, my, mc), (mx, my, 1 - mc), chips


def _two_level_gather(x_ref, out_ref, send_sems, recv_sems, local_sem):
    me, sibling, chips = _gather_partners()
    mc = me[2]

    def slot(px, py, pc):
        return out_ref.at[4 * px + 2 * py + pc]

    def copy(k, block, to, src=None):
        return pltpu.make_async_remote_copy(
            src_ref=slot(*block) if src is None else src, dst_ref=slot(*block),
            send_sem=send_sems.at[k], recv_sem=recv_sems.at[k], device_id=to, device_id_type=MESH)

    mine = pltpu.make_async_copy(x_ref, slot(*me), local_sem)
    mine.start()
    first = [copy(0, me, sibling, src=x_ref)]
    first += [copy(1 + j, me, (*chip, mc), src=x_ref) for j, chip in enumerate(chips)]
    for cp in first:
        cp.start()
    passed = [copy(4 + j, (*chip, mc), sibling) for j, chip in enumerate(chips)]
    for j, chip in enumerate(chips):
        copy(1 + j, (*chip, mc), me).wait_recv()
        passed[j].start()
    copy(0, sibling, me).wait_recv()
    for j, chip in enumerate(chips):
        copy(4 + j, (*chip, 1 - mc), me).wait_recv()
    for cp in first + passed:
        cp.wait_send()
    mine.wait()


def all_gather(x, name):
    return pl.pallas_call(
        functools.partial(_two_level_gather), name=name,
        out_shape=jax.ShapeDtypeStruct((N_DEV,) + x.shape, x.dtype),
        in_specs=[pl.BlockSpec(memory_space=pl.ANY)],
        out_specs=pl.BlockSpec(memory_space=pl.ANY),
        scratch_shapes=[pltpu.SemaphoreType.DMA((7,)), pltpu.SemaphoreType.DMA((7,)), pltpu.SemaphoreType.DMA(())],
    )(x)


AG_COLLECTIVE_ID = 1
RS_COLLECTIVE_ID = 2
_SEQUENCER = dict(axis_name="sequencer", num_cores=1)
_HBM = pltpu.MemorySpace.HBM


def _peers():
    mx, my, mc = lax.axis_index("x"), lax.axis_index("y"), lax.axis_index("c")
    return [((mx + (k >> 2)) % 2, (my + ((k >> 1) & 1)) % 2, (mc + (k & 1)) % 2) for k in range(1, N_DEV)]


def _handshake_all(peers):
    barrier = pltpu.get_barrier_semaphore()
    for peer in peers:
        pl.semaphore_signal(barrier, inc=1, device_id=peer, device_id_type=MESH)
    pl.semaphore_wait(barrier, len(peers))


def _block_of(peer):
    return 4 * peer[0] + 2 * peer[1] + peer[2]


def sc_all_gather(x, name):
    x_ref = jax.new_ref(x, memory_space=_HBM)
    out_ref = jax.empty_ref(jax.ShapeDtypeStruct((N_DEV,) + x.shape, x.dtype), memory_space=_HBM)

    @pl.kernel(mesh=plsc.ScalarSubcoreMesh(**_SEQUENCER), name=name,
               scratch_types=(pltpu.SemaphoreType.DMA((N_DEV - 1,)), pltpu.SemaphoreType.DMA((N_DEV - 1,)),
                              pltpu.SemaphoreType.DMA(())),
               compiler_params=pltpu.CompilerParams(collective_id=AG_COLLECTIVE_ID))
    def launch(send_sems, recv_sems, local_sem):
        _, sibling, chips = _gather_partners()
        _handshake_all([sibling] + [(*chip, lax.axis_index("c")) for chip in chips])
        _two_level_gather(x_ref, out_ref, send_sems, recv_sems, local_sem)

    launch()
    return out_ref[...]


def sc_scatter_parts(g8, name):
    g_ref = jax.new_ref(g8, memory_space=_HBM)
    parts_ref = jax.empty_ref(jax.ShapeDtypeStruct(g8.shape, g8.dtype), memory_space=_HBM)

    @pl.kernel(mesh=plsc.ScalarSubcoreMesh(**_SEQUENCER), name=name,
               scratch_types=(pltpu.SemaphoreType.DMA((N_DEV - 1,)), pltpu.SemaphoreType.DMA((N_DEV - 1,)),
                              pltpu.SemaphoreType.DMA(())),
               compiler_params=pltpu.CompilerParams(collective_id=RS_COLLECTIVE_ID))
    def launch(send_sems, recv_sems, local_sem):
        peers = _peers()
        _handshake_all(peers)
        me = 4 * lax.axis_index("x") + 2 * lax.axis_index("y") + lax.axis_index("c")
        mine = pltpu.make_async_copy(g_ref.at[me], parts_ref.at[0], local_sem)
        mine.start()
        copies = [pltpu.make_async_remote_copy(src_ref=g_ref.at[_block_of(peer)], dst_ref=parts_ref.at[k + 1],
                                               send_sem=send_sems.at[k], recv_sem=recv_sems.at[k],
                                               device_id=peer, device_id_type=MESH) for k, peer in enumerate(peers)]
        for cp in copies:
            cp.start()
        for cp in copies:
            cp.wait()
        mine.wait()

    launch()
    return parts_ref[...]


def _as_lrc(a, lead):
    rest = a.shape[lead:]
    return a.reshape(a.shape[:lead] + (int(np.prod(rest[:-2], dtype=np.int64)),) + rest[-2:])


def _adam_math(w, g, m, v):
    m = ADAM_B1 * m + (1.0 - ADAM_B1) * g
    v = ADAM_B2 * v + (1.0 - ADAM_B2) * jnp.square(g)
    m_hat = m / (1.0 - ADAM_B1 ** ADAM_STEP)
    v_hat = v / (1.0 - ADAM_B2 ** ADAM_STEP)
    delta = -ADAM_LR * (m_hat / (jnp.sqrt(v_hat) + ADAM_EPS) + ADAM_WD * w)
    return delta, m, v


def adamw_sharded(w, parts, m, v, name):
    L, R, C = w.shape
    tr = _largest_tile(R, 64, BF16_ROWS)

    def body(w_ref, *refs):
        p_refs, (m_ref, v_ref, g_out, d_out, m_out, v_out) = refs[:L], refs[L:]
        for layer in range(L):
            @pl.when(pl.program_id(0) == layer)
            def _(p_ref=p_refs[layer]):
                g = p_ref[0].astype(f32)
                for k in range(1, N_DEV):
                    g = g + p_ref[k].astype(f32)
                delta, mn, vn = _adam_math(w_ref[...], g, m_ref[...], v_ref[...])
                g_out[...] = g
                d_out[...] = delta
                m_out[...] = mn
                v_out[...] = vn

    blk = pl.BlockSpec((None, tr, C), lambda l, r: (l, r, 0))
    part_specs = [pl.BlockSpec((N_DEV, tr, C), functools.partial(lambda layer, l, r: (0, jnp.where(l == layer, r, 0), 0), layer))
                  for layer in range(L)]
    return pl.pallas_call(
        body, name=name, out_shape=[jax.ShapeDtypeStruct(w.shape, f32)] * 4, grid=(L, R // tr),
        in_specs=[blk] + part_specs + [blk, blk], out_specs=[blk] * 4,
        compiler_params=_cp("arbitrary", "arbitrary"),
    )(w, *parts, m, v)


def adamw_summed8(w, g8, m, v, name):
    R, C = w.shape
    tr = _largest_tile(R, 512, SUBLANES)

    def body(w_ref, g_ref, m_ref, v_ref, g_out, d_out, m_out, v_out):
        g = g_ref[0]
        for d in range(1, N_DEV):
            g = g + g_ref[d]
        delta, mn, vn = _adam_math(w_ref[...], g, m_ref[...], v_ref[...])
        g_out[...] = g
        d_out[...] = delta
        m_out[...] = mn
        v_out[...] = vn

    blk = pl.BlockSpec((tr, C), lambda r: (r, 0))
    return pl.pallas_call(
        body, name=name, out_shape=[jax.ShapeDtypeStruct((R, C), f32)] * 4, grid=(R // tr,),
        in_specs=[blk, pl.BlockSpec((N_DEV, tr, C), lambda r: (0, r, 0)), blk, blk], out_specs=[blk] * 4,
        compiler_params=_cp("parallel"),
    )(w, g8, m, v)


def _tile_n(ns):
    if ns % 512 == 0:
        return 512
    if ns <= 1536:
        return ns
    return _largest_tile(ns, 512, LANES)


MM_VMEM_BUDGET = VMEM_LIMIT - 8 * 1024 * 1024


def _tiles_desc(n, cap, step):
    return [t for t in range(min(n, cap) // step * step, 0, -step) if n % t == 0] or [n]


def _mm_fits(in_tiles, out_tile, out_dtype, n_red, extra=0):
    rows, cols = out_tile
    total = sum(2 * 2 * r * c for r, c in in_tiles) + 2 * rows * cols * jnp.dtype(out_dtype).itemsize + extra
    if n_red > 1:
        total += 4 * rows * cols
    return total <= MM_VMEM_BUDGET


def _reduce_into(o_ref, acc, part, first, last):
    if acc is None:
        o_ref[...] = part().astype(o_ref.dtype)
        return

    @pl.when(first)
    def _():
        acc[...] = jnp.zeros_like(acc)

    acc[...] += part()

    @pl.when(last)
    def _():
        o_ref[...] = acc[...].astype(o_ref.dtype)


def mm_nn(a, w3, out_dtype, name):
    M, K = a.shape
    J, _, Ns = w3.shape
    tn = _tile_n(Ns)
    tm, tk = next(((tm, tk) for tk in _tiles_desc(K, 4096, LANES) for tm in _tiles_desc(M, 1024, BF16_ROWS)
                   if tm >= min(M, 256) and _mm_fits([(tm, tk), (tk, tn)], (tm, tn), out_dtype, K // tk)),
                  (min(M, 256), _largest_tile(K, 512, LANES)))
    nn, nk = Ns // tn, K // tk

    def body(a_ref, w_ref, o_ref, *acc):
        k = pl.program_id(3)
        _reduce_into(o_ref, acc[0] if acc else None, lambda: _nn(a_ref[...], w_ref[...]), k == 0, k == nk - 1)

    return pl.pallas_call(
        body, name=name, out_shape=jax.ShapeDtypeStruct((M, J * Ns), out_dtype),
        grid=(M // tm, J, nn, nk),
        in_specs=[pl.BlockSpec((tm, tk), lambda i, j, n, k: (i, k)),
                  pl.BlockSpec((None, tk, tn), lambda i, j, n, k: (j, k, n))],
        out_specs=pl.BlockSpec((tm, tn), lambda i, j, n, k: (i, j * nn + n)),
        scratch_shapes=[pltpu.VMEM((tm, tn), f32)] if nk > 1 else [],
        compiler_params=_cp("parallel", "parallel", "parallel", "arbitrary"),
    )(a, w3)


MM_NT_REDUCE_CAP = 5632


def mm_nt(g, w3, out_dtype, name, init=None):
    M = g.shape[0]
    J, K, Ns = w3.shape
    if Ns <= MM_NT_REDUCE_CAP:
        tn, jb = Ns, max(b for b in range(1, J + 1) if J % b == 0 and (b == 1 or b * Ns <= MM_NT_REDUCE_CAP))
    else:
        tn, jb = _largest_tile(Ns, MM_NT_REDUCE_CAP, LANES), 1
    nj, nn = J // jb, Ns // tn
    n_red = nj * nn + (init is not None)
    tk = _largest_tile(K, 512, LANES)
    tm = next((tm for tm in _tiles_desc(M, 1024, BF16_ROWS)
               if _mm_fits([(tm, jb * tn), (jb * tk, tn)], (tm, tk), out_dtype, n_red,
                           extra=0 if init is None else 2 * 4 * tm * tk)), min(M, 256))

    def body(*refs):
        g_ref, w_ref = refs[:2]
        i_ref = None if init is None else refs[2]
        o_ref = refs[2 + (init is not None)]
        acc = refs[3 + (init is not None):]
        j, n = pl.program_id(2), pl.program_id(3)
        first = (j == 0) & (n == 0)

        def part():
            prod = _nt(g_ref[:, :tn], w_ref[0])
            for b in range(1, jb):
                prod = prod + _nt(g_ref[:, b * tn:(b + 1) * tn], w_ref[b])
            return prod if init is None else prod + jnp.where(first, i_ref[...].astype(f32), 0.0)

        _reduce_into(o_ref, acc[0] if acc else None, part, first, (j == nj - 1) & (n == nn - 1))

    in_specs = [pl.BlockSpec((tm, jb * tn), lambda i, k, j, n: (i, j * nn + n)),
                pl.BlockSpec((jb, tk, tn), lambda i, k, j, n: (j, k, n))]
    args = [g, w3]
    if init is not None:
        in_specs.append(pl.BlockSpec((tm, tk), lambda i, k, j, n: (i, k)))
        args.append(init)
    return pl.pallas_call(
        body, name=name, out_shape=jax.ShapeDtypeStruct((M, K), out_dtype),
        grid=(M // tm, K // tk, nj, nn), in_specs=in_specs,
        out_specs=pl.BlockSpec((tm, tk), lambda i, k, j, n: (i, k)),
        scratch_shapes=[pltpu.VMEM((tm, tk), f32)] if n_red > 1 else [],
        compiler_params=_cp("parallel", "parallel", "arbitrary", "arbitrary"),
    )(*args)


def mm_tn(a, g, ns, out_dtype, name):
    M, K = a.shape
    J = g.shape[1] // ns
    tn, tk = _tile_n(ns), _largest_tile(K, 512, LANES)
    tm = next((tm for tm in _tiles_desc(M, 4096, BF16_ROWS)
               if _mm_fits([(tm, tk), (tm, tn)], (tk, tn), out_dtype, M // tm)), _largest_tile(M, 512, BF16_ROWS))
    nn, nm = ns // tn, M // tm

    def body(a_ref, g_ref, o_ref, *acc):
        m = pl.program_id(3)
        _reduce_into(o_ref, acc[0] if acc else None, lambda: _tn(a_ref[...], g_ref[...]), m == 0, m == nm - 1)

    return pl.pallas_call(
        body, name=name, out_shape=jax.ShapeDtypeStruct((J, K, ns), out_dtype),
        grid=(J, K // tk, nn, nm),
        in_specs=[pl.BlockSpec((tm, tk), lambda j, k, n, m: (m, k)),
                  pl.BlockSpec((tm, tn), lambda j, k, n, m: (m, j * nn + n))],
        out_specs=pl.BlockSpec((None, tk, tn), lambda j, k, n, m: (j, k, n)),
        scratch_shapes=[pltpu.VMEM((tk, tn), f32)] if nm > 1 else [],
        compiler_params=_cp("parallel", "parallel", "parallel", "arbitrary"),
    )(a, g)


ROW_TILE = 256
ROW_STEP = 64


def row_fwd(fn, rows, consts, outs, name, n_acc=0):
    T = rows[0].shape[0]
    tr = min(T, ROW_TILE)
    n_rows, n_consts, n_row_out = len(rows), len(consts), len(outs) - n_acc

    def body(*refs):
        r_refs, c_refs, o_refs = refs[:n_rows], refs[n_rows:n_rows + n_consts], refs[n_rows + n_consts:]
        cs = [c[...] for c in c_refs]
        acc_refs = o_refs[n_row_out:]
        if n_acc:
            @pl.when(pl.program_id(0) == 0)
            def _():
                for a in acc_refs:
                    a[...] = jnp.zeros_like(a)

        def step(s, carry):
            rows_s = pl.ds(pl.multiple_of(s * ROW_STEP, ROW_STEP), ROW_STEP)
            res = fn(*[r[rows_s, :].astype(f32) for r in r_refs], *cs)
            for o, val in zip(o_refs[:n_row_out], res[:n_row_out]):
                o[rows_s, :] = val.astype(o.dtype)
            return tuple(c + val for c, val in zip(carry, res[n_row_out:]))

        accs = lax.fori_loop(0, tr // ROW_STEP, step, tuple(jnp.zeros((1, c), f32) for c, _ in outs[n_row_out:]))
        for a, val in zip(acc_refs, accs):
            a[...] += val

    in_specs = [pl.BlockSpec((tr, r.shape[1]), lambda i: (i, 0)) for r in rows]
    in_specs += [pl.BlockSpec(c.shape, lambda i: (0, 0)) for c in consts]
    out_shape = [jax.ShapeDtypeStruct((T, c), dt) for c, dt in outs[:n_row_out]]
    out_shape += [jax.ShapeDtypeStruct((1, c), f32) for c, _ in outs[n_row_out:]]
    out_specs = [pl.BlockSpec((tr, c), lambda i: (i, 0)) for c, _ in outs[:n_row_out]]
    out_specs += [pl.BlockSpec((1, c), lambda i: (0, 0)) for c, _ in outs[n_row_out:]]
    return pl.pallas_call(
        body, name=name, out_shape=out_shape, grid=(T // tr,), in_specs=in_specs, out_specs=out_specs,
        compiler_params=_cp("arbitrary" if n_acc else "parallel"),
    )(*rows, *consts)


def row_bwd(fn, rows, consts, cts, grad_dtypes, name):
    T = rows[0].shape[0]
    tr = min(T, ROW_TILE)
    n_rows, n_consts, n_cts = len(rows), len(consts), len(cts)
    wanted = [i for i, dt in enumerate(grad_dtypes) if dt is not None]

    def body(*refs):
        r_refs = refs[:n_rows]
        c_refs = refs[n_rows:n_rows + n_consts]
        t_refs = refs[n_rows + n_consts:n_rows + n_consts + n_cts]
        o_refs = refs[n_rows + n_consts + n_cts:]
        gr_refs, gc_refs = o_refs[:len(wanted)], o_refs[len(wanted):]
        cs = [c[...] for c in c_refs]

        @pl.when(pl.program_id(0) == 0)
        def _():
            for a in gc_refs:
                a[...] = jnp.zeros_like(a)

        def step(s, carry):
            rows_s = pl.ds(pl.multiple_of(s * ROW_STEP, ROW_STEP), ROW_STEP)
            ins = [r[rows_s, :].astype(f32) for r in r_refs]
            _, vjp = jax.vjp(lambda *a: tuple(fn(*a)), *ins, *cs)
            grads = vjp(tuple(t[rows_s, :].astype(f32) for t in t_refs))
            for o, i in zip(gr_refs, wanted):
                o[rows_s, :] = grads[i].astype(o.dtype)
            return tuple(c + gval for c, gval in zip(carry, grads[n_rows:]))

        accs = lax.fori_loop(0, tr // ROW_STEP, step, tuple(jnp.zeros(c.shape, f32) for c in consts))
        for a, val in zip(gc_refs, accs):
            a[...] += val

    in_specs = [pl.BlockSpec((tr, r.shape[1]), lambda i: (i, 0)) for r in list(rows) + list(cts)]
    in_specs[n_rows:n_rows] = [pl.BlockSpec(c.shape, lambda i: (0, 0)) for c in consts]
    out_shape = [jax.ShapeDtypeStruct(rows[i].shape, grad_dtypes[i]) for i in wanted]
    out_shape += [jax.ShapeDtypeStruct(c.shape, f32) for c in consts]
    out_specs = [pl.BlockSpec((tr, rows[i].shape[1]), lambda i_: (i_, 0)) for i in wanted]
    out_specs += [pl.BlockSpec(c.shape, lambda i: (0, 0)) for c in consts]
    return pl.pallas_call(
        body, name=name, out_shape=out_shape, grid=(T // tr,), in_specs=in_specs, out_specs=out_specs,
        compiler_params=_cp("arbitrary"),
    )(*rows, *consts, *cts)


def _rms(x, g):
    return x * lax.rsqrt(jnp.mean(x * x, axis=-1, keepdims=True) + EPS) * g


def fn_prenorm(x, g):
    return (_rms(x, g),)


def fn_resnorm(x, m, g_post, g_pre):
    x_new = x + _rms(m, g_post)
    return x_new, _rms(x_new, g_pre)


def fn_res(x, m, g_post):
    return (x + _rms(m, g_post),)


def fn_final(x, m, target, g_post):
    err = x + _rms(m, g_post) - target
    n = err.shape[-1]
    return err / n, (0.5 / n) * jnp.sum(jnp.sum(err * err, axis=1, keepdims=True), axis=0, keepdims=True)


def fn_gate_norm(y, z, g):
    return (_rms(y * jax.nn.silu(z), g),)


CONV_ROWS = 256


def _conv_chunk(pre_ref, w, b, r, rb, K):
    t0 = pl.multiple_of(r * rb, rb)
    halo_at = pl.multiple_of(jnp.maximum(t0 - SUBLANES, 0), SUBLANES)
    halo = jnp.where(r > 0, pre_ref[pl.ds(halo_at, SUBLANES), :], 0.0)
    main = pre_ref[pl.ds(t0, rb), :]
    ext = jnp.concatenate([halo, main], axis=0)
    shifted = [main if k == K - 1 else pltpu.roll(ext, K - 1 - k, 0)[SUBLANES:] for k in range(K)]
    conv = b
    for k in range(K):
        conv = conv + w[k:k + 1] * shifted[k]
    return conv, shifted


def conv_fwd(streams, epilogue, out_cols, out_dtype, K, name):
    T = streams[0][0].shape[0]
    rb = min(T, CONV_ROWS)
    S = len(streams)

    def body(*refs):
        pre_refs, w_refs, b_refs, o_ref = refs[:S], refs[S:2 * S], refs[2 * S:3 * S], refs[3 * S]
        ws = [w[...] for w in w_refs]
        bs = [b[...] for b in b_refs]

        def step(r, carry):
            convs = [_conv_chunk(pre_refs[s], ws[s], bs[s], r, rb, K)[0] for s in range(S)]
            o_ref[pl.ds(pl.multiple_of(r * rb, rb), rb), :] = epilogue(*convs).astype(o_ref.dtype)
            return carry

        lax.fori_loop(0, T // rb, step, 0)

    in_specs = [pl.BlockSpec((T, LANES), functools.partial(lambda off, i: (0, off + i), st[1])) for st in streams]
    in_specs += [pl.BlockSpec((K, LANES), functools.partial(lambda off, i: (0, off + i), st[4])) for st in streams]
    in_specs += [pl.BlockSpec((1, LANES), functools.partial(lambda off, i: (0, off + i), st[4])) for st in streams]
    return pl.pallas_call(
        body, name=name, out_shape=jax.ShapeDtypeStruct((T, out_cols), out_dtype), grid=(out_cols // LANES,),
        in_specs=in_specs, out_specs=pl.BlockSpec((T, LANES), lambda i: (0, i)),
        compiler_params=_cp("parallel"),
    )(*[st[0] for st in streams], *[st[2] for st in streams], *[st[3] for st in streams])


def conv_bwd(streams, epilogue, dout, K, name):
    T, cols = dout.shape
    rb = min(T, CONV_ROWS)
    S = len(streams)

    def body(*refs):
        pre_refs, w_refs, b_refs, dout_ref = refs[:S], refs[S:2 * S], refs[2 * S:3 * S], refs[3 * S]
        o = refs[3 * S + 1:]
        dpre_refs, dw_refs, db_refs, scr = o[:S], o[S:2 * S], o[2 * S:3 * S], o[3 * S:]
        ws = [w[...] for w in w_refs]
        bs = [b[...] for b in b_refs]
        for s in range(S):
            scr[s][pl.ds(T, SUBLANES), :] = jnp.zeros((SUBLANES, LANES), f32)

        def phase1(r, carry):
            rows = pl.ds(pl.multiple_of(r * rb, rb), rb)
            convs, shifted = zip(*[_conv_chunk(pre_refs[s], ws[s], bs[s], r, rb, K) for s in range(S)])
            _, vjp = jax.vjp(epilogue, *convs)
            dconvs = vjp(dout_ref[rows, :].astype(f32))
            new = []
            for s in range(S):
                scr[s][rows, :] = dconvs[s]
                sums = [jnp.sum(dconvs[s] * shifted[s][k], axis=0, keepdims=True) for k in range(K)]
                sums.append(jnp.sum(dconvs[s], axis=0, keepdims=True))
                new.append(tuple(c + v for c, v in zip(carry[s], sums)))
            return tuple(new)

        zero = tuple(tuple(jnp.zeros((1, LANES), f32) for _ in range(K + 1)) for _ in range(S))
        sums = lax.fori_loop(0, T // rb, phase1, zero)
        tap = lax.broadcasted_iota(jnp.int32, (K, LANES), 0)
        for s in range(S):
            dw = jnp.zeros((K, LANES), f32)
            for k in range(K):
                dw = jnp.where(tap == k, sums[s][k], dw)
            dw_refs[s][...] = dw
            db_refs[s][...] = sums[s][K]

        def phase2(r, carry):
            t0 = pl.multiple_of(r * rb, rb)
            for s in range(S):
                ext = scr[s][pl.ds(t0, rb + SUBLANES), :]
                dpre = ws[s][K - 1:K] * ext[:rb]
                for k in range(K - 1):
                    j = K - 1 - k
                    dpre = dpre + ws[s][k:k + 1] * pltpu.roll(ext, rb + SUBLANES - j, 0)[:rb]
                dpre_refs[s][pl.ds(t0, rb), :] = dpre.astype(dpre_refs[s].dtype)
            return carry

        lax.fori_loop(0, T // rb, phase2, 0)

    in_specs = [pl.BlockSpec((T, LANES), functools.partial(lambda off, i: (0, off + i), st[1])) for st in streams]
    in_specs += [pl.BlockSpec((K, LANES), functools.partial(lambda off, i: (0, off + i), st[4])) for st in streams]
    in_specs += [pl.BlockSpec((1, LANES), functools.partial(lambda off, i: (0, off + i), st[4])) for st in streams]
    in_specs += [pl.BlockSpec((T, LANES), lambda i: (0, i))]
    out_shape = [jax.ShapeDtypeStruct((T, cols), bf16)] * S
    out_shape += [jax.ShapeDtypeStruct((K, cols), f32)] * S + [jax.ShapeDtypeStruct((1, cols), f32)] * S
    out_specs = [pl.BlockSpec((T, LANES), lambda i: (0, i))] * S
    out_specs += [pl.BlockSpec((K, LANES), lambda i: (0, i))] * S + [pl.BlockSpec((1, LANES), lambda i: (0, i))] * S
    res = pl.pallas_call(
        body, name=name, out_shape=out_shape, grid=(cols // LANES,), in_specs=in_specs, out_specs=out_specs,
        scratch_shapes=[pltpu.VMEM((T + SUBLANES, LANES), f32)] * S,
        compiler_params=_cp("parallel"),
    )(*[st[0] for st in streams], *[st[2] for st in streams], *[st[3] for st in streams], dout)
    return res[:S], res[S:2 * S], res[2 * S:]


def epi_gelu_gate(cg, cu):
    return jax.nn.gelu(cg, approximate=True) * cu


def epi_silu(c):
    return jax.nn.silu(c)


XA_ROWS = 256


def _xa_fn(q, k, v):
    s = _nt(q.astype(bf16), k.astype(bf16)) * (q.shape[-1] ** -0.5)
    p = jax.nn.softmax(s, axis=-1)
    return _nn(p.astype(bf16), v.astype(bf16))


def xa_fwd(q, kv, name):
    T, W = q.shape
    M = kv.shape[0]
    H = W // LANES
    tr = min(T, XA_ROWS)

    def body(q_ref, k_ref, v_ref, o_ref):
        o_ref[...] = _xa_fn(q_ref[...].astype(f32), k_ref[...].astype(f32), v_ref[...].astype(f32)).astype(o_ref.dtype)

    return pl.pallas_call(
        body, name=name, out_shape=jax.ShapeDtypeStruct((T, W), bf16), grid=(T // tr, H),
        in_specs=[pl.BlockSpec((tr, LANES), lambda i, h: (i, h)),
                  pl.BlockSpec((M, LANES), lambda i, h: (0, h)),
                  pl.BlockSpec((M, LANES), lambda i, h: (0, H + h))],
        out_specs=pl.BlockSpec((tr, LANES), lambda i, h: (i, h)),
        compiler_params=_cp("parallel", "parallel"),
    )(q, kv, kv)


def xa_bwd(q, kv, do, name):
    T, W = q.shape
    M = kv.shape[0]
    H = W // LANES
    tr = min(T, XA_ROWS)

    def body(q_ref, k_ref, v_ref, do_ref, dq_ref, dk_ref, dv_ref):
        @pl.when(pl.program_id(1) == 0)
        def _():
            dk_ref[...] = jnp.zeros_like(dk_ref)
            dv_ref[...] = jnp.zeros_like(dv_ref)

        _, vjp = jax.vjp(_xa_fn, q_ref[...].astype(f32), k_ref[...].astype(f32), v_ref[...].astype(f32))
        dq, dk, dv = vjp(do_ref[...].astype(f32))
        dq_ref[...] = dq.astype(dq_ref.dtype)
        dk_ref[...] += dk
        dv_ref[...] += dv

    return pl.pallas_call(
        body, name=name,
        out_shape=[jax.ShapeDtypeStruct((T, W), bf16), jax.ShapeDtypeStruct((M, W), f32), jax.ShapeDtypeStruct((M, W), f32)],
        grid=(H, T // tr),
        in_specs=[pl.BlockSpec((tr, LANES), lambda h, i: (i, h)),
                  pl.BlockSpec((M, LANES), lambda h, i: (0, h)),
                  pl.BlockSpec((M, LANES), lambda h, i: (0, H + h)),
                  pl.BlockSpec((tr, LANES), lambda h, i: (i, h))],
        out_specs=[pl.BlockSpec((tr, LANES), lambda h, i: (i, h)),
                   pl.BlockSpec((M, LANES), lambda h, i: (0, h)),
                   pl.BlockSpec((M, LANES), lambda h, i: (0, h))],
        compiler_params=_cp("parallel", "arbitrary"),
    )(q, kv, kv, do)


def _sgu_norm_fn(v_pre, g, b):
    v = jax.nn.gelu(v_pre, approximate=True)
    mu = jnp.mean(v, axis=-1, keepdims=True)
    vc = v - mu
    return vc * lax.rsqrt(jnp.mean(vc * vc, axis=-1, keepdims=True) + EPS) * g + b


def _sgu_mix_fn(u_pre, vn, w, b):
    q = w.shape[0]
    tril = lax.broadcasted_iota(jnp.int32, (q, q), 0) >= lax.broadcasted_iota(jnp.int32, (q, q), 1)
    mixed = _nn(jnp.where(tril, w, 0.0).astype(bf16), vn.astype(bf16)) + b
    return jax.nn.gelu(u_pre, approximate=True) * mixed


def _sgu_norm_phase(v_ref, g, b, vn_ref):
    def step(s, carry):
        rows = pl.ds(pl.multiple_of(s * ROW_STEP, ROW_STEP), ROW_STEP)
        vn_ref[rows, :] = _sgu_norm_fn(v_ref[rows, :], g, b)
        return carry

    lax.fori_loop(0, CHUNK // ROW_STEP, step, 0)


def sgu_fwd(uv_pre, norm_g, norm_b, w_sp, b_sp, name):
    T, W2 = uv_pre.shape
    W = W2 // 2
    G = w_sp.shape[0]
    gw = W // G

    def body(u_ref, v_ref, g_ref, b_ref, ws_ref, bs_ref, o_ref, vn_ref):
        _sgu_norm_phase(v_ref, g_ref[...], b_ref[...], vn_ref)

        def group(gi, carry):
            cols = pl.ds(pl.multiple_of(gi * gw, LANES), gw)
            o_ref[:, cols] = _sgu_mix_fn(u_ref[:, cols], vn_ref[:, cols], ws_ref[gi], bs_ref[gi]).astype(o_ref.dtype)
            return carry

        lax.fori_loop(0, G, group, 0)

    full = lambda a: pl.BlockSpec(a.shape, lambda c: (0,) * a.ndim)
    return pl.pallas_call(
        body, name=name, out_shape=jax.ShapeDtypeStruct((T, W), bf16), grid=(T // CHUNK,),
        in_specs=[pl.BlockSpec((CHUNK, W), lambda c: (c, 0)), pl.BlockSpec((CHUNK, W), lambda c: (c, 1)),
                  full(norm_g), full(norm_b), full(w_sp), full(b_sp)],
        out_specs=pl.BlockSpec((CHUNK, W), lambda c: (c, 0)),
        scratch_shapes=[pltpu.VMEM((CHUNK, W), f32)],
        compiler_params=_cp("parallel"),
    )(uv_pre, uv_pre, norm_g, norm_b, w_sp, b_sp)


def sgu_bwd(uv_pre, norm_g, norm_b, w_sp, b_sp, dout, name):
    T, W2 = uv_pre.shape
    W = W2 // 2
    G = w_sp.shape[0]
    gw = W // G

    def body(u_ref, v_ref, g_ref, b_ref, ws_ref, bs_ref, do_ref, duv_ref, dg_ref, db_ref, dws_ref, dbs_ref,
             vn_ref, dvn_ref):
        @pl.when(pl.program_id(0) == 0)
        def _():
            for a in (dg_ref, db_ref, dws_ref, dbs_ref):
                a[...] = jnp.zeros_like(a)

        g, b = g_ref[...], b_ref[...]
        _sgu_norm_phase(v_ref, g, b, vn_ref)

        def group(gi, carry):
            cols = pl.ds(pl.multiple_of(gi * gw, LANES), gw)
            _, vjp = jax.vjp(_sgu_mix_fn, u_ref[:, cols], vn_ref[:, cols], ws_ref[gi], bs_ref[gi])
            du, dvn, dw, dbias = vjp(do_ref[:, cols])
            duv_ref[:, cols] = du.astype(duv_ref.dtype)
            dvn_ref[:, cols] = dvn
            dws_ref[gi] += dw
            dbs_ref[gi] += dbias
            return carry

        lax.fori_loop(0, G, group, 0)

        def step(s, carry):
            rows = pl.ds(pl.multiple_of(s * ROW_STEP, ROW_STEP), ROW_STEP)
            _, vjp = jax.vjp(_sgu_norm_fn, v_ref[rows, :], g, b)
            dv, dg, dbn = vjp(dvn_ref[rows, :])
            duv_ref[rows, pl.ds(W, W)] = dv.astype(duv_ref.dtype)
            return carry[0] + dg, carry[1] + dbn

        dg, dbn = lax.fori_loop(0, CHUNK // ROW_STEP, step, (jnp.zeros((1, W), f32), jnp.zeros((1, W), f32)))
        dg_ref[...] += dg
        db_ref[...] += dbn

    full = lambda a: pl.BlockSpec(a.shape, lambda c: (0,) * a.ndim)
    return pl.pallas_call(
        body, name=name,
        out_shape=[jax.ShapeDtypeStruct((T, W2), bf16), jax.ShapeDtypeStruct((1, W), f32), jax.ShapeDtypeStruct((1, W), f32),
                   jax.ShapeDtypeStruct(w_sp.shape, f32), jax.ShapeDtypeStruct(b_sp.shape, f32)],
        grid=(T // CHUNK,),
        in_specs=[pl.BlockSpec((CHUNK, W), lambda c: (c, 0)), pl.BlockSpec((CHUNK, W), lambda c: (c, 1)),
                  full(norm_g), full(norm_b), full(w_sp), full(b_sp), pl.BlockSpec((CHUNK, W), lambda c: (c, 0))],
        out_specs=[pl.BlockSpec((CHUNK, W2), lambda c: (c, 0)), full(norm_g), full(norm_b), full(w_sp), full(b_sp)],
        scratch_shapes=[pltpu.VMEM((CHUNK, W), f32), pltpu.VMEM((CHUNK, W), f32)],
        compiler_params=_cp("arbitrary"),
    )(uv_pre, uv_pre, norm_g, norm_b, w_sp, b_sp, dout)


SB_SUM_COLS = 256


def _sb_key_group(T):
    return 512 if T % 512 == 0 else T


def _sb_query_rows(T):
    return 256 if T % 256 == 0 else CHUNK


def _sb_block(q, k, i, g, diagonal):
    qb, kg = q.shape[0], k.shape[0]
    z = _nt(q, k) * (q.shape[-1] ** -0.5)
    sp = jnp.log1p(jnp.exp(-jnp.abs(z)))
    log_beta = jnp.minimum(z, 0.0) - sp
    log_1mb = -jnp.maximum(z, 0.0) - sp
    if not diagonal:
        return z, None, log_beta, log_1mb
    t_idx = i * qb + lax.broadcasted_iota(jnp.int32, (qb, kg), 0)
    s_idx = g * kg + lax.broadcasted_iota(jnp.int32, (qb, kg), 1)
    valid = s_idx < t_idx
    return z, valid, log_beta, jnp.where(valid, log_1mb, 0.0)


def _sb_masked(valid, x):
    return x if valid is None else jnp.where(valid, x, 0.0)


def _order_matrix(later):
    r = lax.broadcasted_iota(jnp.int32, (SB_SUM_COLS, SB_SUM_COLS), 0)
    c = lax.broadcasted_iota(jnp.int32, (SB_SUM_COLS, SB_SUM_COLS), 1)
    return (r > c if later else r < c).astype(bf16)


def _masked_sums(parts, order):
    terms = []
    for x in parts:
        hi = x.astype(bf16)
        rest = x - hi.astype(f32)
        mid = rest.astype(bf16)
        terms += [hi, mid, (rest - mid.astype(f32)).astype(bf16)]
    rows = parts[0].shape[0]
    prod = _nn(jnp.concatenate(terms, axis=0), order)
    piece = lambda n: prod[n * rows:(n + 1) * rows]
    return [piece(3 * p) + piece(3 * p + 1) + piece(3 * p + 2) for p in range(len(parts))]


def sb_fwd(qkv, name):
    T = qkv.shape[0]
    H = qkv.shape[1] // (3 * LANES)
    qb, kg = _sb_query_rows(T), _sb_key_group(T)
    halves = kg // SB_SUM_COLS

    def body(q_ref, k_ref, v_ref, o_ref, tot_ref):
        i = pl.program_id(1)
        q = q_ref[...]
        later = _order_matrix(True)
        n_groups = (i * qb + qb + kg - 1) // kg

        def step(g, carry, diagonal):
            acc, run = carry
            rows = pl.ds(pl.multiple_of(g * kg, kg), kg)
            _, valid, log_beta, log_1mb = _sb_block(q, k_ref[rows, :], i, g, diagonal)
            parts = [log_1mb[:, hh * SB_SUM_COLS:(hh + 1) * SB_SUM_COLS] for hh in range(halves)]
            tails = _masked_sums(parts, later)
            for hh in reversed(range(halves)):
                tails[hh] = tails[hh] + run
                run = run + jnp.sum(parts[hh], axis=1, keepdims=True)
            a = _sb_masked(valid, jnp.exp(log_beta + jnp.concatenate(tails, axis=1)))
            return acc + _nn(a.astype(bf16), v_ref[rows, :]), run

        carry = step(n_groups - 1, (jnp.zeros((qb, LANES), f32), jnp.zeros((qb, 1), f32)), True)
        acc, run = lax.fori_loop(0, n_groups - 1, lambda gg, c: step(n_groups - 2 - gg, c, False), carry)
        o_ref[...] = acc.astype(o_ref.dtype)
        tot_ref[...] = run

    return pl.pallas_call(
        body, name=name,
        out_shape=[jax.ShapeDtypeStruct((T, H * LANES), bf16), jax.ShapeDtypeStruct((H, T, 1), f32)],
        grid=(H, T // qb),
        in_specs=[pl.BlockSpec((qb, LANES), lambda h, i: (i, h)),
                  pl.BlockSpec((T, LANES), lambda h, i: (0, H + h)),
                  pl.BlockSpec((T, LANES), lambda h, i: (0, 2 * H + h))],
        out_specs=[pl.BlockSpec((qb, LANES), lambda h, i: (i, h)),
                   pl.BlockSpec((None, qb, 1), lambda h, i: (h, i, 0))],
        compiler_params=_cp("parallel", "parallel"),
    )(qkv, qkv, qkv)


def sb_bwd(qkv, tot, do, name):
    T = qkv.shape[0]
    H = qkv.shape[1] // (3 * LANES)
    qb, kg = _sb_query_rows(T), _sb_key_group(T)
    halves = kg // SB_SUM_COLS

    def body(q_ref, k_ref, v_ref, tot_ref, do_ref, dq_ref, dk_ref, dv_ref):
        i = pl.program_id(1)

        @pl.when(i == 0)
        def _():
            dk_ref[...] = jnp.zeros_like(dk_ref)
            dv_ref[...] = jnp.zeros_like(dv_ref)

        q = q_ref[...]
        do = do_ref[...].astype(bf16)
        tot = tot_ref[...]
        later, earlier = _order_matrix(True), _order_matrix(False)
        scale = q.shape[-1] ** -0.5
        n_groups = (i * qb + qb + kg - 1) // kg

        def step(g, carry, diagonal):
            dq, before, d_run = carry
            rows = pl.ds(pl.multiple_of(g * kg, kg), kg)
            k, v = k_ref[rows, :], v_ref[rows, :]
            z, valid, log_beta, log_1mb = _sb_block(q, k, i, g, diagonal)
            parts = [log_1mb[:, hh * SB_SUM_COLS:(hh + 1) * SB_SUM_COLS] for hh in range(halves)]
            tails = _masked_sums(parts, later)
            for hh in range(halves):
                before = before + jnp.sum(parts[hh], axis=1, keepdims=True)
                tails[hh] = tails[hh] + (tot - before)
            a = _sb_masked(valid, jnp.exp(log_beta + jnp.concatenate(tails, axis=1)))
            d_e = _nt(do, v) * a
            parts = [d_e[:, hh * SB_SUM_COLS:(hh + 1) * SB_SUM_COLS] for hh in range(halves)]
            d_l1 = _masked_sums(parts, earlier)
            for hh in range(halves):
                d_l1[hh] = d_l1[hh] + d_run
                d_run = d_run + jnp.sum(parts[hh], axis=1, keepdims=True)
            d_l1 = _sb_masked(valid, jnp.concatenate(d_l1, axis=1))
            sig = jax.nn.sigmoid(z)
            dz = ((d_e * (1.0 - sig) - d_l1 * sig) * scale).astype(bf16)
            dk_ref[rows, :] += _tn(dz, q)
            dv_ref[rows, :] += _tn(a.astype(bf16), do)
            return dq + _nn(dz, k), before, d_run

        zero_col = jnp.zeros((qb, 1), f32)
        carry = lax.fori_loop(0, n_groups - 1, lambda g, c: step(g, c, False),
                              (jnp.zeros((qb, LANES), f32), zero_col, zero_col))
        dq, _, _ = step(n_groups - 1, carry, True)
        dq_ref[...] = dq.astype(dq_ref.dtype)

    W = H * LANES
    return pl.pallas_call(
        body, name=name,
        out_shape=[jax.ShapeDtypeStruct((T, W), bf16), jax.ShapeDtypeStruct((T, W), f32), jax.ShapeDtypeStruct((T, W), f32)],
        grid=(H, T // qb),
        in_specs=[pl.BlockSpec((qb, LANES), lambda h, i: (i, h)),
                  pl.BlockSpec((T, LANES), lambda h, i: (0, H + h)),
                  pl.BlockSpec((T, LANES), lambda h, i: (0, 2 * H + h)),
                  pl.BlockSpec((None, qb, 1), lambda h, i: (h, i, 0)),
                  pl.BlockSpec((qb, LANES), lambda h, i: (i, h))],
        out_specs=[pl.BlockSpec((qb, LANES), lambda h, i: (i, h)),
                   pl.BlockSpec((T, LANES), lambda h, i: (0, h)),
                   pl.BlockSpec((T, LANES), lambda h, i: (0, h))],
        compiler_params=_cp("parallel", "arbitrary"),
    )(qkv, qkv, qkv, tot, do)


def _softplus(x):
    return jnp.maximum(x, 0.0) + jnp.log1p(jnp.exp(-jnp.abs(x)))


def _ssd_chunk_fn(head0, xs, b_mat, c_mat, dt_raw, dt_bias, a_log, d_skip, prev):
    q = dt_raw.shape[0]
    lane = lax.broadcasted_iota(jnp.int32, (q, LANES), 1)
    sub = lax.broadcasted_iota(jnp.int32, (q, LANES), 0)
    causal = sub >= lane
    dt = _softplus(dt_raw + dt_bias)
    a_cum = _nn(causal.astype(f32), dt * (-jnp.exp(a_log)), HIGHEST)
    a_cum_t = a_cum.T
    cb = _nt(c_mat.astype(bf16), b_mat.astype(bf16))
    bm, cm = b_mat.astype(bf16), c_mat.astype(bf16)
    ys, new = [], []
    for r in range(len(xs)):
        in_lane, in_sub = lane == head0 + r, sub == head0 + r
        col_a = jnp.sum(jnp.where(in_lane, a_cum, 0.0), axis=1, keepdims=True)
        row_a = jnp.sum(jnp.where(in_sub, a_cum_t, 0.0), axis=0, keepdims=True)
        col_dt = jnp.sum(jnp.where(in_lane, dt, 0.0), axis=1, keepdims=True)
        skip = jnp.sum(jnp.where(in_lane[:1], d_skip, 0.0), axis=1, keepdims=True)
        a_last = jnp.sum(jnp.where(sub[:, :1] == q - 1, col_a, 0.0), axis=0, keepdims=True)
        decay_in = jnp.exp(jnp.where(causal, col_a - row_a, -jnp.inf))
        xdt = xs[r] * col_dt
        y_diag = _nn((cb * decay_in).astype(bf16), xdt.astype(bf16))
        y_off = _nt(cm, prev[r].astype(bf16)) * jnp.exp(col_a)
        ys.append(y_diag + y_off + xs[r] * skip)
        state = _tn((xdt * jnp.exp(a_last - col_a)).astype(bf16), bm)
        new.append(prev[r] * jnp.exp(a_last) + state)
    return ys, new


SSD_GROUPS_PER_STEP = 2


def _ssd_specs(T, G, reverse):
    nc = T // CHUNK
    R, P, N = SSD_HEADS_PER_GROUP, SSD_HEAD_DIM, SSD_STATE
    gp = SSD_GROUPS_PER_STEP if G % SSD_GROUPS_PER_STEP == 0 else 1
    ch = (lambda c: nc - 1 - c) if reverse else (lambda c: c)
    xs = pl.BlockSpec((CHUNK, gp * R * P), lambda g, c: (ch(c), g))
    bm = pl.BlockSpec((CHUNK, gp * N), lambda g, c: (ch(c), G * R * P // (gp * N) + g))
    cm = pl.BlockSpec((CHUNK, gp * N), lambda g, c: (ch(c), (G * R * P + G * N) // (gp * N) + g))
    dt = pl.BlockSpec((CHUNK, LANES), lambda g, c: (ch(c), 0))
    row = pl.BlockSpec((1, LANES), lambda g, c: (0, 0))
    st = pl.BlockSpec((gp, None, R * P, N), lambda g, c: (g, ch(c), 0, 0))
    return nc, gp, xs, bm, cm, dt, row, st


def ssd_fwd(xbc, dt_raw, dt_bias, a_log, d_skip, name):
    T = xbc.shape[0]
    R, P, N = SSD_HEADS_PER_GROUP, SSD_HEAD_DIM, SSD_STATE
    G = xbc.shape[1] // (R * P + 2 * N)
    nc, gp, xs_s, bm_s, cm_s, dt_s, row_s, st_s = _ssd_specs(T, G, False)

    def body(xs_ref, b_ref, c_ref, dt_ref, bias_ref, alog_ref, skip_ref, y_ref, st_ref, state):
        @pl.when(pl.program_id(1) == 0)
        def _():
            state[...] = jnp.zeros_like(state)

        done = []
        for gi in range(gp):
            x = xs_ref[:, gi * R * P:(gi + 1) * R * P]
            xs = [x[:, r * P:(r + 1) * P] for r in range(R)]
            prev = [state[gi, r] for r in range(R)]
            ys, new = _ssd_chunk_fn((pl.program_id(0) * gp + gi) * R, xs, b_ref[:, gi * N:(gi + 1) * N],
                                    c_ref[:, gi * N:(gi + 1) * N], dt_ref[...], bias_ref[...], alog_ref[...],
                                    skip_ref[...], prev)
            done.append((prev, ys, new))
        for gi, (prev, ys, new) in enumerate(done):
            y_ref[:, gi * R * P:(gi + 1) * R * P] = jnp.concatenate(ys, axis=1)
            for r in range(R):
                st_ref[gi, pl.ds(r * P, P), :] = prev[r]
                state[gi, r] = new[r]

    return pl.pallas_call(
        body, name=name,
        out_shape=[jax.ShapeDtypeStruct((T, G * R * P), f32), jax.ShapeDtypeStruct((G, nc, R * P, N), f32)],
        grid=(G // gp, nc), in_specs=[xs_s, bm_s, cm_s, dt_s, row_s, row_s, row_s], out_specs=[xs_s, st_s],
        scratch_shapes=[pltpu.VMEM((gp, R, P, N), f32)],
        compiler_params=_cp("parallel", "arbitrary"),
    )(xbc, xbc, xbc, dt_raw, dt_bias, a_log, d_skip)


def ssd_bwd(xbc, dt_raw, dt_bias, a_log, d_skip, states, dy, name):
    T = xbc.shape[0]
    R, P, N = SSD_HEADS_PER_GROUP, SSD_HEAD_DIM, SSD_STATE
    G = xbc.shape[1] // (R * P + 2 * N)
    nc, gp, xs_s, bm_s, cm_s, dt_s, row_s, st_s = _ssd_specs(T, G, True)

    def body(xs_ref, b_ref, c_ref, dt_ref, bias_ref, alog_ref, skip_ref, st_ref, dy_ref,
             dx_ref, db_ref, dc_ref, ddt_ref, dbias_ref, dalog_ref, dskip_ref, dstate):
        g, c = pl.program_id(0), pl.program_id(1)

        @pl.when(c == 0)
        def _():
            dstate[...] = jnp.zeros_like(dstate)

        @pl.when((c == 0) & (g == 0))
        def _():
            for a in (dbias_ref, dalog_ref, dskip_ref):
                a[...] = jnp.zeros_like(a)

        done = []
        for gi in range(gp):
            cols, ncols = slice(gi * R * P, (gi + 1) * R * P), slice(gi * N, (gi + 1) * N)
            x, dyv = xs_ref[:, cols], dy_ref[:, cols]
            xs = [x[:, r * P:(r + 1) * P] for r in range(R)]
            prev = [st_ref[gi, pl.ds(r * P, P), :] for r in range(R)]
            _, vjp = jax.vjp(functools.partial(_ssd_chunk_fn, (g * gp + gi) * R), xs, b_ref[:, ncols], c_ref[:, ncols],
                             dt_ref[...], bias_ref[...], alog_ref[...], skip_ref[...], prev)
            done.append(vjp(([dyv[:, r * P:(r + 1) * P] for r in range(R)], [dstate[gi, r] for r in range(R)])))
        for gi, (dxs, dbm, dcm, ddt, dbias, dalog, dskip, dprev) in enumerate(done):
            dx_ref[:, gi * R * P:(gi + 1) * R * P] = jnp.concatenate(dxs, axis=1)
            db_ref[:, gi * N:(gi + 1) * N] = dbm
            dc_ref[:, gi * N:(gi + 1) * N] = dcm
            ddt_ref[gi] = ddt
            for r in range(R):
                dstate[gi, r] = dprev[r]
        dbias_ref[...] += sum(d[4] for d in done)
        dalog_ref[...] += sum(d[5] for d in done)
        dskip_ref[...] += sum(d[6] for d in done)

    small = pl.BlockSpec((CHUNK, gp * N), lambda g, c: (nc - 1 - c, g))
    return pl.pallas_call(
        body, name=name,
        out_shape=[jax.ShapeDtypeStruct((T, G * R * P), f32), jax.ShapeDtypeStruct((T, G * N), f32),
                   jax.ShapeDtypeStruct((T, G * N), f32), jax.ShapeDtypeStruct((G, T, LANES), f32)]
        + [jax.ShapeDtypeStruct((1, LANES), f32)] * 3,
        grid=(G // gp, nc), in_specs=[xs_s, bm_s, cm_s, dt_s, row_s, row_s, row_s, st_s, xs_s],
        out_specs=[xs_s, small, small, pl.BlockSpec((gp, CHUNK, LANES), lambda g, c: (g, nc - 1 - c, 0)),
                   row_s, row_s, row_s],
        scratch_shapes=[pltpu.VMEM((gp, R, P, N), f32)],
        compiler_params=_cp("arbitrary", "arbitrary"),
    )(xbc, xbc, xbc, dt_raw, dt_bias, a_log, d_skip, states, dy)


WEIGHTS = ["ln_mix_pre", "ln_mix_post", "ln_mem", "ln_xa_pre", "ln_xa_post", "ln_ffn_pre", "ln_ffn_post",
           "xa_wq", "xa_wkv", "xa_wo", "ffn_w_in", "ffn_conv_w", "ffn_conv_b", "ffn_w_out",
           "ssd_w_in", "ssd_conv_w", "ssd_conv_b", "ssd_dt_bias", "ssd_a_log", "ssd_d", "ssd_norm", "ssd_w_out",
           "sg_w_in", "sg_v_norm_g", "sg_v_norm_b", "sg_w_spatial", "sg_b_spatial", "sg_w_out", "sb_w_qkv", "sb_w_out"]
BIG = ["xa_wq", "xa_wkv", "xa_wo", "ffn_w_in", "ffn_w_out", "ssd_w_in", "ssd_w_out", "sg_w_in", "sg_w_out",
       "sb_w_qkv", "sb_w_out"]
SMALL_SHARDED = ["ffn_conv_w", "ssd_conv_w", "ssd_conv_b", "ssd_norm"]
REPLICATED = [n for n in WEIGHTS if n not in BIG and n not in SMALL_SHARDED]


def fn_xpre(x, g):
    return x, _rms(x, g)


def _pack_rows(arrs, lead=0):
    head = arrs[0].shape[:lead]
    flat = jnp.concatenate([a.reshape(head + (-1,)) for a in arrs], axis=-1)
    n = flat.shape[-1]
    rows = -(-n // (SUBLANES * LANES)) * SUBLANES
    flat = jnp.pad(flat, [(0, 0)] * lead + [(0, rows * LANES - n)])
    return flat.reshape(head + (rows, LANES))


def _unpack_rows(packed, shapes):
    head = packed.shape[:-2]
    flat = packed.reshape(head + (-1,))
    out, off = [], 0
    for shp in shapes:
        n = int(np.prod(shp, dtype=np.int64))
        out.append(flat[..., off:off + n].reshape(head + tuple(shp)))
        off += n
    return out


def _merge_last(a8):
    return jnp.moveaxis(a8, 0, -2).reshape(a8.shape[1:-1] + (N_DEV * a8.shape[-1],))


def _split_last(a):
    return jnp.moveaxis(a.reshape(a.shape[:-1] + (N_DEV, a.shape[-1] // N_DEV)), -2, 0)


def kernel(x, mem, ln_mix_pre, ln_mix_post, ln_mem, ln_xa_pre, ln_xa_post, ln_ffn_pre, ln_ffn_post, xa_wq, xa_wkv, xa_wo, ffn_w_in, ffn_conv_w, ffn_conv_b, ffn_w_out, ssd_w_in, ssd_conv_w, ssd_conv_b, ssd_dt_bias, ssd_a_log, ssd_d, ssd_norm, ssd_w_out, sg_w_in, sg_v_norm_g, sg_v_norm_b, sg_w_spatial, sg_b_spatial, sg_w_out, sb_w_qkv, sb_w_out, loss_target, m_ln_mix_pre, m_ln_mix_post, m_ln_mem, m_ln_xa_pre, m_ln_xa_post, m_ln_ffn_pre, m_ln_ffn_post, m_xa_wq, m_xa_wkv, m_xa_wo, m_ffn_w_in, m_ffn_conv_w, m_ffn_conv_b, m_ffn_w_out, m_ssd_w_in, m_ssd_conv_w, m_ssd_conv_b, m_ssd_dt_bias, m_ssd_a_log, m_ssd_d, m_ssd_norm, m_ssd_w_out, m_sg_w_in, m_sg_v_norm_g, m_sg_v_norm_b, m_sg_w_spatial, m_sg_b_spatial, m_sg_w_out, m_sb_w_qkv, m_sb_w_out, v_ln_mix_pre, v_ln_mix_post, v_ln_mem, v_ln_xa_pre, v_ln_xa_post, v_ln_ffn_pre, v_ln_ffn_post, v_xa_wq, v_xa_wkv, v_xa_wo, v_ffn_w_in, v_ffn_conv_w, v_ffn_conv_b, v_ffn_w_out, v_ssd_w_in, v_ssd_conv_w, v_ssd_conv_b, v_ssd_dt_bias, v_ssd_a_log, v_ssd_d, v_ssd_norm, v_ssd_w_out, v_sg_w_in, v_sg_v_norm_g, v_sg_v_norm_b, v_sg_w_spatial, v_sg_b_spatial, v_sg_w_out, v_sb_w_qkv, v_sb_w_out):
    p = dict(locals())
    x, mem, target = p["x"][0], p["mem"][0], p["loss_target"][0]
    T, D = x.shape
    depth = ln_mix_pre.shape[0]
    me = 4 * lax.axis_index("x") + 2 * lax.axis_index("y") + lax.axis_index("c")

    def gather(w, name):
        return sc_all_gather(w.astype(bf16), name)

    parts = {n: {} for n in BIG}
    pending, due = [], {}

    def scatter(name, layer, g8):
        pending.append((name, layer, g8))

    def deadline(name, s):
        if name == "ffn_w_out":
            return s - 2, 0
        if name == "ffn_w_in":
            return s - 2, 1
        if name.startswith("xa_"):
            return (s - 3, 0) if s >= 3 else (0, 1)
        return (s - 1, 0) if name.endswith("_out") else (s - 2, 0)

    def ship_pending(carry, s):
        carry, *held = lax.optimization_barrier((carry, *[g8 for _, _, g8 in pending]))
        for (name, layer, _), g8 in zip(pending, held):
            parts[name][layer] = sc_scatter_parts(g8, "rs_" + name)
            due.setdefault(deadline(name, s), []).append((name, layer))
        pending.clear()
        return carry

    def collect_due(carry, s):
        names = due.pop(s, [])
        if names:
            carry, *landed = lax.optimization_barrier((carry, *[parts[n][l] for n, l in names]))
            for (n, l), a in zip(names, landed):
                parts[n][l] = a
        return carry

    small8 = all_gather(_pack_rows([p[n] for n in SMALL_SHARDED]), "ag_small")
    full = {n: _merge_last(a) for n, a in zip(SMALL_SHARDED, _unpack_rows(small8, [p[n].shape for n in SMALL_SHARDED]))}

    grads = {n: {} for n in WEIGHTS}

    def ssd(h, j):
        g_in = gather(p["ssd_w_in"][j], "ag_ssd_in")
        g_in, h = lax.optimization_barrier((g_in, h))
        w_full = jnp.moveaxis(g_in, 0, 1).reshape(D, -1)
        w_out = gather(p["ssd_w_out"][j], "ag_ssd_out").reshape(1, -1, D)
        d_inner, conv_dim, heads = w_out.shape[1], full["ssd_conv_w"].shape[-1], ssd_dt_bias.shape[1]
        w_z, w_x = w_full[None, :, :d_inner], w_full[None, :, d_inner:d_inner + conv_dim]
        w_dt = jnp.pad(w_full[:, d_inner + conv_dim:], ((0, 0), (0, LANES - heads)))[None]
        lane_row = lambda a: jnp.pad(a[j:j + 1], ((0, 0), (0, LANES - heads)))
        bias, a_log, d_skip = lane_row(ssd_dt_bias), lane_row(ssd_a_log), lane_row(ssd_d)
        norm_g = full["ssd_norm"][j:j + 1]
        z = mm_nn(h, w_z, f32, "ssd_in_z")
        xbc_pre = mm_nn(h, w_x, f32, "ssd_in_x")
        dt_raw = mm_nn(h, w_dt, f32, "ssd_in_dt")
        streams = [(xbc_pre, 0, full["ssd_conv_w"][j], full["ssd_conv_b"][j:j + 1], 0)]
        xbc = conv_fwd(streams, epi_silu, conv_dim, f32, 4, "ssd_conv")
        y, states = ssd_fwd(xbc, dt_raw, bias, a_log, d_skip, "ssd_core")
        gated = row_fwd(fn_gate_norm, [y, z], [norm_g], [(d_inner, bf16)], "ssd_gate")[0]
        out = mm_nn(gated, w_out, f32, "ssd_out")

        def bwd(d_out, mid):
            d_gated = mm_nt(d_out, w_out, f32, "ssd_out_dx")
            scatter("ssd_w_out", j, mm_tn(gated, d_out, D, bf16, "ssd_out_dw").reshape(N_DEV, -1, D))
            dy, dz, d_norm = row_bwd(fn_gate_norm, [y, z], [norm_g], [d_gated], [f32, bf16], "ssd_gate_bwd")
            dxs, dbm, dcm, ddt_g, d_bias, d_alog, d_skipg = ssd_bwd(xbc, dt_raw, bias, a_log, d_skip, states, dy, "ssd_core_bwd")
            dxs = mid(dxs)
            (dx_pre,), (d_cw,), (d_cb,) = conv_bwd(streams, epi_silu, jnp.concatenate([dxs, dbm, dcm], axis=1), 4, "ssd_conv_bwd")
            ddt = jnp.sum(ddt_g, axis=0).astype(bf16)
            dh = mm_nt(dz, w_z, f32, "ssd_in_z_dx")
            dh = mm_nt(dx_pre, w_x, f32, "ssd_in_x_dx", init=dh)
            dh = mm_nt(ddt, w_dt, f32, "ssd_in_dt_dx", init=dh)
            dw = jnp.concatenate([mm_tn(h, dz, d_inner, bf16, "ssd_in_z_dw")[0], mm_tn(h, dx_pre, conv_dim, bf16, "ssd_in_x_dw")[0],
                                  mm_tn(h, ddt, LANES, bf16, "ssd_in_dt_dw")[0][:, :heads]], axis=1)
            scatter("ssd_w_in", j, _split_last(dw))
            grads["ssd_conv_w"][j], grads["ssd_conv_b"][j], grads["ssd_norm"][j] = d_cw, d_cb[0], d_norm[0]
            grads["ssd_dt_bias"][j], grads["ssd_a_log"][j], grads["ssd_d"][j] = d_bias[0, :heads], d_alog[0, :heads], d_skipg[0, :heads]
            return dh

        return out, bwd

    def sgu(h, j):
        w_in = gather(p["sg_w_in"][j], "ag_sg_in")
        w_out = gather(p["sg_w_out"][j], "ag_sg_out").reshape(1, -1, D)
        norm_g, norm_b = sg_v_norm_g[j:j + 1], sg_v_norm_b[j:j + 1]
        w_sp, b_sp = sg_w_spatial[j], sg_b_spatial[j][..., None]
        uv = mm_nn(h, w_in, f32, "sg_in")
        gated = sgu_fwd(uv, norm_g, norm_b, w_sp, b_sp, "sg_core")
        out = mm_nn(gated, w_out, f32, "sg_out")

        def bwd(d_out, mid):
            d_gated = mm_nt(d_out, w_out, f32, "sg_out_dx")
            scatter("sg_w_out", j, mm_tn(gated, d_out, D, bf16, "sg_out_dw").reshape(N_DEV, -1, D))
            duv, d_ng, d_nb, d_ws, d_bs = sgu_bwd(uv, norm_g, norm_b, w_sp, b_sp, d_gated, "sg_core_bwd")
            duv = mid(duv)
            grads["sg_v_norm_g"][j], grads["sg_v_norm_b"][j] = d_ng[0], d_nb[0]
            grads["sg_w_spatial"][j], grads["sg_b_spatial"][j] = d_ws, d_bs[..., 0]
            scatter("sg_w_in", j, mm_tn(h, duv, w_in.shape[2], bf16, "sg_in_dw"))
            return mm_nt(duv, w_in, f32, "sg_in_dx")

        return out, bwd

    def stick(h, j):
        w_qkv = gather(p["sb_w_qkv"][j], "ag_sb_qkv")
        w_out = gather(p["sb_w_out"][j], "ag_sb_out").reshape(1, -1, D)
        qkv = mm_nn(h, w_qkv, bf16, "sb_qkv")
        o, tot = sb_fwd(qkv, "sb_core")
        out = mm_nn(o, w_out, f32, "sb_out")

        def bwd(d_out, mid):
            d_o = mm_nt(d_out, w_out, f32, "sb_out_dx")
            scatter("sb_w_out", j, mm_tn(o, d_out, D, bf16, "sb_out_dw").reshape(N_DEV, -1, D))
            dq, dk, dv = sb_bwd(qkv, tot, d_o, "sb_core_bwd")
            dq = mid(dq)
            dqkv = jnp.concatenate([dq, dk.astype(bf16), dv.astype(bf16)], axis=1)
            scatter("sb_w_qkv", j, mm_tn(h, dqkv, w_qkv.shape[2], bf16, "sb_qkv_dw"))
            return mm_nt(dqkv, w_qkv, f32, "sb_qkv_dx")

        return out, bwd

    def cross(h, i):
        gain = ln_mem[i:i + 1]
        mem_n = row_fwd(fn_prenorm, [mem], [gain], [(D, bf16)], "mem_norm")[0]
        w_q = gather(p["xa_wq"][i], "ag_xa_q").reshape(1, D, -1)
        w_kv = gather(p["xa_wkv"][i], "ag_xa_kv").reshape(1, D, -1)
        w_o = gather(p["xa_wo"][i], "ag_xa_o")
        q = mm_nn(h, w_q, bf16, "xa_q")
        kv = mm_nn(mem_n, w_kv, bf16, "xa_kv")
        o = xa_fwd(q, kv, "xa_core")
        out = mm_nn(o, w_o, f32, "xa_out")

        def bwd(d_out, mid):
            d_o = mm_nt(d_out, w_o, f32, "xa_out_dx")
            scatter("xa_wo", i, mm_tn(o, d_out, w_o.shape[2], bf16, "xa_out_dw"))
            dq, dk, dv = xa_bwd(q, kv, d_o, "xa_core_bwd")
            dkv = jnp.concatenate([dk, dv], axis=1).astype(bf16)
            scatter("xa_wq", i, mm_tn(h, dq, w_q.shape[2], bf16, "xa_q_dw").reshape(N_DEV, -1, w_q.shape[2]))
            scatter("xa_wkv", i, mm_tn(mem_n, dkv, w_kv.shape[2], bf16, "xa_kv_dw").reshape(N_DEV, -1, w_kv.shape[2]))
            d_mem_n = mm_nt(dkv, w_kv, f32, "xa_kv_dx")
            grads["ln_mem"][i] = row_bwd(fn_prenorm, [mem], [gain], [d_mem_n], [None], "mem_norm_bwd")[0][0]
            return mm_nt(dq, w_q, f32, "xa_q_dx")

        return out, bwd

    def ffn(h, i):
        w_in = gather(p["ffn_w_in"][i], "ag_ffn_in")
        w_out = gather(p["ffn_w_out"][i], "ag_ffn_out").reshape(1, -1, D)
        width = w_out.shape[1]
        conv_w, conv_b = full["ffn_conv_w"][i], ffn_conv_b[i:i + 1]
        gu = mm_nn(h, w_in, f32, "ffn_in")
        streams = [(gu, 0, conv_w, conv_b, 0), (gu, width // LANES, conv_w, conv_b, width // LANES)]
        act = conv_fwd(streams, epi_gelu_gate, width, bf16, 3, "ffn_gate")
        out = mm_nn(act, w_out, f32, "ffn_out")

        def bwd(d_out, mid):
            d_act = mm_nt(d_out, w_out, f32, "ffn_out_dx")
            scatter("ffn_w_out", i, mm_tn(act, d_out, D, bf16, "ffn_out_dw").reshape(N_DEV, -1, D))
            d_pre, d_cw, d_cb = conv_bwd(streams, epi_gelu_gate, d_act, 3, "ffn_gate_bwd")
            dgu = jnp.concatenate(d_pre, axis=1)
            grads["ffn_conv_w"][i], grads["ffn_conv_b"][i] = jnp.concatenate(d_cw, axis=1), jnp.concatenate(d_cb, axis=1)[0]
            scatter("ffn_w_in", i, mm_tn(h, dgu, w_in.shape[2], bf16, "ffn_in_dw"))
            return mm_nt(dgu, w_in, f32, "ffn_in_dx")

        return out, bwd

    n_sub = 3 * depth
    pre = [w[i:i + 1] for i in range(depth) for w in (ln_mix_pre, ln_xa_pre, ln_ffn_pre)]
    post = [w[i:i + 1] for i in range(depth) for w in (ln_mix_post, ln_xa_post, ln_ffn_post)]
    stream, outs, bwds = [x], [], []
    h = row_fwd(fn_prenorm, [x], [pre[0]], [(D, bf16)], "pre_norm")[0]
    for s in range(n_sub):
        i, t = divmod(s, 3)
        out, bwd = ((ssd, sgu, stick)[i % 3](h, i // 3) if t == 0 else cross(h, i) if t == 1 else ffn(h, i))
        outs.append(out)
        bwds.append(bwd)
        if s < n_sub - 1:
            x_new, h = row_fwd(fn_resnorm, [stream[s], out], [post[s], pre[s + 1]], [(D, f32), (D, bf16)], "res_norm")
            stream.append(x_new)
    dy, loss = row_fwd(fn_final, [stream[-1], outs[-1], target], [post[-1]], [(D, f32), (1, f32)], "loss_head", n_acc=1)
    loss = lax.psum(loss[0, 0], ("x", "y", "c"))

    d_pre, d_post = [None] * n_sub, [None] * n_sub
    dx, d_out, d_post[-1] = row_bwd(fn_res, [stream[-1], outs[-1]], [post[-1]], [dy], [f32, bf16], "res_bwd")
    for s in reversed(range(n_sub)):
        dh = ship_pending(bwds[s](collect_due(d_out, (s, 0)), functools.partial(collect_due, s=(s, 1))), s)
        if s > 0:
            dx, d_out, d_post[s - 1], d_pre[s] = row_bwd(
                fn_resnorm, [stream[s - 1], outs[s - 1]], [post[s - 1], pre[s]], [dx, dh], [f32, bf16], "res_norm_bwd")
        else:
            grad_x, d_pre[0] = row_bwd(fn_xpre, [x], [pre[0]], [dx, dh], [f32], "pre_norm_bwd")
    for t, kind in enumerate(("mix", "xa", "ffn")):
        for i in range(depth):
            grads["ln_%s_pre" % kind][i] = d_pre[3 * i + t][0]
            grads["ln_%s_post" % kind][i] = d_post[3 * i + t][0]

    def stacked(name):
        return jnp.stack([grads[name][l] for l in range(len(grads[name]))], axis=0)

    new = {}
    for name in BIG:
        new[name] = adamw_sharded(p[name], [parts[name][l] for l in range(len(parts[name]))],
                                  p["m_" + name], p["v_" + name], "adamw_" + name)

    rep_shapes = [p[n].shape for n in REPLICATED]
    g8 = all_gather(_pack_rows([stacked(n) for n in REPLICATED]), "ag_grad_rep")
    rep = adamw_summed8(_pack_rows([p[n] for n in REPLICATED]), g8, _pack_rows([p["m_" + n] for n in REPLICATED]),
                        _pack_rows([p["v_" + n] for n in REPLICATED]), "adamw_rep")
    for k, packed in enumerate(rep):
        for n, a in zip(REPLICATED, _unpack_rows(packed, rep_shapes)):
            new.setdefault(n, [None] * 4)[k] = a

    sh_shapes = [p[n].shape for n in SMALL_SHARDED]
    by_owner = _pack_rows([_split_last(stacked(n)) for n in SMALL_SHARDED], lead=1)
    mine8 = lax.dynamic_index_in_dim(all_gather(by_owner, "ag_grad_small"), me, axis=1, keepdims=False)
    sh = adamw_summed8(_pack_rows([p[n] for n in SMALL_SHARDED]), mine8, _pack_rows([p["m_" + n] for n in SMALL_SHARDED]),
                       _pack_rows([p["v_" + n] for n in SMALL_SHARDED]), "adamw_small")
    for k, packed in enumerate(sh):
        for n, a in zip(SMALL_SHARDED, _unpack_rows(packed, sh_shapes)):
            new.setdefault(n, [None] * 4)[k] = a

    return (loss, grad_x[None], *[new[n][0] for n in WEIGHTS], *[new[n][1] for n in WEIGHTS],
            *[new[n][2] for n in WEIGHTS], *[new[n][3] for n in WEIGHTS])
```

```python
import functools

import jax
import jax.numpy as jnp
import numpy as np
from jax import lax
from jax.experimental import pallas as pl
from jax.experimental.pallas import tpu as pltpu
from jax.experimental.pallas import tpu_sc as plsc

f32 = jnp.float32
bf16 = jnp.bfloat16
HIGHEST = lax.Precision.HIGHEST
MESH = pl.DeviceIdType.MESH

V7X_VMEM_BYTES = 64 * 1024 * 1024
VMEM_LIMIT = V7X_VMEM_BYTES * 3 // 4
LANES = 128
SUBLANES = 8
BF16_ROWS = 16

EPS = 1e-6
ADAM_LR = 0.001
ADAM_B1 = 0.9
ADAM_B2 = 0.999
ADAM_EPS = 1e-08
ADAM_WD = 0.01
ADAM_STEP = 10

N_DEV = 8
XA_HEADS = 4
SSD_HEADS_PER_GROUP = 8
SSD_HEAD_DIM = 64
SSD_STATE = 128
SSD_GROUPS = 8
CHUNK = 128


def _cp(*sem):
    return pltpu.CompilerParams(dimension_semantics=sem or None, vmem_limit_bytes=VMEM_LIMIT)


def _dot(a, b, dims, precision=None):
    return lax.dot_general(a, b, (dims, ((), ())), precision=precision, preferred_element_type=f32)


def _nn(a, b, precision=None):
    return _dot(a, b, ((1,), (0,)), precision)


def _nt(a, b, precision=None):
    return _dot(a, b, ((1,), (1,)), precision)


def _tn(a, b, precision=None):
    return _dot(a, b, ((0,), (0,)), precision)


def _largest_tile(n, cap, step):
    for t in range(min(n, cap) // step * step, 0, -step):
        if n % t == 0:
            return t
    return n


def _gather_partners():
    mx, my, mc = lax.axis_index("x"), lax.axis_index("y"), lax.axis_index("c")
    chips = [(1 - mx, my), (mx, 1 - my), (1 - mx, 1 - my)]
    return (mx, my, mc), (mx, my, 1 - mc), chips


def _two_level_gather(x_ref, out_ref, send_sems, recv_sems, local_sem):
    me, sibling, chips = _gather_partners()
    mc = me[2]

    def slot(px, py, pc):
        return out_ref.at[4 * px + 2 * py + pc]

    def copy(k, block, to, src=None):
        return pltpu.make_async_remote_copy(
            src_ref=slot(*block) if src is None else src, dst_ref=slot(*block),
            send_sem=send_sems.at[k], recv_sem=recv_sems.at[k], device_id=to, device_id_type=MESH)

    mine = pltpu.make_async_copy(x_ref, slot(*me), local_sem)
    mine.start()
    first = [copy(0, me, sibling, src=x_ref)]
    first += [copy(1 + j, me, (*chip, mc), src=x_ref) for j, chip in enumerate(chips)]
    for cp in first:
        cp.start()
    passed = [copy(4 + j, (*chip, mc), sibling) for j, chip in enumerate(chips)]
    for j, chip in enumerate(chips):
        copy(1 + j, (*chip, mc), me).wait_recv()
        passed[j].start()
    copy(0, sibling, me).wait_recv()
    for j, chip in enumerate(chips):
        copy(4 + j, (*chip, 1 - mc), me).wait_recv()
    for cp in first + passed:
        cp.wait_send()
    mine.wait()


def all_gather(x, name):
    return pl.pallas_call(
        functools.partial(_two_level_gather), name=name,
        out_shape=jax.ShapeDtypeStruct((N_DEV,) + x.shape, x.dtype),
        in_specs=[pl.BlockSpec(memory_space=pl.ANY)],
        out_specs=pl.BlockSpec(memory_space=pl.ANY),
        scratch_shapes=[pltpu.SemaphoreType.DMA((7,)), pltpu.SemaphoreType.DMA((7,)), pltpu.SemaphoreType.DMA(())],
    )(x)


AG_COLLECTIVE_ID = 1
RS_COLLECTIVE_ID = 2
_SEQUENCER = dict(axis_name="sequencer", num_cores=1)
_HBM = pltpu.MemorySpace.HBM


def _peers():
    mx, my, mc = lax.axis_index("x"), lax.axis_index("y"), lax.axis_index("c")
    return [((mx + (k >> 2)) % 2, (my + ((k >> 1) & 1)) % 2, (mc + (k & 1)) % 2) for k in range(1, N_DEV)]


def _handshake_all(peers):
    barrier = pltpu.get_barrier_semaphore()
    for peer in peers:
        pl.semaphore_signal(barrier, inc=1, device_id=peer, device_id_type=MESH)
    pl.semaphore_wait(barrier, len(peers))


def _block_of(peer):
    return 4 * peer[0] + 2 * peer[1] + peer[2]


def sc_all_gather(x, name):
    x_ref = jax.new_ref(x, memory_space=_HBM)
    out_ref = jax.empty_ref(jax.ShapeDtypeStruct((N_DEV,) + x.shape, x.dtype), memory_space=_HBM)

    @pl.kernel(mesh=plsc.ScalarSubcoreMesh(**_SEQUENCER), name=name,
               scratch_types=(pltpu.SemaphoreType.DMA((N_DEV - 1,)), pltpu.SemaphoreType.DMA((N_DEV - 1,)),
                              pltpu.SemaphoreType.DMA(())),
               compiler_params=pltpu.CompilerParams(collective_id=AG_COLLECTIVE_ID))
    def launch(send_sems, recv_sems, local_sem):
        _, sibling, chips = _gather_partners()
        _handshake_all([sibling] + [(*chip, lax.axis_index("c")) for chip in chips])
        _two_level_gather(x_ref, out_ref, send_sems, recv_sems, local_sem)

    launch()
    return out_ref[...]


def sc_scatter_parts(g8, name):
    g_ref = jax.new_ref(g8, memory_space=_HBM)
    parts_ref = jax.empty_ref(jax.ShapeDtypeStruct(g8.shape, g8.dtype), memory_space=_HBM)

    @pl.kernel(mesh=plsc.ScalarSubcoreMesh(**_SEQUENCER), name=name,
               scratch_types=(pltpu.SemaphoreType.DMA((N_DEV - 1,)), pltpu.SemaphoreType.DMA((N_DEV - 1,)),
                              pltpu.SemaphoreType.DMA(())),
               compiler_params=pltpu.CompilerParams(collective_id=RS_COLLECTIVE_ID))
    def launch(send_sems, recv_sems, local_sem):
        peers = _peers()
        _handshake_all(peers)
        me = 4 * lax.axis_index("x") + 2 * lax.axis_index("y") + lax.axis_index("c")
        mine = pltpu.make_async_copy(g_ref.at[me], parts_ref.at[0], local_sem)
        mine.start()
        copies = [pltpu.make_async_remote_copy(src_ref=g_ref.at[_block_of(peer)], dst_ref=parts_ref.at[k + 1],
                                               send_sem=send_sems.at[k], recv_sem=recv_sems.at[k],
                                               device_id=peer, device_id_type=MESH) for k, peer in enumerate(peers)]
        for cp in copies:
            cp.start()
        for cp in copies:
            cp.wait()
        mine.wait()

    launch()
    return parts_ref[...]


def _as_lrc(a, lead):
    rest = a.shape[lead:]
    return a.reshape(a.shape[:lead] + (int(np.prod(rest[:-2], dtype=np.int64)),) + rest[-2:])


def _adam_math(w, g, m, v):
    m = ADAM_B1 * m + (1.0 - ADAM_B1) * g
    v = ADAM_B2 * v + (1.0 - ADAM_B2) * jnp.square(g)
    m_hat = m / (1.0 - ADAM_B1 ** ADAM_STEP)
    v_hat = v / (1.0 - ADAM_B2 ** ADAM_STEP)
    delta = -ADAM_LR * (m_hat / (jnp.sqrt(v_hat) + ADAM_EPS) + ADAM_WD * w)
    return delta, m, v


def adamw_sharded(w, parts, m, v, name):
    L, R, C = w.shape
    row_bytes = 2 * C * (L * N_DEV * 2 + 7 * 4)
    tr = next((t for t in _tiles_desc(R, 128, BF16_ROWS) if t * row_bytes <= MM_VMEM_BUDGET), BF16_ROWS)

    def body(w_ref, *refs):
        p_refs, (m_ref, v_ref, g_out, d_out, m_out, v_out) = refs[:L], refs[L:]
        for layer in range(L):
            @pl.when(pl.program_id(0) == layer)
            def _(p_ref=p_refs[layer]):
                g = p_ref[0].astype(f32)
                for k in range(1, N_DEV):
                    g = g + p_ref[k].astype(f32)
                delta, mn, vn = _adam_math(w_ref[...], g, m_ref[...], v_ref[...])
                g_out[...] = g
                d_out[...] = delta
                m_out[...] = mn
                v_out[...] = vn

    blk = pl.BlockSpec((None, tr, C), lambda l, r: (l, r, 0))
    part_specs = [pl.BlockSpec((N_DEV, tr, C), functools.partial(lambda layer, l, r: (0, jnp.where(l == layer, r, 0), 0), layer))
                  for layer in range(L)]
    return pl.pallas_call(
        body, name=name, out_shape=[jax.ShapeDtypeStruct(w.shape, f32)] * 4, grid=(L, R // tr),
        in_specs=[blk] + part_specs + [blk, blk], out_specs=[blk] * 4,
        compiler_params=_cp("arbitrary", "arbitrary"),
    )(w, *parts, m, v)


def adamw_summed8(w, g8, m, v, name):
    R, C = w.shape
    tr = _largest_tile(R, 512, SUBLANES)

    def body(w_ref, g_ref, m_ref, v_ref, g_out, d_out, m_out, v_out):
        g = g_ref[0]
        for d in range(1, N_DEV):
            g = g + g_ref[d]
        delta, mn, vn = _adam_math(w_ref[...], g, m_ref[...], v_ref[...])
        g_out[...] = g
        d_out[...] = delta
        m_out[...] = mn
        v_out[...] = vn

    blk = pl.BlockSpec((tr, C), lambda r: (r, 0))
    return pl.pallas_call(
        body, name=name, out_shape=[jax.ShapeDtypeStruct((R, C), f32)] * 4, grid=(R // tr,),
        in_specs=[blk, pl.BlockSpec((N_DEV, tr, C), lambda r: (0, r, 0)), blk, blk], out_specs=[blk] * 4,
        compiler_params=_cp("parallel"),
    )(w, g8, m, v)


def _tile_n(ns):
    if ns % 512 == 0:
        return 512
    if ns <= 1536:
        return ns
    return _largest_tile(ns, 512, LANES)


MM_VMEM_BUDGET = VMEM_LIMIT - 8 * 1024 * 1024


def _tiles_desc(n, cap, step):
    return [t for t in range(min(n, cap) // step * step, 0, -step) if n % t == 0] or [n]


def _mm_fits(in_tiles, out_tile, out_dtype, n_red, extra=0):
    rows, cols = out_tile
    total = sum(2 * 2 * r * c for r, c in in_tiles) + 2 * rows * cols * jnp.dtype(out_dtype).itemsize + extra
    if n_red > 1:
        total += 4 * rows * cols
    return total <= MM_VMEM_BUDGET


def _reduce_into(o_ref, acc, part, first, last):
    if acc is None:
        o_ref[...] = part().astype(o_ref.dtype)
        return

    @pl.when(first)
    def _():
        acc[...] = jnp.zeros_like(acc)

    acc[...] += part()

    @pl.when(last)
    def _():
        o_ref[...] = acc[...].astype(o_ref.dtype)


def mm_nn(a, w3, out_dtype, name):
    M, K = a.shape
    J, _, Ns = w3.shape
    tn = _tile_n(Ns)
    tm, tk = next(((tm, tk) for tk in _tiles_desc(K, 4096, LANES) for tm in _tiles_desc(M, 1024, BF16_ROWS)
                   if tm >= min(M, 256) and _mm_fits([(tm, tk), (tk, tn)], (tm, tn), out_dtype, K // tk)),
                  (min(M, 256), _largest_tile(K, 512, LANES)))
    nn, nk = Ns // tn, K // tk

    def body(a_ref, w_ref, o_ref, *acc):
        k = pl.program_id(3)
        _reduce_into(o_ref, acc[0] if acc else None, lambda: _nn(a_ref[...], w_ref[...]), k == 0, k == nk - 1)

    return pl.pallas_call(
        body, name=name, out_shape=jax.ShapeDtypeStruct((M, J * Ns), out_dtype),
        grid=(M // tm, J, nn, nk),
        in_specs=[pl.BlockSpec((tm, tk), lambda i, j, n, k: (i, k)),
                  pl.BlockSpec((None, tk, tn), lambda i, j, n, k: (j, k, n))],
        out_specs=pl.BlockSpec((tm, tn), lambda i, j, n, k: (i, j * nn + n)),
        scratch_shapes=[pltpu.VMEM((tm, tn), f32)] if nk > 1 else [],
        compiler_params=_cp("parallel", "parallel", "parallel", "arbitrary"),
    )(a, w3)


MM_NT_REDUCE_CAP = 5632


def mm_nt(g, w3, out_dtype, name, init=None):
    M = g.shape[0]
    J, K, Ns = w3.shape
    if Ns <= MM_NT_REDUCE_CAP:
        tn, jb = Ns, max(b for b in range(1, J + 1) if J % b == 0 and (b == 1 or b * Ns <= MM_NT_REDUCE_CAP))
    else:
        tn, jb = _largest_tile(Ns, MM_NT_REDUCE_CAP, LANES), 1
    nj, nn = J // jb, Ns // tn
    n_red = nj * nn + (init is not None)
    tk = _largest_tile(K, 512, LANES)
    tm = next((tm for tm in _tiles_desc(M, 1024, BF16_ROWS)
               if _mm_fits([(tm, jb * tn), (jb * tk, tn)], (tm, tk), out_dtype, n_red,
                           extra=0 if init is None else 2 * 4 * tm * tk)), min(M, 256))

    def body(*refs):
        g_ref, w_ref = refs[:2]
        i_ref = None if init is None else refs[2]
        o_ref = refs[2 + (init is not None)]
        acc = refs[3 + (init is not None):]
        j, n = pl.program_id(2), pl.program_id(3)
        first = (j == 0) & (n == 0)

        def part():
            prod = _nt(g_ref[:, :tn], w_ref[0])
            for b in range(1, jb):
                prod = prod + _nt(g_ref[:, b * tn:(b + 1) * tn], w_ref[b])
            return prod if init is None else prod + jnp.where(first, i_ref[...].astype(f32), 0.0)

        _reduce_into(o_ref, acc[0] if acc else None, part, first, (j == nj - 1) & (n == nn - 1))

    in_specs = [pl.BlockSpec((tm, jb * tn), lambda i, k, j, n: (i, j * nn + n)),
                pl.BlockSpec((jb, tk, tn), lambda i, k, j, n: (j, k, n))]
    args = [g, w3]
    if init is not None:
        in_specs.append(pl.BlockSpec((tm, tk), lambda i, k, j, n: (i, k)))
        args.append(init)
    return pl.pallas_call(
        body, name=name, out_shape=jax.ShapeDtypeStruct((M, K), out_dtype),
        grid=(M // tm, K // tk, nj, nn), in_specs=in_specs,
        out_specs=pl.BlockSpec((tm, tk), lambda i, k, j, n: (i, k)),
        scratch_shapes=[pltpu.VMEM((tm, tk), f32)] if n_red > 1 else [],
        compiler_params=_cp("parallel", "parallel", "arbitrary", "arbitrary"),
    )(*args)


def mm_tn(a, g, ns, out_dtype, name):
    M, K = a.shape
    J = g.shape[1] // ns
    tn, tk = _tile_n(ns), _largest_tile(K, 512, LANES)
    tm = next((tm for tm in _tiles_desc(M, 4096, BF16_ROWS)
               if _mm_fits([(tm, tk), (tm, tn)], (tk, tn), out_dtype, M // tm)), _largest_tile(M, 512, BF16_ROWS))
    nn, nm = ns // tn, M // tm

    def body(a_ref, g_ref, o_ref, *acc):
        m = pl.program_id(3)
        _reduce_into(o_ref, acc[0] if acc else None, lambda: _tn(a_ref[...], g_ref[...]), m == 0, m == nm - 1)

    return pl.pallas_call(
        body, name=name, out_shape=jax.ShapeDtypeStruct((J, K, ns), out_dtype),
        grid=(J, K // tk, nn, nm),
        in_specs=[pl.BlockSpec((tm, tk), lambda j, k, n, m: (m, k)),
                  pl.BlockSpec((tm, tn), lambda j, k, n, m: (m, j * nn + n))],
        out_specs=pl.BlockSpec((None, tk, tn), lambda j, k, n, m: (j, k, n)),
        scratch_shapes=[pltpu.VMEM((tk, tn), f32)] if nm > 1 else [],
        compiler_params=_cp("parallel", "parallel", "parallel", "arbitrary"),
    )(a, g)


ROW_TILE = 256
ROW_STEP = 64


def row_fwd(fn, rows, consts, outs, name, n_acc=0):
    T = rows[0].shape[0]
    tr = min(T, ROW_TILE)
    n_rows, n_consts, n_row_out = len(rows), len(consts), len(outs) - n_acc

    def body(*refs):
        r_refs, c_refs, o_refs = refs[:n_rows], refs[n_rows:n_rows + n_consts], refs[n_rows + n_consts:]
        cs = [c[...] for c in c_refs]
        acc_refs = o_refs[n_row_out:]
        if n_acc:
            @pl.when(pl.program_id(0) == 0)
            def _():
                for a in acc_refs:
                    a[...] = jnp.zeros_like(a)

        def step(s, carry):
            rows_s = pl.ds(pl.multiple_of(s * ROW_STEP, ROW_STEP), ROW_STEP)
            res = fn(*[r[rows_s, :].astype(f32) for r in r_refs], *cs)
            for o, val in zip(o_refs[:n_row_out], res[:n_row_out]):
                o[rows_s, :] = val.astype(o.dtype)
            return tuple(c + val for c, val in zip(carry, res[n_row_out:]))

        accs = lax.fori_loop(0, tr // ROW_STEP, step, tuple(jnp.zeros((1, c), f32) for c, _ in outs[n_row_out:]))
        for a, val in zip(acc_refs, accs):
            a[...] += val

    in_specs = [pl.BlockSpec((tr, r.shape[1]), lambda i: (i, 0)) for r in rows]
    in_specs += [pl.BlockSpec(c.shape, lambda i: (0, 0)) for c in consts]
    out_shape = [jax.ShapeDtypeStruct((T, c), dt) for c, dt in outs[:n_row_out]]
    out_shape += [jax.ShapeDtypeStruct((1, c), f32) for c, _ in outs[n_row_out:]]
    out_specs = [pl.BlockSpec((tr, c), lambda i: (i, 0)) for c, _ in outs[:n_row_out]]
    out_specs += [pl.BlockSpec((1, c), lambda i: (0, 0)) for c, _ in outs[n_row_out:]]
    return pl.pallas_call(
        body, name=name, out_shape=out_shape, grid=(T // tr,), in_specs=in_specs, out_specs=out_specs,
        compiler_params=_cp("arbitrary" if n_acc else "parallel"),
    )(*rows, *consts)


def row_bwd(fn, rows, consts, cts, grad_dtypes, name):
    T = rows[0].shape[0]
    tr = min(T, ROW_TILE)
    n_rows, n_consts, n_cts = len(rows), len(consts), len(cts)
    wanted = [i for i, dt in enumerate(grad_dtypes) if dt is not None]

    def body(*refs):
        r_refs = refs[:n_rows]
        c_refs = refs[n_rows:n_rows + n_consts]
        t_refs = refs[n_rows + n_consts:n_rows + n_consts + n_cts]
        o_refs = refs[n_rows + n_consts + n_cts:]
        gr_refs, gc_refs = o_refs[:len(wanted)], o_refs[len(wanted):]
        cs = [c[...] for c in c_refs]

        @pl.when(pl.program_id(0) == 0)
        def _():
            for a in gc_refs:
                a[...] = jnp.zeros_like(a)

        def step(s, carry):
            rows_s = pl.ds(pl.multiple_of(s * ROW_STEP, ROW_STEP), ROW_STEP)
            ins = [r[rows_s, :].astype(f32) for r in r_refs]
            _, vjp = jax.vjp(lambda *a: tuple(fn(*a)), *ins, *cs)
            grads = vjp(tuple(t[rows_s, :].astype(f32) for t in t_refs))
            for o, i in zip(gr_refs, wanted):
                o[rows_s, :] = grads[i].astype(o.dtype)
            return tuple(c + gval for c, gval in zip(carry, grads[n_rows:]))

        accs = lax.fori_loop(0, tr // ROW_STEP, step, tuple(jnp.zeros(c.shape, f32) for c in consts))
        for a, val in zip(gc_refs, accs):
            a[...] += val

    in_specs = [pl.BlockSpec((tr, r.shape[1]), lambda i: (i, 0)) for r in list(rows) + list(cts)]
    in_specs[n_rows:n_rows] = [pl.BlockSpec(c.shape, lambda i: (0, 0)) for c in consts]
    out_shape = [jax.ShapeDtypeStruct(rows[i].shape, grad_dtypes[i]) for i in wanted]
    out_shape += [jax.ShapeDtypeStruct(c.shape, f32) for c in consts]
    out_specs = [pl.BlockSpec((tr, rows[i].shape[1]), lambda i_: (i_, 0)) for i in wanted]
    out_specs += [pl.BlockSpec(c.shape, lambda i: (0, 0)) for c in consts]
    return pl.pallas_call(
        body, name=name, out_shape=out_shape, grid=(T // tr,), in_specs=in_specs, out_specs=out_specs,
        compiler_params=_cp("arbitrary"),
    )(*rows, *consts, *cts)


def _rms(x, g):
    return x * lax.rsqrt(jnp.mean(x * x, axis=-1, keepdims=True) + EPS) * g


def fn_prenorm(x, g):
    return (_rms(x, g),)


def fn_resnorm(x, m, g_post, g_pre):
    x_new = x + _rms(m, g_post)
    return x_new, _rms(x_new, g_pre)


def fn_res(x, m, g_post):
    return (x + _rms(m, g_post),)


def fn_final(x, m, target, g_post):
    err = x + _rms(m, g_post) - target
    n = err.shape[-1]
    return err / n, (0.5 / n) * jnp.sum(jnp.sum(err * err, axis=1, keepdims=True), axis=0, keepdims=True)


def fn_gate_norm(y, z, g):
    return (_rms(y * jax.nn.silu(z), g),)


CONV_ROWS = 256


def _conv_chunk(pre_ref, w, b, r, rb, K):
    t0 = pl.multiple_of(r * rb, rb)
    halo_at = pl.multiple_of(jnp.maximum(t0 - SUBLANES, 0), SUBLANES)
    halo = jnp.where(r > 0, pre_ref[pl.ds(halo_at, SUBLANES), :], 0.0)
    main = pre_ref[pl.ds(t0, rb), :]
    ext = jnp.concatenate([halo, main], axis=0)
    shifted = [main if k == K - 1 else pltpu.roll(ext, K - 1 - k, 0)[SUBLANES:] for k in range(K)]
    conv = b
    for k in range(K):
        conv = conv + w[k:k + 1] * shifted[k]
    return conv, shifted


def conv_fwd(streams, epilogue, out_cols, out_dtype, K, name):
    T = streams[0][0].shape[0]
    rb = min(T, CONV_ROWS)
    S = len(streams)

    def body(*refs):
        pre_refs, w_refs, b_refs, o_ref = refs[:S], refs[S:2 * S], refs[2 * S:3 * S], refs[3 * S]
        ws = [w[...] for w in w_refs]
        bs = [b[...] for b in b_refs]

        def step(r, carry):
            convs = [_conv_chunk(pre_refs[s], ws[s], bs[s], r, rb, K)[0] for s in range(S)]
            o_ref[pl.ds(pl.multiple_of(r * rb, rb), rb), :] = epilogue(*convs).astype(o_ref.dtype)
            return carry

        lax.fori_loop(0, T // rb, step, 0)

    in_specs = [pl.BlockSpec((T, LANES), functools.partial(lambda off, i: (0, off + i), st[1])) for st in streams]
    in_specs += [pl.BlockSpec((K, LANES), functools.partial(lambda off, i: (0, off + i), st[4])) for st in streams]
    in_specs += [pl.BlockSpec((1, LANES), functools.partial(lambda off, i: (0, off + i), st[4])) for st in streams]
    return pl.pallas_call(
        body, name=name, out_shape=jax.ShapeDtypeStruct((T, out_cols), out_dtype), grid=(out_cols // LANES,),
        in_specs=in_specs, out_specs=pl.BlockSpec((T, LANES), lambda i: (0, i)),
        compiler_params=_cp("parallel"),
    )(*[st[0] for st in streams], *[st[2] for st in streams], *[st[3] for st in streams])


def conv_bwd(streams, epilogue, dout, K, name):
    T, cols = dout.shape
    rb = min(T, CONV_ROWS)
    S = len(streams)

    def body(*refs):
        pre_refs, w_refs, b_refs, dout_ref = refs[:S], refs[S:2 * S], refs[2 * S:3 * S], refs[3 * S]
        o = refs[3 * S + 1:]
        dpre_refs, dw_refs, db_refs, scr = o[:S], o[S:2 * S], o[2 * S:3 * S], o[3 * S:]
        ws = [w[...] for w in w_refs]
        bs = [b[...] for b in b_refs]
        for s in range(S):
            scr[s][pl.ds(T, SUBLANES), :] = jnp.zeros((SUBLANES, LANES), f32)

        def phase1(r, carry):
            rows = pl.ds(pl.multiple_of(r * rb, rb), rb)
            convs, shifted = zip(*[_conv_chunk(pre_refs[s], ws[s], bs[s], r, rb, K) for s in range(S)])
            _, vjp = jax.vjp(epilogue, *convs)
            dconvs = vjp(dout_ref[rows, :].astype(f32))
            new = []
            for s in range(S):
                scr[s][rows, :] = dconvs[s]
                sums = [jnp.sum(dconvs[s] * shifted[s][k], axis=0, keepdims=True) for k in range(K)]
                sums.append(jnp.sum(dconvs[s], axis=0, keepdims=True))
                new.append(tuple(c + v for c, v in zip(carry[s], sums)))
            return tuple(new)

        zero = tuple(tuple(jnp.zeros((1, LANES), f32) for _ in range(K + 1)) for _ in range(S))
        sums = lax.fori_loop(0, T // rb, phase1, zero)
        tap = lax.broadcasted_iota(jnp.int32, (K, LANES), 0)
        for s in range(S):
            dw = jnp.zeros((K, LANES), f32)
            for k in range(K):
                dw = jnp.where(tap == k, sums[s][k], dw)
            dw_refs[s][...] = dw
            db_refs[s][...] = sums[s][K]

        def phase2(r, carry):
            t0 = pl.multiple_of(r * rb, rb)
            for s in range(S):
                ext = scr[s][pl.ds(t0, rb + SUBLANES), :]
                dpre = ws[s][K - 1:K] * ext[:rb]
                for k in range(K - 1):
                    j = K - 1 - k
                    dpre = dpre + ws[s][k:k + 1] * pltpu.roll(ext, rb + SUBLANES - j, 0)[:rb]
                dpre_refs[s][pl.ds(t0, rb), :] = dpre.astype(dpre_refs[s].dtype)
            return carry

        lax.fori_loop(0, T // rb, phase2, 0)

    in_specs = [pl.BlockSpec((T, LANES), functools.partial(lambda off, i: (0, off + i), st[1])) for st in streams]
    in_specs += [pl.BlockSpec((K, LANES), functools.partial(lambda off, i: (0, off + i), st[4])) for st in streams]
    in_specs += [pl.BlockSpec((1, LANES), functools.partial(lambda off, i: (0, off + i), st[4])) for st in streams]
    in_specs += [pl.BlockSpec((T, LANES), lambda i: (0, i))]
    out_shape = [jax.ShapeDtypeStruct((T, cols), bf16)] * S
    out_shape += [jax.ShapeDtypeStruct((K, cols), f32)] * S + [jax.ShapeDtypeStruct((1, cols), f32)] * S
    out_specs = [pl.BlockSpec((T, LANES), lambda i: (0, i))] * S
    out_specs += [pl.BlockSpec((K, LANES), lambda i: (0, i))] * S + [pl.BlockSpec((1, LANES), lambda i: (0, i))] * S
    res = pl.pallas_call(
        body, name=name, out_shape=out_shape, grid=(cols // LANES,), in_specs=in_specs, out_specs=out_specs,
        scratch_shapes=[pltpu.VMEM((T + SUBLANES, LANES), f32)] * S,
        compiler_params=_cp("parallel"),
    )(*[st[0] for st in streams], *[st[2] for st in streams], *[st[3] for st in streams], dout)
    return res[:S], res[S:2 * S], res[2 * S:]


def epi_gelu_gate(cg, cu):
    return jax.nn.gelu(cg, approximate=True) * cu


def epi_silu(c):
    return jax.nn.silu(c)


XA_ROWS = 256


def _xa_fn(q, k, v):
    s = _nt(q.astype(bf16), k.astype(bf16)) * (q.shape[-1] ** -0.5)
    p = jax.nn.softmax(s, axis=-1)
    return _nn(p.astype(bf16), v.astype(bf16))


def xa_fwd(q, kv, name):
    T, W = q.shape
    M = kv.shape[0]
    H = W // LANES
    tr = min(T, XA_ROWS)

    def body(q_ref, k_ref, v_ref, o_ref):
        o_ref[...] = _xa_fn(q_ref[...].astype(f32), k_ref[...].astype(f32), v_ref[...].astype(f32)).astype(o_ref.dtype)

    return pl.pallas_call(
        body, name=name, out_shape=jax.ShapeDtypeStruct((T, W), bf16), grid=(T // tr, H),
        in_specs=[pl.BlockSpec((tr, LANES), lambda i, h: (i, h)),
                  pl.BlockSpec((M, LANES), lambda i, h: (0, h)),
                  pl.BlockSpec((M, LANES), lambda i, h: (0, H + h))],
        out_specs=pl.BlockSpec((tr, LANES), lambda i, h: (i, h)),
        compiler_params=_cp("parallel", "parallel"),
    )(q, kv, kv)


def xa_bwd(q, kv, do, name):
    T, W = q.shape
    M = kv.shape[0]
    H = W // LANES
    tr = min(T, XA_ROWS)

    def body(q_ref, k_ref, v_ref, do_ref, dq_ref, dk_ref, dv_ref):
        @pl.when(pl.program_id(1) == 0)
        def _():
            dk_ref[...] = jnp.zeros_like(dk_ref)
            dv_ref[...] = jnp.zeros_like(dv_ref)

        _, vjp = jax.vjp(_xa_fn, q_ref[...].astype(f32), k_ref[...].astype(f32), v_ref[...].astype(f32))
        dq, dk, dv = vjp(do_ref[...].astype(f32))
        dq_ref[...] = dq.astype(dq_ref.dtype)
        dk_ref[...] += dk
        dv_ref[...] += dv

    return pl.pallas_call(
        body, name=name,
        out_shape=[jax.ShapeDtypeStruct((T, W), bf16), jax.ShapeDtypeStruct((M, W), f32), jax.ShapeDtypeStruct((M, W), f32)],
        grid=(H, T // tr),
        in_specs=[pl.BlockSpec((tr, LANES), lambda h, i: (i, h)),
                  pl.BlockSpec((M, LANES), lambda h, i: (0, h)),
                  pl.BlockSpec((M, LANES), lambda h, i: (0, H + h)),
                  pl.BlockSpec((tr, LANES), lambda h, i: (i, h))],
        out_specs=[pl.BlockSpec((tr, LANES), lambda h, i: (i, h)),
                   pl.BlockSpec((M, LANES), lambda h, i: (0, h)),
                   pl.BlockSpec((M, LANES), lambda h, i: (0, h))],
        compiler_params=_cp("parallel", "arbitrary"),
    )(q, kv, kv, do)


def _sgu_norm_fn(v_pre, g, b):
    v = jax.nn.gelu(v_pre, approximate=True)
    mu = jnp.mean(v, axis=-1, keepdims=True)
    vc = v - mu
    return vc * lax.rsqrt(jnp.mean(vc * vc, axis=-1, keepdims=True) + EPS) * g + b


def _sgu_mix_fn(u_pre, vn, w, b):
    q = w.shape[0]
    tril = lax.broadcasted_iota(jnp.int32, (q, q), 0) >= lax.broadcasted_iota(jnp.int32, (q, q), 1)
    mixed = _nn(jnp.where(tril, w, 0.0).astype(bf16), vn.astype(bf16)) + b
    return jax.nn.gelu(u_pre, approximate=True) * mixed


def _sgu_norm_phase(v_ref, g, b, vn_ref):
    def step(s, carry):
        rows = pl.ds(pl.multiple_of(s * ROW_STEP, ROW_STEP), ROW_STEP)
        vn_ref[rows, :] = _sgu_norm_fn(v_ref[rows, :], g, b)
        return carry

    lax.fori_loop(0, CHUNK // ROW_STEP, step, 0)


def sgu_fwd(uv_pre, norm_g, norm_b, w_sp, b_sp, name):
    T, W2 = uv_pre.shape
    W = W2 // 2
    G = w_sp.shape[0]
    gw = W // G

    def body(u_ref, v_ref, g_ref, b_ref, ws_ref, bs_ref, o_ref, vn_ref):
        _sgu_norm_phase(v_ref, g_ref[...], b_ref[...], vn_ref)

        def group(gi, carry):
            cols = pl.ds(pl.multiple_of(gi * gw, LANES), gw)
            o_ref[:, cols] = _sgu_mix_fn(u_ref[:, cols], vn_ref[:, cols], ws_ref[gi], bs_ref[gi]).astype(o_ref.dtype)
            return carry

        lax.fori_loop(0, G, group, 0)

    full = lambda a: pl.BlockSpec(a.shape, lambda c: (0,) * a.ndim)
    return pl.pallas_call(
        body, name=name, out_shape=jax.ShapeDtypeStruct((T, W), bf16), grid=(T // CHUNK,),
        in_specs=[pl.BlockSpec((CHUNK, W), lambda c: (c, 0)), pl.BlockSpec((CHUNK, W), lambda c: (c, 1)),
                  full(norm_g), full(norm_b), full(w_sp), full(b_sp)],
        out_specs=pl.BlockSpec((CHUNK, W), lambda c: (c, 0)),
        scratch_shapes=[pltpu.VMEM((CHUNK, W), f32)],
        compiler_params=_cp("parallel"),
    )(uv_pre, uv_pre, norm_g, norm_b, w_sp, b_sp)


def sgu_bwd(uv_pre, norm_g, norm_b, w_sp, b_sp, dout, name):
    T, W2 = uv_pre.shape
    W = W2 // 2
    G = w_sp.shape[0]
    gw = W // G

    def body(u_ref, v_ref, g_ref, b_ref, ws_ref, bs_ref, do_ref, duv_ref, dg_ref, db_ref, dws_ref, dbs_ref,
             vn_ref, dvn_ref):
        @pl.when(pl.program_id(0) == 0)
        def _():
            for a in (dg_ref, db_ref, dws_ref, dbs_ref):
                a[...] = jnp.zeros_like(a)

        g, b = g_ref[...], b_ref[...]
        _sgu_norm_phase(v_ref, g, b, vn_ref)

        def group(gi, carry):
            cols = pl.ds(pl.multiple_of(gi * gw, LANES), gw)
            _, vjp = jax.vjp(_sgu_mix_fn, u_ref[:, cols], vn_ref[:, cols], ws_ref[gi], bs_ref[gi])
            du, dvn, dw, dbias = vjp(do_ref[:, cols])
            duv_ref[:, cols] = du.astype(duv_ref.dtype)
            dvn_ref[:, cols] = dvn
            dws_ref[gi] += dw
            dbs_ref[gi] += dbias
            return carry

        lax.fori_loop(0, G, group, 0)

        def step(s, carry):
            rows = pl.ds(pl.multiple_of(s * ROW_STEP, ROW_STEP), ROW_STEP)
            _, vjp = jax.vjp(_sgu_norm_fn, v_ref[rows, :], g, b)
            dv, dg, dbn = vjp(dvn_ref[rows, :])
            duv_ref[rows, pl.ds(W, W)] = dv.astype(duv_ref.dtype)
            return carry[0] + dg, carry[1] + dbn

        dg, dbn = lax.fori_loop(0, CHUNK // ROW_STEP, step, (jnp.zeros((1, W), f32), jnp.zeros((1, W), f32)))
        dg_ref[...] += dg
        db_ref[...] += dbn

    full = lambda a: pl.BlockSpec(a.shape, lambda c: (0,) * a.ndim)
    return pl.pallas_call(
        body, name=name,
        out_shape=[jax.ShapeDtypeStruct((T, W2), bf16), jax.ShapeDtypeStruct((1, W), f32), jax.ShapeDtypeStruct((1, W), f32),
                   jax.ShapeDtypeStruct(w_sp.shape, f32), jax.ShapeDtypeStruct(b_sp.shape, f32)],
        grid=(T // CHUNK,),
        in_specs=[pl.BlockSpec((CHUNK, W), lambda c: (c, 0)), pl.BlockSpec((CHUNK, W), lambda c: (c, 1)),
                  full(norm_g), full(norm_b), full(w_sp), full(b_sp), pl.BlockSpec((CHUNK, W), lambda c: (c, 0))],
        out_specs=[pl.BlockSpec((CHUNK, W2), lambda c: (c, 0)), full(norm_g), full(norm_b), full(w_sp), full(b_sp)],
        scratch_shapes=[pltpu.VMEM((CHUNK, W), f32), pltpu.VMEM((CHUNK, W), f32)],
        compiler_params=_cp("arbitrary"),
    )(uv_pre, uv_pre, norm_g, norm_b, w_sp, b_sp, dout)


SB_SUM_COLS = 256


def _sb_key_group(T):
    return 512 if T % 512 == 0 else T


def _sb_query_rows(T):
    return 256 if T % 256 == 0 else CHUNK


def _sb_block(q, k, i, g, diagonal):
    qb, kg = q.shape[0], k.shape[0]
    z = _nt(q, k) * (q.shape[-1] ** -0.5)
    sp = jnp.log(1.0 + jnp.exp(-jnp.abs(z)))
    log_beta = jnp.minimum(z, 0.0) - sp
    log_1mb = -jnp.maximum(z, 0.0) - sp
    if not diagonal:
        return z, None, log_beta, log_1mb
    t_idx = i * qb + lax.broadcasted_iota(jnp.int32, (qb, kg), 0)
    s_idx = g * kg + lax.broadcasted_iota(jnp.int32, (qb, kg), 1)
    valid = s_idx < t_idx
    return z, valid, log_beta, jnp.where(valid, log_1mb, 0.0)


def _sb_masked(valid, x):
    return x if valid is None else jnp.where(valid, x, 0.0)


def _order_matrix(later):
    r = lax.broadcasted_iota(jnp.int32, (SB_SUM_COLS, SB_SUM_COLS), 0)
    c = lax.broadcasted_iota(jnp.int32, (SB_SUM_COLS, SB_SUM_COLS), 1)
    return (r > c if later else r < c).astype(bf16)


def _masked_sums(parts, order):
    terms = []
    for x in parts:
        hi = x.astype(bf16)
        rest = x - hi.astype(f32)
        mid = rest.astype(bf16)
        terms += [hi, mid, (rest - mid.astype(f32)).astype(bf16)]
    rows = parts[0].shape[0]
    prod = _nn(jnp.concatenate(terms, axis=0), order)
    piece = lambda n: prod[n * rows:(n + 1) * rows]
    return [piece(3 * p) + piece(3 * p + 1) + piece(3 * p + 2) for p in range(len(parts))]


def sb_fwd(qkv, name):
    T = qkv.shape[0]
    H = qkv.shape[1] // (3 * LANES)
    qb, kg = _sb_query_rows(T), _sb_key_group(T)
    halves = kg // SB_SUM_COLS

    def body(q_ref, k_ref, v_ref, o_ref, tot_ref):
        i = pl.program_id(1)
        q = q_ref[...]
        later = _order_matrix(True)
        n_groups = (i * qb + qb + kg - 1) // kg

        def step(g, carry, diagonal):
            acc, run = carry
            rows = pl.ds(pl.multiple_of(g * kg, kg), kg)
            _, valid, log_beta, log_1mb = _sb_block(q, k_ref[rows, :], i, g, diagonal)
            parts = [log_1mb[:, hh * SB_SUM_COLS:(hh + 1) * SB_SUM_COLS] for hh in range(halves)]
            tails = _masked_sums(parts, later)
            for hh in reversed(range(halves)):
                tails[hh] = tails[hh] + run
                run = run + jnp.sum(parts[hh], axis=1, keepdims=True)
            a = _sb_masked(valid, jnp.exp(log_beta + jnp.concatenate(tails, axis=1)))
            return acc + _nn(a.astype(bf16), v_ref[rows, :]), run

        carry = step(n_groups - 1, (jnp.zeros((qb, LANES), f32), jnp.zeros((qb, 1), f32)), True)
        acc, run = lax.fori_loop(0, n_groups - 1, lambda gg, c: step(n_groups - 2 - gg, c, False), carry)
        o_ref[...] = acc.astype(o_ref.dtype)
        tot_ref[...] = run

    return pl.pallas_call(
        body, name=name,
        out_shape=[jax.ShapeDtypeStruct((T, H * LANES), bf16), jax.ShapeDtypeStruct((H, T, 1), f32)],
        grid=(H, T // qb),
        in_specs=[pl.BlockSpec((qb, LANES), lambda h, i: (i, h)),
                  pl.BlockSpec((T, LANES), lambda h, i: (0, H + h)),
                  pl.BlockSpec((T, LANES), lambda h, i: (0, 2 * H + h))],
        out_specs=[pl.BlockSpec((qb, LANES), lambda h, i: (i, h)),
                   pl.BlockSpec((None, qb, 1), lambda h, i: (h, i, 0))],
        compiler_params=_cp("parallel", "parallel"),
    )(qkv, qkv, qkv)


def sb_bwd(qkv, tot, do, name):
    T = qkv.shape[0]
    H = qkv.shape[1] // (3 * LANES)
    qb, kg = _sb_query_rows(T), _sb_key_group(T)
    halves = kg // SB_SUM_COLS

    def body(q_ref, k_ref, v_ref, tot_ref, do_ref, dq_ref, dk_ref, dv_ref):
        i = pl.program_id(1)

        @pl.when(i == 0)
        def _():
            dk_ref[...] = jnp.zeros_like(dk_ref)
            dv_ref[...] = jnp.zeros_like(dv_ref)

        q = q_ref[...]
        do = do_ref[...].astype(bf16)
        tot = tot_ref[...]
        later, earlier = _order_matrix(True), _order_matrix(False)
        scale = q.shape[-1] ** -0.5
        n_groups = (i * qb + qb + kg - 1) // kg

        def step(g, carry, diagonal):
            dq, before, d_run = carry
            rows = pl.ds(pl.multiple_of(g * kg, kg), kg)
            k, v = k_ref[rows, :], v_ref[rows, :]
            z, valid, log_beta, log_1mb = _sb_block(q, k, i, g, diagonal)
            parts = [log_1mb[:, hh * SB_SUM_COLS:(hh + 1) * SB_SUM_COLS] for hh in range(halves)]
            tails = _masked_sums(parts, later)
            for hh in range(halves):
                before = before + jnp.sum(parts[hh], axis=1, keepdims=True)
                tails[hh] = tails[hh] + (tot - before)
            a = _sb_masked(valid, jnp.exp(log_beta + jnp.concatenate(tails, axis=1)))
            d_e = _nt(do, v) * a
            parts = [d_e[:, hh * SB_SUM_COLS:(hh + 1) * SB_SUM_COLS] for hh in range(halves)]
            d_l1 = _masked_sums(parts, earlier)
            for hh in range(halves):
                d_l1[hh] = d_l1[hh] + d_run
                d_run = d_run + jnp.sum(parts[hh], axis=1, keepdims=True)
            d_l1 = _sb_masked(valid, jnp.concatenate(d_l1, axis=1))
            sig = jax.nn.sigmoid(z)
            dz = ((d_e * (1.0 - sig) - d_l1 * sig) * scale).astype(bf16)
            dk_ref[rows, :] += _tn(dz, q)
            dv_ref[rows, :] += _tn(a.astype(bf16), do)
            return dq + _nn(dz, k), before, d_run

        zero_col = jnp.zeros((qb, 1), f32)
        carry = lax.fori_loop(0, n_groups - 1, lambda g, c: step(g, c, False),
                              (jnp.zeros((qb, LANES), f32), zero_col, zero_col))
        dq, _, _ = step(n_groups - 1, carry, True)
        dq_ref[...] = dq.astype(dq_ref.dtype)

    W = H * LANES
    return pl.pallas_call(
        body, name=name,
        out_shape=[jax.ShapeDtypeStruct((T, W), bf16), jax.ShapeDtypeStruct((T, W), f32), jax.ShapeDtypeStruct((T, W), f32)],
        grid=(H, T // qb),
        in_specs=[pl.BlockSpec((qb, LANES), lambda h, i: (i, h)),
                  pl.BlockSpec((T, LANES), lambda h, i: (0, H + h)),
                  pl.BlockSpec((T, LANES), lambda h, i: (0, 2 * H + h)),
                  pl.BlockSpec((None, qb, 1), lambda h, i: (h, i, 0)),
                  pl.BlockSpec((qb, LANES), lambda h, i: (i, h))],
        out_specs=[pl.BlockSpec((qb, LANES), lambda h, i: (i, h)),
                   pl.BlockSpec((T, LANES), lambda h, i: (0, h)),
                   pl.BlockSpec((T, LANES), lambda h, i: (0, h))],
        compiler_params=_cp("parallel", "arbitrary"),
    )(qkv, qkv, qkv, tot, do)


def _softplus(x):
    return jnp.maximum(x, 0.0) + jnp.log1p(jnp.exp(-jnp.abs(x)))


def _ssd_chunk_fn(head0, xs, b_mat, c_mat, dt_raw, dt_bias, a_log, d_skip, prev):
    q = dt_raw.shape[0]
    lane = lax.broadcasted_iota(jnp.int32, (q, LANES), 1)
    sub = lax.broadcasted_iota(jnp.int32, (q, LANES), 0)
    causal = sub >= lane
    dt = _softplus(dt_raw + dt_bias)
    a_cum = _nn(causal.astype(f32), dt * (-jnp.exp(a_log)), HIGHEST)
    a_cum_t = a_cum.T
    cb = _nt(c_mat.astype(bf16), b_mat.astype(bf16))
    bm, cm = b_mat.astype(bf16), c_mat.astype(bf16)
    ys, new = [], []
    for r in range(len(xs)):
        in_lane, in_sub = lane == head0 + r, sub == head0 + r
        col_a = jnp.sum(jnp.where(in_lane, a_cum, 0.0), axis=1, keepdims=True)
        row_a = jnp.sum(jnp.where(in_sub, a_cum_t, 0.0), axis=0, keepdims=True)
        col_dt = jnp.sum(jnp.where(in_lane, dt, 0.0), axis=1, keepdims=True)
        skip = jnp.sum(jnp.where(in_lane[:1], d_skip, 0.0), axis=1, keepdims=True)
        a_last = jnp.sum(jnp.where(sub[:, :1] == q - 1, col_a, 0.0), axis=0, keepdims=True)
        decay_in = jnp.exp(jnp.where(causal, col_a - row_a, -jnp.inf))
        xdt = xs[r] * col_dt
        y_diag = _nn((cb * decay_in).astype(bf16), xdt.astype(bf16))
        y_off = _nt(cm, prev[r].astype(bf16)) * jnp.exp(col_a)
        ys.append(y_diag + y_off + xs[r] * skip)
        state = _tn((xdt * jnp.exp(a_last - col_a)).astype(bf16), bm)
        new.append(prev[r] * jnp.exp(a_last) + state)
    return ys, new


SSD_GROUPS_PER_STEP = 2


def _ssd_specs(T, G, reverse):
    nc = T // CHUNK
    R, P, N = SSD_HEADS_PER_GROUP, SSD_HEAD_DIM, SSD_STATE
    gp = SSD_GROUPS_PER_STEP if G % SSD_GROUPS_PER_STEP == 0 else 1
    ch = (lambda c: nc - 1 - c) if reverse else (lambda c: c)
    xs = pl.BlockSpec((CHUNK, gp * R * P), lambda g, c: (ch(c), g))
    bm = pl.BlockSpec((CHUNK, gp * N), lambda g, c: (ch(c), G * R * P // (gp * N) + g))
    cm = pl.BlockSpec((CHUNK, gp * N), lambda g, c: (ch(c), (G * R * P + G * N) // (gp * N) + g))
    dt = pl.BlockSpec((CHUNK, LANES), lambda g, c: (ch(c), 0))
    row = pl.BlockSpec((1, LANES), lambda g, c: (0, 0))
    st = pl.BlockSpec((gp, None, R * P, N), lambda g, c: (g, ch(c), 0, 0))
    return nc, gp, xs, bm, cm, dt, row, st


def ssd_fwd(xbc, dt_raw, dt_bias, a_log, d_skip, name):
    T = xbc.shape[0]
    R, P, N = SSD_HEADS_PER_GROUP, SSD_HEAD_DIM, SSD_STATE
    G = xbc.shape[1] // (R * P + 2 * N)
    nc, gp, xs_s, bm_s, cm_s, dt_s, row_s, st_s = _ssd_specs(T, G, False)

    def body(xs_ref, b_ref, c_ref, dt_ref, bias_ref, alog_ref, skip_ref, y_ref, st_ref, state):
        @pl.when(pl.program_id(1) == 0)
        def _():
            state[...] = jnp.zeros_like(state)

        done = []
        for gi in range(gp):
            x = xs_ref[:, gi * R * P:(gi + 1) * R * P]
            xs = [x[:, r * P:(r + 1) * P] for r in range(R)]
            prev = [state[gi, r] for r in range(R)]
            ys, new = _ssd_chunk_fn((pl.program_id(0) * gp + gi) * R, xs, b_ref[:, gi * N:(gi + 1) * N],
                                    c_ref[:, gi * N:(gi + 1) * N], dt_ref[...], bias_ref[...], alog_ref[...],
                                    skip_ref[...], prev)
            done.append((prev, ys, new))
        for gi, (prev, ys, new) in enumerate(done):
            y_ref[:, gi * R * P:(gi + 1) * R * P] = jnp.concatenate(ys, axis=1)
            for r in range(R):
                st_ref[gi, pl.ds(r * P, P), :] = prev[r]
                state[gi, r] = new[r]

    return pl.pallas_call(
        body, name=name,
        out_shape=[jax.ShapeDtypeStruct((T, G * R * P), f32), jax.ShapeDtypeStruct((G, nc, R * P, N), f32)],
        grid=(G // gp, nc), in_specs=[xs_s, bm_s, cm_s, dt_s, row_s, row_s, row_s], out_specs=[xs_s, st_s],
        scratch_shapes=[pltpu.VMEM((gp, R, P, N), f32)],
        compiler_params=_cp("parallel", "arbitrary"),
    )(xbc, xbc, xbc, dt_raw, dt_bias, a_log, d_skip)


def ssd_bwd(xbc, dt_raw, dt_bias, a_log, d_skip, states, dy, name):
    T = xbc.shape[0]
    R, P, N = SSD_HEADS_PER_GROUP, SSD_HEAD_DIM, SSD_STATE
    G = xbc.shape[1] // (R * P + 2 * N)
    nc, gp, xs_s, bm_s, cm_s, dt_s, row_s, st_s = _ssd_specs(T, G, True)

    def body(xs_ref, b_ref, c_ref, dt_ref, bias_ref, alog_ref, skip_ref, st_ref, dy_ref,
             dx_ref, db_ref, dc_ref, ddt_ref, dbias_ref, dalog_ref, dskip_ref, dstate):
        g, c = pl.program_id(0), pl.program_id(1)

        @pl.when(c == 0)
        def _():
            dstate[...] = jnp.zeros_like(dstate)

        @pl.when((c == 0) & (g == 0))
        def _():
            for a in (dbias_ref, dalog_ref, dskip_ref):
                a[...] = jnp.zeros_like(a)

        done = []
        for gi in range(gp):
            cols, ncols = slice(gi * R * P, (gi + 1) * R * P), slice(gi * N, (gi + 1) * N)
            x, dyv = xs_ref[:, cols], dy_ref[:, cols]
            xs = [x[:, r * P:(r + 1) * P] for r in range(R)]
            prev = [st_ref[gi, pl.ds(r * P, P), :] for r in range(R)]
            _, vjp = jax.vjp(functools.partial(_ssd_chunk_fn, (g * gp + gi) * R), xs, b_ref[:, ncols], c_ref[:, ncols],
                             dt_ref[...], bias_ref[...], alog_ref[...], skip_ref[...], prev)
            done.append(vjp(([dyv[:, r * P:(r + 1) * P] for r in range(R)], [dstate[gi, r] for r in range(R)])))
        for gi, (dxs, dbm, dcm, ddt, dbias, dalog, dskip, dprev) in enumerate(done):
            dx_ref[:, gi * R * P:(gi + 1) * R * P] = jnp.concatenate(dxs, axis=1)
            db_ref[:, gi * N:(gi + 1) * N] = dbm
            dc_ref[:, gi * N:(gi + 1) * N] = dcm
            ddt_ref[gi] = ddt
            for r in range(R):
                dstate[gi, r] = dprev[r]
        dbias_ref[...] += sum(d[4] for d in done)
        dalog_ref[...] += sum(d[5] for d in done)
        dskip_ref[...] += sum(d[6] for d in done)

    small = pl.BlockSpec((CHUNK, gp * N), lambda g, c: (nc - 1 - c, g))
    return pl.pallas_call(
        body, name=name,
        out_shape=[jax.ShapeDtypeStruct((T, G * R * P), f32), jax.ShapeDtypeStruct((T, G * N), f32),
                   jax.ShapeDtypeStruct((T, G * N), f32), jax.ShapeDtypeStruct((G, T, LANES), f32)]
        + [jax.ShapeDtypeStruct((1, LANES), f32)] * 3,
        grid=(G // gp, nc), in_specs=[xs_s, bm_s, cm_s, dt_s, row_s, row_s, row_s, st_s, xs_s],
        out_specs=[xs_s, small, small, pl.BlockSpec((gp, CHUNK, LANES), lambda g, c: (g, nc - 1 - c, 0)),
                   row_s, row_s, row_s],
        scratch_shapes=[pltpu.VMEM((gp, R, P, N), f32)],
        compiler_params=_cp("arbitrary", "arbitrary"),
    )(xbc, xbc, xbc, dt_raw, dt_bias, a_log, d_skip, states, dy)


WEIGHTS = ["ln_mix_pre", "ln_mix_post", "ln_mem", "ln_xa_pre", "ln_xa_post", "ln_ffn_pre", "ln_ffn_post",
           "xa_wq", "xa_wkv", "xa_wo", "ffn_w_in", "ffn_conv_w", "ffn_conv_b", "ffn_w_out",
           "ssd_w_in", "ssd_conv_w", "ssd_conv_b", "ssd_dt_bias", "ssd_a_log", "ssd_d", "ssd_norm", "ssd_w_out",
           "sg_w_in", "sg_v_norm_g", "sg_v_norm_b", "sg_w_spatial", "sg_b_spatial", "sg_w_out", "sb_w_qkv", "sb_w_out"]
BIG = ["xa_wq", "xa_wkv", "xa_wo", "ffn_w_in", "ffn_w_out", "ssd_w_in", "ssd_w_out", "sg_w_in", "sg_w_out",
       "sb_w_qkv", "sb_w_out"]
SMALL_SHARDED = ["ffn_conv_w", "ssd_conv_w", "ssd_conv_b", "ssd_norm"]
REPLICATED = [n for n in WEIGHTS if n not in BIG and n not in SMALL_SHARDED]


def fn_xpre(x, g):
    return x, _rms(x, g)


def _pack_rows(arrs, lead=0):
    head = arrs[0].shape[:lead]
    flat = jnp.concatenate([a.reshape(head + (-1,)) for a in arrs], axis=-1)
    n = flat.shape[-1]
    rows = -(-n // (SUBLANES * LANES)) * SUBLANES
    flat = jnp.pad(flat, [(0, 0)] * lead + [(0, rows * LANES - n)])
    return flat.reshape(head + (rows, LANES))


def _unpack_rows(packed, shapes):
    head = packed.shape[:-2]
    flat = packed.reshape(head + (-1,))
    out, off = [], 0
    for shp in shapes:
        n = int(np.prod(shp, dtype=np.int64))
        out.append(flat[..., off:off + n].reshape(head + tuple(shp)))
        off += n
    return out


def _merge_last(a8):
    return jnp.moveaxis(a8, 0, -2).reshape(a8.shape[1:-1] + (N_DEV * a8.shape[-1],))


def _split_last(a):
    return jnp.moveaxis(a.reshape(a.shape[:-1] + (N_DEV, a.shape[-1] // N_DEV)), -2, 0)


def kernel(x, mem, ln_mix_pre, ln_mix_post, ln_mem, ln_xa_pre, ln_xa_post, ln_ffn_pre, ln_ffn_post, xa_wq, xa_wkv, xa_wo, ffn_w_in, ffn_conv_w, ffn_conv_b, ffn_w_out, ssd_w_in, ssd_conv_w, ssd_conv_b, ssd_dt_bias, ssd_a_log, ssd_d, ssd_norm, ssd_w_out, sg_w_in, sg_v_norm_g, sg_v_norm_b, sg_w_spatial, sg_b_spatial, sg_w_out, sb_w_qkv, sb_w_out, loss_target, m_ln_mix_pre, m_ln_mix_post, m_ln_mem, m_ln_xa_pre, m_ln_xa_post, m_ln_ffn_pre, m_ln_ffn_post, m_xa_wq, m_xa_wkv, m_xa_wo, m_ffn_w_in, m_ffn_conv_w, m_ffn_conv_b, m_ffn_w_out, m_ssd_w_in, m_ssd_conv_w, m_ssd_conv_b, m_ssd_dt_bias, m_ssd_a_log, m_ssd_d, m_ssd_norm, m_ssd_w_out, m_sg_w_in, m_sg_v_norm_g, m_sg_v_norm_b, m_sg_w_spatial, m_sg_b_spatial, m_sg_w_out, m_sb_w_qkv, m_sb_w_out, v_ln_mix_pre, v_ln_mix_post, v_ln_mem, v_ln_xa_pre, v_ln_xa_post, v_ln_ffn_pre, v_ln_ffn_post, v_xa_wq, v_xa_wkv, v_xa_wo, v_ffn_w_in, v_ffn_conv_w, v_ffn_conv_b, v_ffn_w_out, v_ssd_w_in, v_ssd_conv_w, v_ssd_conv_b, v_ssd_dt_bias, v_ssd_a_log, v_ssd_d, v_ssd_norm, v_ssd_w_out, v_sg_w_in, v_sg_v_norm_g, v_sg_v_norm_b, v_sg_w_spatial, v_sg_b_spatial, v_sg_w_out, v_sb_w_qkv, v_sb_w_out):
    p = dict(locals())
    x, mem, target = p["x"][0], p["mem"][0], p["loss_target"][0]
    T, D = x.shape
    depth = ln_mix_pre.shape[0]
    me = 4 * lax.axis_index("x") + 2 * lax.axis_index("y") + lax.axis_index("c")

    def gather(w, name):
        return sc_all_gather(w.astype(bf16), name)

    parts = {n: {} for n in BIG}
    pending, due = [], {}

    def scatter(name, layer, g8):
        pending.append((name, layer, g8))

    def deadline(name, s):
        if name == "ffn_w_out":
            return s - 2, 0
        if name == "ffn_w_in":
            return s - 2, 1
        if name.startswith("xa_"):
            return (s - 3, 0) if s >= 3 else (0, 1)
        return (s - 1, 0) if name.endswith("_out") else (s - 2, 0)

    def ship_pending(carry, s):
        carry, *held = lax.optimization_barrier((carry, *[g8 for _, _, g8 in pending]))
        for (name, layer, _), g8 in zip(pending, held):
            parts[name][layer] = sc_scatter_parts(g8, "rs_" + name)
            due.setdefault(deadline(name, s), []).append((name, layer))
        pending.clear()
        return carry

    def collect_due(carry, s):
        names = due.pop(s, [])
        if names:
            carry, *landed = lax.optimization_barrier((carry, *[parts[n][l] for n, l in names]))
            for (n, l), a in zip(names, landed):
                parts[n][l] = a
        return carry

    small8 = all_gather(_pack_rows([p[n] for n in SMALL_SHARDED]), "ag_small")
    full = {n: _merge_last(a) for n, a in zip(SMALL_SHARDED, _unpack_rows(small8, [p[n].shape for n in SMALL_SHARDED]))}

    grads = {n: {} for n in WEIGHTS}

    def ssd(h, j):
        g_in = gather(p["ssd_w_in"][j], "ag_ssd_in")
        g_in, h = lax.optimization_barrier((g_in, h))
        w_full = jnp.moveaxis(g_in, 0, 1).reshape(D, -1)
        w_out = gather(p["ssd_w_out"][j], "ag_ssd_out").reshape(1, -1, D)
        d_inner, conv_dim, heads = w_out.shape[1], full["ssd_conv_w"].shape[-1], ssd_dt_bias.shape[1]
        w_z, w_x = w_full[None, :, :d_inner], w_full[None, :, d_inner:d_inner + conv_dim]
        w_dt = jnp.pad(w_full[:, d_inner + conv_dim:], ((0, 0), (0, LANES - heads)))[None]
        lane_row = lambda a: jnp.pad(a[j:j + 1], ((0, 0), (0, LANES - heads)))
        bias, a_log, d_skip = lane_row(ssd_dt_bias), lane_row(ssd_a_log), lane_row(ssd_d)
        norm_g = full["ssd_norm"][j:j + 1]
        z = mm_nn(h, w_z, f32, "ssd_in_z")
        xbc_pre = mm_nn(h, w_x, f32, "ssd_in_x")
        dt_raw = mm_nn(h, w_dt, f32, "ssd_in_dt")
        streams = [(xbc_pre, 0, full["ssd_conv_w"][j], full["ssd_conv_b"][j:j + 1], 0)]
        xbc = conv_fwd(streams, epi_silu, conv_dim, f32, 4, "ssd_conv")
        y, states = ssd_fwd(xbc, dt_raw, bias, a_log, d_skip, "ssd_core")
        gated = row_fwd(fn_gate_norm, [y, z], [norm_g], [(d_inner, bf16)], "ssd_gate")[0]
        out = mm_nn(gated, w_out, f32, "ssd_out")

        def bwd(d_out, mid):
            d_gated = mm_nt(d_out, w_out, f32, "ssd_out_dx")
            scatter("ssd_w_out", j, mm_tn(gated, d_out, D, bf16, "ssd_out_dw").reshape(N_DEV, -1, D))
            dy, dz, d_norm = row_bwd(fn_gate_norm, [y, z], [norm_g], [d_gated], [f32, bf16], "ssd_gate_bwd")
            dxs, dbm, dcm, ddt_g, d_bias, d_alog, d_skipg = ssd_bwd(xbc, dt_raw, bias, a_log, d_skip, states, dy, "ssd_core_bwd")
            dxs = mid(dxs)
            (dx_pre,), (d_cw,), (d_cb,) = conv_bwd(streams, epi_silu, jnp.concatenate([dxs, dbm, dcm], axis=1), 4, "ssd_conv_bwd")
            ddt = jnp.sum(ddt_g, axis=0).astype(bf16)
            dh = mm_nt(dz, w_z, f32, "ssd_in_z_dx")
            dh = mm_nt(dx_pre, w_x, f32, "ssd_in_x_dx", init=dh)
            dh = mm_nt(ddt, w_dt, f32, "ssd_in_dt_dx", init=dh)
            dw = jnp.concatenate([mm_tn(h, dz, d_inner, bf16, "ssd_in_z_dw")[0], mm_tn(h, dx_pre, conv_dim, bf16, "ssd_in_x_dw")[0],
                                  mm_tn(h, ddt, LANES, bf16, "ssd_in_dt_dw")[0][:, :heads]], axis=1)
            scatter("ssd_w_in", j, _split_last(dw))
            grads["ssd_conv_w"][j], grads["ssd_conv_b"][j], grads["ssd_norm"][j] = d_cw, d_cb[0], d_norm[0]
            grads["ssd_dt_bias"][j], grads["ssd_a_log"][j], grads["ssd_d"][j] = d_bias[0, :heads], d_alog[0, :heads], d_skipg[0, :heads]
            return dh

        return out, bwd

    def sgu(h, j):
        w_in = gather(p["sg_w_in"][j], "ag_sg_in")
        w_out = gather(p["sg_w_out"][j], "ag_sg_out").reshape(1, -1, D)
        norm_g, norm_b = sg_v_norm_g[j:j + 1], sg_v_norm_b[j:j + 1]
        w_sp, b_sp = sg_w_spatial[j], sg_b_spatial[j][..., None]
        uv = mm_nn(h, w_in, f32, "sg_in")
        gated = sgu_fwd(uv, norm_g, norm_b, w_sp, b_sp, "sg_core")
        out = mm_nn(gated, w_out, f32, "sg_out")

        def bwd(d_out, mid):
            d_gated = mm_nt(d_out, w_out, f32, "sg_out_dx")
            scatter("sg_w_out", j, mm_tn(gated, d_out, D, bf16, "sg_out_dw").reshape(N_DEV, -1, D))
            duv, d_ng, d_nb, d_ws, d_bs = sgu_bwd(uv, norm_g, norm_b, w_sp, b_sp, d_gated, "sg_core_bwd")
            duv = mid(duv)
            grads["sg_v_norm_g"][j], grads["sg_v_norm_b"][j] = d_ng[0], d_nb[0]
            grads["sg_w_spatial"][j], grads["sg_b_spatial"][j] = d_ws, d_bs[..., 0]
            scatter("sg_w_in", j, mm_tn(h, duv, w_in.shape[2], bf16, "sg_in_dw"))
            return mm_nt(duv, w_in, f32, "sg_in_dx")

        return out, bwd

    def stick(h, j):
        w_qkv = gather(p["sb_w_qkv"][j], "ag_sb_qkv")
        w_out = gather(p["sb_w_out"][j], "ag_sb_out").reshape(1, -1, D)
        qkv = mm_nn(h, w_qkv, bf16, "sb_qkv")
        o, tot = sb_fwd(qkv, "sb_core")
        out = mm_nn(o, w_out, f32, "sb_out")

        def bwd(d_out, mid):
            d_o = mm_nt(d_out, w_out, f32, "sb_out_dx")
            scatter("sb_w_out", j, mm_tn(o, d_out, D, bf16, "sb_out_dw").reshape(N_DEV, -1, D))
            dq, dk, dv = sb_bwd(qkv, tot, d_o, "sb_core_bwd")
            dq = mid(dq)
            dqkv = jnp.concatenate([dq, dk.astype(bf16), dv.astype(bf16)], axis=1)
            scatter("sb_w_qkv", j, mm_tn(h, dqkv, w_qkv.shape[2], bf16, "sb_qkv_dw"))
            return mm_nt(dqkv, w_qkv, f32, "sb_qkv_dx")

        return out, bwd

    def cross(h, i):
        gain = ln_mem[i:i + 1]
        mem_n = row_fwd(fn_prenorm, [mem], [gain], [(D, bf16)], "mem_norm")[0]
        w_q = gather(p["xa_wq"][i], "ag_xa_q").reshape(1, D, -1)
        w_kv = gather(p["xa_wkv"][i], "ag_xa_kv").reshape(1, D, -1)
        w_o = gather(p["xa_wo"][i], "ag_xa_o")
        q = mm_nn(h, w_q, bf16, "xa_q")
        kv = mm_nn(mem_n, w_kv, bf16, "xa_kv")
        o = xa_fwd(q, kv, "xa_core")
        out = mm_nn(o, w_o, f32, "xa_out")

        def bwd(d_out, mid):
            d_o = mm_nt(d_out, w_o, f32, "xa_out_dx")
            scatter("xa_wo", i, mm_tn(o, d_out, w_o.shape[2], bf16, "xa_out_dw"))
            dq, dk, dv = xa_bwd(q, kv, d_o, "xa_core_bwd")
            dkv = jnp.concatenate([dk, dv], axis=1).astype(bf16)
            scatter("xa_wq", i, mm_tn(h, dq, w_q.shape[2], bf16, "xa_q_dw").reshape(N_DEV, -1, w_q.shape[2]))
            scatter("xa_wkv", i, mm_tn(mem_n, dkv, w_kv.shape[2], bf16, "xa_kv_dw").reshape(N_DEV, -1, w_kv.shape[2]))
            d_mem_n = mm_nt(dkv, w_kv, f32, "xa_kv_dx")
            grads["ln_mem"][i] = row_bwd(fn_prenorm, [mem], [gain], [d_mem_n], [None], "mem_norm_bwd")[0][0]
            return mm_nt(dq, w_q, f32, "xa_q_dx")

        return out, bwd

    def ffn(h, i):
        w_in = gather(p["ffn_w_in"][i], "ag_ffn_in")
        w_out = gather(p["ffn_w_out"][i], "ag_ffn_out").reshape(1, -1, D)
        width = w_out.shape[1]
        conv_w, conv_b = full["ffn_conv_w"][i], ffn_conv_b[i:i + 1]
        gu = mm_nn(h, w_in, f32, "ffn_in")
        streams = [(gu, 0, conv_w, conv_b, 0), (gu, width // LANES, conv_w, conv_b, width // LANES)]
        act = conv_fwd(streams, epi_gelu_gate, width, bf16, 3, "ffn_gate")
        out = mm_nn(act, w_out, f32, "ffn_out")

        def bwd(d_out, mid):
            d_act = mm_nt(d_out, w_out, f32, "ffn_out_dx")
            scatter("ffn_w_out", i, mm_tn(act, d_out, D, bf16, "ffn_out_dw").reshape(N_DEV, -1, D))
            d_pre, d_cw, d_cb = conv_bwd(streams, epi_gelu_gate, d_act, 3, "ffn_gate_bwd")
            dgu = jnp.concatenate(d_pre, axis=1)
            grads["ffn_conv_w"][i], grads["ffn_conv_b"][i] = jnp.concatenate(d_cw, axis=1), jnp.concatenate(d_cb, axis=1)[0]
            scatter("ffn_w_in", i, mm_tn(h, dgu, w_in.shape[2], bf16, "ffn_in_dw"))
            return mm_nt(dgu, w_in, f32, "ffn_in_dx")

        return out, bwd

    n_sub = 3 * depth
    pre = [w[i:i + 1] for i in range(depth) for w in (ln_mix_pre, ln_xa_pre, ln_ffn_pre)]
    post = [w[i:i + 1] for i in range(depth) for w in (ln_mix_post, ln_xa_post, ln_ffn_post)]
    stream, outs, bwds = [x], [], []
    h = row_fwd(fn_prenorm, [x], [pre[0]], [(D, bf16)], "pre_norm")[0]
    for s in range(n_sub):
        i, t = divmod(s, 3)
        out, bwd = ((ssd, sgu, stick)[i % 3](h, i // 3) if t == 0 else cross(h, i) if t == 1 else ffn(h, i))
        outs.append(out)
        bwds.append(bwd)
        if s < n_sub - 1:
            x_new, h = row_fwd(fn_resnorm, [stream[s], out], [post[s], pre[s + 1]], [(D, f32), (D, bf16)], "res_norm")
            stream.append(x_new)
    dy, loss = row_fwd(fn_final, [stream[-1], outs[-1], target], [post[-1]], [(D, f32), (1, f32)], "loss_head", n_acc=1)
    loss = lax.psum(loss[0, 0], ("x", "y", "c"))

    d_pre, d_post = [None] * n_sub, [None] * n_sub
    dx, d_out, d_post[-1] = row_bwd(fn_res, [stream[-1], outs[-1]], [post[-1]], [dy], [f32, bf16], "res_bwd")
    for s in reversed(range(n_sub)):
        dh = ship_pending(bwds[s](collect_due(d_out, (s, 0)), functools.partial(collect_due, s=(s, 1))), s)
        if s > 0:
            dx, d_out, d_post[s - 1], d_pre[s] = row_bwd(
                fn_resnorm, [stream[s - 1], outs[s - 1]], [post[s - 1], pre[s]], [dx, dh], [f32, bf16], "res_norm_bwd")
        else:
            grad_x, d_pre[0] = row_bwd(fn_xpre, [x], [pre[0]], [dx, dh], [f32], "pre_norm_bwd")
    for t, kind in enumerate(("mix", "xa", "ffn")):
        for i in range(depth):
            grads["ln_%s_pre" % kind][i] = d_pre[3 * i + t][0]
            grads["ln_%s_post" % kind][i] = d_post[3 * i + t][0]

    def stacked(name):
        return jnp.stack([grads[name][l] for l in range(len(grads[name]))], axis=0)

    new = {}
    for name in BIG:
        new[name] = adamw_sharded(p[name], [parts[name][l] for l in range(len(parts[name]))],
                                  p["m_" + name], p["v_" + name], "adamw_" + name)

    rep_shapes = [p[n].shape for n in REPLICATED]
    g8 = all_gather(_pack_rows([stacked(n) for n in REPLICATED]), "ag_grad_rep")
    rep = adamw_summed8(_pack_rows([p[n] for n in REPLICATED]), g8, _pack_rows([p["m_" + n] for n in REPLICATED]),
                        _pack_rows([p["v_" + n] for n in REPLICATED]), "adamw_rep")
    for k, packed in enumerate(rep):
        for n, a in zip(REPLICATED, _unpack_rows(packed, rep_shapes)):
            new.setdefault(n, [None] * 4)[k] = a

    sh_shapes = [p[n].shape for n in SMALL_SHARDED]
    by_owner = _pack_rows([_split_last(stacked(n)) for n in SMALL_SHARDED], lead=1)
    mine8 = lax.dynamic_index_in_dim(all_gather(by_owner, "ag_grad_small"), me, axis=1, keepdims=False)
    sh = adamw_summed8(_pack_rows([p[n] for n in SMALL_SHARDED]), mine8, _pack_rows([p["m_" + n] for n in SMALL_SHARDED]),
                       _pack_rows([p["v_" + n] for n in SMALL_SHARDED]), "adamw_small")
    for k, packed in enumerate(sh):
        for n, a in zip(SMALL_SHARDED, _unpack_rows(packed, sh_shapes)):
            new.setdefault(n, [None] * 4)[k] = a

    return (loss, grad_x[None], *[new[n][0] for n in WEIGHTS], *[new[n][1] for n in WEIGHTS],
            *[new[n][2] for n in WEIGHTS], *[new[n][3] for n in WEIGHTS])
```

```python
import functools

import jax
import jax.numpy as jnp
import numpy as np
from jax import lax
from jax.experimental import pallas as pl
from jax.experimental.pallas import tpu as pltpu
from jax.experimental.pallas import tpu_sc as plsc

f32 = jnp.float32
bf16 = jnp.bfloat16
HIGHEST = lax.Precision.HIGHEST
MESH = pl.DeviceIdType.MESH

V7X_VMEM_BYTES = 64 * 1024 * 1024
VMEM_LIMIT = V7X_VMEM_BYTES * 3 // 4
LANES = 128
SUBLANES = 8
BF16_ROWS = 16

EPS = 1e-6
ADAM_LR = 0.001
ADAM_B1 = 0.9
ADAM_B2 = 0.999
ADAM_EPS = 1e-08
ADAM_WD = 0.01
ADAM_STEP = 10

N_DEV = 8
XA_HEADS = 4
SSD_HEADS_PER_GROUP = 8
SSD_HEAD_DIM = 64
SSD_STATE = 128
SSD_GROUPS = 8
CHUNK = 128


def _cp(*sem):
    return pltpu.CompilerParams(dimension_semantics=sem or None, vmem_limit_bytes=VMEM_LIMIT)


def _dot(a, b, dims, precision=None):
    return lax.dot_general(a, b, (dims, ((), ())), precision=precision, preferred_element_type=f32)


def _nn(a, b, precision=None):
    return _dot(a, b, ((1,), (0,)), precision)


def _nt(a, b, precision=None):
    return _dot(a, b, ((1,), (1,)), precision)


def _tn(a, b, precision=None):
    return _dot(a, b, ((0,), (0,)), precision)


def _largest_tile(n, cap, step):
    for t in range(min(n, cap) // step * step, 0, -step):
        if n % t == 0:
            return t
    return n


def _gather_partners():
    mx, my, mc = lax.axis_index("x"), lax.axis_index("y"), lax.axis_index("c")
    chips = [(1 - mx, my), (mx, 1 - my), (1 - mx, 1 - my)]
    return (mx, my, mc), (mx, my, 1 - mc), chips


def _two_level_gather(x_ref, out_ref, send_sems, recv_sems, local_sem):
    me, sibling, chips = _gather_partners()
    mc = me[2]

    def slot(px, py, pc):
        return out_ref.at[4 * px + 2 * py + pc]

    def copy(k, block, to, src=None):
        return pltpu.make_async_remote_copy(
            src_ref=slot(*block) if src is None else src, dst_ref=slot(*block),
            send_sem=send_sems.at[k], recv_sem=recv_sems.at[k], device_id=to, device_id_type=MESH)

    mine = pltpu.make_async_copy(x_ref, slot(*me), local_sem)
    mine.start()
    first = [copy(0, me, sibling, src=x_ref)]
    first += [copy(1 + j, me, (*chip, mc), src=x_ref) for j, chip in enumerate(chips)]
    for cp in first:
        cp.start()
    passed = [copy(4 + j, (*chip, mc), sibling) for j, chip in enumerate(chips)]
    for j, chip in enumerate(chips):
        copy(1 + j, (*chip, mc), me).wait_recv()
        passed[j].start()
    copy(0, sibling, me).wait_recv()
    for j, chip in enumerate(chips):
        copy(4 + j, (*chip, 1 - mc), me).wait_recv()
    for cp in first + passed:
        cp.wait_send()
    mine.wait()


def all_gather(x, name):
    return pl.pallas_call(
        functools.partial(_two_level_gather), name=name,
        out_shape=jax.ShapeDtypeStruct((N_DEV,) + x.shape, x.dtype),
        in_specs=[pl.BlockSpec(memory_space=pl.ANY)],
        out_specs=pl.BlockSpec(memory_space=pl.ANY),
        scratch_shapes=[pltpu.SemaphoreType.DMA((7,)), pltpu.SemaphoreType.DMA((7,)), pltpu.SemaphoreType.DMA(())],
    )(x)


AG_COLLECTIVE_ID = 1
RS_COLLECTIVE_ID = 2
_SEQUENCER = dict(axis_name="sequencer", num_cores=1)
_HBM = pltpu.MemorySpace.HBM


def _peers():
    mx, my, mc = lax.axis_index("x"), lax.axis_index("y"), lax.axis_index("c")
    return [((mx + (k >> 2)) % 2, (my + ((k >> 1) & 1)) % 2, (mc + (k & 1)) % 2) for k in range(1, N_DEV)]


def _handshake_all(peers):
    barrier = pltpu.get_barrier_semaphore()
    for peer in peers:
        pl.semaphore_signal(barrier, inc=1, device_id=peer, device_id_type=MESH)
    pl.semaphore_wait(barrier, len(peers))


def _block_of(peer):
    return 4 * peer[0] + 2 * peer[1] + peer[2]


def sc_all_gather(x, name):
    x_ref = jax.new_ref(x, memory_space=_HBM)
    out_ref = jax.empty_ref(jax.ShapeDtypeStruct((N_DEV,) + x.shape, x.dtype), memory_space=_HBM)

    @pl.kernel(mesh=plsc.ScalarSubcoreMesh(**_SEQUENCER), name=name,
               scratch_types=(pltpu.SemaphoreType.DMA((N_DEV - 1,)), pltpu.SemaphoreType.DMA((N_DEV - 1,)),
                              pltpu.SemaphoreType.DMA(())),
               compiler_params=pltpu.CompilerParams(collective_id=AG_COLLECTIVE_ID))
    def launch(send_sems, recv_sems, local_sem):
        _, sibling, chips = _gather_partners()
        _handshake_all([sibling] + [(*chip, lax.axis_index("c")) for chip in chips])
        _two_level_gather(x_ref, out_ref, send_sems, recv_sems, local_sem)

    launch()
    return out_ref[...]


def sc_scatter_parts(g8, name):
    g_ref = jax.new_ref(g8, memory_space=_HBM)
    parts_ref = jax.empty_ref(jax.ShapeDtypeStruct(g8.shape, g8.dtype), memory_space=_HBM)

    @pl.kernel(mesh=plsc.ScalarSubcoreMesh(**_SEQUENCER), name=name,
               scratch_types=(pltpu.SemaphoreType.DMA((N_DEV - 1,)), pltpu.SemaphoreType.DMA((N_DEV - 1,)),
                              pltpu.SemaphoreType.DMA(())),
               compiler_params=pltpu.CompilerParams(collective_id=RS_COLLECTIVE_ID))
    def launch(send_sems, recv_sems, local_sem):
        peers = _peers()
        _handshake_all(peers)
        me = 4 * lax.axis_index("x") + 2 * lax.axis_index("y") + lax.axis_index("c")
        mine = pltpu.make_async_copy(g_ref.at[me], parts_ref.at[0], local_sem)
        mine.start()
        copies = [pltpu.make_async_remote_copy(src_ref=g_ref.at[_block_of(peer)], dst_ref=parts_ref.at[k + 1],
                                               send_sem=send_sems.at[k], recv_sem=recv_sems.at[k],
                                               device_id=peer, device_id_type=MESH) for k, peer in enumerate(peers)]
        for cp in copies:
            cp.start()
        for cp in copies:
            cp.wait()
        mine.wait()

    launch()
    return parts_ref[...]


def _as_lrc(a, lead):
    rest = a.shape[lead:]
    return a.reshape(a.shape[:lead] + (int(np.prod(rest[:-2], dtype=np.int64)),) + rest[-2:])


def _adam_math(w, g, m, v):
    m = ADAM_B1 * m + (1.0 - ADAM_B1) * g
    v = ADAM_B2 * v + (1.0 - ADAM_B2) * jnp.square(g)
    m_hat = m / (1.0 - ADAM_B1 ** ADAM_STEP)
    v_hat = v / (1.0 - ADAM_B2 ** ADAM_STEP)
    delta = -ADAM_LR * (m_hat / (jnp.sqrt(v_hat) + ADAM_EPS) + ADAM_WD * w)
    return delta, m, v


def adamw_sharded(w, parts, m, v, name):
    L, R, C = w.shape
    row_bytes = 2 * C * (L * N_DEV * 2 + 7 * 4)
    tr = next((t for t in _tiles_desc(R, 128, BF16_ROWS) if t * row_bytes <= MM_VMEM_BUDGET), BF16_ROWS)

    def body(w_ref, *refs):
        p_refs, (m_ref, v_ref, g_out, d_out, m_out, v_out) = refs[:L], refs[L:]
        for layer in range(L):
            @pl.when(pl.program_id(0) == layer)
            def _(p_ref=p_refs[layer]):
                g = p_ref[0].astype(f32)
                for k in range(1, N_DEV):
                    g = g + p_ref[k].astype(f32)
                delta, mn, vn = _adam_math(w_ref[...], g, m_ref[...], v_ref[...])
                g_out[...] = g
                d_out[...] = delta
                m_out[...] = mn
                v_out[...] = vn

    blk = pl.BlockSpec((None, tr, C), lambda l, r: (l, r, 0))
    part_specs = [pl.BlockSpec((N_DEV, tr, C), functools.partial(lambda layer, l, r: (0, jnp.where(l == layer, r, 0), 0), layer))
                  for layer in range(L)]
    return pl.pallas_call(
        body, name=name, out_shape=[jax.ShapeDtypeStruct(w.shape, f32)] * 4, grid=(L, R // tr),
        in_specs=[blk] + part_specs + [blk, blk], out_specs=[blk] * 4,
        compiler_params=_cp("arbitrary", "arbitrary"),
    )(w, *parts, m, v)


def adamw_summed8(w, g8, m, v, name):
    R, C = w.shape
    tr = _largest_tile(R, 512, SUBLANES)

    def body(w_ref, g_ref, m_ref, v_ref, g_out, d_out, m_out, v_out):
        g = g_ref[0]
        for d in range(1, N_DEV):
            g = g + g_ref[d]
        delta, mn, vn = _adam_math(w_ref[...], g, m_ref[...], v_ref[...])
        g_out[...] = g
        d_out[...] = delta
        m_out[...] = mn
        v_out[...] = vn

    blk = pl.BlockSpec((tr, C), lambda r: (r, 0))
    return pl.pallas_call(
        body, name=name, out_shape=[jax.ShapeDtypeStruct((R, C), f32)] * 4, grid=(R // tr,),
        in_specs=[blk, pl.BlockSpec((N_DEV, tr, C), lambda r: (0, r, 0)), blk, blk], out_specs=[blk] * 4,
        compiler_params=_cp("parallel"),
    )(w, g8, m, v)


def _tile_n(ns):
    if ns % 512 == 0:
        return 512
    if ns <= 1536:
        return ns
    return _largest_tile(ns, 512, LANES)


MM_VMEM_BUDGET = VMEM_LIMIT - 8 * 1024 * 1024


def _tiles_desc(n, cap, step):
    return [t for t in range(min(n, cap) // step * step, 0, -step) if n % t == 0] or [n]


def _mm_fits(in_tiles, out_tile, out_dtype, n_red, extra=0):
    rows, cols = out_tile
    total = sum(2 * 2 * r * c for r, c in in_tiles) + 2 * rows * cols * jnp.dtype(out_dtype).itemsize + extra
    if n_red > 1:
        total += 4 * rows * cols
    return total <= MM_VMEM_BUDGET


def _reduce_into(o_ref, acc, part, first, last):
    if acc is None:
        o_ref[...] = part().astype(o_ref.dtype)
        return

    @pl.when(first)
    def _():
        acc[...] = jnp.zeros_like(acc)

    acc[...] += part()

    @pl.when(last)
    def _():
        o_ref[...] = acc[...].astype(o_ref.dtype)


def mm_nn(a, w3, out_dtype, name):
    M, K = a.shape
    J, _, Ns = w3.shape
    tn = _tile_n(Ns)
    tm, tk = next(((tm, tk) for tk in _tiles_desc(K, 4096, LANES) for tm in _tiles_desc(M, 1024, BF16_ROWS)
                   if tm >= min(M, 256) and _mm_fits([(tm, tk), (tk, tn)], (tm, tn), out_dtype, K // tk)),
                  (min(M, 256), _largest_tile(K, 512, LANES)))
    nn, nk = Ns // tn, K // tk

    def body(a_ref, w_ref, o_ref, *acc):
        k = pl.program_id(3)
        _reduce_into(o_ref, acc[0] if acc else None, lambda: _nn(a_ref[...], w_ref[...]), k == 0, k == nk - 1)

    return pl.pallas_call(
        body, name=name, out_shape=jax.ShapeDtypeStruct((M, J * Ns), out_dtype),
        grid=(M // tm, J, nn, nk),
        in_specs=[pl.BlockSpec((tm, tk), lambda i, j, n, k: (i, k)),
                  pl.BlockSpec((None, tk, tn), lambda i, j, n, k: (j, k, n))],
        out_specs=pl.BlockSpec((tm, tn), lambda i, j, n, k: (i, j * nn + n)),
        scratch_shapes=[pltpu.VMEM((tm, tn), f32)] if nk > 1 else [],
        compiler_params=_cp("parallel", "parallel", "parallel", "arbitrary"),
    )(a, w3)


MM_NT_REDUCE_CAP = 5632


def mm_nt(g, w3, out_dtype, name, init=None):
    M = g.shape[0]
    J, K, Ns = w3.shape
    if Ns <= MM_NT_REDUCE_CAP:
        tn, jb = Ns, max(b for b in range(1, J + 1) if J % b == 0 and (b == 1 or b * Ns <= MM_NT_REDUCE_CAP))
    else:
        tn, jb = _largest_tile(Ns, MM_NT_REDUCE_CAP, LANES), 1
    nj, nn = J // jb, Ns // tn
    n_red = nj * nn + (init is not None)
    tk = _largest_tile(K, 512, LANES)
    tm = next((tm for tm in _tiles_desc(M, 1024, BF16_ROWS)
               if _mm_fits([(tm, jb * tn), (jb * tk, tn)], (tm, tk), out_dtype, n_red,
                           extra=0 if init is None else 2 * 4 * tm * tk)), min(M, 256))

    def body(*refs):
        g_ref, w_ref = refs[:2]
        i_ref = None if init is None else refs[2]
        o_ref = refs[2 + (init is not None)]
        acc = refs[3 + (init is not None):]
        j, n = pl.program_id(2), pl.program_id(3)
        first = (j == 0) & (n == 0)

        def part():
            prod = _nt(g_ref[:, :tn], w_ref[0])
            for b in range(1, jb):
                prod = prod + _nt(g_ref[:, b * tn:(b + 1) * tn], w_ref[b])
            return prod if init is None else prod + jnp.where(first, i_ref[...].astype(f32), 0.0)

        _reduce_into(o_ref, acc[0] if acc else None, part, first, (j == nj - 1) & (n == nn - 1))

    in_specs = [pl.BlockSpec((tm, jb * tn), lambda i, k, j, n: (i, j * nn + n)),
                pl.BlockSpec((jb, tk, tn), lambda i, k, j, n: (j, k, n))]
    args = [g, w3]
    if init is not None:
        in_specs.append(pl.BlockSpec((tm, tk), lambda i, k, j, n: (i, k)))
        args.append(init)
    return pl.pallas_call(
        body, name=name, out_shape=jax.ShapeDtypeStruct((M, K), out_dtype),
        grid=(M // tm, K // tk, nj, nn), in_specs=in_specs,
        out_specs=pl.BlockSpec((tm, tk), lambda i, k, j, n: (i, k)),
        scratch_shapes=[pltpu.VMEM((tm, tk), f32)] if n_red > 1 else [],
        compiler_params=_cp("parallel", "parallel", "arbitrary", "arbitrary"),
    )(*args)


def mm_tn(a, g, ns, out_dtype, name):
    M, K = a.shape
    J = g.shape[1] // ns
    tn, tk = _tile_n(ns), _largest_tile(K, 512, LANES)
    tm = next((tm for tm in _tiles_desc(M, 4096, BF16_ROWS)
               if _mm_fits([(tm, tk), (tm, tn)], (tk, tn), out_dtype, M // tm)), _largest_tile(M, 512, BF16_ROWS))
    nn, nm = ns // tn, M // tm

    def body(a_ref, g_ref, o_ref, *acc):
        m = pl.program_id(3)
        _reduce_into(o_ref, acc[0] if acc else None, lambda: _tn(a_ref[...], g_ref[...]), m == 0, m == nm - 1)

    return pl.pallas_call(
        body, name=name, out_shape=jax.ShapeDtypeStruct((J, K, ns), out_dtype),
        grid=(J, K // tk, nn, nm),
        in_specs=[pl.BlockSpec((tm, tk), lambda j, k, n, m: (m, k)),
                  pl.BlockSpec((tm, tn), lambda j, k, n, m: (m, j * nn + n))],
        out_specs=pl.BlockSpec((None, tk, tn), lambda j, k, n, m: (j, k, n)),
        scratch_shapes=[pltpu.VMEM((tk, tn), f32)] if nm > 1 else [],
        compiler_params=_cp("parallel", "parallel", "parallel", "arbitrary"),
    )(a, g)


ROW_TILE = 256
ROW_STEP = 64


def row_fwd(fn, rows, consts, outs, name, n_acc=0):
    T = rows[0].shape[0]
    tr = min(T, ROW_TILE)
    n_rows, n_consts, n_row_out = len(rows), len(consts), len(outs) - n_acc

    def body(*refs):
        r_refs, c_refs, o_refs = refs[:n_rows], refs[n_rows:n_rows + n_consts], refs[n_rows + n_consts:]
        cs = [c[...] for c in c_refs]
        acc_refs = o_refs[n_row_out:]
        if n_acc:
            @pl.when(pl.program_id(0) == 0)
            def _():
                for a in acc_refs:
                    a[...] = jnp.zeros_like(a)

        def step(s, carry):
            rows_s = pl.ds(pl.multiple_of(s * ROW_STEP, ROW_STEP), ROW_STEP)
            res = fn(*[r[rows_s, :].astype(f32) for r in r_refs], *cs)
            for o, val in zip(o_refs[:n_row_out], res[:n_row_out]):
                o[rows_s, :] = val.astype(o.dtype)
            return tuple(c + val for c, val in zip(carry, res[n_row_out:]))

        accs = lax.fori_loop(0, tr // ROW_STEP, step, tuple(jnp.zeros((1, c), f32) for c, _ in outs[n_row_out:]))
        for a, val in zip(acc_refs, accs):
            a[...] += val

    in_specs = [pl.BlockSpec((tr, r.shape[1]), lambda i: (i, 0)) for r in rows]
    in_specs += [pl.BlockSpec(c.shape, lambda i: (0, 0)) for c in consts]
    out_shape = [jax.ShapeDtypeStruct((T, c), dt) for c, dt in outs[:n_row_out]]
    out_shape += [jax.ShapeDtypeStruct((1, c), f32) for c, _ in outs[n_row_out:]]
    out_specs = [pl.BlockSpec((tr, c), lambda i: (i, 0)) for c, _ in outs[:n_row_out]]
    out_specs += [pl.BlockSpec((1, c), lambda i: (0, 0)) for c, _ in outs[n_row_out:]]
    return pl.pallas_call(
        body, name=name, out_shape=out_shape, grid=(T // tr,), in_specs=in_specs, out_specs=out_specs,
        compiler_params=_cp("arbitrary" if n_acc else "parallel"),
    )(*rows, *consts)


def row_bwd(fn, rows, consts, cts, grad_dtypes, name):
    T = rows[0].shape[0]
    tr = min(T, ROW_TILE)
    n_rows, n_consts, n_cts = len(rows), len(consts), len(cts)
    wanted = [i for i, dt in enumerate(grad_dtypes) if dt is not None]

    def body(*refs):
        r_refs = refs[:n_rows]
        c_refs = refs[n_rows:n_rows + n_consts]
        t_refs = refs[n_rows + n_consts:n_rows + n_consts + n_cts]
        o_refs = refs[n_rows + n_consts + n_cts:]
        gr_refs, gc_refs = o_refs[:len(wanted)], o_refs[len(wanted):]
        cs = [c[...] for c in c_refs]

        @pl.when(pl.program_id(0) == 0)
        def _():
            for a in gc_refs:
                a[...] = jnp.zeros_like(a)

        def step(s, carry):
            rows_s = pl.ds(pl.multiple_of(s * ROW_STEP, ROW_STEP), ROW_STEP)
            ins = [r[rows_s, :].astype(f32) for r in r_refs]
            _, vjp = jax.vjp(lambda *a: tuple(fn(*a)), *ins, *cs)
            grads = vjp(tuple(t[rows_s, :].astype(f32) for t in t_refs))
            for o, i in zip(gr_refs, wanted):
                o[rows_s, :] = grads[i].astype(o.dtype)
            return tuple(c + gval for c, gval in zip(carry, grads[n_rows:]))

        accs = lax.fori_loop(0, tr // ROW_STEP, step, tuple(jnp.zeros(c.shape, f32) for c in consts))
        for a, val in zip(gc_refs, accs):
            a[...] += val

    in_specs = [pl.BlockSpec((tr, r.shape[1]), lambda i: (i, 0)) for r in list(rows) + list(cts)]
    in_specs[n_rows:n_rows] = [pl.BlockSpec(c.shape, lambda i: (0, 0)) for c in consts]
    out_shape = [jax.ShapeDtypeStruct(rows[i].shape, grad_dtypes[i]) for i in wanted]
    out_shape += [jax.ShapeDtypeStruct(c.shape, f32) for c in consts]
    out_specs = [pl.BlockSpec((tr, rows[i].shape[1]), lambda i_: (i_, 0)) for i in wanted]
    out_specs += [pl.BlockSpec(c.shape, lambda i: (0, 0)) for c in consts]
    return pl.pallas_call(
        body, name=name, out_shape=out_shape, grid=(T // tr,), in_specs=in_specs, out_specs=out_specs,
        compiler_params=_cp("arbitrary"),
    )(*rows, *consts, *cts)


def _rms(x, g):
    return x * lax.rsqrt(jnp.mean(x * x, axis=-1, keepdims=True) + EPS) * g


def fn_prenorm(x, g):
    return (_rms(x, g),)


def fn_resnorm(x, m, g_post, g_pre):
    x_new = x + _rms(m, g_post)
    return x_new, _rms(x_new, g_pre)


def fn_res(x, m, g_post):
    return (x + _rms(m, g_post),)


def fn_final(x, m, target, g_post):
    err = x + _rms(m, g_post) - target
    n = err.shape[-1]
    return err / n, (0.5 / n) * jnp.sum(jnp.sum(err * err, axis=1, keepdims=True), axis=0, keepdims=True)


def fn_gate_norm(y, z, g):
    return (_rms(y * jax.nn.silu(z), g),)


CONV_ROWS = 256


def _conv_chunk(pre_ref, w, b, r, rb, K):
    t0 = pl.multiple_of(r * rb, rb)
    halo_at = pl.multiple_of(jnp.maximum(t0 - SUBLANES, 0), SUBLANES)
    halo = jnp.where(r > 0, pre_ref[pl.ds(halo_at, SUBLANES), :], 0.0)
    main = pre_ref[pl.ds(t0, rb), :]
    ext = jnp.concatenate([halo, main], axis=0)
    shifted = [main if k == K - 1 else pltpu.roll(ext, K - 1 - k, 0)[SUBLANES:] for k in range(K)]
    conv = b
    for k in range(K):
        conv = conv + w[k:k + 1] * shifted[k]
    return conv, shifted


def conv_fwd(streams, epilogue, out_cols, out_dtype, K, name):
    T = streams[0][0].shape[0]
    rb = min(T, CONV_ROWS)
    S = len(streams)

    def body(*refs):
        pre_refs, w_refs, b_refs, o_ref = refs[:S], refs[S:2 * S], refs[2 * S:3 * S], refs[3 * S]
        ws = [w[...] for w in w_refs]
        bs = [b[...] for b in b_refs]

        def step(r, carry):
            convs = [_conv_chunk(pre_refs[s], ws[s], bs[s], r, rb, K)[0] for s in range(S)]
            o_ref[pl.ds(pl.multiple_of(r * rb, rb), rb), :] = epilogue(*convs).astype(o_ref.dtype)
            return carry

        lax.fori_loop(0, T // rb, step, 0)

    in_specs = [pl.BlockSpec((T, LANES), functools.partial(lambda off, i: (0, off + i), st[1])) for st in streams]
    in_specs += [pl.BlockSpec((K, LANES), functools.partial(lambda off, i: (0, off + i), st[4])) for st in streams]
    in_specs += [pl.BlockSpec((1, LANES), functools.partial(lambda off, i: (0, off + i), st[4])) for st in streams]
    return pl.pallas_call(
        body, name=name, out_shape=jax.ShapeDtypeStruct((T, out_cols), out_dtype), grid=(out_cols // LANES,),
        in_specs=in_specs, out_specs=pl.BlockSpec((T, LANES), lambda i: (0, i)),
        compiler_params=_cp("parallel"),
    )(*[st[0] for st in streams], *[st[2] for st in streams], *[st[3] for st in streams])


def conv_bwd(streams, epilogue, dout, K, name):
    pieces = list(dout) if isinstance(dout, (list, tuple)) else [dout]
    T = pieces[0].shape[0]
    ends = [int(e) for e in np.cumsum([d.shape[1] // LANES for d in pieces])]
    cols = int(ends[-1]) * LANES
    rb = min(T, CONV_ROWS)
    S, n_pieces = len(streams), len(pieces)

    def body(*refs):
        pre_refs, w_refs, b_refs = refs[:S], refs[S:2 * S], refs[2 * S:3 * S]
        dout_refs = refs[3 * S:3 * S + n_pieces]
        o = refs[3 * S + n_pieces:]

        def dout_chunk(rows):
            d = dout_refs[-1][rows, :].astype(f32)
            for k in reversed(range(n_pieces - 1)):
                d = jnp.where(pl.program_id(0) < ends[k], dout_refs[k][rows, :].astype(f32), d)
            return d

        dpre_refs, dw_refs, db_refs, scr = o[:S], o[S:2 * S], o[2 * S:3 * S], o[3 * S:]
        ws = [w[...] for w in w_refs]
        bs = [b[...] for b in b_refs]
        for s in range(S):
            scr[s][pl.ds(T, SUBLANES), :] = jnp.zeros((SUBLANES, LANES), f32)

        def phase1(r, carry):
            rows = pl.ds(pl.multiple_of(r * rb, rb), rb)
            convs, shifted = zip(*[_conv_chunk(pre_refs[s], ws[s], bs[s], r, rb, K) for s in range(S)])
            _, vjp = jax.vjp(epilogue, *convs)
            dconvs = vjp(dout_chunk(rows))
            new = []
            for s in range(S):
                scr[s][rows, :] = dconvs[s]
                sums = [jnp.sum(dconvs[s] * shifted[s][k], axis=0, keepdims=True) for k in range(K)]
                sums.append(jnp.sum(dconvs[s], axis=0, keepdims=True))
                new.append(tuple(c + v for c, v in zip(carry[s], sums)))
            return tuple(new)

        zero = tuple(tuple(jnp.zeros((1, LANES), f32) for _ in range(K + 1)) for _ in range(S))
        sums = lax.fori_loop(0, T // rb, phase1, zero)
        tap = lax.broadcasted_iota(jnp.int32, (K, LANES), 0)
        for s in range(S):
            dw = jnp.zeros((K, LANES), f32)
            for k in range(K):
                dw = jnp.where(tap == k, sums[s][k], dw)
            dw_refs[s][...] = dw
            db_refs[s][...] = sums[s][K]

        def phase2(r, carry):
            t0 = pl.multiple_of(r * rb, rb)
            for s in range(S):
                ext = scr[s][pl.ds(t0, rb + SUBLANES), :]
                dpre = ws[s][K - 1:K] * ext[:rb]
                for k in range(K - 1):
                    j = K - 1 - k
                    dpre = dpre + ws[s][k:k + 1] * pltpu.roll(ext, rb + SUBLANES - j, 0)[:rb]
                dpre_refs[s][pl.ds(t0, rb), :] = dpre.astype(dpre_refs[s].dtype)
            return carry

        lax.fori_loop(0, T // rb, phase2, 0)

    in_specs = [pl.BlockSpec((T, LANES), functools.partial(lambda off, i: (0, off + i), st[1])) for st in streams]
    in_specs += [pl.BlockSpec((K, LANES), functools.partial(lambda off, i: (0, off + i), st[4])) for st in streams]
    in_specs += [pl.BlockSpec((1, LANES), functools.partial(lambda off, i: (0, off + i), st[4])) for st in streams]
    starts = [0] + [int(e) for e in ends[:-1]]
    in_specs += [pl.BlockSpec((T, LANES), functools.partial(lambda lo, hi, i: (0, jnp.clip(i, lo, hi - 1) - lo), lo, int(hi)))
                 for lo, hi in zip(starts, ends)]
    out_shape = [jax.ShapeDtypeStruct((T, cols), bf16)] * S
    out_shape += [jax.ShapeDtypeStruct((K, cols), f32)] * S + [jax.ShapeDtypeStruct((1, cols), f32)] * S
    out_specs = [pl.BlockSpec((T, LANES), lambda i: (0, i))] * S
    out_specs += [pl.BlockSpec((K, LANES), lambda i: (0, i))] * S + [pl.BlockSpec((1, LANES), lambda i: (0, i))] * S
    res = pl.pallas_call(
        body, name=name, out_shape=out_shape, grid=(cols // LANES,), in_specs=in_specs, out_specs=out_specs,
        scratch_shapes=[pltpu.VMEM((T + SUBLANES, LANES), f32)] * S,
        compiler_params=_cp("parallel"),
    )(*[st[0] for st in streams], *[st[2] for st in streams], *[st[3] for st in streams], *pieces)
    return res[:S], res[S:2 * S], res[2 * S:]


def epi_gelu_gate(cg, cu):
    return jax.nn.gelu(cg, approximate=True) * cu


def epi_silu(c):
    return jax.nn.silu(c)


XA_ROWS = 256


def _xa_fn(q, k, v):
    s = _nt(q.astype(bf16), k.astype(bf16)) * (q.shape[-1] ** -0.5)
    p = jax.nn.softmax(s, axis=-1)
    return _nn(p.astype(bf16), v.astype(bf16))


def xa_fwd(q, kv, name):
    T, W = q.shape
    M = kv.shape[0]
    H = W // LANES
    tr = min(T, XA_ROWS)

    def body(q_ref, k_ref, v_ref, o_ref):
        o_ref[...] = _xa_fn(q_ref[...].astype(f32), k_ref[...].astype(f32), v_ref[...].astype(f32)).astype(o_ref.dtype)

    return pl.pallas_call(
        body, name=name, out_shape=jax.ShapeDtypeStruct((T, W), bf16), grid=(T // tr, H),
        in_specs=[pl.BlockSpec((tr, LANES), lambda i, h: (i, h)),
                  pl.BlockSpec((M, LANES), lambda i, h: (0, h)),
                  pl.BlockSpec((M, LANES), lambda i, h: (0, H + h))],
        out_specs=pl.BlockSpec((tr, LANES), lambda i, h: (i, h)),
        compiler_params=_cp("parallel", "parallel"),
    )(q, kv, kv)


def xa_bwd(q, kv, do, name):
    T, W = q.shape
    M = kv.shape[0]
    H = W // LANES
    tr = min(T, XA_ROWS)

    def body(q_ref, k_ref, v_ref, do_ref, dq_ref, dk_ref, dv_ref):
        @pl.when(pl.program_id(1) == 0)
        def _():
            dk_ref[...] = jnp.zeros_like(dk_ref)
            dv_ref[...] = jnp.zeros_like(dv_ref)

        _, vjp = jax.vjp(_xa_fn, q_ref[...].astype(f32), k_ref[...].astype(f32), v_ref[...].astype(f32))
        dq, dk, dv = vjp(do_ref[...].astype(f32))
        dq_ref[...] = dq.astype(dq_ref.dtype)
        dk_ref[...] += dk
        dv_ref[...] += dv

    return pl.pallas_call(
        body, name=name,
        out_shape=[jax.ShapeDtypeStruct((T, W), bf16), jax.ShapeDtypeStruct((M, W), f32), jax.ShapeDtypeStruct((M, W), f32)],
        grid=(H, T // tr),
        in_specs=[pl.BlockSpec((tr, LANES), lambda h, i: (i, h)),
                  pl.BlockSpec((M, LANES), lambda h, i: (0, h)),
                  pl.BlockSpec((M, LANES), lambda h, i: (0, H + h)),
                  pl.BlockSpec((tr, LANES), lambda h, i: (i, h))],
        out_specs=[pl.BlockSpec((tr, LANES), lambda h, i: (i, h)),
                   pl.BlockSpec((M, LANES), lambda h, i: (0, h)),
                   pl.BlockSpec((M, LANES), lambda h, i: (0, h))],
        compiler_params=_cp("parallel", "arbitrary"),
    )(q, kv, kv, do)


def _sgu_norm_fn(v_pre, g, b):
    v = jax.nn.gelu(v_pre, approximate=True)
    mu = jnp.mean(v, axis=-1, keepdims=True)
    vc = v - mu
    return vc * lax.rsqrt(jnp.mean(vc * vc, axis=-1, keepdims=True) + EPS) * g + b


def _sgu_mix_fn(u_pre, vn, w, b):
    q = w.shape[0]
    tril = lax.broadcasted_iota(jnp.int32, (q, q), 0) >= lax.broadcasted_iota(jnp.int32, (q, q), 1)
    mixed = _nn(jnp.where(tril, w, 0.0).astype(bf16), vn.astype(bf16)) + b
    return jax.nn.gelu(u_pre, approximate=True) * mixed


def _sgu_norm_phase(v_ref, g, b, vn_ref):
    def step(s, carry):
        rows = pl.ds(pl.multiple_of(s * ROW_STEP, ROW_STEP), ROW_STEP)
        vn_ref[rows, :] = _sgu_norm_fn(v_ref[rows, :], g, b)
        return carry

    lax.fori_loop(0, CHUNK // ROW_STEP, step, 0)


def sgu_fwd(uv_pre, norm_g, norm_b, w_sp, b_sp, name):
    T, W2 = uv_pre.shape
    W = W2 // 2
    G = w_sp.shape[0]
    gw = W // G

    def body(u_ref, v_ref, g_ref, b_ref, ws_ref, bs_ref, o_ref, vn_ref):
        _sgu_norm_phase(v_ref, g_ref[...], b_ref[...], vn_ref)

        def group(gi, carry):
            cols = pl.ds(pl.multiple_of(gi * gw, LANES), gw)
            o_ref[:, cols] = _sgu_mix_fn(u_ref[:, cols], vn_ref[:, cols], ws_ref[gi], bs_ref[gi]).astype(o_ref.dtype)
            return carry

        lax.fori_loop(0, G, group, 0)

    full = lambda a: pl.BlockSpec(a.shape, lambda c: (0,) * a.ndim)
    return pl.pallas_call(
        body, name=name, out_shape=jax.ShapeDtypeStruct((T, W), bf16), grid=(T // CHUNK,),
        in_specs=[pl.BlockSpec((CHUNK, W), lambda c: (c, 0)), pl.BlockSpec((CHUNK, W), lambda c: (c, 1)),
                  full(norm_g), full(norm_b), full(w_sp), full(b_sp)],
        out_specs=pl.BlockSpec((CHUNK, W), lambda c: (c, 0)),
        scratch_shapes=[pltpu.VMEM((CHUNK, W), f32)],
        compiler_params=_cp("parallel"),
    )(uv_pre, uv_pre, norm_g, norm_b, w_sp, b_sp)


def sgu_bwd(uv_pre, norm_g, norm_b, w_sp, b_sp, dout, name):
    T, W2 = uv_pre.shape
    W = W2 // 2
    G = w_sp.shape[0]
    gw = W // G

    def body(u_ref, v_ref, g_ref, b_ref, ws_ref, bs_ref, do_ref, duv_ref, dg_ref, db_ref, dws_ref, dbs_ref,
             vn_ref, dvn_ref):
        @pl.when(pl.program_id(0) == 0)
        def _():
            for a in (dg_ref, db_ref, dws_ref, dbs_ref):
                a[...] = jnp.zeros_like(a)

        g, b = g_ref[...], b_ref[...]
        _sgu_norm_phase(v_ref, g, b, vn_ref)

        def group(gi, carry):
            cols = pl.ds(pl.multiple_of(gi * gw, LANES), gw)
            _, vjp = jax.vjp(_sgu_mix_fn, u_ref[:, cols], vn_ref[:, cols], ws_ref[gi], bs_ref[gi])
            du, dvn, dw, dbias = vjp(do_ref[:, cols])
            duv_ref[:, cols] = du.astype(duv_ref.dtype)
            dvn_ref[:, cols] = dvn
            dws_ref[gi] += dw
            dbs_ref[gi] += dbias
            return carry

        lax.fori_loop(0, G, group, 0)

        def step(s, carry):
            rows = pl.ds(pl.multiple_of(s * ROW_STEP, ROW_STEP), ROW_STEP)
            _, vjp = jax.vjp(_sgu_norm_fn, v_ref[rows, :], g, b)
            dv, dg, dbn = vjp(dvn_ref[rows, :])
            duv_ref[rows, pl.ds(W, W)] = dv.astype(duv_ref.dtype)
            return carry[0] + dg, carry[1] + dbn

        dg, dbn = lax.fori_loop(0, CHUNK // ROW_STEP, step, (jnp.zeros((1, W), f32), jnp.zeros((1, W), f32)))
        dg_ref[...] += dg
        db_ref[...] += dbn

    full = lambda a: pl.BlockSpec(a.shape, lambda c: (0,) * a.ndim)
    return pl.pallas_call(
        body, name=name,
        out_shape=[jax.ShapeDtypeStruct((T, W2), bf16), jax.ShapeDtypeStruct((1, W), f32), jax.ShapeDtypeStruct((1, W), f32),
                   jax.ShapeDtypeStruct(w_sp.shape, f32), jax.ShapeDtypeStruct(b_sp.shape, f32)],
        grid=(T // CHUNK,),
        in_specs=[pl.BlockSpec((CHUNK, W), lambda c: (c, 0)), pl.BlockSpec((CHUNK, W), lambda c: (c, 1)),
                  full(norm_g), full(norm_b), full(w_sp), full(b_sp), pl.BlockSpec((CHUNK, W), lambda c: (c, 0))],
        out_specs=[pl.BlockSpec((CHUNK, W2), lambda c: (c, 0)), full(norm_g), full(norm_b), full(w_sp), full(b_sp)],
        scratch_shapes=[pltpu.VMEM((CHUNK, W), f32), pltpu.VMEM((CHUNK, W), f32)],
        compiler_params=_cp("arbitrary"),
    )(uv_pre, uv_pre, norm_g, norm_b, w_sp, b_sp, dout)


SB_SUM_COLS = 256


def _sb_key_group(T):
    return 512 if T % 512 == 0 else T


def _sb_query_rows(T):
    return 256 if T % 256 == 0 else CHUNK


def _sb_block(q, k, i, g, diagonal):
    qb, kg = q.shape[0], k.shape[0]
    z = _nt(q, k) * (q.shape[-1] ** -0.5)
    sp = jnp.log(1.0 + jnp.exp(-jnp.abs(z)))
    log_beta = jnp.minimum(z, 0.0) - sp
    log_1mb = -jnp.maximum(z, 0.0) - sp
    if not diagonal:
        return z, None, log_beta, log_1mb
    t_idx = i * qb + lax.broadcasted_iota(jnp.int32, (qb, kg), 0)
    s_idx = g * kg + lax.broadcasted_iota(jnp.int32, (qb, kg), 1)
    valid = s_idx < t_idx
    return z, valid, log_beta, jnp.where(valid, log_1mb, 0.0)


def _sb_masked(valid, x):
    return x if valid is None else jnp.where(valid, x, 0.0)


def _order_matrix(later):
    r = lax.broadcasted_iota(jnp.int32, (SB_SUM_COLS, SB_SUM_COLS), 0)
    c = lax.broadcasted_iota(jnp.int32, (SB_SUM_COLS, SB_SUM_COLS), 1)
    return (r > c if later else r < c).astype(bf16)


def _masked_sums(parts, order):
    terms = []
    for x in parts:
        hi = x.astype(bf16)
        rest = x - hi.astype(f32)
        mid = rest.astype(bf16)
        terms += [hi, mid, (rest - mid.astype(f32)).astype(bf16)]
    rows = parts[0].shape[0]
    prod = _nn(jnp.concatenate(terms, axis=0), order)
    piece = lambda n: prod[n * rows:(n + 1) * rows]
    return [piece(3 * p) + piece(3 * p + 1) + piece(3 * p + 2) for p in range(len(parts))]


def sb_fwd(qkv, name):
    T = qkv.shape[0]
    H = qkv.shape[1] // (3 * LANES)
    qb, kg = _sb_query_rows(T), _sb_key_group(T)
    halves = kg // SB_SUM_COLS

    def body(q_ref, k_ref, v_ref, o_ref, tot_ref):
        i = pl.program_id(1)
        q = q_ref[...]
        later = _order_matrix(True)
        n_groups = (i * qb + qb + kg - 1) // kg

        def step(g, carry, diagonal):
            acc, run = carry
            rows = pl.ds(pl.multiple_of(g * kg, kg), kg)
            _, valid, log_beta, log_1mb = _sb_block(q, k_ref[rows, :], i, g, diagonal)
            parts = [log_1mb[:, hh * SB_SUM_COLS:(hh + 1) * SB_SUM_COLS] for hh in range(halves)]
            tails = _masked_sums(parts, later)
            for hh in reversed(range(halves)):
                tails[hh] = tails[hh] + run
                run = run + jnp.sum(parts[hh], axis=1, keepdims=True)
            a = _sb_masked(valid, jnp.exp(log_beta + jnp.concatenate(tails, axis=1)))
            return acc + _nn(a.astype(bf16), v_ref[rows, :]), run

        carry = step(n_groups - 1, (jnp.zeros((qb, LANES), f32), jnp.zeros((qb, 1), f32)), True)
        acc, run = lax.fori_loop(0, n_groups - 1, lambda gg, c: step(n_groups - 2 - gg, c, False), carry)
        o_ref[...] = acc.astype(o_ref.dtype)
        tot_ref[...] = run

    return pl.pallas_call(
        body, name=name,
        out_shape=[jax.ShapeDtypeStruct((T, H * LANES), bf16), jax.ShapeDtypeStruct((H, T, 1), f32)],
        grid=(H, T // qb),
        in_specs=[pl.BlockSpec((qb, LANES), lambda h, i: (i, h)),
                  pl.BlockSpec((T, LANES), lambda h, i: (0, H + h)),
                  pl.BlockSpec((T, LANES), lambda h, i: (0, 2 * H + h))],
        out_specs=[pl.BlockSpec((qb, LANES), lambda h, i: (i, h)),
                   pl.BlockSpec((None, qb, 1), lambda h, i: (h, i, 0))],
        compiler_params=_cp("parallel", "parallel"),
    )(qkv, qkv, qkv)


def sb_bwd(qkv, tot, do, name):
    T = qkv.shape[0]
    H = qkv.shape[1] // (3 * LANES)
    qb, kg = _sb_query_rows(T), _sb_key_group(T)
    halves = kg // SB_SUM_COLS

    def body(q_ref, k_ref, v_ref, tot_ref, do_ref, dq_ref, dk_ref, dv_ref):
        i = pl.program_id(1)

        @pl.when(i == 0)
        def _():
            dk_ref[...] = jnp.zeros_like(dk_ref)
            dv_ref[...] = jnp.zeros_like(dv_ref)

        q = q_ref[...]
        do = do_ref[...].astype(bf16)
        tot = tot_ref[...]
        later, earlier = _order_matrix(True), _order_matrix(False)
        scale = q.shape[-1] ** -0.5
        n_groups = (i * qb + qb + kg - 1) // kg

        def step(g, carry, diagonal):
            dq, before, d_run = carry
            rows = pl.ds(pl.multiple_of(g * kg, kg), kg)
            k, v = k_ref[rows, :], v_ref[rows, :]
            z, valid, log_beta, log_1mb = _sb_block(q, k, i, g, diagonal)
            parts = [log_1mb[:, hh * SB_SUM_COLS:(hh + 1) * SB_SUM_COLS] for hh in range(halves)]
            tails = _masked_sums(parts, later)
            for hh in range(halves):
                before = before + jnp.sum(parts[hh], axis=1, keepdims=True)
                tails[hh] = tails[hh] + (tot - before)
            a = _sb_masked(valid, jnp.exp(log_beta + jnp.concatenate(tails, axis=1)))
            d_e = _nt(do, v) * a
            parts = [d_e[:, hh * SB_SUM_COLS:(hh + 1) * SB_SUM_COLS] for hh in range(halves)]
            d_l1 = _masked_sums(parts, earlier)
            for hh in range(halves):
                d_l1[hh] = d_l1[hh] + d_run
                d_run = d_run + jnp.sum(parts[hh], axis=1, keepdims=True)
            d_l1 = _sb_masked(valid, jnp.concatenate(d_l1, axis=1))
            sig = jax.nn.sigmoid(z)
            dz = ((d_e * (1.0 - sig) - d_l1 * sig) * scale).astype(bf16)
            dk_ref[rows, :] += _tn(dz, q)
            dv_ref[rows, :] += _tn(a.astype(bf16), do)
            return dq + _nn(dz, k), before, d_run

        zero_col = jnp.zeros((qb, 1), f32)
        carry = lax.fori_loop(0, n_groups - 1, lambda g, c: step(g, c, False),
                              (jnp.zeros((qb, LANES), f32), zero_col, zero_col))
        dq, _, _ = step(n_groups - 1, carry, True)
        dq_ref[...] = dq.astype(dq_ref.dtype)

    W = H * LANES
    return pl.pallas_call(
        body, name=name,
        out_shape=[jax.ShapeDtypeStruct((T, W), bf16), jax.ShapeDtypeStruct((T, W), f32), jax.ShapeDtypeStruct((T, W), f32)],
        grid=(H, T // qb),
        in_specs=[pl.BlockSpec((qb, LANES), lambda h, i: (i, h)),
                  pl.BlockSpec((T, LANES), lambda h, i: (0, H + h)),
                  pl.BlockSpec((T, LANES), lambda h, i: (0, 2 * H + h)),
                  pl.BlockSpec((None, qb, 1), lambda h, i: (h, i, 0)),
                  pl.BlockSpec((qb, LANES), lambda h, i: (i, h))],
        out_specs=[pl.BlockSpec((qb, LANES), lambda h, i: (i, h)),
                   pl.BlockSpec((T, LANES), lambda h, i: (0, h)),
                   pl.BlockSpec((T, LANES), lambda h, i: (0, h))],
        compiler_params=_cp("parallel", "arbitrary"),
    )(qkv, qkv, qkv, tot, do)


def _softplus(x):
    return jnp.maximum(x, 0.0) + jnp.log1p(jnp.exp(-jnp.abs(x)))


def _ssd_chunk_fn(head0, xs, b_mat, c_mat, dt_raw, dt_bias, a_log, d_skip, prev):
    q = dt_raw.shape[0]
    lane = lax.broadcasted_iota(jnp.int32, (q, LANES), 1)
    sub = lax.broadcasted_iota(jnp.int32, (q, LANES), 0)
    causal = sub >= lane
    dt = _softplus(dt_raw + dt_bias)
    a_cum = _nn(causal.astype(f32), dt * (-jnp.exp(a_log)), HIGHEST)
    a_cum_t = a_cum.T
    cb = _nt(c_mat.astype(bf16), b_mat.astype(bf16))
    bm, cm = b_mat.astype(bf16), c_mat.astype(bf16)
    ys, new = [], []
    for r in range(len(xs)):
        in_lane, in_sub = lane == head0 + r, sub == head0 + r
        col_a = jnp.sum(jnp.where(in_lane, a_cum, 0.0), axis=1, keepdims=True)
        row_a = jnp.sum(jnp.where(in_sub, a_cum_t, 0.0), axis=0, keepdims=True)
        col_dt = jnp.sum(jnp.where(in_lane, dt, 0.0), axis=1, keepdims=True)
        skip = jnp.sum(jnp.where(in_lane[:1], d_skip, 0.0), axis=1, keepdims=True)
        a_last = jnp.sum(jnp.where(sub[:, :1] == q - 1, col_a, 0.0), axis=0, keepdims=True)
        decay_in = jnp.exp(jnp.where(causal, col_a - row_a, -jnp.inf))
        xdt = xs[r] * col_dt
        y_diag = _nn((cb * decay_in).astype(bf16), xdt.astype(bf16))
        y_off = _nt(cm, prev[r].astype(bf16)) * jnp.exp(col_a)
        ys.append(y_diag + y_off + xs[r] * skip)
        state = _tn((xdt * jnp.exp(a_last - col_a)).astype(bf16), bm)
        new.append(prev[r] * jnp.exp(a_last) + state)
    return ys, new


SSD_GROUPS_PER_STEP = 2


def _ssd_specs(T, G, reverse):
    nc = T // CHUNK
    R, P, N = SSD_HEADS_PER_GROUP, SSD_HEAD_DIM, SSD_STATE
    gp = SSD_GROUPS_PER_STEP if G % SSD_GROUPS_PER_STEP == 0 else 1
    ch = (lambda c: nc - 1 - c) if reverse else (lambda c: c)
    xs = pl.BlockSpec((CHUNK, gp * R * P), lambda g, c: (ch(c), g))
    bm = pl.BlockSpec((CHUNK, gp * N), lambda g, c: (ch(c), G * R * P // (gp * N) + g))
    cm = pl.BlockSpec((CHUNK, gp * N), lambda g, c: (ch(c), (G * R * P + G * N) // (gp * N) + g))
    dt = pl.BlockSpec((CHUNK, LANES), lambda g, c: (ch(c), 0))
    row = pl.BlockSpec((1, LANES), lambda g, c: (0, 0))
    st = pl.BlockSpec((gp, None, R * P, N), lambda g, c: (g, ch(c), 0, 0))
    return nc, gp, xs, bm, cm, dt, row, st


def ssd_fwd(xbc, dt_raw, dt_bias, a_log, d_skip, name):
    T = xbc.shape[0]
    R, P, N = SSD_HEADS_PER_GROUP, SSD_HEAD_DIM, SSD_STATE
    G = xbc.shape[1] // (R * P + 2 * N)
    nc, gp, xs_s, bm_s, cm_s, dt_s, row_s, st_s = _ssd_specs(T, G, False)

    def body(xs_ref, b_ref, c_ref, dt_ref, bias_ref, alog_ref, skip_ref, y_ref, st_ref, state):
        @pl.when(pl.program_id(1) == 0)
        def _():
            state[...] = jnp.zeros_like(state)

        done = []
        for gi in range(gp):
            x = xs_ref[:, gi * R * P:(gi + 1) * R * P]
            xs = [x[:, r * P:(r + 1) * P] for r in range(R)]
            prev = [state[gi, r] for r in range(R)]
            ys, new = _ssd_chunk_fn((pl.program_id(0) * gp + gi) * R, xs, b_ref[:, gi * N:(gi + 1) * N],
                                    c_ref[:, gi * N:(gi + 1) * N], dt_ref[...], bias_ref[...], alog_ref[...],
                                    skip_ref[...], prev)
            done.append((prev, ys, new))
        for gi, (prev, ys, new) in enumerate(done):
            y_ref[:, gi * R * P:(gi + 1) * R * P] = jnp.concatenate(ys, axis=1)
            for r in range(R):
                st_ref[gi, pl.ds(r * P, P), :] = prev[r]
                state[gi, r] = new[r]

    return pl.pallas_call(
        body, name=name,
        out_shape=[jax.ShapeDtypeStruct((T, G * R * P), f32), jax.ShapeDtypeStruct((G, nc, R * P, N), f32)],
        grid=(G // gp, nc), in_specs=[xs_s, bm_s, cm_s, dt_s, row_s, row_s, row_s], out_specs=[xs_s, st_s],
        scratch_shapes=[pltpu.VMEM((gp, R, P, N), f32)],
        compiler_params=_cp("parallel", "arbitrary"),
    )(xbc, xbc, xbc, dt_raw, dt_bias, a_log, d_skip)


def ssd_bwd(xbc, dt_raw, dt_bias, a_log, d_skip, states, dy, name):
    T = xbc.shape[0]
    R, P, N = SSD_HEADS_PER_GROUP, SSD_HEAD_DIM, SSD_STATE
    G = xbc.shape[1] // (R * P + 2 * N)
    nc, gp, xs_s, bm_s, cm_s, dt_s, row_s, st_s = _ssd_specs(T, G, True)

    def body(xs_ref, b_ref, c_ref, dt_ref, bias_ref, alog_ref, skip_ref, st_ref, dy_ref,
             dx_ref, db_ref, dc_ref, ddt_ref, dbias_ref, dalog_ref, dskip_ref, dstate):
        g, c = pl.program_id(0), pl.program_id(1)

        @pl.when(c == 0)
        def _():
            dstate[...] = jnp.zeros_like(dstate)

        @pl.when((c == 0) & (g == 0))
        def _():
            for a in (dbias_ref, dalog_ref, dskip_ref):
                a[...] = jnp.zeros_like(a)

        done = []
        for gi in range(gp):
            cols, ncols = slice(gi * R * P, (gi + 1) * R * P), slice(gi * N, (gi + 1) * N)
            x, dyv = xs_ref[:, cols], dy_ref[:, cols]
            xs = [x[:, r * P:(r + 1) * P] for r in range(R)]
            prev = [st_ref[gi, pl.ds(r * P, P), :] for r in range(R)]
            _, vjp = jax.vjp(functools.partial(_ssd_chunk_fn, (g * gp + gi) * R), xs, b_ref[:, ncols], c_ref[:, ncols],
                             dt_ref[...], bias_ref[...], alog_ref[...], skip_ref[...], prev)
            done.append(vjp(([dyv[:, r * P:(r + 1) * P] for r in range(R)], [dstate[gi, r] for r in range(R)])))
        for gi, (dxs, dbm, dcm, ddt, dbias, dalog, dskip, dprev) in enumerate(done):
            dx_ref[:, gi * R * P:(gi + 1) * R * P] = jnp.concatenate(dxs, axis=1)
            db_ref[:, gi * N:(gi + 1) * N] = dbm
            dc_ref[:, gi * N:(gi + 1) * N] = dcm
            ddt_ref[gi] = ddt
            for r in range(R):
                dstate[gi, r] = dprev[r]
        dbias_ref[...] += sum(d[4] for d in done)
        dalog_ref[...] += sum(d[5] for d in done)
        dskip_ref[...] += sum(d[6] for d in done)

    small = pl.BlockSpec((CHUNK, gp * N), lambda g, c: (nc - 1 - c, g))
    return pl.pallas_call(
        body, name=name,
        out_shape=[jax.ShapeDtypeStruct((T, G * R * P), f32), jax.ShapeDtypeStruct((T, G * N), f32),
                   jax.ShapeDtypeStruct((T, G * N), f32), jax.ShapeDtypeStruct((G, T, LANES), f32)]
        + [jax.ShapeDtypeStruct((1, LANES), f32)] * 3,
        grid=(G // gp, nc), in_specs=[xs_s, bm_s, cm_s, dt_s, row_s, row_s, row_s, st_s, xs_s],
        out_specs=[xs_s, small, small, pl.BlockSpec((gp, CHUNK, LANES), lambda g, c: (g, nc - 1 - c, 0)),
                   row_s, row_s, row_s],
        scratch_shapes=[pltpu.VMEM((gp, R, P, N), f32)],
        compiler_params=_cp("arbitrary", "arbitrary"),
    )(xbc, xbc, xbc, dt_raw, dt_bias, a_log, d_skip, states, dy)


WEIGHTS = ["ln_mix_pre", "ln_mix_post", "ln_mem", "ln_xa_pre", "ln_xa_post", "ln_ffn_pre", "ln_ffn_post",
           "xa_wq", "xa_wkv", "xa_wo", "ffn_w_in", "ffn_conv_w", "ffn_conv_b", "ffn_w_out",
           "ssd_w_in", "ssd_conv_w", "ssd_conv_b", "ssd_dt_bias", "ssd_a_log", "ssd_d", "ssd_norm", "ssd_w_out",
           "sg_w_in", "sg_v_norm_g", "sg_v_norm_b", "sg_w_spatial", "sg_b_spatial", "sg_w_out", "sb_w_qkv", "sb_w_out"]
BIG = ["xa_wq", "xa_wkv", "xa_wo", "ffn_w_in", "ffn_w_out", "ssd_w_in", "ssd_w_out", "sg_w_in", "sg_w_out",
       "sb_w_qkv", "sb_w_out"]
SMALL_SHARDED = ["ffn_conv_w", "ssd_conv_w", "ssd_conv_b", "ssd_norm"]
REPLICATED = [n for n in WEIGHTS if n not in BIG and n not in SMALL_SHARDED]


def fn_xpre(x, g):
    return x, _rms(x, g)


PACK_ROW_TILE = 512


def _pack_rows(arrs, lead=0):
    head = arrs[0].shape[:lead]
    flat = jnp.concatenate([a.reshape(head + (-1,)) for a in arrs], axis=-1)
    n = flat.shape[-1]
    rows = -(-n // (SUBLANES * LANES)) * SUBLANES
    if rows > PACK_ROW_TILE:
        rows = -(-rows // PACK_ROW_TILE) * PACK_ROW_TILE
    flat = jnp.pad(flat, [(0, 0)] * lead + [(0, rows * LANES - n)])
    return flat.reshape(head + (rows, LANES))


def _unpack_rows(packed, shapes):
    head = packed.shape[:-2]
    flat = packed.reshape(head + (-1,))
    out, off = [], 0
    for shp in shapes:
        n = int(np.prod(shp, dtype=np.int64))
        out.append(flat[..., off:off + n].reshape(head + tuple(shp)))
        off += n
    return out


def _merge_last(a8):
    return jnp.moveaxis(a8, 0, -2).reshape(a8.shape[1:-1] + (N_DEV * a8.shape[-1],))


def _split_last(a):
    return jnp.moveaxis(a.reshape(a.shape[:-1] + (N_DEV, a.shape[-1] // N_DEV)), -2, 0)


def kernel(x, mem, ln_mix_pre, ln_mix_post, ln_mem, ln_xa_pre, ln_xa_post, ln_ffn_pre, ln_ffn_post, xa_wq, xa_wkv, xa_wo, ffn_w_in, ffn_conv_w, ffn_conv_b, ffn_w_out, ssd_w_in, ssd_conv_w, ssd_conv_b, ssd_dt_bias, ssd_a_log, ssd_d, ssd_norm, ssd_w_out, sg_w_in, sg_v_norm_g, sg_v_norm_b, sg_w_spatial, sg_b_spatial, sg_w_out, sb_w_qkv, sb_w_out, loss_target, m_ln_mix_pre, m_ln_mix_post, m_ln_mem, m_ln_xa_pre, m_ln_xa_post, m_ln_ffn_pre, m_ln_ffn_post, m_xa_wq, m_xa_wkv, m_xa_wo, m_ffn_w_in, m_ffn_conv_w, m_ffn_conv_b, m_ffn_w_out, m_ssd_w_in, m_ssd_conv_w, m_ssd_conv_b, m_ssd_dt_bias, m_ssd_a_log, m_ssd_d, m_ssd_norm, m_ssd_w_out, m_sg_w_in, m_sg_v_norm_g, m_sg_v_norm_b, m_sg_w_spatial, m_sg_b_spatial, m_sg_w_out, m_sb_w_qkv, m_sb_w_out, v_ln_mix_pre, v_ln_mix_post, v_ln_mem, v_ln_xa_pre, v_ln_xa_post, v_ln_ffn_pre, v_ln_ffn_post, v_xa_wq, v_xa_wkv, v_xa_wo, v_ffn_w_in, v_ffn_conv_w, v_ffn_conv_b, v_ffn_w_out, v_ssd_w_in, v_ssd_conv_w, v_ssd_conv_b, v_ssd_dt_bias, v_ssd_a_log, v_ssd_d, v_ssd_norm, v_ssd_w_out, v_sg_w_in, v_sg_v_norm_g, v_sg_v_norm_b, v_sg_w_spatial, v_sg_b_spatial, v_sg_w_out, v_sb_w_qkv, v_sb_w_out):
    p = dict(locals())
    x, mem, target = p["x"][0], p["mem"][0], p["loss_target"][0]
    T, D = x.shape
    depth = ln_mix_pre.shape[0]
    me = 4 * lax.axis_index("x") + 2 * lax.axis_index("y") + lax.axis_index("c")

    def gather(w, name):
        return sc_all_gather(w.astype(bf16), name)

    parts = {n: {} for n in BIG}
    pending, due = [], {}

    def scatter(name, layer, g8):
        pending.append((name, layer, g8))

    def deadline(name, s):
        if name == "ffn_w_out":
            return s - 2, 0
        if name == "ffn_w_in":
            return s - 2, 1
        if name.startswith("xa_"):
            return (s - 3, 0) if s >= 3 else (0, 1)
        return (s - 1, 0) if name.endswith("_out") else (s - 2, 0)

    def ship_pending(carry, s):
        carry, *held = lax.optimization_barrier((carry, *[g8 for _, _, g8 in pending]))
        for (name, layer, _), g8 in zip(pending, held):
            parts[name][layer] = sc_scatter_parts(g8, "rs_" + name)
            due.setdefault(deadline(name, s), []).append((name, layer))
        pending.clear()
        return carry

    def collect_due(carry, s):
        names = due.pop(s, [])
        if names:
            carry, *landed = lax.optimization_barrier((carry, *[parts[n][l] for n, l in names]))
            for (n, l), a in zip(names, landed):
                parts[n][l] = a
        return carry

    small8 = all_gather(_pack_rows([p[n] for n in SMALL_SHARDED]), "ag_small")
    full = {n: _merge_last(a) for n, a in zip(SMALL_SHARDED, _unpack_rows(small8, [p[n].shape for n in SMALL_SHARDED]))}

    grads = {n: {} for n in WEIGHTS}

    def ssd(h, j):
        g_in = gather(p["ssd_w_in"][j], "ag_ssd_in")
        g_in, h = lax.optimization_barrier((g_in, h))
        w_full = jnp.moveaxis(g_in, 0, 1).reshape(D, -1)
        w_out = gather(p["ssd_w_out"][j], "ag_ssd_out").reshape(1, -1, D)
        d_inner, conv_dim, heads = w_out.shape[1], full["ssd_conv_w"].shape[-1], ssd_dt_bias.shape[1]
        w_z, w_x = w_full[None, :, :d_inner], w_full[None, :, d_inner:d_inner + conv_dim]
        w_dt = jnp.pad(w_full[:, d_inner + conv_dim:], ((0, 0), (0, LANES - heads)))[None]
        lane_row = lambda a: jnp.pad(a[j:j + 1], ((0, 0), (0, LANES - heads)))
        bias, a_log, d_skip = lane_row(ssd_dt_bias), lane_row(ssd_a_log), lane_row(ssd_d)
        norm_g = full["ssd_norm"][j:j + 1]
        z = mm_nn(h, w_z, f32, "ssd_in_z")
        xbc_pre = mm_nn(h, w_x, f32, "ssd_in_x")
        dt_raw = mm_nn(h, w_dt, f32, "ssd_in_dt")
        streams = [(xbc_pre, 0, full["ssd_conv_w"][j], full["ssd_conv_b"][j:j + 1], 0)]
        xbc = conv_fwd(streams, epi_silu, conv_dim, f32, 4, "ssd_conv")
        y, states = ssd_fwd(xbc, dt_raw, bias, a_log, d_skip, "ssd_core")
        gated = row_fwd(fn_gate_norm, [y, z], [norm_g], [(d_inner, bf16)], "ssd_gate")[0]
        out = mm_nn(gated, w_out, f32, "ssd_out")

        def bwd(d_out, mid):
            d_gated = mm_nt(d_out, w_out, f32, "ssd_out_dx")
            scatter("ssd_w_out", j, mm_tn(gated, d_out, D, bf16, "ssd_out_dw").reshape(N_DEV, -1, D))
            dy, dz, d_norm = row_bwd(fn_gate_norm, [y, z], [norm_g], [d_gated], [f32, bf16], "ssd_gate_bwd")
            dxs, dbm, dcm, ddt_g, d_bias, d_alog, d_skipg = ssd_bwd(xbc, dt_raw, bias, a_log, d_skip, states, dy, "ssd_core_bwd")
            dxs = mid(dxs)
            (dx_pre,), (d_cw,), (d_cb,) = conv_bwd(streams, epi_silu, [dxs, dbm, dcm], 4, "ssd_conv_bwd")
            ddt = jnp.sum(ddt_g, axis=0).astype(bf16)
            dh = mm_nt(dz, w_z, f32, "ssd_in_z_dx")
            dh = mm_nt(dx_pre, w_x, f32, "ssd_in_x_dx", init=dh)
            dh = mm_nt(ddt, w_dt, f32, "ssd_in_dt_dx", init=dh)
            dw = jnp.concatenate([mm_tn(h, dz, d_inner, bf16, "ssd_in_z_dw")[0], mm_tn(h, dx_pre, conv_dim, bf16, "ssd_in_x_dw")[0],
                                  mm_tn(h, ddt, LANES, bf16, "ssd_in_dt_dw")[0][:, :heads]], axis=1)
            scatter("ssd_w_in", j, _split_last(dw))
            grads["ssd_conv_w"][j], grads["ssd_conv_b"][j], grads["ssd_norm"][j] = d_cw, d_cb[0], d_norm[0]
            grads["ssd_dt_bias"][j], grads["ssd_a_log"][j], grads["ssd_d"][j] = d_bias[0, :heads], d_alog[0, :heads], d_skipg[0, :heads]
            return dh

        return out, bwd

    def sgu(h, j):
        w_in = gather(p["sg_w_in"][j], "ag_sg_in")
        w_out = gather(p["sg_w_out"][j], "ag_sg_out").reshape(1, -1, D)
        norm_g, norm_b = sg_v_norm_g[j:j + 1], sg_v_norm_b[j:j + 1]
        w_sp, b_sp = sg_w_spatial[j], sg_b_spatial[j][..., None]
        uv = mm_nn(h, w_in, f32, "sg_in")
        gated = sgu_fwd(uv, norm_g, norm_b, w_sp, b_sp, "sg_core")
        out = mm_nn(gated, w_out, f32, "sg_out")

        def bwd(d_out, mid):
            d_gated = mm_nt(d_out, w_out, f32, "sg_out_dx")
            scatter("sg_w_out", j, mm_tn(gated, d_out, D, bf16, "sg_out_dw").reshape(N_DEV, -1, D))
            duv, d_ng, d_nb, d_ws, d_bs = sgu_bwd(uv, norm_g, norm_b, w_sp, b_sp, d_gated, "sg_core_bwd")
            duv = mid(duv)
            grads["sg_v_norm_g"][j], grads["sg_v_norm_b"][j] = d_ng[0], d_nb[0]
            grads["sg_w_spatial"][j], grads["sg_b_spatial"][j] = d_ws, d_bs[..., 0]
            scatter("sg_w_in", j, mm_tn(h, duv, w_in.shape[2], bf16, "sg_in_dw"))
            return mm_nt(duv, w_in, f32, "sg_in_dx")

        return out, bwd

    def stick(h, j):
        w_qkv = gather(p["sb_w_qkv"][j], "ag_sb_qkv")
        w_out = gather(p["sb_w_out"][j], "ag_sb_out").reshape(1, -1, D)
        qkv = mm_nn(h, w_qkv, bf16, "sb_qkv")
        o, tot = sb_fwd(qkv, "sb_core")
        out = mm_nn(o, w_out, f32, "sb_out")

        def bwd(d_out, mid):
            d_o = mm_nt(d_out, w_out, f32, "sb_out_dx")
            scatter("sb_w_out", j, mm_tn(o, d_out, D, bf16, "sb_out_dw").reshape(N_DEV, -1, D))
            dq, dk, dv = sb_bwd(qkv, tot, d_o, "sb_core_bwd")
            dq = mid(dq)
            dqkv = jnp.concatenate([dq, dk.astype(bf16), dv.astype(bf16)], axis=1)
            scatter("sb_w_qkv", j, mm_tn(h, dqkv, w_qkv.shape[2], bf16, "sb_qkv_dw"))
            return mm_nt(dqkv, w_qkv, f32, "sb_qkv_dx")

        return out, bwd

    def cross(h, i):
        gain = ln_mem[i:i + 1]
        mem_n = row_fwd(fn_prenorm, [mem], [gain], [(D, bf16)], "mem_norm")[0]
        w_q = gather(p["xa_wq"][i], "ag_xa_q").reshape(1, D, -1)
        w_kv = gather(p["xa_wkv"][i], "ag_xa_kv").reshape(1, D, -1)
        w_o = gather(p["xa_wo"][i], "ag_xa_o")
        q = mm_nn(h, w_q, bf16, "xa_q")
        kv = mm_nn(mem_n, w_kv, bf16, "xa_kv")
        o = xa_fwd(q, kv, "xa_core")
        out = mm_nn(o, w_o, f32, "xa_out")

        def bwd(d_out, mid):
            d_o = mm_nt(d_out, w_o, f32, "xa_out_dx")
            scatter("xa_wo", i, mm_tn(o, d_out, w_o.shape[2], bf16, "xa_out_dw"))
            dq, dk, dv = xa_bwd(q, kv, d_o, "xa_core_bwd")
            dkv = jnp.concatenate([dk, dv], axis=1).astype(bf16)
            scatter("xa_wq", i, mm_tn(h, dq, w_q.shape[2], bf16, "xa_q_dw").reshape(N_DEV, -1, w_q.shape[2]))
            scatter("xa_wkv", i, mm_tn(mem_n, dkv, w_kv.shape[2], bf16, "xa_kv_dw").reshape(N_DEV, -1, w_kv.shape[2]))
            d_mem_n = mm_nt(dkv, w_kv, f32, "xa_kv_dx")
            grads["ln_mem"][i] = row_bwd(fn_prenorm, [mem], [gain], [d_mem_n], [None], "mem_norm_bwd")[0][0]
            return mm_nt(dq, w_q, f32, "xa_q_dx")

        return out, bwd

    def ffn(h, i):
        w_in = gather(p["ffn_w_in"][i], "ag_ffn_in")
        w_out = gather(p["ffn_w_out"][i], "ag_ffn_out").reshape(1, -1, D)
        width = w_out.shape[1]
        conv_w, conv_b = full["ffn_conv_w"][i], ffn_conv_b[i:i + 1]
        gu = mm_nn(h, w_in, f32, "ffn_in")
        streams = [(gu, 0, conv_w, conv_b, 0), (gu, width // LANES, conv_w, conv_b, width // LANES)]
        act = conv_fwd(streams, epi_gelu_gate, width, bf16, 3, "ffn_gate")
        out = mm_nn(act, w_out, f32, "ffn_out")

        def bwd(d_out, mid):
            d_act = mm_nt(d_out, w_out, f32, "ffn_out_dx")
            scatter("ffn_w_out", i, mm_tn(act, d_out, D, bf16, "ffn_out_dw").reshape(N_DEV, -1, D))
            d_pre, d_cw, d_cb = conv_bwd(streams, epi_gelu_gate, d_act, 3, "ffn_gate_bwd")
            dgu = jnp.concatenate(d_pre, axis=1)
            grads["ffn_conv_w"][i], grads["ffn_conv_b"][i] = jnp.concatenate(d_cw, axis=1), jnp.concatenate(d_cb, axis=1)[0]
            scatter("ffn_w_in", i, mm_tn(h, dgu, w_in.shape[2], bf16, "ffn_in_dw"))
            return mm_nt(dgu, w_in, f32, "ffn_in_dx")

        return out, bwd

    n_sub = 3 * depth
    pre = [w[i:i + 1] for i in range(depth) for w in (ln_mix_pre, ln_xa_pre, ln_ffn_pre)]
    post = [w[i:i + 1] for i in range(depth) for w in (ln_mix_post, ln_xa_post, ln_ffn_post)]
    stream, outs, bwds = [x], [], []
    h = row_fwd(fn_prenorm, [x], [pre[0]], [(D, bf16)], "pre_norm")[0]
    for s in range(n_sub):
        i, t = divmod(s, 3)
        out, bwd = ((ssd, sgu, stick)[i % 3](h, i // 3) if t == 0 else cross(h, i) if t == 1 else ffn(h, i))
        outs.append(out)
        bwds.append(bwd)
        if s < n_sub - 1:
            x_new, h = row_fwd(fn_resnorm, [stream[s], out], [post[s], pre[s + 1]], [(D, f32), (D, bf16)], "res_norm")
            stream.append(x_new)
    dy, loss = row_fwd(fn_final, [stream[-1], outs[-1], target], [post[-1]], [(D, f32), (1, f32)], "loss_head", n_acc=1)
    loss = lax.psum(loss[0, 0], ("x", "y", "c"))

    d_pre, d_post = [None] * n_sub, [None] * n_sub
    dx, d_out, d_post[-1] = row_bwd(fn_res, [stream[-1], outs[-1]], [post[-1]], [dy], [f32, bf16], "res_bwd")
    for s in reversed(range(n_sub)):
        dh = ship_pending(bwds[s](collect_due(d_out, (s, 0)), functools.partial(collect_due, s=(s, 1))), s)
        if s > 0:
            dx, d_out, d_post[s - 1], d_pre[s] = row_bwd(
                fn_resnorm, [stream[s - 1], outs[s - 1]], [post[s - 1], pre[s]], [dx, dh], [f32, bf16], "res_norm_bwd")
        else:
            grad_x, d_pre[0] = row_bwd(fn_xpre, [x], [pre[0]], [dx, dh], [f32], "pre_norm_bwd")
    for t, kind in enumerate(("mix", "xa", "ffn")):
        for i in range(depth):
            grads["ln_%s_pre" % kind][i] = d_pre[3 * i + t][0]
            grads["ln_%s_post" % kind][i] = d_post[3 * i + t][0]

    def stacked(name):
        return jnp.stack([grads[name][l] for l in range(len(grads[name]))], axis=0)

    new = {}
    for name in BIG:
        new[name] = adamw_sharded(p[name], [parts[name][l] for l in range(len(parts[name]))],
                                  p["m_" + name], p["v_" + name], "adamw_" + name)

    rep_shapes = [p[n].shape for n in REPLICATED]
    g8 = all_gather(_pack_rows([stacked(n) for n in REPLICATED]), "ag_grad_rep")
    rep = adamw_summed8(_pack_rows([p[n] for n in REPLICATED]), g8, _pack_rows([p["m_" + n] for n in REPLICATED]),
                        _pack_rows([p["v_" + n] for n in REPLICATED]), "adamw_rep")
    for k, packed in enumerate(rep):
        for n, a in zip(REPLICATED, _unpack_rows(packed, rep_shapes)):
            new.setdefault(n, [None] * 4)[k] = a

    sh_shapes = [p[n].shape for n in SMALL_SHARDED]
    by_owner = _pack_rows([_split_last(stacked(n)) for n in SMALL_SHARDED], lead=1)
    mine8 = lax.dynamic_index_in_dim(all_gather(by_owner, "ag_grad_small"), me, axis=1, keepdims=False)
    sh = adamw_summed8(_pack_rows([p[n] for n in SMALL_SHARDED]), mine8, _pack_rows([p["m_" + n] for n in SMALL_SHARDED]),
                       _pack_rows([p["v_" + n] for n in SMALL_SHARDED]), "adamw_small")
    for k, packed in enumerate(sh):
        for n, a in zip(SMALL_SHARDED, _unpack_rows(packed, sh_shapes)):
            new.setdefault(n, [None] * 4)[k] = a

    return (loss, grad_x[None], *[new[n][0] for n in WEIGHTS], *[new[n][1] for n in WEIGHTS],
            *[new[n][2] for n in WEIGHTS], *[new[n][3] for n in WEIGHTS])
```

```python
import functools

import jax
import jax.numpy as jnp
import numpy as np
from jax import lax
from jax.experimental import pallas as pl
from jax.experimental.pallas import tpu as pltpu
from jax.experimental.pallas import tpu_sc as plsc

f32 = jnp.float32
bf16 = jnp.bfloat16
HIGHEST = lax.Precision.HIGHEST
MESH = pl.DeviceIdType.MESH

V7X_VMEM_BYTES = 64 * 1024 * 1024
VMEM_LIMIT = V7X_VMEM_BYTES * 3 // 4
LANES = 128
SUBLANES = 8
BF16_ROWS = 16

EPS = 1e-6
ADAM_LR = 0.001
ADAM_B1 = 0.9
ADAM_B2 = 0.999
ADAM_EPS = 1e-08
ADAM_WD = 0.01
ADAM_STEP = 10

N_DEV = 8
XA_HEADS = 4
SSD_HEADS_PER_GROUP = 8
SSD_HEAD_DIM = 64
SSD_STATE = 128
SSD_GROUPS = 8
CHUNK = 128


def _cp(*sem):
    return pltpu.CompilerParams(dimension_semantics=sem or None, vmem_limit_bytes=VMEM_LIMIT)


def _dot(a, b, dims, precision=None):
    return lax.dot_general(a, b, (dims, ((), ())), precision=precision, preferred_element_type=f32)


def _nn(a, b, precision=None):
    return _dot(a, b, ((1,), (0,)), precision)


def _nt(a, b, precision=None):
    return _dot(a, b, ((1,), (1,)), precision)


def _tn(a, b, precision=None):
    return _dot(a, b, ((0,), (0,)), precision)


def _largest_tile(n, cap, step):
    for t in range(min(n, cap) // step * step, 0, -step):
        if n % t == 0:
            return t
    return n


def _gather_partners():
    mx, my, mc = lax.axis_index("x"), lax.axis_index("y"), lax.axis_index("c")
    chips = [(1 - mx, my), (mx, 1 - my), (1 - mx, 1 - my)]
    return (mx, my, mc), (mx, my, 1 - mc), chips


def _two_level_gather(x_ref, out_ref, send_sems, recv_sems, local_sem):
    me, sibling, chips = _gather_partners()
    mc = me[2]

    def slot(px, py, pc):
        return out_ref.at[4 * px + 2 * py + pc]

    def copy(k, block, to, src=None):
        return pltpu.make_async_remote_copy(
            src_ref=slot(*block) if src is None else src, dst_ref=slot(*block),
            send_sem=send_sems.at[k], recv_sem=recv_sems.at[k], device_id=to, device_id_type=MESH)

    mine = pltpu.make_async_copy(x_ref, slot(*me), local_sem)
    mine.start()
    first = [copy(0, me, sibling, src=x_ref)]
    first += [copy(1 + j, me, (*chip, mc), src=x_ref) for j, chip in enumerate(chips)]
    for cp in first:
        cp.start()
    passed = [copy(4 + j, (*chip, mc), sibling) for j, chip in enumerate(chips)]
    for j, chip in enumerate(chips):
        copy(1 + j, (*chip, mc), me).wait_recv()
        passed[j].start()
    copy(0, sibling, me).wait_recv()
    for j, chip in enumerate(chips):
        copy(4 + j, (*chip, 1 - mc), me).wait_recv()
    for cp in first + passed:
        cp.wait_send()
    mine.wait()


def all_gather(x, name):
    return pl.pallas_call(
        functools.partial(_two_level_gather), name=name,
        out_shape=jax.ShapeDtypeStruct((N_DEV,) + x.shape, x.dtype),
        in_specs=[pl.BlockSpec(memory_space=pl.ANY)],
        out_specs=pl.BlockSpec(memory_space=pl.ANY),
        scratch_shapes=[pltpu.SemaphoreType.DMA((7,)), pltpu.SemaphoreType.DMA((7,)), pltpu.SemaphoreType.DMA(())],
    )(x)


AG_COLLECTIVE_ID = 1
RS_COLLECTIVE_ID = 2
_SEQUENCER = dict(axis_name="sequencer", num_cores=1)
_HBM = pltpu.MemorySpace.HBM


def _peers():
    mx, my, mc = lax.axis_index("x"), lax.axis_index("y"), lax.axis_index("c")
    return [((mx + (k >> 2)) % 2, (my + ((k >> 1) & 1)) % 2, (mc + (k & 1)) % 2) for k in range(1, N_DEV)]


def _handshake_all(peers):
    barrier = pltpu.get_barrier_semaphore()
    for peer in peers:
        pl.semaphore_signal(barrier, inc=1, device_id=peer, device_id_type=MESH)
    pl.semaphore_wait(barrier, len(peers))


def _block_of(peer):
    return 4 * peer[0] + 2 * peer[1] + peer[2]


def sc_all_gather(x, name):
    x_ref = jax.new_ref(x, memory_space=_HBM)
    out_ref = jax.empty_ref(jax.ShapeDtypeStruct((N_DEV,) + x.shape, x.dtype), memory_space=_HBM)

    @pl.kernel(mesh=plsc.ScalarSubcoreMesh(**_SEQUENCER), name=name,
               scratch_types=(pltpu.SemaphoreType.DMA((N_DEV - 1,)), pltpu.SemaphoreType.DMA((N_DEV - 1,)),
                              pltpu.SemaphoreType.DMA(())),
               compiler_params=pltpu.CompilerParams(collective_id=AG_COLLECTIVE_ID))
    def launch(send_sems, recv_sems, local_sem):
        _, sibling, chips = _gather_partners()
        _handshake_all([sibling] + [(*chip, lax.axis_index("c")) for chip in chips])
        _two_level_gather(x_ref, out_ref, send_sems, recv_sems, local_sem)

    launch()
    return out_ref[...]


def sc_scatter_parts(g8, name):
    g_ref = jax.new_ref(g8, memory_space=_HBM)
    parts_ref = jax.empty_ref(jax.ShapeDtypeStruct(g8.shape, g8.dtype), memory_space=_HBM)

    @pl.kernel(mesh=plsc.ScalarSubcoreMesh(**_SEQUENCER), name=name,
               scratch_types=(pltpu.SemaphoreType.DMA((N_DEV - 1,)), pltpu.SemaphoreType.DMA((N_DEV - 1,)),
                              pltpu.SemaphoreType.DMA(())),
               compiler_params=pltpu.CompilerParams(collective_id=RS_COLLECTIVE_ID))
    def launch(send_sems, recv_sems, local_sem):
        peers = _peers()
        _handshake_all(peers)
        me = 4 * lax.axis_index("x") + 2 * lax.axis_index("y") + lax.axis_index("c")
        mine = pltpu.make_async_copy(g_ref.at[me], parts_ref.at[0], local_sem)
        mine.start()
        copies = [pltpu.make_async_remote_copy(src_ref=g_ref.at[_block_of(peer)], dst_ref=parts_ref.at[k + 1],
                                               send_sem=send_sems.at[k], recv_sem=recv_sems.at[k],
                                               device_id=peer, device_id_type=MESH) for k, peer in enumerate(peers)]
        for cp in copies:
            cp.start()
        for cp in copies:
            cp.wait()
        mine.wait()

    launch()
    return parts_ref[...]


def _as_lrc(a, lead):
    rest = a.shape[lead:]
    return a.reshape(a.shape[:lead] + (int(np.prod(rest[:-2], dtype=np.int64)),) + rest[-2:])


def _adam_math(w, g, m, v):
    m = ADAM_B1 * m + (1.0 - ADAM_B1) * g
    v = ADAM_B2 * v + (1.0 - ADAM_B2) * jnp.square(g)
    m_hat = m / (1.0 - ADAM_B1 ** ADAM_STEP)
    v_hat = v / (1.0 - ADAM_B2 ** ADAM_STEP)
    delta = -ADAM_LR * (m_hat / (jnp.sqrt(v_hat) + ADAM_EPS) + ADAM_WD * w)
    return delta, m, v


def adamw_sharded(w, parts, m, v, name):
    L, R, C = w.shape
    row_bytes = 2 * C * (L * N_DEV * 2 + 7 * 4)
    tr = next((t for t in _tiles_desc(R, 128, BF16_ROWS) if t * row_bytes <= MM_VMEM_BUDGET), BF16_ROWS)

    def body(w_ref, *refs):
        p_refs, (m_ref, v_ref, g_out, d_out, m_out, v_out) = refs[:L], refs[L:]
        for layer in range(L):
            @pl.when(pl.program_id(0) == layer)
            def _(p_ref=p_refs[layer]):
                g = p_ref[0].astype(f32)
                for k in range(1, N_DEV):
                    g = g + p_ref[k].astype(f32)
                delta, mn, vn = _adam_math(w_ref[...], g, m_ref[...], v_ref[...])
                g_out[...] = g
                d_out[...] = delta
                m_out[...] = mn
                v_out[...] = vn

    blk = pl.BlockSpec((None, tr, C), lambda l, r: (l, r, 0))
    part_specs = [pl.BlockSpec((N_DEV, tr, C), functools.partial(lambda layer, l, r: (0, jnp.where(l == layer, r, 0), 0), layer))
                  for layer in range(L)]
    return pl.pallas_call(
        body, name=name, out_shape=[jax.ShapeDtypeStruct(w.shape, f32)] * 4, grid=(L, R // tr),
        in_specs=[blk] + part_specs + [blk, blk], out_specs=[blk] * 4,
        compiler_params=_cp("arbitrary", "arbitrary"),
    )(w, *parts, m, v)


def adamw_summed8(w, g8, m, v, name):
    R, C = w.shape
    tr = _largest_tile(R, 512, SUBLANES)

    def body(w_ref, g_ref, m_ref, v_ref, g_out, d_out, m_out, v_out):
        g = g_ref[0]
        for d in range(1, N_DEV):
            g = g + g_ref[d]
        delta, mn, vn = _adam_math(w_ref[...], g, m_ref[...], v_ref[...])
        g_out[...] = g
        d_out[...] = delta
        m_out[...] = mn
        v_out[...] = vn

    blk = pl.BlockSpec((tr, C), lambda r: (r, 0))
    return pl.pallas_call(
        body, name=name, out_shape=[jax.ShapeDtypeStruct((R, C), f32)] * 4, grid=(R // tr,),
        in_specs=[blk, pl.BlockSpec((N_DEV, tr, C), lambda r: (0, r, 0)), blk, blk], out_specs=[blk] * 4,
        compiler_params=_cp("parallel"),
    )(w, g8, m, v)


def _tile_n(ns):
    if ns % 512 == 0:
        return 512
    if ns <= 1536:
        return ns
    return _largest_tile(ns, 512, LANES)


MM_VMEM_BUDGET = VMEM_LIMIT - 8 * 1024 * 1024


def _tiles_desc(n, cap, step):
    return [t for t in range(min(n, cap) // step * step, 0, -step) if n % t == 0] or [n]


def _mm_fits(in_tiles, out_tile, out_dtype, n_red, extra=0):
    rows, cols = out_tile
    total = sum(2 * 2 * r * c for r, c in in_tiles) + 2 * rows * cols * jnp.dtype(out_dtype).itemsize + extra
    if n_red > 1:
        total += 4 * rows * cols
    return total <= MM_VMEM_BUDGET


def _reduce_into(o_ref, acc, part, first, last):
    if acc is None:
        o_ref[...] = part().astype(o_ref.dtype)
        return

    @pl.when(first)
    def _():
        acc[...] = jnp.zeros_like(acc)

    acc[...] += part()

    @pl.when(last)
    def _():
        o_ref[...] = acc[...].astype(o_ref.dtype)


def mm_nn(a, w3, out_dtype, name):
    M, K = a.shape
    J, _, Ns = w3.shape
    tn = _tile_n(Ns)
    tm, tk = next(((tm, tk) for tk in _tiles_desc(K, 4096, LANES) for tm in _tiles_desc(M, 1024, BF16_ROWS)
                   if tm >= min(M, 256) and _mm_fits([(tm, tk), (tk, tn)], (tm, tn), out_dtype, K // tk)),
                  (min(M, 256), _largest_tile(K, 512, LANES)))
    nn, nk = Ns // tn, K // tk

    def body(a_ref, w_ref, o_ref, *acc):
        k = pl.program_id(3)
        _reduce_into(o_ref, acc[0] if acc else None, lambda: _nn(a_ref[...], w_ref[...]), k == 0, k == nk - 1)

    return pl.pallas_call(
        body, name=name, out_shape=jax.ShapeDtypeStruct((M, J * Ns), out_dtype),
        grid=(M // tm, J, nn, nk),
        in_specs=[pl.BlockSpec((tm, tk), lambda i, j, n, k: (i, k)),
                  pl.BlockSpec((None, tk, tn), lambda i, j, n, k: (j, k, n))],
        out_specs=pl.BlockSpec((tm, tn), lambda i, j, n, k: (i, j * nn + n)),
        scratch_shapes=[pltpu.VMEM((tm, tn), f32)] if nk > 1 else [],
        compiler_params=_cp("parallel", "parallel", "parallel", "arbitrary"),
    )(a, w3)


MM_NT_REDUCE_CAP = 5632


def mm_nt(g, w3, out_dtype, name, init=None):
    M = g.shape[0]
    J, K, Ns = w3.shape
    if Ns <= MM_NT_REDUCE_CAP:
        tn, jb = Ns, max(b for b in range(1, J + 1) if J % b == 0 and (b == 1 or b * Ns <= MM_NT_REDUCE_CAP))
    else:
        tn, jb = _largest_tile(Ns, MM_NT_REDUCE_CAP, LANES), 1
    nj, nn = J // jb, Ns // tn
    n_red = nj * nn + (init is not None)
    tk = _largest_tile(K, 512, LANES)
    tm = next((tm for tm in _tiles_desc(M, 1024, BF16_ROWS)
               if _mm_fits([(tm, jb * tn), (jb * tk, tn)], (tm, tk), out_dtype, n_red,
                           extra=0 if init is None else 2 * 4 * tm * tk)), min(M, 256))

    def body(*refs):
        g_ref, w_ref = refs[:2]
        i_ref = None if init is None else refs[2]
        o_ref = refs[2 + (init is not None)]
        acc = refs[3 + (init is not None):]
        j, n = pl.program_id(2), pl.program_id(3)
        first = (j == 0) & (n == 0)

        def part():
            prod = _nt(g_ref[:, :tn], w_ref[0])
            for b in range(1, jb):
                prod = prod + _nt(g_ref[:, b * tn:(b + 1) * tn], w_ref[b])
            return prod if init is None else prod + jnp.where(first, i_ref[...].astype(f32), 0.0)

        _reduce_into(o_ref, acc[0] if acc else None, part, first, (j == nj - 1) & (n == nn - 1))

    in_specs = [pl.BlockSpec((tm, jb * tn), lambda i, k, j, n: (i, j * nn + n)),
                pl.BlockSpec((jb, tk, tn), lambda i, k, j, n: (j, k, n))]
    args = [g, w3]
    if init is not None:
        in_specs.append(pl.BlockSpec((tm, tk), lambda i, k, j, n: (i, k)))
        args.append(init)
    return pl.pallas_call(
        body, name=name, out_shape=jax.ShapeDtypeStruct((M, K), out_dtype),
        grid=(M // tm, K // tk, nj, nn), in_specs=in_specs,
        out_specs=pl.BlockSpec((tm, tk), lambda i, k, j, n: (i, k)),
        scratch_shapes=[pltpu.VMEM((tm, tk), f32)] if n_red > 1 else [],
        compiler_params=_cp("parallel", "parallel", "arbitrary", "arbitrary"),
    )(*args)


def mm_tn(a, g, ns, out_dtype, name):
    M, K = a.shape
    J = g.shape[1] // ns
    tn, tk = _tile_n(ns), _largest_tile(K, 512, LANES)
    tm = next((tm for tm in _tiles_desc(M, 4096, BF16_ROWS)
               if _mm_fits([(tm, tk), (tm, tn)], (tk, tn), out_dtype, M // tm)), _largest_tile(M, 512, BF16_ROWS))
    nn, nm = ns // tn, M // tm

    def body(a_ref, g_ref, o_ref, *acc):
        m = pl.program_id(3)
        _reduce_into(o_ref, acc[0] if acc else None, lambda: _tn(a_ref[...], g_ref[...]), m == 0, m == nm - 1)

    return pl.pallas_call(
        body, name=name, out_shape=jax.ShapeDtypeStruct((J, K, ns), out_dtype),
        grid=(J, K // tk, nn, nm),
        in_specs=[pl.BlockSpec((tm, tk), lambda j, k, n, m: (m, k)),
                  pl.BlockSpec((tm, tn), lambda j, k, n, m: (m, j * nn + n))],
        out_specs=pl.BlockSpec((None, tk, tn), lambda j, k, n, m: (j, k, n)),
        scratch_shapes=[pltpu.VMEM((tk, tn), f32)] if nm > 1 else [],
        compiler_params=_cp("parallel", "parallel", "parallel", "arbitrary"),
    )(a, g)


ROW_TILE = 256
ROW_STEP = 64


def row_fwd(fn, rows, consts, outs, name, n_acc=0):
    T = rows[0].shape[0]
    tr = min(T, ROW_TILE)
    n_rows, n_consts, n_row_out = len(rows), len(consts), len(outs) - n_acc

    def body(*refs):
        r_refs, c_refs, o_refs = refs[:n_rows], refs[n_rows:n_rows + n_consts], refs[n_rows + n_consts:]
        cs = [c[...] for c in c_refs]
        acc_refs = o_refs[n_row_out:]
        if n_acc:
            @pl.when(pl.program_id(0) == 0)
            def _():
                for a in acc_refs:
                    a[...] = jnp.zeros_like(a)

        def step(s, carry):
            rows_s = pl.ds(pl.multiple_of(s * ROW_STEP, ROW_STEP), ROW_STEP)
            res = fn(*[r[rows_s, :].astype(f32) for r in r_refs], *cs)
            for o, val in zip(o_refs[:n_row_out], res[:n_row_out]):
                o[rows_s, :] = val.astype(o.dtype)
            return tuple(c + val for c, val in zip(carry, res[n_row_out:]))

        accs = lax.fori_loop(0, tr // ROW_STEP, step, tuple(jnp.zeros((1, c), f32) for c, _ in outs[n_row_out:]))
        for a, val in zip(acc_refs, accs):
            a[...] += val

    in_specs = [pl.BlockSpec((tr, r.shape[1]), lambda i: (i, 0)) for r in rows]
    in_specs += [pl.BlockSpec(c.shape, lambda i: (0, 0)) for c in consts]
    out_shape = [jax.ShapeDtypeStruct((T, c), dt) for c, dt in outs[:n_row_out]]
    out_shape += [jax.ShapeDtypeStruct((1, c), f32) for c, _ in outs[n_row_out:]]
    out_specs = [pl.BlockSpec((tr, c), lambda i: (i, 0)) for c, _ in outs[:n_row_out]]
    out_specs += [pl.BlockSpec((1, c), lambda i: (0, 0)) for c, _ in outs[n_row_out:]]
    return pl.pallas_call(
        body, name=name, out_shape=out_shape, grid=(T // tr,), in_specs=in_specs, out_specs=out_specs,
        compiler_params=_cp("arbitrary" if n_acc else "parallel"),
    )(*rows, *consts)


def row_bwd(fn, rows, consts, cts, grad_dtypes, name):
    T = rows[0].shape[0]
    tr = min(T, ROW_TILE)
    n_rows, n_consts, n_cts = len(rows), len(consts), len(cts)
    wanted = [i for i, dt in enumerate(grad_dtypes) if dt is not None]

    def body(*refs):
        r_refs = refs[:n_rows]
        c_refs = refs[n_rows:n_rows + n_consts]
        t_refs = refs[n_rows + n_consts:n_rows + n_consts + n_cts]
        o_refs = refs[n_rows + n_consts + n_cts:]
        gr_refs, gc_refs = o_refs[:len(wanted)], o_refs[len(wanted):]
        cs = [c[...] for c in c_refs]

        @pl.when(pl.program_id(0) == 0)
        def _():
            for a in gc_refs:
                a[...] = jnp.zeros_like(a)

        def step(s, carry):
            rows_s = pl.ds(pl.multiple_of(s * ROW_STEP, ROW_STEP), ROW_STEP)
            ins = [r[rows_s, :].astype(f32) for r in r_refs]
            _, vjp = jax.vjp(lambda *a: tuple(fn(*a)), *ins, *cs)
            grads = vjp(tuple(t[rows_s, :].astype(f32) for t in t_refs))
            for o, i in zip(gr_refs, wanted):
                o[rows_s, :] = grads[i].astype(o.dtype)
            return tuple(c + gval for c, gval in zip(carry, grads[n_rows:]))

        accs = lax.fori_loop(0, tr // ROW_STEP, step, tuple(jnp.zeros(c.shape, f32) for c in consts))
        for a, val in zip(gc_refs, accs):
            a[...] += val

    in_specs = [pl.BlockSpec((tr, r.shape[1]), lambda i: (i, 0)) for r in list(rows) + list(cts)]
    in_specs[n_rows:n_rows] = [pl.BlockSpec(c.shape, lambda i: (0, 0)) for c in consts]
    out_shape = [jax.ShapeDtypeStruct(rows[i].shape, grad_dtypes[i]) for i in wanted]
    out_shape += [jax.ShapeDtypeStruct(c.shape, f32) for c in consts]
    out_specs = [pl.BlockSpec((tr, rows[i].shape[1]), lambda i_: (i_, 0)) for i in wanted]
    out_specs += [pl.BlockSpec(c.shape, lambda i: (0, 0)) for c in consts]
    return pl.pallas_call(
        body, name=name, out_shape=out_shape, grid=(T // tr,), in_specs=in_specs, out_specs=out_specs,
        compiler_params=_cp("arbitrary"),
    )(*rows, *consts, *cts)


def _rms(x, g):
    return x * lax.rsqrt(jnp.mean(x * x, axis=-1, keepdims=True) + EPS) * g


def fn_prenorm(x, g):
    return (_rms(x, g),)


def fn_resnorm(x, m, g_post, g_pre):
    x_new = x + _rms(m, g_post)
    return x_new, _rms(x_new, g_pre)


def fn_res(x, m, g_post):
    return (x + _rms(m, g_post),)


def fn_final(x, m, target, g_post):
    err = x + _rms(m, g_post) - target
    n = err.shape[-1]
    return err / n, (0.5 / n) * jnp.sum(jnp.sum(err * err, axis=1, keepdims=True), axis=0, keepdims=True)


def fn_gate_norm(y, z, g):
    return (_rms(y * jax.nn.silu(z), g),)


CONV_ROWS = 256


def _conv_chunk(pre_ref, w, b, r, rb, K):
    t0 = pl.multiple_of(r * rb, rb)
    halo_at = pl.multiple_of(jnp.maximum(t0 - SUBLANES, 0), SUBLANES)
    halo = jnp.where(r > 0, pre_ref[pl.ds(halo_at, SUBLANES), :], 0.0)
    main = pre_ref[pl.ds(t0, rb), :]
    ext = jnp.concatenate([halo, main], axis=0)
    shifted = [main if k == K - 1 else pltpu.roll(ext, K - 1 - k, 0)[SUBLANES:] for k in range(K)]
    conv = b
    for k in range(K):
        conv = conv + w[k:k + 1] * shifted[k]
    return conv, shifted


def conv_fwd(streams, epilogue, out_cols, out_dtype, K, name):
    T = streams[0][0].shape[0]
    rb = min(T, CONV_ROWS)
    S = len(streams)

    def body(*refs):
        pre_refs, w_refs, b_refs, o_ref = refs[:S], refs[S:2 * S], refs[2 * S:3 * S], refs[3 * S]
        ws = [w[...] for w in w_refs]
        bs = [b[...] for b in b_refs]

        def step(r, carry):
            convs = [_conv_chunk(pre_refs[s], ws[s], bs[s], r, rb, K)[0] for s in range(S)]
            o_ref[pl.ds(pl.multiple_of(r * rb, rb), rb), :] = epilogue(*convs).astype(o_ref.dtype)
            return carry

        lax.fori_loop(0, T // rb, step, 0)

    in_specs = [pl.BlockSpec((T, LANES), functools.partial(lambda off, i: (0, off + i), st[1])) for st in streams]
    in_specs += [pl.BlockSpec((K, LANES), functools.partial(lambda off, i: (0, off + i), st[4])) for st in streams]
    in_specs += [pl.BlockSpec((1, LANES), functools.partial(lambda off, i: (0, off + i), st[4])) for st in streams]
    return pl.pallas_call(
        body, name=name, out_shape=jax.ShapeDtypeStruct((T, out_cols), out_dtype), grid=(out_cols // LANES,),
        in_specs=in_specs, out_specs=pl.BlockSpec((T, LANES), lambda i: (0, i)),
        compiler_params=_cp("parallel"),
    )(*[st[0] for st in streams], *[st[2] for st in streams], *[st[3] for st in streams])


def conv_bwd(streams, epilogue, dout, K, name):
    pieces = list(dout) if isinstance(dout, (list, tuple)) else [dout]
    T = pieces[0].shape[0]
    ends = [int(e) for e in np.cumsum([d.shape[1] // LANES for d in pieces])]
    cols = int(ends[-1]) * LANES
    rb = min(T, CONV_ROWS)
    S, n_pieces = len(streams), len(pieces)

    def body(*refs):
        pre_refs, w_refs, b_refs = refs[:S], refs[S:2 * S], refs[2 * S:3 * S]
        dout_refs = refs[3 * S:3 * S + n_pieces]
        o = refs[3 * S + n_pieces:]

        def dout_chunk(rows):
            d = dout_refs[-1][rows, :].astype(f32)
            for k in reversed(range(n_pieces - 1)):
                d = jnp.where(pl.program_id(0) < ends[k], dout_refs[k][rows, :].astype(f32), d)
            return d

        dpre_refs, dw_refs, db_refs, scr = o[:S], o[S:2 * S], o[2 * S:3 * S], o[3 * S:]
        ws = [w[...] for w in w_refs]
        bs = [b[...] for b in b_refs]
        for s in range(S):
            scr[s][pl.ds(T, SUBLANES), :] = jnp.zeros((SUBLANES, LANES), f32)

        def phase1(r, carry):
            rows = pl.ds(pl.multiple_of(r * rb, rb), rb)
            convs, shifted = zip(*[_conv_chunk(pre_refs[s], ws[s], bs[s], r, rb, K) for s in range(S)])
            _, vjp = jax.vjp(epilogue, *convs)
            dconvs = vjp(dout_chunk(rows))
            new = []
            for s in range(S):
                scr[s][rows, :] = dconvs[s]
                sums = [jnp.sum(dconvs[s] * shifted[s][k], axis=0, keepdims=True) for k in range(K)]
                sums.append(jnp.sum(dconvs[s], axis=0, keepdims=True))
                new.append(tuple(c + v for c, v in zip(carry[s], sums)))
            return tuple(new)

        zero = tuple(tuple(jnp.zeros((1, LANES), f32) for _ in range(K + 1)) for _ in range(S))
        sums = lax.fori_loop(0, T // rb, phase1, zero)
        tap = lax.broadcasted_iota(jnp.int32, (K, LANES), 0)
        for s in range(S):
            dw = jnp.zeros((K, LANES), f32)
            for k in range(K):
                dw = jnp.where(tap == k, sums[s][k], dw)
            dw_refs[s][...] = dw
            db_refs[s][...] = sums[s][K]

        def phase2(r, carry):
            t0 = pl.multiple_of(r * rb, rb)
            for s in range(S):
                ext = scr[s][pl.ds(t0, rb + SUBLANES), :]
                dpre = ws[s][K - 1:K] * ext[:rb]
                for k in range(K - 1):
                    j = K - 1 - k
                    dpre = dpre + ws[s][k:k + 1] * pltpu.roll(ext, rb + SUBLANES - j, 0)[:rb]
                dpre_refs[s][pl.ds(t0, rb), :] = dpre.astype(dpre_refs[s].dtype)
            return carry

        lax.fori_loop(0, T // rb, phase2, 0)

    in_specs = [pl.BlockSpec((T, LANES), functools.partial(lambda off, i: (0, off + i), st[1])) for st in streams]
    in_specs += [pl.BlockSpec((K, LANES), functools.partial(lambda off, i: (0, off + i), st[4])) for st in streams]
    in_specs += [pl.BlockSpec((1, LANES), functools.partial(lambda off, i: (0, off + i), st[4])) for st in streams]
    starts = [0] + [int(e) for e in ends[:-1]]
    in_specs += [pl.BlockSpec((T, LANES), functools.partial(lambda lo, hi, i: (0, jnp.clip(i, lo, hi - 1) - lo), lo, int(hi)))
                 for lo, hi in zip(starts, ends)]
    out_shape = [jax.ShapeDtypeStruct((T, cols), bf16)] * S
    out_shape += [jax.ShapeDtypeStruct((K, cols), f32)] * S + [jax.ShapeDtypeStruct((1, cols), f32)] * S
    out_specs = [pl.BlockSpec((T, LANES), lambda i: (0, i))] * S
    out_specs += [pl.BlockSpec((K, LANES), lambda i: (0, i))] * S + [pl.BlockSpec((1, LANES), lambda i: (0, i))] * S
    res = pl.pallas_call(
        body, name=name, out_shape=out_shape, grid=(cols // LANES,), in_specs=in_specs, out_specs=out_specs,
        scratch_shapes=[pltpu.VMEM((T + SUBLANES, LANES), f32)] * S,
        compiler_params=_cp("parallel"),
    )(*[st[0] for st in streams], *[st[2] for st in streams], *[st[3] for st in streams], *pieces)
    return res[:S], res[S:2 * S], res[2 * S:]


def epi_gelu_gate(cg, cu):
    return jax.nn.gelu(cg, approximate=True) * cu


def epi_silu(c):
    return jax.nn.silu(c)


XA_ROWS = 256


def _xa_fn(q, k, v):
    s = _nt(q.astype(bf16), k.astype(bf16)) * (q.shape[-1] ** -0.5)
    p = jax.nn.softmax(s, axis=-1)
    return _nn(p.astype(bf16), v.astype(bf16))


def xa_fwd(q, kv, name):
    T, W = q.shape
    M = kv.shape[0]
    H = W // LANES
    tr = min(T, XA_ROWS)

    def body(q_ref, k_ref, v_ref, o_ref):
        o_ref[...] = _xa_fn(q_ref[...].astype(f32), k_ref[...].astype(f32), v_ref[...].astype(f32)).astype(o_ref.dtype)

    return pl.pallas_call(
        body, name=name, out_shape=jax.ShapeDtypeStruct((T, W), bf16), grid=(T // tr, H),
        in_specs=[pl.BlockSpec((tr, LANES), lambda i, h: (i, h)),
                  pl.BlockSpec((M, LANES), lambda i, h: (0, h)),
                  pl.BlockSpec((M, LANES), lambda i, h: (0, H + h))],
        out_specs=pl.BlockSpec((tr, LANES), lambda i, h: (i, h)),
        compiler_params=_cp("parallel", "parallel"),
    )(q, kv, kv)


def xa_bwd(q, kv, do, name):
    T, W = q.shape
    M = kv.shape[0]
    H = W // LANES
    tr = min(T, XA_ROWS)

    def body(q_ref, k_ref, v_ref, do_ref, dq_ref, dk_ref, dv_ref):
        @pl.when(pl.program_id(1) == 0)
        def _():
            dk_ref[...] = jnp.zeros_like(dk_ref)
            dv_ref[...] = jnp.zeros_like(dv_ref)

        _, vjp = jax.vjp(_xa_fn, q_ref[...].astype(f32), k_ref[...].astype(f32), v_ref[...].astype(f32))
        dq, dk, dv = vjp(do_ref[...].astype(f32))
        dq_ref[...] = dq.astype(dq_ref.dtype)
        dk_ref[...] += dk
        dv_ref[...] += dv

    return pl.pallas_call(
        body, name=name,
        out_shape=[jax.ShapeDtypeStruct((T, W), bf16), jax.ShapeDtypeStruct((M, W), f32), jax.ShapeDtypeStruct((M, W), f32)],
        grid=(H, T // tr),
        in_specs=[pl.BlockSpec((tr, LANES), lambda h, i: (i, h)),
                  pl.BlockSpec((M, LANES), lambda h, i: (0, h)),
                  pl.BlockSpec((M, LANES), lambda h, i: (0, H + h)),
                  pl.BlockSpec((tr, LANES), lambda h, i: (i, h))],
        out_specs=[pl.BlockSpec((tr, LANES), lambda h, i: (i, h)),
                   pl.BlockSpec((M, LANES), lambda h, i: (0, h)),
                   pl.BlockSpec((M, LANES), lambda h, i: (0, h))],
        compiler_params=_cp("parallel", "arbitrary"),
    )(q, kv, kv, do)


def _sgu_norm_fn(v_pre, g, b):
    v = jax.nn.gelu(v_pre, approximate=True)
    mu = jnp.mean(v, axis=-1, keepdims=True)
    vc = v - mu
    return vc * lax.rsqrt(jnp.mean(vc * vc, axis=-1, keepdims=True) + EPS) * g + b


def _sgu_mix_fn(u_pre, vn, w, b):
    q = w.shape[0]
    tril = lax.broadcasted_iota(jnp.int32, (q, q), 0) >= lax.broadcasted_iota(jnp.int32, (q, q), 1)
    mixed = _nn(jnp.where(tril, w, 0.0).astype(bf16), vn.astype(bf16)) + b
    return jax.nn.gelu(u_pre, approximate=True) * mixed


def _sgu_norm_phase(v_ref, g, b, vn_ref):
    def step(s, carry):
        rows = pl.ds(pl.multiple_of(s * ROW_STEP, ROW_STEP), ROW_STEP)
        vn_ref[rows, :] = _sgu_norm_fn(v_ref[rows, :], g, b)
        return carry

    lax.fori_loop(0, CHUNK // ROW_STEP, step, 0)


def sgu_fwd(uv_pre, norm_g, norm_b, w_sp, b_sp, name):
    T, W2 = uv_pre.shape
    W = W2 // 2
    G = w_sp.shape[0]
    gw = W // G

    def body(u_ref, v_ref, g_ref, b_ref, ws_ref, bs_ref, o_ref, vn_ref):
        _sgu_norm_phase(v_ref, g_ref[...], b_ref[...], vn_ref)

        def group(gi, carry):
            cols = pl.ds(pl.multiple_of(gi * gw, LANES), gw)
            o_ref[:, cols] = _sgu_mix_fn(u_ref[:, cols], vn_ref[:, cols], ws_ref[gi], bs_ref[gi]).astype(o_ref.dtype)
            return carry

        lax.fori_loop(0, G, group, 0)

    full = lambda a: pl.BlockSpec(a.shape, lambda c: (0,) * a.ndim)
    return pl.pallas_call(
        body, name=name, out_shape=jax.ShapeDtypeStruct((T, W), bf16), grid=(T // CHUNK,),
        in_specs=[pl.BlockSpec((CHUNK, W), lambda c: (c, 0)), pl.BlockSpec((CHUNK, W), lambda c: (c, 1)),
                  full(norm_g), full(norm_b), full(w_sp), full(b_sp)],
        out_specs=pl.BlockSpec((CHUNK, W), lambda c: (c, 0)),
        scratch_shapes=[pltpu.VMEM((CHUNK, W), f32)],
        compiler_params=_cp("parallel"),
    )(uv_pre, uv_pre, norm_g, norm_b, w_sp, b_sp)


def sgu_bwd(uv_pre, norm_g, norm_b, w_sp, b_sp, dout, name):
    T, W2 = uv_pre.shape
    W = W2 // 2
    G = w_sp.shape[0]
    gw = W // G

    def body(u_ref, v_ref, g_ref, b_ref, ws_ref, bs_ref, do_ref, duv_ref, dg_ref, db_ref, dws_ref, dbs_ref,
             vn_ref, dvn_ref):
        @pl.when(pl.program_id(0) == 0)
        def _():
            for a in (dg_ref, db_ref, dws_ref, dbs_ref):
                a[...] = jnp.zeros_like(a)

        g, b = g_ref[...], b_ref[...]
        _sgu_norm_phase(v_ref, g, b, vn_ref)

        def group(gi, carry):
            cols = pl.ds(pl.multiple_of(gi * gw, LANES), gw)
            _, vjp = jax.vjp(_sgu_mix_fn, u_ref[:, cols], vn_ref[:, cols], ws_ref[gi], bs_ref[gi])
            du, dvn, dw, dbias = vjp(do_ref[:, cols])
            duv_ref[:, cols] = du.astype(duv_ref.dtype)
            dvn_ref[:, cols] = dvn
            dws_ref[gi] += dw
            dbs_ref[gi] += dbias
            return carry

        lax.fori_loop(0, G, group, 0)

        def step(s, carry):
            rows = pl.ds(pl.multiple_of(s * ROW_STEP, ROW_STEP), ROW_STEP)
            _, vjp = jax.vjp(_sgu_norm_fn, v_ref[rows, :], g, b)
            dv, dg, dbn = vjp(dvn_ref[rows, :])
            duv_ref[rows, pl.ds(W, W)] = dv.astype(duv_ref.dtype)
            return carry[0] + dg, carry[1] + dbn

        dg, dbn = lax.fori_loop(0, CHUNK // ROW_STEP, step, (jnp.zeros((1, W), f32), jnp.zeros((1, W), f32)))
        dg_ref[...] += dg
        db_ref[...] += dbn

    full = lambda a: pl.BlockSpec(a.shape, lambda c: (0,) * a.ndim)
    return pl.pallas_call(
        body, name=name,
        out_shape=[jax.ShapeDtypeStruct((T, W2), bf16), jax.ShapeDtypeStruct((1, W), f32), jax.ShapeDtypeStruct((1, W), f32),
                   jax.ShapeDtypeStruct(w_sp.shape, f32), jax.ShapeDtypeStruct(b_sp.shape, f32)],
        grid=(T // CHUNK,),
        in_specs=[pl.BlockSpec((CHUNK, W), lambda c: (c, 0)), pl.BlockSpec((CHUNK, W), lambda c: (c, 1)),
                  full(norm_g), full(norm_b), full(w_sp), full(b_sp), pl.BlockSpec((CHUNK, W), lambda c: (c, 0))],
        out_specs=[pl.BlockSpec((CHUNK, W2), lambda c: (c, 0)), full(norm_g), full(norm_b), full(w_sp), full(b_sp)],
        scratch_shapes=[pltpu.VMEM((CHUNK, W), f32), pltpu.VMEM((CHUNK, W), f32)],
        compiler_params=_cp("arbitrary"),
    )(uv_pre, uv_pre, norm_g, norm_b, w_sp, b_sp, dout)


SB_SUM_COLS = 256


def _sb_key_group(T):
    return 512 if T % 512 == 0 else T


def _sb_query_rows(T):
    return 256 if T % 256 == 0 else CHUNK


def _sb_block(q, k, i, g, diagonal):
    qb, kg = q.shape[0], k.shape[0]
    z = _nt(q, k) * (q.shape[-1] ** -0.5)
    sp = jnp.log(1.0 + jnp.exp(-jnp.abs(z)))
    log_beta = jnp.minimum(z, 0.0) - sp
    log_1mb = -jnp.maximum(z, 0.0) - sp
    if not diagonal:
        return z, None, log_beta, log_1mb
    t_idx = i * qb + lax.broadcasted_iota(jnp.int32, (qb, kg), 0)
    s_idx = g * kg + lax.broadcasted_iota(jnp.int32, (qb, kg), 1)
    valid = s_idx < t_idx
    return z, valid, log_beta, jnp.where(valid, log_1mb, 0.0)


def _sb_masked(valid, x):
    return x if valid is None else jnp.where(valid, x, 0.0)


def _order_matrix(later):
    r = lax.broadcasted_iota(jnp.int32, (SB_SUM_COLS, SB_SUM_COLS), 0)
    c = lax.broadcasted_iota(jnp.int32, (SB_SUM_COLS, SB_SUM_COLS), 1)
    return (r > c if later else r < c).astype(bf16)


def _masked_sums(parts, order):
    terms = []
    for x in parts:
        hi = x.astype(bf16)
        rest = x - hi.astype(f32)
        mid = rest.astype(bf16)
        terms += [hi, mid, (rest - mid.astype(f32)).astype(bf16)]
    rows = parts[0].shape[0]
    prod = _nn(jnp.concatenate(terms, axis=0), order)
    piece = lambda n: prod[n * rows:(n + 1) * rows]
    return [piece(3 * p) + piece(3 * p + 1) + piece(3 * p + 2) for p in range(len(parts))]


def sb_fwd(qkv, name):
    T = qkv.shape[0]
    H = qkv.shape[1] // (3 * LANES)
    qb, kg = _sb_query_rows(T), _sb_key_group(T)
    halves = kg // SB_SUM_COLS

    def body(q_ref, k_ref, v_ref, o_ref, tot_ref):
        i = pl.program_id(1)
        q = q_ref[...]
        later = _order_matrix(True)
        n_groups = (i * qb + qb + kg - 1) // kg

        def step(g, carry, diagonal):
            acc, run = carry
            rows = pl.ds(pl.multiple_of(g * kg, kg), kg)
            _, valid, log_beta, log_1mb = _sb_block(q, k_ref[rows, :], i, g, diagonal)
            parts = [log_1mb[:, hh * SB_SUM_COLS:(hh + 1) * SB_SUM_COLS] for hh in range(halves)]
            tails = _masked_sums(parts, later)
            for hh in reversed(range(halves)):
                tails[hh] = tails[hh] + run
                run = run + jnp.sum(parts[hh], axis=1, keepdims=True)
            a = _sb_masked(valid, jnp.exp(log_beta + jnp.concatenate(tails, axis=1)))
            return acc + _nn(a.astype(bf16), v_ref[rows, :]), run

        carry = step(n_groups - 1, (jnp.zeros((qb, LANES), f32), jnp.zeros((qb, 1), f32)), True)
        acc, run = lax.fori_loop(0, n_groups - 1, lambda gg, c: step(n_groups - 2 - gg, c, False), carry)
        o_ref[...] = acc.astype(o_ref.dtype)
        tot_ref[...] = run

    return pl.pallas_call(
        body, name=name,
        out_shape=[jax.ShapeDtypeStruct((T, H * LANES), bf16), jax.ShapeDtypeStruct((H, T, 1), f32)],
        grid=(H, T // qb),
        in_specs=[pl.BlockSpec((qb, LANES), lambda h, i: (i, h)),
                  pl.BlockSpec((T, LANES), lambda h, i: (0, H + h)),
                  pl.BlockSpec((T, LANES), lambda h, i: (0, 2 * H + h))],
        out_specs=[pl.BlockSpec((qb, LANES), lambda h, i: (i, h)),
                   pl.BlockSpec((None, qb, 1), lambda h, i: (h, i, 0))],
        compiler_params=_cp("parallel", "parallel"),
    )(qkv, qkv, qkv)


def sb_bwd(qkv, tot, do, name):
    T = qkv.shape[0]
    H = qkv.shape[1] // (3 * LANES)
    qb, kg = _sb_query_rows(T), _sb_key_group(T)
    halves = kg // SB_SUM_COLS

    def body(q_ref, k_ref, v_ref, tot_ref, do_ref, dq_ref, dk_ref, dv_ref):
        i = pl.program_id(1)

        @pl.when(i == 0)
        def _():
            dk_ref[...] = jnp.zeros_like(dk_ref)
            dv_ref[...] = jnp.zeros_like(dv_ref)

        q = q_ref[...]
        do = do_ref[...].astype(bf16)
        tot = tot_ref[...]
        later, earlier = _order_matrix(True), _order_matrix(False)
        scale = q.shape[-1] ** -0.5
        n_groups = (i * qb + qb + kg - 1) // kg

        def step(g, carry, diagonal):
            dq, before, d_run = carry
            rows = pl.ds(pl.multiple_of(g * kg, kg), kg)
            k, v = k_ref[rows, :], v_ref[rows, :]
            z, valid, log_beta, log_1mb = _sb_block(q, k, i, g, diagonal)
            parts = [log_1mb[:, hh * SB_SUM_COLS:(hh + 1) * SB_SUM_COLS] for hh in range(halves)]
            tails = _masked_sums(parts, later)
            for hh in range(halves):
                before = before + jnp.sum(parts[hh], axis=1, keepdims=True)
                tails[hh] = tails[hh] + (tot - before)
            a = _sb_masked(valid, jnp.exp(log_beta + jnp.concatenate(tails, axis=1)))
            d_e = _nt(do, v) * a
            parts = [d_e[:, hh * SB_SUM_COLS:(hh + 1) * SB_SUM_COLS] for hh in range(halves)]
            d_l1 = _masked_sums(parts, earlier)
            for hh in range(halves):
                d_l1[hh] = d_l1[hh] + d_run
                d_run = d_run + jnp.sum(parts[hh], axis=1, keepdims=True)
            d_l1 = _sb_masked(valid, jnp.concatenate(d_l1, axis=1))
            sig = jax.nn.sigmoid(z)
            dz = ((d_e * (1.0 - sig) - d_l1 * sig) * scale).astype(bf16)
            dk_ref[rows, :] += _tn(dz, q)
            dv_ref[rows, :] += _tn(a.astype(bf16), do)
            return dq + _nn(dz, k), before, d_run

        zero_col = jnp.zeros((qb, 1), f32)
        carry = lax.fori_loop(0, n_groups - 1, lambda g, c: step(g, c, False),
                              (jnp.zeros((qb, LANES), f32), zero_col, zero_col))
        dq, _, _ = step(n_groups - 1, carry, True)
        dq_ref[...] = dq.astype(dq_ref.dtype)

    W = H * LANES
    return pl.pallas_call(
        body, name=name,
        out_shape=[jax.ShapeDtypeStruct((T, W), bf16), jax.ShapeDtypeStruct((T, W), f32), jax.ShapeDtypeStruct((T, W), f32)],
        grid=(H, T // qb),
        in_specs=[pl.BlockSpec((qb, LANES), lambda h, i: (i, h)),
                  pl.BlockSpec((T, LANES), lambda h, i: (0, H + h)),
                  pl.BlockSpec((T, LANES), lambda h, i: (0, 2 * H + h)),
                  pl.BlockSpec((None, qb, 1), lambda h, i: (h, i, 0)),
                  pl.BlockSpec((qb, LANES), lambda h, i: (i, h))],
        out_specs=[pl.BlockSpec((qb, LANES), lambda h, i: (i, h)),
                   pl.BlockSpec((T, LANES), lambda h, i: (0, h)),
                   pl.BlockSpec((T, LANES), lambda h, i: (0, h))],
        compiler_params=_cp("parallel", "arbitrary"),
    )(qkv, qkv, qkv, tot, do)


def _softplus(x):
    return jnp.maximum(x, 0.0) + jnp.log1p(jnp.exp(-jnp.abs(x)))


def _ssd_chunk_fn(head0, xs, b_mat, c_mat, dt_raw, dt_bias, a_log, d_skip, prev):
    q = dt_raw.shape[0]
    lane = lax.broadcasted_iota(jnp.int32, (q, LANES), 1)
    sub = lax.broadcasted_iota(jnp.int32, (q, LANES), 0)
    causal = sub >= lane
    dt = _softplus(dt_raw + dt_bias)
    a_cum = _nn(causal.astype(f32), dt * (-jnp.exp(a_log)), HIGHEST)
    a_cum_t = a_cum.T
    cb = _nt(c_mat.astype(bf16), b_mat.astype(bf16))
    bm, cm = b_mat.astype(bf16), c_mat.astype(bf16)
    ys, new = [], []
    for r in range(len(xs)):
        in_lane, in_sub = lane == head0 + r, sub == head0 + r
        col_a = jnp.sum(jnp.where(in_lane, a_cum, 0.0), axis=1, keepdims=True)
        row_a = jnp.sum(jnp.where(in_sub, a_cum_t, 0.0), axis=0, keepdims=True)
        col_dt = jnp.sum(jnp.where(in_lane, dt, 0.0), axis=1, keepdims=True)
        skip = jnp.sum(jnp.where(in_lane[:1], d_skip, 0.0), axis=1, keepdims=True)
        a_last = jnp.sum(jnp.where(sub[:, :1] == q - 1, col_a, 0.0), axis=0, keepdims=True)
        decay_in = jnp.exp(jnp.where(causal, col_a - row_a, -jnp.inf))
        xdt = xs[r] * col_dt
        y_diag = _nn((cb * decay_in).astype(bf16), xdt.astype(bf16))
        y_off = _nt(cm, prev[r].astype(bf16)) * jnp.exp(col_a)
        ys.append(y_diag + y_off + xs[r] * skip)
        state = _tn((xdt * jnp.exp(a_last - col_a)).astype(bf16), bm)
        new.append(prev[r] * jnp.exp(a_last) + state)
    return ys, new


SSD_GROUPS_PER_STEP = 2


def _ssd_specs(T, G, reverse):
    nc = T // CHUNK
    R, P, N = SSD_HEADS_PER_GROUP, SSD_HEAD_DIM, SSD_STATE
    gp = SSD_GROUPS_PER_STEP if G % SSD_GROUPS_PER_STEP == 0 else 1
    ch = (lambda c: nc - 1 - c) if reverse else (lambda c: c)
    xs = pl.BlockSpec((CHUNK, gp * R * P), lambda g, c: (ch(c), g))
    bm = pl.BlockSpec((CHUNK, gp * N), lambda g, c: (ch(c), G * R * P // (gp * N) + g))
    cm = pl.BlockSpec((CHUNK, gp * N), lambda g, c: (ch(c), (G * R * P + G * N) // (gp * N) + g))
    dt = pl.BlockSpec((CHUNK, LANES), lambda g, c: (ch(c), 0))
    row = pl.BlockSpec((1, LANES), lambda g, c: (0, 0))
    st = pl.BlockSpec((gp, None, R * P, N), lambda g, c: (g, ch(c), 0, 0))
    return nc, gp, xs, bm, cm, dt, row, st


def ssd_fwd(xbc, dt_raw, dt_bias, a_log, d_skip, name):
    T = xbc.shape[0]
    R, P, N = SSD_HEADS_PER_GROUP, SSD_HEAD_DIM, SSD_STATE
    G = xbc.shape[1] // (R * P + 2 * N)
    nc, gp, xs_s, bm_s, cm_s, dt_s, row_s, st_s = _ssd_specs(T, G, False)

    def body(xs_ref, b_ref, c_ref, dt_ref, bias_ref, alog_ref, skip_ref, y_ref, st_ref, state):
        @pl.when(pl.program_id(1) == 0)
        def _():
            state[...] = jnp.zeros_like(state)

        done = []
        for gi in range(gp):
            x = xs_ref[:, gi * R * P:(gi + 1) * R * P]
            xs = [x[:, r * P:(r + 1) * P] for r in range(R)]
            prev = [state[gi, r] for r in range(R)]
            ys, new = _ssd_chunk_fn((pl.program_id(0) * gp + gi) * R, xs, b_ref[:, gi * N:(gi + 1) * N],
                                    c_ref[:, gi * N:(gi + 1) * N], dt_ref[...], bias_ref[...], alog_ref[...],
                                    skip_ref[...], prev)
            done.append((prev, ys, new))
        for gi, (prev, ys, new) in enumerate(done):
            y_ref[:, gi * R * P:(gi + 1) * R * P] = jnp.concatenate(ys, axis=1)
            for r in range(R):
                st_ref[gi, pl.ds(r * P, P), :] = prev[r]
                state[gi, r] = new[r]

    return pl.pallas_call(
        body, name=name,
        out_shape=[jax.ShapeDtypeStruct((T, G * R * P), f32), jax.ShapeDtypeStruct((G, nc, R * P, N), f32)],
        grid=(G // gp, nc), in_specs=[xs_s, bm_s, cm_s, dt_s, row_s, row_s, row_s], out_specs=[xs_s, st_s],
        scratch_shapes=[pltpu.VMEM((gp, R, P, N), f32)],
        compiler_params=_cp("parallel", "arbitrary"),
    )(xbc, xbc, xbc, dt_raw, dt_bias, a_log, d_skip)


def ssd_bwd(xbc, dt_raw, dt_bias, a_log, d_skip, states, dy, name):
    T = xbc.shape[0]
    R, P, N = SSD_HEADS_PER_GROUP, SSD_HEAD_DIM, SSD_STATE
    G = xbc.shape[1] // (R * P + 2 * N)
    nc, gp, xs_s, bm_s, cm_s, dt_s, row_s, st_s = _ssd_specs(T, G, True)

    def body(xs_ref, b_ref, c_ref, dt_ref, bias_ref, alog_ref, skip_ref, st_ref, dy_ref,
             dx_ref, db_ref, dc_ref, ddt_ref, dbias_ref, dalog_ref, dskip_ref, dstate):
        g, c = pl.program_id(0), pl.program_id(1)

        @pl.when(c == 0)
        def _():
            dstate[...] = jnp.zeros_like(dstate)

        @pl.when((c == 0) & (g == 0))
        def _():
            for a in (dbias_ref, dalog_ref, dskip_ref):
                a[...] = jnp.zeros_like(a)

        done = []
        for gi in range(gp):
            cols, ncols = slice(gi * R * P, (gi + 1) * R * P), slice(gi * N, (gi + 1) * N)
            x, dyv = xs_ref[:, cols], dy_ref[:, cols]
            xs = [x[:, r * P:(r + 1) * P] for r in range(R)]
            prev = [st_ref[gi, pl.ds(r * P, P), :] for r in range(R)]
            _, vjp = jax.vjp(functools.partial(_ssd_chunk_fn, (g * gp + gi) * R), xs, b_ref[:, ncols], c_ref[:, ncols],
                             dt_ref[...], bias_ref[...], alog_ref[...], skip_ref[...], prev)
            done.append(vjp(([dyv[:, r * P:(r + 1) * P] for r in range(R)], [dstate[gi, r] for r in range(R)])))
        for gi, (dxs, dbm, dcm, ddt, dbias, dalog, dskip, dprev) in enumerate(done):
            dx_ref[:, gi * R * P:(gi + 1) * R * P] = jnp.concatenate(dxs, axis=1)
            db_ref[:, gi * N:(gi + 1) * N] = dbm
            dc_ref[:, gi * N:(gi + 1) * N] = dcm
            ddt_ref[gi] = ddt
            for r in range(R):
                dstate[gi, r] = dprev[r]
        dbias_ref[...] += sum(d[4] for d in done)
        dalog_ref[...] += sum(d[5] for d in done)
        dskip_ref[...] += sum(d[6] for d in done)

    small = pl.BlockSpec((CHUNK, gp * N), lambda g, c: (nc - 1 - c, g))
    return pl.pallas_call(
        body, name=name,
        out_shape=[jax.ShapeDtypeStruct((T, G * R * P), f32), jax.ShapeDtypeStruct((T, G * N), f32),
                   jax.ShapeDtypeStruct((T, G * N), f32), jax.ShapeDtypeStruct((G, T, LANES), f32)]
        + [jax.ShapeDtypeStruct((1, LANES), f32)] * 3,
        grid=(G // gp, nc), in_specs=[xs_s, bm_s, cm_s, dt_s, row_s, row_s, row_s, st_s, xs_s],
        out_specs=[xs_s, small, small, pl.BlockSpec((gp, CHUNK, LANES), lambda g, c: (g, nc - 1 - c, 0)),
                   row_s, row_s, row_s],
        scratch_shapes=[pltpu.VMEM((gp, R, P, N), f32)],
        compiler_params=_cp("arbitrary", "arbitrary"),
    )(xbc, xbc, xbc, dt_raw, dt_bias, a_log, d_skip, states, dy)


WEIGHTS = ["ln_mix_pre", "ln_mix_post", "ln_mem", "ln_xa_pre", "ln_xa_post", "ln_ffn_pre", "ln_ffn_post",
           "xa_wq", "xa_wkv", "xa_wo", "ffn_w_in", "ffn_conv_w", "ffn_conv_b", "ffn_w_out",
           "ssd_w_in", "ssd_conv_w", "ssd_conv_b", "ssd_dt_bias", "ssd_a_log", "ssd_d", "ssd_norm", "ssd_w_out",
           "sg_w_in", "sg_v_norm_g", "sg_v_norm_b", "sg_w_spatial", "sg_b_spatial", "sg_w_out", "sb_w_qkv", "sb_w_out"]
BIG = ["xa_wq", "xa_wkv", "xa_wo", "ffn_w_in", "ffn_w_out", "ssd_w_in", "ssd_w_out", "sg_w_in", "sg_w_out",
       "sb_w_qkv", "sb_w_out"]
SMALL_SHARDED = ["ffn_conv_w", "ssd_conv_w", "ssd_conv_b", "ssd_norm"]
REPLICATED = [n for n in WEIGHTS if n not in BIG and n not in SMALL_SHARDED]


def fn_xpre(x, g):
    return x, _rms(x, g)


PACK_ROW_TILE = 512


def _pack_rows(arrs, lead=0):
    head = arrs[0].shape[:lead]
    flat = jnp.concatenate([a.reshape(head + (-1,)) for a in arrs], axis=-1)
    n = flat.shape[-1]
    rows = -(-n // (SUBLANES * LANES)) * SUBLANES
    if rows > PACK_ROW_TILE:
        rows = -(-rows // PACK_ROW_TILE) * PACK_ROW_TILE
    flat = jnp.pad(flat, [(0, 0)] * lead + [(0, rows * LANES - n)])
    return flat.reshape(head + (rows, LANES))


def _unpack_rows(packed, shapes):
    head = packed.shape[:-2]
    flat = packed.reshape(head + (-1,))
    out, off = [], 0
    for shp in shapes:
        n = int(np.prod(shp, dtype=np.int64))
        out.append(flat[..., off:off + n].reshape(head + tuple(shp)))
        off += n
    return out


def _merge_last(a8):
    return jnp.moveaxis(a8, 0, -2).reshape(a8.shape[1:-1] + (N_DEV * a8.shape[-1],))


def _split_last(a):
    return jnp.moveaxis(a.reshape(a.shape[:-1] + (N_DEV, a.shape[-1] // N_DEV)), -2, 0)


def kernel(x, mem, ln_mix_pre, ln_mix_post, ln_mem, ln_xa_pre, ln_xa_post, ln_ffn_pre, ln_ffn_post, xa_wq, xa_wkv, xa_wo, ffn_w_in, ffn_conv_w, ffn_conv_b, ffn_w_out, ssd_w_in, ssd_conv_w, ssd_conv_b, ssd_dt_bias, ssd_a_log, ssd_d, ssd_norm, ssd_w_out, sg_w_in, sg_v_norm_g, sg_v_norm_b, sg_w_spatial, sg_b_spatial, sg_w_out, sb_w_qkv, sb_w_out, loss_target, m_ln_mix_pre, m_ln_mix_post, m_ln_mem, m_ln_xa_pre, m_ln_xa_post, m_ln_ffn_pre, m_ln_ffn_post, m_xa_wq, m_xa_wkv, m_xa_wo, m_ffn_w_in, m_ffn_conv_w, m_ffn_conv_b, m_ffn_w_out, m_ssd_w_in, m_ssd_conv_w, m_ssd_conv_b, m_ssd_dt_bias, m_ssd_a_log, m_ssd_d, m_ssd_norm, m_ssd_w_out, m_sg_w_in, m_sg_v_norm_g, m_sg_v_norm_b, m_sg_w_spatial, m_sg_b_spatial, m_sg_w_out, m_sb_w_qkv, m_sb_w_out, v_ln_mix_pre, v_ln_mix_post, v_ln_mem, v_ln_xa_pre, v_ln_xa_post, v_ln_ffn_pre, v_ln_ffn_post, v_xa_wq, v_xa_wkv, v_xa_wo, v_ffn_w_in, v_ffn_conv_w, v_ffn_conv_b, v_ffn_w_out, v_ssd_w_in, v_ssd_conv_w, v_ssd_conv_b, v_ssd_dt_bias, v_ssd_a_log, v_ssd_d, v_ssd_norm, v_ssd_w_out, v_sg_w_in, v_sg_v_norm_g, v_sg_v_norm_b, v_sg_w_spatial, v_sg_b_spatial, v_sg_w_out, v_sb_w_qkv, v_sb_w_out):
    p = dict(locals())
    x, mem, target = p["x"][0], p["mem"][0], p["loss_target"][0]
    T, D = x.shape
    depth = ln_mix_pre.shape[0]
    me = 4 * lax.axis_index("x") + 2 * lax.axis_index("y") + lax.axis_index("c")

    def gather(w, name):
        return sc_all_gather(w.astype(bf16), name)

    parts = {n: {} for n in BIG}
    pending, due = [], {}

    def scatter(name, layer, g8):
        pending.append((name, layer, g8))

    def deadline(name, s):
        if name == "ffn_w_out":
            return s - 2, 0
        if name == "ffn_w_in":
            return s - 2, 1
        if name.startswith("xa_"):
            return (s - 3, 0) if s >= 3 else (0, 1)
        return (s - 1, 0) if name.endswith("_out") else (s - 2, 0)

    def ship_pending(carry, s):
        carry, *held = lax.optimization_barrier((carry, *[g8 for _, _, g8 in pending]))
        for (name, layer, _), g8 in zip(pending, held):
            parts[name][layer] = sc_scatter_parts(g8, "rs_" + name)
            due.setdefault(deadline(name, s), []).append((name, layer))
        pending.clear()
        return carry

    def collect_due(carry, s):
        names = due.pop(s, [])
        if names:
            carry, *landed = lax.optimization_barrier((carry, *[parts[n][l] for n, l in names]))
            for (n, l), a in zip(names, landed):
                parts[n][l] = a
        return carry

    small8 = all_gather(_pack_rows([p[n] for n in SMALL_SHARDED]), "ag_small")
    full = {n: _merge_last(a) for n, a in zip(SMALL_SHARDED, _unpack_rows(small8, [p[n].shape for n in SMALL_SHARDED]))}

    grads = {n: {} for n in WEIGHTS}

    def ssd(h, j):
        g_in = gather(p["ssd_w_in"][j], "ag_ssd_in")
        g_in, h = lax.optimization_barrier((g_in, h))
        w_full = jnp.moveaxis(g_in, 0, 1).reshape(D, -1)
        w_out = gather(p["ssd_w_out"][j], "ag_ssd_out").reshape(1, -1, D)
        d_inner, conv_dim, heads = w_out.shape[1], full["ssd_conv_w"].shape[-1], ssd_dt_bias.shape[1]
        w_z, w_x = w_full[None, :, :d_inner], w_full[None, :, d_inner:d_inner + conv_dim]
        w_dt = jnp.pad(w_full[:, d_inner + conv_dim:], ((0, 0), (0, LANES - heads)))[None]
        lane_row = lambda a: jnp.pad(a[j:j + 1], ((0, 0), (0, LANES - heads)))
        bias, a_log, d_skip = lane_row(ssd_dt_bias), lane_row(ssd_a_log), lane_row(ssd_d)
        norm_g = full["ssd_norm"][j:j + 1]
        z = mm_nn(h, w_z, f32, "ssd_in_z")
        xbc_pre = mm_nn(h, w_x, f32, "ssd_in_x")
        dt_raw = mm_nn(h, w_dt, f32, "ssd_in_dt")
        streams = [(xbc_pre, 0, full["ssd_conv_w"][j], full["ssd_conv_b"][j:j + 1], 0)]
        xbc = conv_fwd(streams, epi_silu, conv_dim, f32, 4, "ssd_conv")
        y, states = ssd_fwd(xbc, dt_raw, bias, a_log, d_skip, "ssd_core")
        gated = row_fwd(fn_gate_norm, [y, z], [norm_g], [(d_inner, bf16)], "ssd_gate")[0]
        out = mm_nn(gated, w_out, f32, "ssd_out")

        def bwd(d_out, mid, ship):
            d_gated = mm_nt(d_out, w_out, f32, "ssd_out_dx")
            scatter("ssd_w_out", j, mm_tn(gated, d_out, D, bf16, "ssd_out_dw").reshape(N_DEV, -1, D))
            d_gated = ship(d_gated)
            dy, dz, d_norm = row_bwd(fn_gate_norm, [y, z], [norm_g], [d_gated], [f32, bf16], "ssd_gate_bwd")
            dxs, dbm, dcm, ddt_g, d_bias, d_alog, d_skipg = ssd_bwd(xbc, dt_raw, bias, a_log, d_skip, states, dy, "ssd_core_bwd")
            dxs = mid(dxs)
            (dx_pre,), (d_cw,), (d_cb,) = conv_bwd(streams, epi_silu, [dxs, dbm, dcm], 4, "ssd_conv_bwd")
            ddt = jnp.sum(ddt_g, axis=0).astype(bf16)
            dh = mm_nt(dz, w_z, f32, "ssd_in_z_dx")
            dh = mm_nt(dx_pre, w_x, f32, "ssd_in_x_dx", init=dh)
            dh = mm_nt(ddt, w_dt, f32, "ssd_in_dt_dx", init=dh)
            dw = jnp.concatenate([mm_tn(h, dz, d_inner, bf16, "ssd_in_z_dw")[0], mm_tn(h, dx_pre, conv_dim, bf16, "ssd_in_x_dw")[0],
                                  mm_tn(h, ddt, LANES, bf16, "ssd_in_dt_dw")[0][:, :heads]], axis=1)
            scatter("ssd_w_in", j, _split_last(dw))
            grads["ssd_conv_w"][j], grads["ssd_conv_b"][j], grads["ssd_norm"][j] = d_cw, d_cb[0], d_norm[0]
            grads["ssd_dt_bias"][j], grads["ssd_a_log"][j], grads["ssd_d"][j] = d_bias[0, :heads], d_alog[0, :heads], d_skipg[0, :heads]
            return dh

        return out, bwd

    def sgu(h, j):
        w_in = gather(p["sg_w_in"][j], "ag_sg_in")
        w_out = gather(p["sg_w_out"][j], "ag_sg_out").reshape(1, -1, D)
        norm_g, norm_b = sg_v_norm_g[j:j + 1], sg_v_norm_b[j:j + 1]
        w_sp, b_sp = sg_w_spatial[j], sg_b_spatial[j][..., None]
        uv = mm_nn(h, w_in, f32, "sg_in")
        gated = sgu_fwd(uv, norm_g, norm_b, w_sp, b_sp, "sg_core")
        out = mm_nn(gated, w_out, f32, "sg_out")

        def bwd(d_out, mid, ship):
            d_gated = mm_nt(d_out, w_out, f32, "sg_out_dx")
            scatter("sg_w_out", j, mm_tn(gated, d_out, D, bf16, "sg_out_dw").reshape(N_DEV, -1, D))
            d_gated = ship(d_gated)
            duv, d_ng, d_nb, d_ws, d_bs = sgu_bwd(uv, norm_g, norm_b, w_sp, b_sp, d_gated, "sg_core_bwd")
            duv = mid(duv)
            grads["sg_v_norm_g"][j], grads["sg_v_norm_b"][j] = d_ng[0], d_nb[0]
            grads["sg_w_spatial"][j], grads["sg_b_spatial"][j] = d_ws, d_bs[..., 0]
            scatter("sg_w_in", j, mm_tn(h, duv, w_in.shape[2], bf16, "sg_in_dw"))
            return mm_nt(duv, w_in, f32, "sg_in_dx")

        return out, bwd

    def stick(h, j):
        w_qkv = gather(p["sb_w_qkv"][j], "ag_sb_qkv")
        w_out = gather(p["sb_w_out"][j], "ag_sb_out").reshape(1, -1, D)
        qkv = mm_nn(h, w_qkv, bf16, "sb_qkv")
        o, tot = sb_fwd(qkv, "sb_core")
        out = mm_nn(o, w_out, f32, "sb_out")

        def bwd(d_out, mid, ship):
            d_o = mm_nt(d_out, w_out, f32, "sb_out_dx")
            scatter("sb_w_out", j, mm_tn(o, d_out, D, bf16, "sb_out_dw").reshape(N_DEV, -1, D))
            d_o = ship(d_o)
            dq, dk, dv = sb_bwd(qkv, tot, d_o, "sb_core_bwd")
            dq = mid(dq)
            dqkv = jnp.concatenate([dq, dk.astype(bf16), dv.astype(bf16)], axis=1)
            scatter("sb_w_qkv", j, mm_tn(h, dqkv, w_qkv.shape[2], bf16, "sb_qkv_dw"))
            return mm_nt(dqkv, w_qkv, f32, "sb_qkv_dx")

        return out, bwd

    def cross(h, i):
        gain = ln_mem[i:i + 1]
        mem_n = row_fwd(fn_prenorm, [mem], [gain], [(D, bf16)], "mem_norm")[0]
        w_q = gather(p["xa_wq"][i], "ag_xa_q").reshape(1, D, -1)
        w_kv = gather(p["xa_wkv"][i], "ag_xa_kv").reshape(1, D, -1)
        w_o = gather(p["xa_wo"][i], "ag_xa_o")
        q = mm_nn(h, w_q, bf16, "xa_q")
        kv = mm_nn(mem_n, w_kv, bf16, "xa_kv")
        o = xa_fwd(q, kv, "xa_core")
        out = mm_nn(o, w_o, f32, "xa_out")

        def bwd(d_out, mid, ship):
            d_o = mm_nt(d_out, w_o, f32, "xa_out_dx")
            scatter("xa_wo", i, mm_tn(o, d_out, w_o.shape[2], bf16, "xa_out_dw"))
            dq, dk, dv = xa_bwd(q, kv, d_o, "xa_core_bwd")
            dkv = jnp.concatenate([dk, dv], axis=1).astype(bf16)
            scatter("xa_wq", i, mm_tn(h, dq, w_q.shape[2], bf16, "xa_q_dw").reshape(N_DEV, -1, w_q.shape[2]))
            scatter("xa_wkv", i, mm_tn(mem_n, dkv, w_kv.shape[2], bf16, "xa_kv_dw").reshape(N_DEV, -1, w_kv.shape[2]))
            d_mem_n = mm_nt(dkv, w_kv, f32, "xa_kv_dx")
            grads["ln_mem"][i] = row_bwd(fn_prenorm, [mem], [gain], [d_mem_n], [None], "mem_norm_bwd")[0][0]
            return mm_nt(dq, w_q, f32, "xa_q_dx")

        return out, bwd

    def ffn(h, i):
        w_in = gather(p["ffn_w_in"][i], "ag_ffn_in")
        w_out = gather(p["ffn_w_out"][i], "ag_ffn_out").reshape(1, -1, D)
        width = w_out.shape[1]
        conv_w, conv_b = full["ffn_conv_w"][i], ffn_conv_b[i:i + 1]
        gu = mm_nn(h, w_in, f32, "ffn_in")
        streams = [(gu, 0, conv_w, conv_b, 0), (gu, width // LANES, conv_w, conv_b, width // LANES)]
        act = conv_fwd(streams, epi_gelu_gate, width, bf16, 3, "ffn_gate")
        out = mm_nn(act, w_out, f32, "ffn_out")

        def bwd(d_out, mid, ship):
            d_act = mm_nt(d_out, w_out, f32, "ffn_out_dx")
            scatter("ffn_w_out", i, mm_tn(act, d_out, D, bf16, "ffn_out_dw").reshape(N_DEV, -1, D))
            d_pre, d_cw, d_cb = conv_bwd(streams, epi_gelu_gate, d_act, 3, "ffn_gate_bwd")
            dgu = jnp.concatenate(d_pre, axis=1)
            grads["ffn_conv_w"][i], grads["ffn_conv_b"][i] = jnp.concatenate(d_cw, axis=1), jnp.concatenate(d_cb, axis=1)[0]
            scatter("ffn_w_in", i, mm_tn(h, dgu, w_in.shape[2], bf16, "ffn_in_dw"))
            return mm_nt(dgu, w_in, f32, "ffn_in_dx")

        return out, bwd

    n_sub = 3 * depth
    pre = [w[i:i + 1] for i in range(depth) for w in (ln_mix_pre, ln_xa_pre, ln_ffn_pre)]
    post = [w[i:i + 1] for i in range(depth) for w in (ln_mix_post, ln_xa_post, ln_ffn_post)]
    stream, outs, bwds = [x], [], []
    h = row_fwd(fn_prenorm, [x], [pre[0]], [(D, bf16)], "pre_norm")[0]
    for s in range(n_sub):
        i, t = divmod(s, 3)
        out, bwd = ((ssd, sgu, stick)[i % 3](h, i // 3) if t == 0 else cross(h, i) if t == 1 else ffn(h, i))
        outs.append(out)
        bwds.append(bwd)
        if s < n_sub - 1:
            x_new, h = row_fwd(fn_resnorm, [stream[s], out], [post[s], pre[s + 1]], [(D, f32), (D, bf16)], "res_norm")
            stream.append(x_new)
    dy, loss = row_fwd(fn_final, [stream[-1], outs[-1], target], [post[-1]], [(D, f32), (1, f32)], "loss_head", n_acc=1)
    loss = lax.psum(loss[0, 0], ("x", "y", "c"))

    d_pre, d_post = [None] * n_sub, [None] * n_sub
    dx, d_out, d_post[-1] = row_bwd(fn_res, [stream[-1], outs[-1]], [post[-1]], [dy], [f32, bf16], "res_bwd")
    for s in reversed(range(n_sub)):
        dh = bwds[s](collect_due(d_out, (s, 0)), functools.partial(collect_due, s=(s, 1)), functools.partial(ship_pending, s=s))
        dh = ship_pending(dh, s)
        if s > 0:
            dx, d_out, d_post[s - 1], d_pre[s] = row_bwd(
                fn_resnorm, [stream[s - 1], outs[s - 1]], [post[s - 1], pre[s]], [dx, dh], [f32, bf16], "res_norm_bwd")
        else:
            grad_x, d_pre[0] = row_bwd(fn_xpre, [x], [pre[0]], [dx, dh], [f32], "pre_norm_bwd")
    for t, kind in enumerate(("mix", "xa", "ffn")):
        for i in range(depth):
            grads["ln_%s_pre" % kind][i] = d_pre[3 * i + t][0]
            grads["ln_%s_post" % kind][i] = d_post[3 * i + t][0]

    def stacked(name):
        return jnp.stack([grads[name][l] for l in range(len(grads[name]))], axis=0)

    new = {}
    for name in BIG:
        new[name] = adamw_sharded(p[name], [parts[name][l] for l in range(len(parts[name]))],
                                  p["m_" + name], p["v_" + name], "adamw_" + name)

    rep_shapes = [p[n].shape for n in REPLICATED]
    g8 = all_gather(_pack_rows([stacked(n) for n in REPLICATED]), "ag_grad_rep")
    rep = adamw_summed8(_pack_rows([p[n] for n in REPLICATED]), g8, _pack_rows([p["m_" + n] for n in REPLICATED]),
                        _pack_rows([p["v_" + n] for n in REPLICATED]), "adamw_rep")
    for k, packed in enumerate(rep):
        for n, a in zip(REPLICATED, _unpack_rows(packed, rep_shapes)):
            new.setdefault(n, [None] * 4)[k] = a

    sh_shapes = [p[n].shape for n in SMALL_SHARDED]
    by_owner = _pack_rows([_split_last(stacked(n)) for n in SMALL_SHARDED], lead=1)
    mine8 = lax.dynamic_index_in_dim(all_gather(by_owner, "ag_grad_small"), me, axis=1, keepdims=False)
    sh = adamw_summed8(_pack_rows([p[n] for n in SMALL_SHARDED]), mine8, _pack_rows([p["m_" + n] for n in SMALL_SHARDED]),
                       _pack_rows([p["v_" + n] for n in SMALL_SHARDED]), "adamw_small")
    for k, packed in enumerate(sh):
        for n, a in zip(SMALL_SHARDED, _unpack_rows(packed, sh_shapes)):
            new.setdefault(n, [None] * 4)[k] = a

    return (loss, grad_x[None], *[new[n][0] for n in WEIGHTS], *[new[n][1] for n in WEIGHTS],
            *[new[n][2] for n in WEIGHTS], *[new[n][3] for n in WEIGHTS])
```

```python
import functools

import jax
import jax.numpy as jnp
import numpy as np
from jax import lax
from jax.experimental import pallas as pl
from jax.experimental.pallas import tpu as pltpu
from jax.experimental.pallas import tpu_sc as plsc

f32 = jnp.float32
bf16 = jnp.bfloat16
HIGHEST = lax.Precision.HIGHEST
MESH = pl.DeviceIdType.MESH

V7X_VMEM_BYTES = 64 * 1024 * 1024
VMEM_LIMIT = V7X_VMEM_BYTES * 3 // 4
LANES = 128
SUBLANES = 8
BF16_ROWS = 16

EPS = 1e-6
ADAM_LR = 0.001
ADAM_B1 = 0.9
ADAM_B2 = 0.999
ADAM_EPS = 1e-08
ADAM_WD = 0.01
ADAM_STEP = 10

N_DEV = 8
XA_HEADS = 4
SSD_HEADS_PER_GROUP = 8
SSD_HEAD_DIM = 64
SSD_STATE = 128
SSD_GROUPS = 8
CHUNK = 128


def _cp(*sem):
    return pltpu.CompilerParams(dimension_semantics=sem or None, vmem_limit_bytes=VMEM_LIMIT)


def _dot(a, b, dims, precision=None):
    return lax.dot_general(a, b, (dims, ((), ())), precision=precision, preferred_element_type=f32)


def _nn(a, b, precision=None):
    return _dot(a, b, ((1,), (0,)), precision)


def _nt(a, b, precision=None):
    return _dot(a, b, ((1,), (1,)), precision)


def _tn(a, b, precision=None):
    return _dot(a, b, ((0,), (0,)), precision)


def _largest_tile(n, cap, step):
    for t in range(min(n, cap) // step * step, 0, -step):
        if n % t == 0:
            return t
    return n


def _gather_partners():
    mx, my, mc = lax.axis_index("x"), lax.axis_index("y"), lax.axis_index("c")
    chips = [(1 - mx, my), (mx, 1 - my), (1 - mx, 1 - my)]
    return (mx, my, mc), (mx, my, 1 - mc), chips


def _two_level_gather(x_ref, out_ref, send_sems, recv_sems, local_sem):
    me, sibling, chips = _gather_partners()
    mc = me[2]

    def slot(px, py, pc):
        return out_ref.at[4 * px + 2 * py + pc]

    def copy(k, block, to, src=None):
        return pltpu.make_async_remote_copy(
            src_ref=slot(*block) if src is None else src, dst_ref=slot(*block),
            send_sem=send_sems.at[k], recv_sem=recv_sems.at[k], device_id=to, device_id_type=MESH)

    mine = pltpu.make_async_copy(x_ref, slot(*me), local_sem)
    mine.start()
    first = [copy(0, me, sibling, src=x_ref)]
    first += [copy(1 + j, me, (*chip, mc), src=x_ref) for j, chip in enumerate(chips)]
    for cp in first:
        cp.start()
    passed = [copy(4 + j, (*chip, mc), sibling) for j, chip in enumerate(chips)]
    for j, chip in enumerate(chips):
        copy(1 + j, (*chip, mc), me).wait_recv()
        passed[j].start()
    copy(0, sibling, me).wait_recv()
    for j, chip in enumerate(chips):
        copy(4 + j, (*chip, 1 - mc), me).wait_recv()
    for cp in first + passed:
        cp.wait_send()
    mine.wait()


def all_gather(x, name):
    return pl.pallas_call(
        functools.partial(_two_level_gather), name=name,
        out_shape=jax.ShapeDtypeStruct((N_DEV,) + x.shape, x.dtype),
        in_specs=[pl.BlockSpec(memory_space=pl.ANY)],
        out_specs=pl.BlockSpec(memory_space=pl.ANY),
        scratch_shapes=[pltpu.SemaphoreType.DMA((7,)), pltpu.SemaphoreType.DMA((7,)), pltpu.SemaphoreType.DMA(())],
    )(x)


AG_COLLECTIVE_ID = 1
RS_COLLECTIVE_ID = 2
_SEQUENCER = dict(axis_name="sequencer", num_cores=1)
_HBM = pltpu.MemorySpace.HBM


def _peers():
    mx, my, mc = lax.axis_index("x"), lax.axis_index("y"), lax.axis_index("c")
    return [((mx + (k >> 2)) % 2, (my + ((k >> 1) & 1)) % 2, (mc + (k & 1)) % 2) for k in range(1, N_DEV)]


def _handshake_all(peers):
    barrier = pltpu.get_barrier_semaphore()
    for peer in peers:
        pl.semaphore_signal(barrier, inc=1, device_id=peer, device_id_type=MESH)
    pl.semaphore_wait(barrier, len(peers))


def _block_of(peer):
    return 4 * peer[0] + 2 * peer[1] + peer[2]


def sc_all_gather(x, name):
    x_ref = jax.new_ref(x, memory_space=_HBM)
    out_ref = jax.empty_ref(jax.ShapeDtypeStruct((N_DEV,) + x.shape, x.dtype), memory_space=_HBM)

    @pl.kernel(mesh=plsc.ScalarSubcoreMesh(**_SEQUENCER), name=name,
               scratch_types=(pltpu.SemaphoreType.DMA((N_DEV - 1,)), pltpu.SemaphoreType.DMA((N_DEV - 1,)),
                              pltpu.SemaphoreType.DMA(())),
               compiler_params=pltpu.CompilerParams(collective_id=AG_COLLECTIVE_ID))
    def launch(send_sems, recv_sems, local_sem):
        _, sibling, chips = _gather_partners()
        _handshake_all([sibling] + [(*chip, lax.axis_index("c")) for chip in chips])
        _two_level_gather(x_ref, out_ref, send_sems, recv_sems, local_sem)

    launch()
    return out_ref[...]


def sc_scatter_parts(g8, name):
    g_ref = jax.new_ref(g8, memory_space=_HBM)
    parts_ref = jax.empty_ref(jax.ShapeDtypeStruct(g8.shape, g8.dtype), memory_space=_HBM)

    @pl.kernel(mesh=plsc.ScalarSubcoreMesh(**_SEQUENCER), name=name,
               scratch_types=(pltpu.SemaphoreType.DMA((N_DEV - 1,)), pltpu.SemaphoreType.DMA((N_DEV - 1,)),
                              pltpu.SemaphoreType.DMA(())),
               compiler_params=pltpu.CompilerParams(collective_id=RS_COLLECTIVE_ID))
    def launch(send_sems, recv_sems, local_sem):
        peers = _peers()
        _handshake_all(peers)
        me = 4 * lax.axis_index("x") + 2 * lax.axis_index("y") + lax.axis_index("c")
        mine = pltpu.make_async_copy(g_ref.at[me], parts_ref.at[0], local_sem)
        mine.start()
        copies = [pltpu.make_async_remote_copy(src_ref=g_ref.at[_block_of(peer)], dst_ref=parts_ref.at[k + 1],
                                               send_sem=send_sems.at[k], recv_sem=recv_sems.at[k],
                                               device_id=peer, device_id_type=MESH) for k, peer in enumerate(peers)]
        for cp in copies:
            cp.start()
        for cp in copies:
            cp.wait()
        mine.wait()

    launch()
    return parts_ref[...]


def _as_lrc(a, lead):
    rest = a.shape[lead:]
    return a.reshape(a.shape[:lead] + (int(np.prod(rest[:-2], dtype=np.int64)),) + rest[-2:])


def _adam_math(w, g, m, v):
    m = ADAM_B1 * m + (1.0 - ADAM_B1) * g
    v = ADAM_B2 * v + (1.0 - ADAM_B2) * jnp.square(g)
    m_hat = m / (1.0 - ADAM_B1 ** ADAM_STEP)
    v_hat = v / (1.0 - ADAM_B2 ** ADAM_STEP)
    delta = -ADAM_LR * (m_hat / (jnp.sqrt(v_hat) + ADAM_EPS) + ADAM_WD * w)
    return delta, m, v


def adamw_sharded(w, parts, m, v, name):
    L, R, C = w.shape
    row_bytes = 2 * C * (L * N_DEV * 2 + 7 * 4)
    tr = next((t for t in _tiles_desc(R, 128, BF16_ROWS) if t * row_bytes <= MM_VMEM_BUDGET), BF16_ROWS)

    def body(w_ref, *refs):
        p_refs, (m_ref, v_ref, g_out, d_out, m_out, v_out) = refs[:L], refs[L:]
        for layer in range(L):
            @pl.when(pl.program_id(0) == layer)
            def _(p_ref=p_refs[layer]):
                g = p_ref[0].astype(f32)
                for k in range(1, N_DEV):
                    g = g + p_ref[k].astype(f32)
                delta, mn, vn = _adam_math(w_ref[...], g, m_ref[...], v_ref[...])
                g_out[...] = g
                d_out[...] = delta
                m_out[...] = mn
                v_out[...] = vn

    blk = pl.BlockSpec((None, tr, C), lambda l, r: (l, r, 0))
    part_specs = [pl.BlockSpec((N_DEV, tr, C), functools.partial(lambda layer, l, r: (0, jnp.where(l == layer, r, 0), 0), layer))
                  for layer in range(L)]
    return pl.pallas_call(
        body, name=name, out_shape=[jax.ShapeDtypeStruct(w.shape, f32)] * 4, grid=(L, R // tr),
        in_specs=[blk] + part_specs + [blk, blk], out_specs=[blk] * 4,
        compiler_params=_cp("arbitrary", "arbitrary"),
    )(w, *parts, m, v)


def adamw_summed8(w, g8, m, v, name):
    R, C = w.shape
    tr = _largest_tile(R, 512, SUBLANES)

    def body(w_ref, g_ref, m_ref, v_ref, g_out, d_out, m_out, v_out):
        g = g_ref[0]
        for d in range(1, N_DEV):
            g = g + g_ref[d]
        delta, mn, vn = _adam_math(w_ref[...], g, m_ref[...], v_ref[...])
        g_out[...] = g
        d_out[...] = delta
        m_out[...] = mn
        v_out[...] = vn

    blk = pl.BlockSpec((tr, C), lambda r: (r, 0))
    return pl.pallas_call(
        body, name=name, out_shape=[jax.ShapeDtypeStruct((R, C), f32)] * 4, grid=(R // tr,),
        in_specs=[blk, pl.BlockSpec((N_DEV, tr, C), lambda r: (0, r, 0)), blk, blk], out_specs=[blk] * 4,
        compiler_params=_cp("parallel"),
    )(w, g8, m, v)


def _tile_n(ns):
    if ns % 512 == 0:
        return 512
    if ns <= 1536:
        return ns
    return _largest_tile(ns, 512, LANES)


MM_VMEM_BUDGET = VMEM_LIMIT - 8 * 1024 * 1024


def _tiles_desc(n, cap, step):
    return [t for t in range(min(n, cap) // step * step, 0, -step) if n % t == 0] or [n]


def _mm_fits(in_tiles, out_tile, out_dtype, n_red, extra=0):
    rows, cols = out_tile
    total = sum(2 * 2 * r * c for r, c in in_tiles) + 2 * rows * cols * jnp.dtype(out_dtype).itemsize + extra
    if n_red > 1:
        total += 4 * rows * cols
    return total <= MM_VMEM_BUDGET


def _reduce_into(o_ref, acc, part, first, last):
    if acc is None:
        o_ref[...] = part().astype(o_ref.dtype)
        return

    @pl.when(first)
    def _():
        acc[...] = jnp.zeros_like(acc)

    acc[...] += part()

    @pl.when(last)
    def _():
        o_ref[...] = acc[...].astype(o_ref.dtype)


def mm_nn(a, w3, out_dtype, name):
    M, K = a.shape
    J, _, Ns = w3.shape
    tn = _tile_n(Ns)
    tm, tk = next(((tm, tk) for tk in _tiles_desc(K, 4096, LANES) for tm in _tiles_desc(M, 1024, BF16_ROWS)
                   if tm >= min(M, 256) and _mm_fits([(tm, tk), (tk, tn)], (tm, tn), out_dtype, K // tk)),
                  (min(M, 256), _largest_tile(K, 512, LANES)))
    nn, nk = Ns // tn, K // tk

    def body(a_ref, w_ref, o_ref, *acc):
        k = pl.program_id(3)
        _reduce_into(o_ref, acc[0] if acc else None, lambda: _nn(a_ref[...], w_ref[...]), k == 0, k == nk - 1)

    return pl.pallas_call(
        body, name=name, out_shape=jax.ShapeDtypeStruct((M, J * Ns), out_dtype),
        grid=(M // tm, J, nn, nk),
        in_specs=[pl.BlockSpec((tm, tk), lambda i, j, n, k: (i, k)),
                  pl.BlockSpec((None, tk, tn), lambda i, j, n, k: (j, k, n))],
        out_specs=pl.BlockSpec((tm, tn), lambda i, j, n, k: (i, j * nn + n)),
        scratch_shapes=[pltpu.VMEM((tm, tn), f32)] if nk > 1 else [],
        compiler_params=_cp("parallel", "parallel", "parallel", "arbitrary"),
    )(a, w3)


MM_NT_REDUCE_CAP = 5632


def mm_nt(g, w3, out_dtype, name, init=None):
    M = g.shape[0]
    J, K, Ns = w3.shape
    if Ns <= MM_NT_REDUCE_CAP:
        tn, jb = Ns, max(b for b in range(1, J + 1) if J % b == 0 and (b == 1 or b * Ns <= MM_NT_REDUCE_CAP))
    else:
        tn, jb = _largest_tile(Ns, MM_NT_REDUCE_CAP, LANES), 1
    nj, nn = J // jb, Ns // tn
    n_red = nj * nn + (init is not None)
    tk = _largest_tile(K, 512, LANES)
    tm = next((tm for tm in _tiles_desc(M, 1024, BF16_ROWS)
               if _mm_fits([(tm, jb * tn), (jb * tk, tn)], (tm, tk), out_dtype, n_red,
                           extra=0 if init is None else 2 * 4 * tm * tk)), min(M, 256))

    def body(*refs):
        g_ref, w_ref = refs[:2]
        i_ref = None if init is None else refs[2]
        o_ref = refs[2 + (init is not None)]
        acc = refs[3 + (init is not None):]
        j, n = pl.program_id(2), pl.program_id(3)
        first = (j == 0) & (n == 0)

        def part():
            prod = _nt(g_ref[:, :tn], w_ref[0])
            for b in range(1, jb):
                prod = prod + _nt(g_ref[:, b * tn:(b + 1) * tn], w_ref[b])
            return prod if init is None else prod + jnp.where(first, i_ref[...].astype(f32), 0.0)

        _reduce_into(o_ref, acc[0] if acc else None, part, first, (j == nj - 1) & (n == nn - 1))

    in_specs = [pl.BlockSpec((tm, jb * tn), lambda i, k, j, n: (i, j * nn + n)),
                pl.BlockSpec((jb, tk, tn), lambda i, k, j, n: (j, k, n))]
    args = [g, w3]
    if init is not None:
        in_specs.append(pl.BlockSpec((tm, tk), lambda i, k, j, n: (i, k)))
        args.append(init)
    return pl.pallas_call(
        body, name=name, out_shape=jax.ShapeDtypeStruct((M, K), out_dtype),
        grid=(M // tm, K // tk, nj, nn), in_specs=in_specs,
        out_specs=pl.BlockSpec((tm, tk), lambda i, k, j, n: (i, k)),
        scratch_shapes=[pltpu.VMEM((tm, tk), f32)] if n_red > 1 else [],
        compiler_params=_cp("parallel", "parallel", "arbitrary", "arbitrary"),
    )(*args)


def mm_tn(a, g, ns, out_dtype, name):
    M, K = a.shape
    J = g.shape[1] // ns
    tn, tk = _tile_n(ns), _largest_tile(K, 512, LANES)
    tm = next((tm for tm in _tiles_desc(M, 4096, BF16_ROWS)
               if _mm_fits([(tm, tk), (tm, tn)], (tk, tn), out_dtype, M // tm)), _largest_tile(M, 512, BF16_ROWS))
    nn, nm = ns // tn, M // tm

    def body(a_ref, g_ref, o_ref, *acc):
        m = pl.program_id(3)
        _reduce_into(o_ref, acc[0] if acc else None, lambda: _tn(a_ref[...], g_ref[...]), m == 0, m == nm - 1)

    return pl.pallas_call(
        body, name=name, out_shape=jax.ShapeDtypeStruct((J, K, ns), out_dtype),
        grid=(J, K // tk, nn, nm),
        in_specs=[pl.BlockSpec((tm, tk), lambda j, k, n, m: (m, k)),
                  pl.BlockSpec((tm, tn), lambda j, k, n, m: (m, j * nn + n))],
        out_specs=pl.BlockSpec((None, tk, tn), lambda j, k, n, m: (j, k, n)),
        scratch_shapes=[pltpu.VMEM((tk, tn), f32)] if nm > 1 else [],
        compiler_params=_cp("parallel", "parallel", "parallel", "arbitrary"),
    )(a, g)


ROW_TILE = 256
ROW_STEP = 64


def row_fwd(fn, rows, consts, outs, name, n_acc=0):
    T = rows[0].shape[0]
    tr = min(T, ROW_TILE)
    n_rows, n_consts, n_row_out = len(rows), len(consts), len(outs) - n_acc

    def body(*refs):
        r_refs, c_refs, o_refs = refs[:n_rows], refs[n_rows:n_rows + n_consts], refs[n_rows + n_consts:]
        cs = [c[...] for c in c_refs]
        acc_refs = o_refs[n_row_out:]
        if n_acc:
            @pl.when(pl.program_id(0) == 0)
            def _():
                for a in acc_refs:
                    a[...] = jnp.zeros_like(a)

        def step(s, carry):
            rows_s = pl.ds(pl.multiple_of(s * ROW_STEP, ROW_STEP), ROW_STEP)
            res = fn(*[r[rows_s, :].astype(f32) for r in r_refs], *cs)
            for o, val in zip(o_refs[:n_row_out], res[:n_row_out]):
                o[rows_s, :] = val.astype(o.dtype)
            return tuple(c + val for c, val in zip(carry, res[n_row_out:]))

        accs = lax.fori_loop(0, tr // ROW_STEP, step, tuple(jnp.zeros((1, c), f32) for c, _ in outs[n_row_out:]))
        for a, val in zip(acc_refs, accs):
            a[...] += val

    in_specs = [pl.BlockSpec((tr, r.shape[1]), lambda i: (i, 0)) for r in rows]
    in_specs += [pl.BlockSpec(c.shape, lambda i: (0, 0)) for c in consts]
    out_shape = [jax.ShapeDtypeStruct((T, c), dt) for c, dt in outs[:n_row_out]]
    out_shape += [jax.ShapeDtypeStruct((1, c), f32) for c, _ in outs[n_row_out:]]
    out_specs = [pl.BlockSpec((tr, c), lambda i: (i, 0)) for c, _ in outs[:n_row_out]]
    out_specs += [pl.BlockSpec((1, c), lambda i: (0, 0)) for c, _ in outs[n_row_out:]]
    return pl.pallas_call(
        body, name=name, out_shape=out_shape, grid=(T // tr,), in_specs=in_specs, out_specs=out_specs,
        compiler_params=_cp("arbitrary" if n_acc else "parallel"),
    )(*rows, *consts)


def row_bwd(fn, rows, consts, cts, grad_dtypes, name):
    T = rows[0].shape[0]
    tr = min(T, ROW_TILE)
    n_rows, n_consts, n_cts = len(rows), len(consts), len(cts)
    wanted = [i for i, dt in enumerate(grad_dtypes) if dt is not None]

    def body(*refs):
        r_refs = refs[:n_rows]
        c_refs = refs[n_rows:n_rows + n_consts]
        t_refs = refs[n_rows + n_consts:n_rows + n_consts + n_cts]
        o_refs = refs[n_rows + n_consts + n_cts:]
        gr_refs, gc_refs = o_refs[:len(wanted)], o_refs[len(wanted):]
        cs = [c[...] for c in c_refs]

        @pl.when(pl.program_id(0) == 0)
        def _():
            for a in gc_refs:
                a[...] = jnp.zeros_like(a)

        def step(s, carry):
            rows_s = pl.ds(pl.multiple_of(s * ROW_STEP, ROW_STEP), ROW_STEP)
            ins = [r[rows_s, :].astype(f32) for r in r_refs]
            _, vjp = jax.vjp(lambda *a: tuple(fn(*a)), *ins, *cs)
            grads = vjp(tuple(t[rows_s, :].astype(f32) for t in t_refs))
            for o, i in zip(gr_refs, wanted):
                o[rows_s, :] = grads[i].astype(o.dtype)
            return tuple(c + gval for c, gval in zip(carry, grads[n_rows:]))

        accs = lax.fori_loop(0, tr // ROW_STEP, step, tuple(jnp.zeros(c.shape, f32) for c in consts))
        for a, val in zip(gc_refs, accs):
            a[...] += val

    in_specs = [pl.BlockSpec((tr, r.shape[1]), lambda i: (i, 0)) for r in list(rows) + list(cts)]
    in_specs[n_rows:n_rows] = [pl.BlockSpec(c.shape, lambda i: (0, 0)) for c in consts]
    out_shape = [jax.ShapeDtypeStruct(rows[i].shape, grad_dtypes[i]) for i in wanted]
    out_shape += [jax.ShapeDtypeStruct(c.shape, f32) for c in consts]
    out_specs = [pl.BlockSpec((tr, rows[i].shape[1]), lambda i_: (i_, 0)) for i in wanted]
    out_specs += [pl.BlockSpec(c.shape, lambda i: (0, 0)) for c in consts]
    return pl.pallas_call(
        body, name=name, out_shape=out_shape, grid=(T // tr,), in_specs=in_specs, out_specs=out_specs,
        compiler_params=_cp("arbitrary"),
    )(*rows, *consts, *cts)


def _rms(x, g):
    return x * lax.rsqrt(jnp.mean(x * x, axis=-1, keepdims=True) + EPS) * g


def fn_prenorm(x, g):
    return (_rms(x, g),)


def fn_resnorm(x, m, g_post, g_pre):
    x_new = x + _rms(m, g_post)
    return x_new, _rms(x_new, g_pre)


def fn_res(x, m, g_post):
    return (x + _rms(m, g_post),)


def fn_final(x, m, target, g_post):
    err = x + _rms(m, g_post) - target
    n = err.shape[-1]
    return err / n, (0.5 / n) * jnp.sum(jnp.sum(err * err, axis=1, keepdims=True), axis=0, keepdims=True)


def fn_gate_norm(y, z, g):
    return (_rms(y * jax.nn.silu(z), g),)


CONV_ROWS = 256


def _conv_chunk(pre_ref, w, b, r, rb, K):
    t0 = pl.multiple_of(r * rb, rb)
    halo_at = pl.multiple_of(jnp.maximum(t0 - SUBLANES, 0), SUBLANES)
    halo = jnp.where(r > 0, pre_ref[pl.ds(halo_at, SUBLANES), :], 0.0)
    main = pre_ref[pl.ds(t0, rb), :]
    ext = jnp.concatenate([halo, main], axis=0)
    shifted = [main if k == K - 1 else pltpu.roll(ext, K - 1 - k, 0)[SUBLANES:] for k in range(K)]
    conv = b
    for k in range(K):
        conv = conv + w[k:k + 1] * shifted[k]
    return conv, shifted


def conv_fwd(streams, epilogue, out_cols, out_dtype, K, name):
    T = streams[0][0].shape[0]
    rb = min(T, CONV_ROWS)
    S = len(streams)

    def body(*refs):
        pre_refs, w_refs, b_refs, o_ref = refs[:S], refs[S:2 * S], refs[2 * S:3 * S], refs[3 * S]
        ws = [w[...] for w in w_refs]
        bs = [b[...] for b in b_refs]

        def step(r, carry):
            convs = [_conv_chunk(pre_refs[s], ws[s], bs[s], r, rb, K)[0] for s in range(S)]
            o_ref[pl.ds(pl.multiple_of(r * rb, rb), rb), :] = epilogue(*convs).astype(o_ref.dtype)
            return carry

        lax.fori_loop(0, T // rb, step, 0)

    in_specs = [pl.BlockSpec((T, LANES), functools.partial(lambda off, i: (0, off + i), st[1])) for st in streams]
    in_specs += [pl.BlockSpec((K, LANES), functools.partial(lambda off, i: (0, off + i), st[4])) for st in streams]
    in_specs += [pl.BlockSpec((1, LANES), functools.partial(lambda off, i: (0, off + i), st[4])) for st in streams]
    return pl.pallas_call(
        body, name=name, out_shape=jax.ShapeDtypeStruct((T, out_cols), out_dtype), grid=(out_cols // LANES,),
        in_specs=in_specs, out_specs=pl.BlockSpec((T, LANES), lambda i: (0, i)),
        compiler_params=_cp("parallel"),
    )(*[st[0] for st in streams], *[st[2] for st in streams], *[st[3] for st in streams])


def conv_bwd(streams, epilogue, dout, K, name):
    pieces = list(dout) if isinstance(dout, (list, tuple)) else [dout]
    T = pieces[0].shape[0]
    ends = [int(e) for e in np.cumsum([d.shape[1] // LANES for d in pieces])]
    cols = int(ends[-1]) * LANES
    rb = min(T, CONV_ROWS)
    S, n_pieces = len(streams), len(pieces)

    def body(*refs):
        pre_refs, w_refs, b_refs = refs[:S], refs[S:2 * S], refs[2 * S:3 * S]
        dout_refs = refs[3 * S:3 * S + n_pieces]
        o = refs[3 * S + n_pieces:]

        def dout_chunk(rows):
            d = dout_refs[-1][rows, :].astype(f32)
            for k in reversed(range(n_pieces - 1)):
                d = jnp.where(pl.program_id(0) < ends[k], dout_refs[k][rows, :].astype(f32), d)
            return d

        dpre_refs, dw_refs, db_refs, scr = o[:S], o[S:2 * S], o[2 * S:3 * S], o[3 * S:]
        ws = [w[...] for w in w_refs]
        bs = [b[...] for b in b_refs]
        for s in range(S):
            scr[s][pl.ds(T, SUBLANES), :] = jnp.zeros((SUBLANES, LANES), f32)

        def phase1(r, carry):
            rows = pl.ds(pl.multiple_of(r * rb, rb), rb)
            convs, shifted = zip(*[_conv_chunk(pre_refs[s], ws[s], bs[s], r, rb, K) for s in range(S)])
            _, vjp = jax.vjp(epilogue, *convs)
            dconvs = vjp(dout_chunk(rows))
            new = []
            for s in range(S):
                scr[s][rows, :] = dconvs[s]
                sums = [jnp.sum(dconvs[s] * shifted[s][k], axis=0, keepdims=True) for k in range(K)]
                sums.append(jnp.sum(dconvs[s], axis=0, keepdims=True))
                new.append(tuple(c + v for c, v in zip(carry[s], sums)))
            return tuple(new)

        zero = tuple(tuple(jnp.zeros((1, LANES), f32) for _ in range(K + 1)) for _ in range(S))
        sums = lax.fori_loop(0, T // rb, phase1, zero)
        tap = lax.broadcasted_iota(jnp.int32, (K, LANES), 0)
        for s in range(S):
            dw = jnp.zeros((K, LANES), f32)
            for k in range(K):
                dw = jnp.where(tap == k, sums[s][k], dw)
            dw_refs[s][...] = dw
            db_refs[s][...] = sums[s][K]

        def phase2(r, carry):
            t0 = pl.multiple_of(r * rb, rb)
            for s in range(S):
                ext = scr[s][pl.ds(t0, rb + SUBLANES), :]
                dpre = ws[s][K - 1:K] * ext[:rb]
                for k in range(K - 1):
                    j = K - 1 - k
                    dpre = dpre + ws[s][k:k + 1] * pltpu.roll(ext, rb + SUBLANES - j, 0)[:rb]
                dpre_refs[s][pl.ds(t0, rb), :] = dpre.astype(dpre_refs[s].dtype)
            return carry

        lax.fori_loop(0, T // rb, phase2, 0)

    in_specs = [pl.BlockSpec((T, LANES), functools.partial(lambda off, i: (0, off + i), st[1])) for st in streams]
    in_specs += [pl.BlockSpec((K, LANES), functools.partial(lambda off, i: (0, off + i), st[4])) for st in streams]
    in_specs += [pl.BlockSpec((1, LANES), functools.partial(lambda off, i: (0, off + i), st[4])) for st in streams]
    starts = [0] + [int(e) for e in ends[:-1]]
    in_specs += [pl.BlockSpec((T, LANES), functools.partial(lambda lo, hi, i: (0, jnp.clip(i, lo, hi - 1) - lo), lo, int(hi)))
                 for lo, hi in zip(starts, ends)]
    out_shape = [jax.ShapeDtypeStruct((T, cols), bf16)] * S
    out_shape += [jax.ShapeDtypeStruct((K, cols), f32)] * S + [jax.ShapeDtypeStruct((1, cols), f32)] * S
    out_specs = [pl.BlockSpec((T, LANES), lambda i: (0, i))] * S
    out_specs += [pl.BlockSpec((K, LANES), lambda i: (0, i))] * S + [pl.BlockSpec((1, LANES), lambda i: (0, i))] * S
    res = pl.pallas_call(
        body, name=name, out_shape=out_shape, grid=(cols // LANES,), in_specs=in_specs, out_specs=out_specs,
        scratch_shapes=[pltpu.VMEM((T + SUBLANES, LANES), f32)] * S,
        compiler_params=_cp("parallel"),
    )(*[st[0] for st in streams], *[st[2] for st in streams], *[st[3] for st in streams], *pieces)
    return res[:S], res[S:2 * S], res[2 * S:]


def epi_gelu_gate(cg, cu):
    return jax.nn.gelu(cg, approximate=True) * cu


def epi_silu(c):
    return jax.nn.silu(c)


XA_ROWS = 256


def _xa_fn(q, k, v):
    s = _nt(q.astype(bf16), k.astype(bf16)) * (q.shape[-1] ** -0.5)
    p = jax.nn.softmax(s, axis=-1)
    return _nn(p.astype(bf16), v.astype(bf16))


def xa_fwd(q, kv, name):
    T, W = q.shape
    M = kv.shape[0]
    H = W // LANES
    tr = min(T, XA_ROWS)

    def body(q_ref, k_ref, v_ref, o_ref):
        o_ref[...] = _xa_fn(q_ref[...].astype(f32), k_ref[...].astype(f32), v_ref[...].astype(f32)).astype(o_ref.dtype)

    return pl.pallas_call(
        body, name=name, out_shape=jax.ShapeDtypeStruct((T, W), bf16), grid=(T // tr, H),
        in_specs=[pl.BlockSpec((tr, LANES), lambda i, h: (i, h)),
                  pl.BlockSpec((M, LANES), lambda i, h: (0, h)),
                  pl.BlockSpec((M, LANES), lambda i, h: (0, H + h))],
        out_specs=pl.BlockSpec((tr, LANES), lambda i, h: (i, h)),
        compiler_params=_cp("parallel", "parallel"),
    )(q, kv, kv)


def xa_bwd(q, kv, do, name):
    T, W = q.shape
    M = kv.shape[0]
    H = W // LANES
    tr = min(T, XA_ROWS)

    def body(q_ref, k_ref, v_ref, do_ref, dq_ref, dk_ref, dv_ref):
        @pl.when(pl.program_id(1) == 0)
        def _():
            dk_ref[...] = jnp.zeros_like(dk_ref)
            dv_ref[...] = jnp.zeros_like(dv_ref)

        _, vjp = jax.vjp(_xa_fn, q_ref[...].astype(f32), k_ref[...].astype(f32), v_ref[...].astype(f32))
        dq, dk, dv = vjp(do_ref[...].astype(f32))
        dq_ref[...] = dq.astype(dq_ref.dtype)
        dk_ref[...] += dk
        dv_ref[...] += dv

    return pl.pallas_call(
        body, name=name,
        out_shape=[jax.ShapeDtypeStruct((T, W), bf16), jax.ShapeDtypeStruct((M, W), f32), jax.ShapeDtypeStruct((M, W), f32)],
        grid=(H, T // tr),
        in_specs=[pl.BlockSpec((tr, LANES), lambda h, i: (i, h)),
                  pl.BlockSpec((M, LANES), lambda h, i: (0, h)),
                  pl.BlockSpec((M, LANES), lambda h, i: (0, H + h)),
                  pl.BlockSpec((tr, LANES), lambda h, i: (i, h))],
        out_specs=[pl.BlockSpec((tr, LANES), lambda h, i: (i, h)),
                   pl.BlockSpec((M, LANES), lambda h, i: (0, h)),
                   pl.BlockSpec((M, LANES), lambda h, i: (0, h))],
        compiler_params=_cp("parallel", "arbitrary"),
    )(q, kv, kv, do)


def _sgu_norm_fn(v_pre, g, b):
    v = jax.nn.gelu(v_pre, approximate=True)
    mu = jnp.mean(v, axis=-1, keepdims=True)
    vc = v - mu
    return vc * lax.rsqrt(jnp.mean(vc * vc, axis=-1, keepdims=True) + EPS) * g + b


def _sgu_mix_fn(u_pre, vn, w, b):
    q = w.shape[0]
    tril = lax.broadcasted_iota(jnp.int32, (q, q), 0) >= lax.broadcasted_iota(jnp.int32, (q, q), 1)
    mixed = _nn(jnp.where(tril, w, 0.0).astype(bf16), vn.astype(bf16)) + b
    return jax.nn.gelu(u_pre, approximate=True) * mixed


def _sgu_norm_phase(v_ref, g, b, vn_ref):
    def step(s, carry):
        rows = pl.ds(pl.multiple_of(s * ROW_STEP, ROW_STEP), ROW_STEP)
        vn_ref[rows, :] = _sgu_norm_fn(v_ref[rows, :], g, b)
        return carry

    lax.fori_loop(0, CHUNK // ROW_STEP, step, 0)


def sgu_fwd(uv_pre, norm_g, norm_b, w_sp, b_sp, name):
    T, W2 = uv_pre.shape
    W = W2 // 2
    G = w_sp.shape[0]
    gw = W // G

    def body(u_ref, v_ref, g_ref, b_ref, ws_ref, bs_ref, o_ref, vn_ref):
        _sgu_norm_phase(v_ref, g_ref[...], b_ref[...], vn_ref)

        def group(gi, carry):
            cols = pl.ds(pl.multiple_of(gi * gw, LANES), gw)
            o_ref[:, cols] = _sgu_mix_fn(u_ref[:, cols], vn_ref[:, cols], ws_ref[gi], bs_ref[gi]).astype(o_ref.dtype)
            return carry

        lax.fori_loop(0, G, group, 0)

    full = lambda a: pl.BlockSpec(a.shape, lambda c: (0,) * a.ndim)
    return pl.pallas_call(
        body, name=name, out_shape=jax.ShapeDtypeStruct((T, W), bf16), grid=(T // CHUNK,),
        in_specs=[pl.BlockSpec((CHUNK, W), lambda c: (c, 0)), pl.BlockSpec((CHUNK, W), lambda c: (c, 1)),
                  full(norm_g), full(norm_b), full(w_sp), full(b_sp)],
        out_specs=pl.BlockSpec((CHUNK, W), lambda c: (c, 0)),
        scratch_shapes=[pltpu.VMEM((CHUNK, W), f32)],
        compiler_params=_cp("parallel"),
    )(uv_pre, uv_pre, norm_g, norm_b, w_sp, b_sp)


def sgu_bwd(uv_pre, norm_g, norm_b, w_sp, b_sp, dout, name):
    T, W2 = uv_pre.shape
    W = W2 // 2
    G = w_sp.shape[0]
    gw = W // G

    def body(u_ref, v_ref, g_ref, b_ref, ws_ref, bs_ref, do_ref, duv_ref, dg_ref, db_ref, dws_ref, dbs_ref,
             vn_ref, dvn_ref):
        @pl.when(pl.program_id(0) == 0)
        def _():
            for a in (dg_ref, db_ref, dws_ref, dbs_ref):
                a[...] = jnp.zeros_like(a)

        g, b = g_ref[...], b_ref[...]
        _sgu_norm_phase(v_ref, g, b, vn_ref)

        def group(gi, carry):
            cols = pl.ds(pl.multiple_of(gi * gw, LANES), gw)
            _, vjp = jax.vjp(_sgu_mix_fn, u_ref[:, cols], vn_ref[:, cols], ws_ref[gi], bs_ref[gi])
            du, dvn, dw, dbias = vjp(do_ref[:, cols])
            duv_ref[:, cols] = du.astype(duv_ref.dtype)
            dvn_ref[:, cols] = dvn
            dws_ref[gi] += dw
            dbs_ref[gi] += dbias
            return carry

        lax.fori_loop(0, G, group, 0)

        def step(s, carry):
            rows = pl.ds(pl.multiple_of(s * ROW_STEP, ROW_STEP), ROW_STEP)
            _, vjp = jax.vjp(_sgu_norm_fn, v_ref[rows, :], g, b)
            dv, dg, dbn = vjp(dvn_ref[rows, :])
            duv_ref[rows, pl.ds(W, W)] = dv.astype(duv_ref.dtype)
            return carry[0] + dg, carry[1] + dbn

        dg, dbn = lax.fori_loop(0, CHUNK // ROW_STEP, step, (jnp.zeros((1, W), f32), jnp.zeros((1, W), f32)))
        dg_ref[...] += dg
        db_ref[...] += dbn

    full = lambda a: pl.BlockSpec(a.shape, lambda c: (0,) * a.ndim)
    return pl.pallas_call(
        body, name=name,
        out_shape=[jax.ShapeDtypeStruct((T, W2), bf16), jax.ShapeDtypeStruct((1, W), f32), jax.ShapeDtypeStruct((1, W), f32),
                   jax.ShapeDtypeStruct(w_sp.shape, f32), jax.ShapeDtypeStruct(b_sp.shape, f32)],
        grid=(T // CHUNK,),
        in_specs=[pl.BlockSpec((CHUNK, W), lambda c: (c, 0)), pl.BlockSpec((CHUNK, W), lambda c: (c, 1)),
                  full(norm_g), full(norm_b), full(w_sp), full(b_sp), pl.BlockSpec((CHUNK, W), lambda c: (c, 0))],
        out_specs=[pl.BlockSpec((CHUNK, W2), lambda c: (c, 0)), full(norm_g), full(norm_b), full(w_sp), full(b_sp)],
        scratch_shapes=[pltpu.VMEM((CHUNK, W), f32), pltpu.VMEM((CHUNK, W), f32)],
        compiler_params=_cp("arbitrary"),
    )(uv_pre, uv_pre, norm_g, norm_b, w_sp, b_sp, dout)


SB_SUM_COLS = 256


def _sb_key_group(T):
    return 512 if T % 512 == 0 else T


def _sb_query_rows(T):
    return 256 if T % 256 == 0 else CHUNK


def _sb_block(q, k, i, g, diagonal):
    qb, kg = q.shape[0], k.shape[0]
    z = _nt(q, k) * (q.shape[-1] ** -0.5)
    sp = jnp.log(1.0 + jnp.exp(-jnp.abs(z)))
    log_beta = jnp.minimum(z, 0.0) - sp
    log_1mb = -jnp.maximum(z, 0.0) - sp
    if not diagonal:
        return z, None, log_beta, log_1mb
    t_idx = i * qb + lax.broadcasted_iota(jnp.int32, (qb, kg), 0)
    s_idx = g * kg + lax.broadcasted_iota(jnp.int32, (qb, kg), 1)
    valid = s_idx < t_idx
    return z, valid, log_beta, jnp.where(valid, log_1mb, 0.0)


def _sb_masked(valid, x):
    return x if valid is None else jnp.where(valid, x, 0.0)


def _order_matrix(later):
    r = lax.broadcasted_iota(jnp.int32, (SB_SUM_COLS, SB_SUM_COLS), 0)
    c = lax.broadcasted_iota(jnp.int32, (SB_SUM_COLS, SB_SUM_COLS), 1)
    return (r > c if later else r < c).astype(bf16)


def _masked_sums(parts, order, n_terms=3):
    terms = []
    for x in parts:
        rest = x
        for _ in range(n_terms):
            term = rest.astype(bf16)
            terms.append(term)
            rest = rest - term.astype(f32)
    rows = parts[0].shape[0]
    prod = _nn(jnp.concatenate(terms, axis=0), order)
    piece = lambda n: prod[n * rows:(n + 1) * rows]
    return [sum(piece(n_terms * p + t) for t in range(1, n_terms)) + piece(n_terms * p) for p in range(len(parts))]


def sb_fwd(qkv, name):
    T = qkv.shape[0]
    H = qkv.shape[1] // (3 * LANES)
    qb, kg = _sb_query_rows(T), _sb_key_group(T)
    halves = kg // SB_SUM_COLS

    def body(q_ref, k_ref, v_ref, o_ref, tot_ref):
        i = pl.program_id(1)
        q = q_ref[...]
        later = _order_matrix(True)
        n_groups = (i * qb + qb + kg - 1) // kg

        def step(g, carry, diagonal):
            acc, run = carry
            rows = pl.ds(pl.multiple_of(g * kg, kg), kg)
            _, valid, log_beta, log_1mb = _sb_block(q, k_ref[rows, :], i, g, diagonal)
            parts = [log_1mb[:, hh * SB_SUM_COLS:(hh + 1) * SB_SUM_COLS] for hh in range(halves)]
            tails = _masked_sums(parts, later)
            for hh in reversed(range(halves)):
                tails[hh] = tails[hh] + run
                run = run + jnp.sum(parts[hh], axis=1, keepdims=True)
            a = _sb_masked(valid, jnp.exp(log_beta + jnp.concatenate(tails, axis=1)))
            return acc + _nn(a.astype(bf16), v_ref[rows, :]), run

        carry = step(n_groups - 1, (jnp.zeros((qb, LANES), f32), jnp.zeros((qb, 1), f32)), True)
        acc, run = lax.fori_loop(0, n_groups - 1, lambda gg, c: step(n_groups - 2 - gg, c, False), carry)
        o_ref[...] = acc.astype(o_ref.dtype)
        tot_ref[...] = run

    return pl.pallas_call(
        body, name=name,
        out_shape=[jax.ShapeDtypeStruct((T, H * LANES), bf16), jax.ShapeDtypeStruct((H, T, 1), f32)],
        grid=(H, T // qb),
        in_specs=[pl.BlockSpec((qb, LANES), lambda h, i: (i, h)),
                  pl.BlockSpec((T, LANES), lambda h, i: (0, H + h)),
                  pl.BlockSpec((T, LANES), lambda h, i: (0, 2 * H + h))],
        out_specs=[pl.BlockSpec((qb, LANES), lambda h, i: (i, h)),
                   pl.BlockSpec((None, qb, 1), lambda h, i: (h, i, 0))],
        compiler_params=_cp("parallel", "parallel"),
    )(qkv, qkv, qkv)


def sb_bwd(qkv, tot, do, name):
    T = qkv.shape[0]
    H = qkv.shape[1] // (3 * LANES)
    qb, kg = _sb_query_rows(T), _sb_key_group(T)
    halves = kg // SB_SUM_COLS

    def body(q_ref, k_ref, v_ref, tot_ref, do_ref, dq_ref, dk_ref, dv_ref):
        i = pl.program_id(1)

        @pl.when(i == 0)
        def _():
            dk_ref[...] = jnp.zeros_like(dk_ref)
            dv_ref[...] = jnp.zeros_like(dv_ref)

        q = q_ref[...]
        do = do_ref[...].astype(bf16)
        tot = tot_ref[...]
        later, earlier = _order_matrix(True), _order_matrix(False)
        scale = q.shape[-1] ** -0.5
        n_groups = (i * qb + qb + kg - 1) // kg

        def step(g, carry, diagonal):
            dq, before, d_run = carry
            rows = pl.ds(pl.multiple_of(g * kg, kg), kg)
            k, v = k_ref[rows, :], v_ref[rows, :]
            z, valid, log_beta, log_1mb = _sb_block(q, k, i, g, diagonal)
            parts = [log_1mb[:, hh * SB_SUM_COLS:(hh + 1) * SB_SUM_COLS] for hh in range(halves)]
            tails = _masked_sums(parts, later)
            for hh in range(halves):
                before = before + jnp.sum(parts[hh], axis=1, keepdims=True)
                tails[hh] = tails[hh] + (tot - before)
            a = _sb_masked(valid, jnp.exp(log_beta + jnp.concatenate(tails, axis=1)))
            d_e = _nt(do, v) * a
            parts = [d_e[:, hh * SB_SUM_COLS:(hh + 1) * SB_SUM_COLS] for hh in range(halves)]
            d_l1 = _masked_sums(parts, earlier, n_terms=2)
            for hh in range(halves):
                d_l1[hh] = d_l1[hh] + d_run
                d_run = d_run + jnp.sum(parts[hh], axis=1, keepdims=True)
            d_l1 = _sb_masked(valid, jnp.concatenate(d_l1, axis=1))
            sig = jax.nn.sigmoid(z)
            dz = ((d_e * (1.0 - sig) - d_l1 * sig) * scale).astype(bf16)
            dk_ref[rows, :] += _tn(dz, q)
            dv_ref[rows, :] += _tn(a.astype(bf16), do)
            return dq + _nn(dz, k), before, d_run

        zero_col = jnp.zeros((qb, 1), f32)
        carry = lax.fori_loop(0, n_groups - 1, lambda g, c: step(g, c, False),
                              (jnp.zeros((qb, LANES), f32), zero_col, zero_col))
        dq, _, _ = step(n_groups - 1, carry, True)
        dq_ref[...] = dq.astype(dq_ref.dtype)

    W = H * LANES
    return pl.pallas_call(
        body, name=name,
        out_shape=[jax.ShapeDtypeStruct((T, W), bf16), jax.ShapeDtypeStruct((T, W), f32), jax.ShapeDtypeStruct((T, W), f32)],
        grid=(H, T // qb),
        in_specs=[pl.BlockSpec((qb, LANES), lambda h, i: (i, h)),
                  pl.BlockSpec((T, LANES), lambda h, i: (0, H + h)),
                  pl.BlockSpec((T, LANES), lambda h, i: (0, 2 * H + h)),
                  pl.BlockSpec((None, qb, 1), lambda h, i: (h, i, 0)),
                  pl.BlockSpec((qb, LANES), lambda h, i: (i, h))],
        out_specs=[pl.BlockSpec((qb, LANES), lambda h, i: (i, h)),
                   pl.BlockSpec((T, LANES), lambda h, i: (0, h)),
                   pl.BlockSpec((T, LANES), lambda h, i: (0, h))],
        compiler_params=_cp("parallel", "arbitrary"),
    )(qkv, qkv, qkv, tot, do)


def _softplus(x):
    return jnp.maximum(x, 0.0) + jnp.log1p(jnp.exp(-jnp.abs(x)))


def _ssd_chunk_fn(head0, xs, b_mat, c_mat, dt_raw, dt_bias, a_log, d_skip, prev):
    q = dt_raw.shape[0]
    lane = lax.broadcasted_iota(jnp.int32, (q, LANES), 1)
    sub = lax.broadcasted_iota(jnp.int32, (q, LANES), 0)
    causal = sub >= lane
    dt = _softplus(dt_raw + dt_bias)
    a_cum = _nn(causal.astype(f32), dt * (-jnp.exp(a_log)), HIGHEST)
    a_cum_t = a_cum.T
    cb = _nt(c_mat.astype(bf16), b_mat.astype(bf16))
    bm, cm = b_mat.astype(bf16), c_mat.astype(bf16)
    ys, new = [], []
    for r in range(len(xs)):
        in_lane, in_sub = lane == head0 + r, sub == head0 + r
        col_a = jnp.sum(jnp.where(in_lane, a_cum, 0.0), axis=1, keepdims=True)
        row_a = jnp.sum(jnp.where(in_sub, a_cum_t, 0.0), axis=0, keepdims=True)
        col_dt = jnp.sum(jnp.where(in_lane, dt, 0.0), axis=1, keepdims=True)
        skip = jnp.sum(jnp.where(in_lane[:1], d_skip, 0.0), axis=1, keepdims=True)
        a_last = jnp.sum(jnp.where(sub[:, :1] == q - 1, col_a, 0.0), axis=0, keepdims=True)
        decay_in = jnp.exp(jnp.where(causal, col_a - row_a, -jnp.inf))
        xdt = xs[r] * col_dt
        y_diag = _nn((cb * decay_in).astype(bf16), xdt.astype(bf16))
        y_off = _nt(cm, prev[r].astype(bf16)) * jnp.exp(col_a)
        ys.append(y_diag + y_off + xs[r] * skip)
        state = _tn((xdt * jnp.exp(a_last - col_a)).astype(bf16), bm)
        new.append(prev[r] * jnp.exp(a_last) + state)
    return ys, new


SSD_GROUPS_PER_STEP = 2


def _ssd_specs(T, G, reverse):
    nc = T // CHUNK
    R, P, N = SSD_HEADS_PER_GROUP, SSD_HEAD_DIM, SSD_STATE
    gp = SSD_GROUPS_PER_STEP if G % SSD_GROUPS_PER_STEP == 0 else 1
    ch = (lambda c: nc - 1 - c) if reverse else (lambda c: c)
    xs = pl.BlockSpec((CHUNK, gp * R * P), lambda g, c: (ch(c), g))
    bm = pl.BlockSpec((CHUNK, gp * N), lambda g, c: (ch(c), G * R * P // (gp * N) + g))
    cm = pl.BlockSpec((CHUNK, gp * N), lambda g, c: (ch(c), (G * R * P + G * N) // (gp * N) + g))
    dt = pl.BlockSpec((CHUNK, LANES), lambda g, c: (ch(c), 0))
    row = pl.BlockSpec((1, LANES), lambda g, c: (0, 0))
    st = pl.BlockSpec((gp, None, R * P, N), lambda g, c: (g, ch(c), 0, 0))
    return nc, gp, xs, bm, cm, dt, row, st


def ssd_fwd(xbc, dt_raw, dt_bias, a_log, d_skip, name):
    T = xbc.shape[0]
    R, P, N = SSD_HEADS_PER_GROUP, SSD_HEAD_DIM, SSD_STATE
    G = xbc.shape[1] // (R * P + 2 * N)
    nc, gp, xs_s, bm_s, cm_s, dt_s, row_s, st_s = _ssd_specs(T, G, False)

    def body(xs_ref, b_ref, c_ref, dt_ref, bias_ref, alog_ref, skip_ref, y_ref, st_ref, state):
        @pl.when(pl.program_id(1) == 0)
        def _():
            state[...] = jnp.zeros_like(state)

        done = []
        for gi in range(gp):
            x = xs_ref[:, gi * R * P:(gi + 1) * R * P]
            xs = [x[:, r * P:(r + 1) * P] for r in range(R)]
            prev = [state[gi, r] for r in range(R)]
            ys, new = _ssd_chunk_fn((pl.program_id(0) * gp + gi) * R, xs, b_ref[:, gi * N:(gi + 1) * N],
                                    c_ref[:, gi * N:(gi + 1) * N], dt_ref[...], bias_ref[...], alog_ref[...],
                                    skip_ref[...], prev)
            done.append((prev, ys, new))
        for gi, (prev, ys, new) in enumerate(done):
            y_ref[:, gi * R * P:(gi + 1) * R * P] = jnp.concatenate(ys, axis=1)
            for r in range(R):
                st_ref[gi, pl.ds(r * P, P), :] = prev[r]
                state[gi, r] = new[r]

    return pl.pallas_call(
        body, name=name,
        out_shape=[jax.ShapeDtypeStruct((T, G * R * P), f32), jax.ShapeDtypeStruct((G, nc, R * P, N), f32)],
        grid=(G // gp, nc), in_specs=[xs_s, bm_s, cm_s, dt_s, row_s, row_s, row_s], out_specs=[xs_s, st_s],
        scratch_shapes=[pltpu.VMEM((gp, R, P, N), f32)],
        compiler_params=_cp("parallel", "arbitrary"),
    )(xbc, xbc, xbc, dt_raw, dt_bias, a_log, d_skip)


def ssd_bwd(xbc, dt_raw, dt_bias, a_log, d_skip, states, dy, name):
    T = xbc.shape[0]
    R, P, N = SSD_HEADS_PER_GROUP, SSD_HEAD_DIM, SSD_STATE
    G = xbc.shape[1] // (R * P + 2 * N)
    nc, gp, xs_s, bm_s, cm_s, dt_s, row_s, st_s = _ssd_specs(T, G, True)

    def body(xs_ref, b_ref, c_ref, dt_ref, bias_ref, alog_ref, skip_ref, st_ref, dy_ref,
             dx_ref, db_ref, dc_ref, ddt_ref, dbias_ref, dalog_ref, dskip_ref, dstate):
        g, c = pl.program_id(0), pl.program_id(1)

        @pl.when(c == 0)
        def _():
            dstate[...] = jnp.zeros_like(dstate)

        @pl.when((c == 0) & (g == 0))
        def _():
            for a in (dbias_ref, dalog_ref, dskip_ref):
                a[...] = jnp.zeros_like(a)

        done = []
        for gi in range(gp):
            cols, ncols = slice(gi * R * P, (gi + 1) * R * P), slice(gi * N, (gi + 1) * N)
            x, dyv = xs_ref[:, cols], dy_ref[:, cols]
            xs = [x[:, r * P:(r + 1) * P] for r in range(R)]
            prev = [st_ref[gi, pl.ds(r * P, P), :] for r in range(R)]
            _, vjp = jax.vjp(functools.partial(_ssd_chunk_fn, (g * gp + gi) * R), xs, b_ref[:, ncols], c_ref[:, ncols],
                             dt_ref[...], bias_ref[...], alog_ref[...], skip_ref[...], prev)
            done.append(vjp(([dyv[:, r * P:(r + 1) * P] for r in range(R)], [dstate[gi, r] for r in range(R)])))
        for gi, (dxs, dbm, dcm, ddt, dbias, dalog, dskip, dprev) in enumerate(done):
            dx_ref[:, gi * R * P:(gi + 1) * R * P] = jnp.concatenate(dxs, axis=1)
            db_ref[:, gi * N:(gi + 1) * N] = dbm
            dc_ref[:, gi * N:(gi + 1) * N] = dcm
            ddt_ref[gi] = ddt
            for r in range(R):
                dstate[gi, r] = dprev[r]
        dbias_ref[...] += sum(d[4] for d in done)
        dalog_ref[...] += sum(d[5] for d in done)
        dskip_ref[...] += sum(d[6] for d in done)

    small = pl.BlockSpec((CHUNK, gp * N), lambda g, c: (nc - 1 - c, g))
    return pl.pallas_call(
        body, name=name,
        out_shape=[jax.ShapeDtypeStruct((T, G * R * P), f32), jax.ShapeDtypeStruct((T, G * N), f32),
                   jax.ShapeDtypeStruct((T, G * N), f32), jax.ShapeDtypeStruct((G, T, LANES), f32)]
        + [jax.ShapeDtypeStruct((1, LANES), f32)] * 3,
        grid=(G // gp, nc), in_specs=[xs_s, bm_s, cm_s, dt_s, row_s, row_s, row_s, st_s, xs_s],
        out_specs=[xs_s, small, small, pl.BlockSpec((gp, CHUNK, LANES), lambda g, c: (g, nc - 1 - c, 0)),
                   row_s, row_s, row_s],
        scratch_shapes=[pltpu.VMEM((gp, R, P, N), f32)],
        compiler_params=_cp("arbitrary", "arbitrary"),
    )(xbc, xbc, xbc, dt_raw, dt_bias, a_log, d_skip, states, dy)


WEIGHTS = ["ln_mix_pre", "ln_mix_post", "ln_mem", "ln_xa_pre", "ln_xa_post", "ln_ffn_pre", "ln_ffn_post",
           "xa_wq", "xa_wkv", "xa_wo", "ffn_w_in", "ffn_conv_w", "ffn_conv_b", "ffn_w_out",
           "ssd_w_in", "ssd_conv_w", "ssd_conv_b", "ssd_dt_bias", "ssd_a_log", "ssd_d", "ssd_norm", "ssd_w_out",
           "sg_w_in", "sg_v_norm_g", "sg_v_norm_b", "sg_w_spatial", "sg_b_spatial", "sg_w_out", "sb_w_qkv", "sb_w_out"]
BIG = ["xa_wq", "xa_wkv", "xa_wo", "ffn_w_in", "ffn_w_out", "ssd_w_in", "ssd_w_out", "sg_w_in", "sg_w_out",
       "sb_w_qkv", "sb_w_out"]
SMALL_SHARDED = ["ffn_conv_w", "ssd_conv_w", "ssd_conv_b", "ssd_norm"]
REPLICATED = [n for n in WEIGHTS if n not in BIG and n not in SMALL_SHARDED]


def fn_xpre(x, g):
    return x, _rms(x, g)


PACK_ROW_TILE = 512


def _pack_rows(arrs, lead=0):
    head = arrs[0].shape[:lead]
    flat = jnp.concatenate([a.reshape(head + (-1,)) for a in arrs], axis=-1)
    n = flat.shape[-1]
    rows = -(-n // (SUBLANES * LANES)) * SUBLANES
    if rows > PACK_ROW_TILE:
        rows = -(-rows // PACK_ROW_TILE) * PACK_ROW_TILE
    flat = jnp.pad(flat, [(0, 0)] * lead + [(0, rows * LANES - n)])
    return flat.reshape(head + (rows, LANES))


def _unpack_rows(packed, shapes):
    head = packed.shape[:-2]
    flat = packed.reshape(head + (-1,))
    out, off = [], 0
    for shp in shapes:
        n = int(np.prod(shp, dtype=np.int64))
        out.append(flat[..., off:off + n].reshape(head + tuple(shp)))
        off += n
    return out


def _merge_last(a8):
    return jnp.moveaxis(a8, 0, -2).reshape(a8.shape[1:-1] + (N_DEV * a8.shape[-1],))


def _split_last(a):
    return jnp.moveaxis(a.reshape(a.shape[:-1] + (N_DEV, a.shape[-1] // N_DEV)), -2, 0)


def kernel(x, mem, ln_mix_pre, ln_mix_post, ln_mem, ln_xa_pre, ln_xa_post, ln_ffn_pre, ln_ffn_post, xa_wq, xa_wkv, xa_wo, ffn_w_in, ffn_conv_w, ffn_conv_b, ffn_w_out, ssd_w_in, ssd_conv_w, ssd_conv_b, ssd_dt_bias, ssd_a_log, ssd_d, ssd_norm, ssd_w_out, sg_w_in, sg_v_norm_g, sg_v_norm_b, sg_w_spatial, sg_b_spatial, sg_w_out, sb_w_qkv, sb_w_out, loss_target, m_ln_mix_pre, m_ln_mix_post, m_ln_mem, m_ln_xa_pre, m_ln_xa_post, m_ln_ffn_pre, m_ln_ffn_post, m_xa_wq, m_xa_wkv, m_xa_wo, m_ffn_w_in, m_ffn_conv_w, m_ffn_conv_b, m_ffn_w_out, m_ssd_w_in, m_ssd_conv_w, m_ssd_conv_b, m_ssd_dt_bias, m_ssd_a_log, m_ssd_d, m_ssd_norm, m_ssd_w_out, m_sg_w_in, m_sg_v_norm_g, m_sg_v_norm_b, m_sg_w_spatial, m_sg_b_spatial, m_sg_w_out, m_sb_w_qkv, m_sb_w_out, v_ln_mix_pre, v_ln_mix_post, v_ln_mem, v_ln_xa_pre, v_ln_xa_post, v_ln_ffn_pre, v_ln_ffn_post, v_xa_wq, v_xa_wkv, v_xa_wo, v_ffn_w_in, v_ffn_conv_w, v_ffn_conv_b, v_ffn_w_out, v_ssd_w_in, v_ssd_conv_w, v_ssd_conv_b, v_ssd_dt_bias, v_ssd_a_log, v_ssd_d, v_ssd_norm, v_ssd_w_out, v_sg_w_in, v_sg_v_norm_g, v_sg_v_norm_b, v_sg_w_spatial, v_sg_b_spatial, v_sg_w_out, v_sb_w_qkv, v_sb_w_out):
    p = dict(locals())
    x, mem, target = p["x"][0], p["mem"][0], p["loss_target"][0]
    T, D = x.shape
    depth = ln_mix_pre.shape[0]
    me = 4 * lax.axis_index("x") + 2 * lax.axis_index("y") + lax.axis_index("c")

    def gather(w, name):
        return sc_all_gather(w.astype(bf16), name)

    parts = {n: {} for n in BIG}
    pending, due = [], {}

    def scatter(name, layer, g8):
        pending.append((name, layer, g8))

    def deadline(name, s):
        if name == "ffn_w_out":
            return s - 2, 0
        if name == "ffn_w_in":
            return s - 2, 1
        if name.startswith("xa_"):
            return (s - 3, 0) if s >= 3 else (0, 1)
        return (s - 1, 0) if name.endswith("_out") else (s - 2, 0)

    def ship_pending(carry, s):
        carry, *held = lax.optimization_barrier((carry, *[g8 for _, _, g8 in pending]))
        for (name, layer, _), g8 in zip(pending, held):
            parts[name][layer] = sc_scatter_parts(g8, "rs_" + name)
            due.setdefault(deadline(name, s), []).append((name, layer))
        pending.clear()
        return carry

    def collect_due(carry, s):
        names = due.pop(s, [])
        if names:
            carry, *landed = lax.optimization_barrier((carry, *[parts[n][l] for n, l in names]))
            for (n, l), a in zip(names, landed):
                parts[n][l] = a
        return carry

    small8 = all_gather(_pack_rows([p[n] for n in SMALL_SHARDED]), "ag_small")
    full = {n: _merge_last(a) for n, a in zip(SMALL_SHARDED, _unpack_rows(small8, [p[n].shape for n in SMALL_SHARDED]))}

    grads = {n: {} for n in WEIGHTS}

    def ssd(h, j):
        g_in = gather(p["ssd_w_in"][j], "ag_ssd_in")
        g_in, h = lax.optimization_barrier((g_in, h))
        w_full = jnp.moveaxis(g_in, 0, 1).reshape(D, -1)
        w_out = gather(p["ssd_w_out"][j], "ag_ssd_out").reshape(1, -1, D)
        d_inner, conv_dim, heads = w_out.shape[1], full["ssd_conv_w"].shape[-1], ssd_dt_bias.shape[1]
        w_z, w_x = w_full[None, :, :d_inner], w_full[None, :, d_inner:d_inner + conv_dim]
        w_dt = jnp.pad(w_full[:, d_inner + conv_dim:], ((0, 0), (0, LANES - heads)))[None]
        lane_row = lambda a: jnp.pad(a[j:j + 1], ((0, 0), (0, LANES - heads)))
        bias, a_log, d_skip = lane_row(ssd_dt_bias), lane_row(ssd_a_log), lane_row(ssd_d)
        norm_g = full["ssd_norm"][j:j + 1]
        z = mm_nn(h, w_z, f32, "ssd_in_z")
        xbc_pre = mm_nn(h, w_x, f32, "ssd_in_x")
        dt_raw = mm_nn(h, w_dt, f32, "ssd_in_dt")
        streams = [(xbc_pre, 0, full["ssd_conv_w"][j], full["ssd_conv_b"][j:j + 1], 0)]
        xbc = conv_fwd(streams, epi_silu, conv_dim, f32, 4, "ssd_conv")
        y, states = ssd_fwd(xbc, dt_raw, bias, a_log, d_skip, "ssd_core")
        gated = row_fwd(fn_gate_norm, [y, z], [norm_g], [(d_inner, bf16)], "ssd_gate")[0]
        out = mm_nn(gated, w_out, f32, "ssd_out")

        def bwd(d_out, mid, ship):
            d_gated = mm_nt(d_out, w_out, f32, "ssd_out_dx")
            scatter("ssd_w_out", j, mm_tn(gated, d_out, D, bf16, "ssd_out_dw").reshape(N_DEV, -1, D))
            d_gated = ship(d_gated)
            dy, dz, d_norm = row_bwd(fn_gate_norm, [y, z], [norm_g], [d_gated], [f32, bf16], "ssd_gate_bwd")
            dxs, dbm, dcm, ddt_g, d_bias, d_alog, d_skipg = ssd_bwd(xbc, dt_raw, bias, a_log, d_skip, states, dy, "ssd_core_bwd")
            dxs = mid(dxs)
            (dx_pre,), (d_cw,), (d_cb,) = conv_bwd(streams, epi_silu, [dxs, dbm, dcm], 4, "ssd_conv_bwd")
            ddt = jnp.sum(ddt_g, axis=0).astype(bf16)
            dh = mm_nt(dz, w_z, f32, "ssd_in_z_dx")
            dh = mm_nt(dx_pre, w_x, f32, "ssd_in_x_dx", init=dh)
            dh = mm_nt(ddt, w_dt, f32, "ssd_in_dt_dx", init=dh)
            dw = jnp.concatenate([mm_tn(h, dz, d_inner, bf16, "ssd_in_z_dw")[0], mm_tn(h, dx_pre, conv_dim, bf16, "ssd_in_x_dw")[0],
                                  mm_tn(h, ddt, LANES, bf16, "ssd_in_dt_dw")[0][:, :heads]], axis=1)
            scatter("ssd_w_in", j, _split_last(dw))
            grads["ssd_conv_w"][j], grads["ssd_conv_b"][j], grads["ssd_norm"][j] = d_cw, d_cb[0], d_norm[0]
            grads["ssd_dt_bias"][j], grads["ssd_a_log"][j], grads["ssd_d"][j] = d_bias[0, :heads], d_alog[0, :heads], d_skipg[0, :heads]
            return dh

        return out, bwd

    def sgu(h, j):
        w_in = gather(p["sg_w_in"][j], "ag_sg_in")
        w_out = gather(p["sg_w_out"][j], "ag_sg_out").reshape(1, -1, D)
        norm_g, norm_b = sg_v_norm_g[j:j + 1], sg_v_norm_b[j:j + 1]
        w_sp, b_sp = sg_w_spatial[j], sg_b_spatial[j][..., None]
        uv = mm_nn(h, w_in, f32, "sg_in")
        gated = sgu_fwd(uv, norm_g, norm_b, w_sp, b_sp, "sg_core")
        out = mm_nn(gated, w_out, f32, "sg_out")

        def bwd(d_out, mid, ship):
            d_gated = mm_nt(d_out, w_out, f32, "sg_out_dx")
            scatter("sg_w_out", j, mm_tn(gated, d_out, D, bf16, "sg_out_dw").reshape(N_DEV, -1, D))
            d_gated = ship(d_gated)
            duv, d_ng, d_nb, d_ws, d_bs = sgu_bwd(uv, norm_g, norm_b, w_sp, b_sp, d_gated, "sg_core_bwd")
            duv = mid(duv)
            grads["sg_v_norm_g"][j], grads["sg_v_norm_b"][j] = d_ng[0], d_nb[0]
            grads["sg_w_spatial"][j], grads["sg_b_spatial"][j] = d_ws, d_bs[..., 0]
            scatter("sg_w_in", j, mm_tn(h, duv, w_in.shape[2], bf16, "sg_in_dw"))
            return mm_nt(duv, w_in, f32, "sg_in_dx")

        return out, bwd

    def stick(h, j):
        w_qkv = gather(p["sb_w_qkv"][j], "ag_sb_qkv")
        w_out = gather(p["sb_w_out"][j], "ag_sb_out").reshape(1, -1, D)
        qkv = mm_nn(h, w_qkv, bf16, "sb_qkv")
        o, tot = sb_fwd(qkv, "sb_core")
        out = mm_nn(o, w_out, f32, "sb_out")

        def bwd(d_out, mid, ship):
            d_o = mm_nt(d_out, w_out, f32, "sb_out_dx")
            scatter("sb_w_out", j, mm_tn(o, d_out, D, bf16, "sb_out_dw").reshape(N_DEV, -1, D))
            d_o = ship(d_o)
            dq, dk, dv = sb_bwd(qkv, tot, d_o, "sb_core_bwd")
            dq = mid(dq)
            dqkv = jnp.concatenate([dq, dk.astype(bf16), dv.astype(bf16)], axis=1)
            scatter("sb_w_qkv", j, mm_tn(h, dqkv, w_qkv.shape[2], bf16, "sb_qkv_dw"))
            return mm_nt(dqkv, w_qkv, f32, "sb_qkv_dx")

        return out, bwd

    def cross(h, i):
        gain = ln_mem[i:i + 1]
        mem_n = row_fwd(fn_prenorm, [mem], [gain], [(D, bf16)], "mem_norm")[0]
        w_q = gather(p["xa_wq"][i], "ag_xa_q").reshape(1, D, -1)
        w_kv = gather(p["xa_wkv"][i], "ag_xa_kv").reshape(1, D, -1)
        w_o = gather(p["xa_wo"][i], "ag_xa_o")
        q = mm_nn(h, w_q, bf16, "xa_q")
        kv = mm_nn(mem_n, w_kv, bf16, "xa_kv")
        o = xa_fwd(q, kv, "xa_core")
        out = mm_nn(o, w_o, f32, "xa_out")

        def bwd(d_out, mid, ship):
            d_o = mm_nt(d_out, w_o, f32, "xa_out_dx")
            scatter("xa_wo", i, mm_tn(o, d_out, w_o.shape[2], bf16, "xa_out_dw"))
            dq, dk, dv = xa_bwd(q, kv, d_o, "xa_core_bwd")
            dkv = jnp.concatenate([dk, dv], axis=1).astype(bf16)
            scatter("xa_wq", i, mm_tn(h, dq, w_q.shape[2], bf16, "xa_q_dw").reshape(N_DEV, -1, w_q.shape[2]))
            scatter("xa_wkv", i, mm_tn(mem_n, dkv, w_kv.shape[2], bf16, "xa_kv_dw").reshape(N_DEV, -1, w_kv.shape[2]))
            d_mem_n = mm_nt(dkv, w_kv, f32, "xa_kv_dx")
            grads["ln_mem"][i] = row_bwd(fn_prenorm, [mem], [gain], [d_mem_n], [None], "mem_norm_bwd")[0][0]
            return mm_nt(dq, w_q, f32, "xa_q_dx")

        return out, bwd

    def ffn(h, i):
        w_in = gather(p["ffn_w_in"][i], "ag_ffn_in")
        w_out = gather(p["ffn_w_out"][i], "ag_ffn_out").reshape(1, -1, D)
        width = w_out.shape[1]
        conv_w, conv_b = full["ffn_conv_w"][i], ffn_conv_b[i:i + 1]
        gu = mm_nn(h, w_in, f32, "ffn_in")
        streams = [(gu, 0, conv_w, conv_b, 0), (gu, width // LANES, conv_w, conv_b, width // LANES)]
        act = conv_fwd(streams, epi_gelu_gate, width, bf16, 3, "ffn_gate")
        out = mm_nn(act, w_out, f32, "ffn_out")

        def bwd(d_out, mid, ship):
            d_act = mm_nt(d_out, w_out, f32, "ffn_out_dx")
            scatter("ffn_w_out", i, mm_tn(act, d_out, D, bf16, "ffn_out_dw").reshape(N_DEV, -1, D))
            d_pre, d_cw, d_cb = conv_bwd(streams, epi_gelu_gate, d_act, 3, "ffn_gate_bwd")
            dgu = jnp.concatenate(d_pre, axis=1)
            grads["ffn_conv_w"][i], grads["ffn_conv_b"][i] = jnp.concatenate(d_cw, axis=1), jnp.concatenate(d_cb, axis=1)[0]
            scatter("ffn_w_in", i, mm_tn(h, dgu, w_in.shape[2], bf16, "ffn_in_dw"))
            return mm_nt(dgu, w_in, f32, "ffn_in_dx")

        return out, bwd

    n_sub = 3 * depth
    pre = [w[i:i + 1] for i in range(depth) for w in (ln_mix_pre, ln_xa_pre, ln_ffn_pre)]
    post = [w[i:i + 1] for i in range(depth) for w in (ln_mix_post, ln_xa_post, ln_ffn_post)]
    stream, outs, bwds = [x], [], []
    h = row_fwd(fn_prenorm, [x], [pre[0]], [(D, bf16)], "pre_norm")[0]
    for s in range(n_sub):
        i, t = divmod(s, 3)
        out, bwd = ((ssd, sgu, stick)[i % 3](h, i // 3) if t == 0 else cross(h, i) if t == 1 else ffn(h, i))
        outs.append(out)
        bwds.append(bwd)
        if s < n_sub - 1:
            x_new, h = row_fwd(fn_resnorm, [stream[s], out], [post[s], pre[s + 1]], [(D, f32), (D, bf16)], "res_norm")
            stream.append(x_new)
    dy, loss = row_fwd(fn_final, [stream[-1], outs[-1], target], [post[-1]], [(D, f32), (1, f32)], "loss_head", n_acc=1)
    loss = lax.psum(loss[0, 0], ("x", "y", "c"))

    d_pre, d_post = [None] * n_sub, [None] * n_sub
    dx, d_out, d_post[-1] = row_bwd(fn_res, [stream[-1], outs[-1]], [post[-1]], [dy], [f32, bf16], "res_bwd")
    for s in reversed(range(n_sub)):
        dh = bwds[s](collect_due(d_out, (s, 0)), functools.partial(collect_due, s=(s, 1)), functools.partial(ship_pending, s=s))
        dh = ship_pending(dh, s)
        if s > 0:
            dx, d_out, d_post[s - 1], d_pre[s] = row_bwd(
                fn_resnorm, [stream[s - 1], outs[s - 1]], [post[s - 1], pre[s]], [dx, dh], [f32, bf16], "res_norm_bwd")
        else:
            grad_x, d_pre[0] = row_bwd(fn_xpre, [x], [pre[0]], [dx, dh], [f32], "pre_norm_bwd")
    for t, kind in enumerate(("mix", "xa", "ffn")):
        for i in range(depth):
            grads["ln_%s_pre" % kind][i] = d_pre[3 * i + t][0]
            grads["ln_%s_post" % kind][i] = d_post[3 * i + t][0]

    def stacked(name):
        return jnp.stack([grads[name][l] for l in range(len(grads[name]))], axis=0)

    new = {}
    for name in BIG:
        new[name] = adamw_sharded(p[name], [parts[name][l] for l in range(len(parts[name]))],
                                  p["m_" + name], p["v_" + name], "adamw_" + name)

    rep_shapes = [p[n].shape for n in REPLICATED]
    g8 = all_gather(_pack_rows([stacked(n) for n in REPLICATED]), "ag_grad_rep")
    rep = adamw_summed8(_pack_rows([p[n] for n in REPLICATED]), g8, _pack_rows([p["m_" + n] for n in REPLICATED]),
                        _pack_rows([p["v_" + n] for n in REPLICATED]), "adamw_rep")
    for k, packed in enumerate(rep):
        for n, a in zip(REPLICATED, _unpack_rows(packed, rep_shapes)):
            new.setdefault(n, [None] * 4)[k] = a

    sh_shapes = [p[n].shape for n in SMALL_SHARDED]
    by_owner = _pack_rows([_split_last(stacked(n)) for n in SMALL_SHARDED], lead=1)
    mine8 = lax.dynamic_index_in_dim(all_gather(by_owner, "ag_grad_small"), me, axis=1, keepdims=False)
    sh = adamw_summed8(_pack_rows([p[n] for n in SMALL_SHARDED]), mine8, _pack_rows([p["m_" + n] for n in SMALL_SHARDED]),
                       _pack_rows([p["v_" + n] for n in SMALL_SHARDED]), "adamw_small")
    for k, packed in enumerate(sh):
        for n, a in zip(SMALL_SHARDED, _unpack_rows(packed, sh_shapes)):
            new.setdefault(n, [None] * 4)[k] = a

    return (loss, grad_x[None], *[new[n][0] for n in WEIGHTS], *[new[n][1] for n in WEIGHTS],
            *[new[n][2] for n in WEIGHTS], *[new[n][3] for n in WEIGHTS])
```
